```python
import jax
import jax.numpy as jnp
from jax import lax
import numpy as np

D_MODEL = 1024
BATCH = 32
SEQ = 256
DEPTH = 2
DEC_BATCH = 2
DEC_SEQ = 4096
PAST_LEN = 256

GRID_W = 64
N_EVEN = (DEPTH + 1) // 2
N_ODD = DEPTH // 2
HA_HEADS = 4
HA_DK = 128
HA_DV = 128
HA_W = HA_HEADS * HA_DK
CHUNK = 32
MLA_HEADS = 8
MLA_DN = 64
MLA_DR = 32
MLA_DV = 64
MLA_Q_RANK = 384
MLA_KV_RANK = 256
ROPE_BASE = 10000.0
Q_BLOCK = 128
AB_IN = 5 * HA_W + MLA_Q_RANK + MLA_KV_RANK + MLA_DR
AB_OUT = HA_HEADS * HA_DV + MLA_HEADS * MLA_DV
AB_SPLITS = (HA_W, 2 * HA_W, 3 * HA_W, 4 * HA_W, 5 * HA_W, 5 * HA_W + MLA_Q_RANK, 5 * HA_W + MLA_Q_RANK + MLA_KV_RANK)
SC_W = 512
SC_K = 3
CF_W = 512
CF_K = 31
CD_IN = 3 * SC_W + 2 * CF_W
CD_OUT = SC_W + CF_W
CD_SPLITS = (SC_W, 2 * SC_W, 3 * SC_W, 3 * SC_W + CF_W)
N_GROUPS = 4
EXP_PER_GROUP = 8
N_EXPERTS = N_GROUPS * EXP_PER_GROUP
EXP_FF = 128
TOP_K = 2
ALPHA = (2.0 * DEPTH) ** 0.25
BETA = (8.0 * DEPTH) ** -0.25
LN_EPS = 1e-5
RMS_EPS = 1e-6
F32 = jnp.float32

kernel_name = 'hybrid_dit_hgrn2_mla_conv_hmoe_step'


def _layer_norm(x, g, b):
    x32 = x.astype(F32)
    mu = jnp.mean(x32, axis=-1, keepdims=True)
    var = jnp.mean(jnp.square(x32 - mu), axis=-1, keepdims=True)
    y = (x32 - mu) * lax.rsqrt(var + LN_EPS)
    return (y * g.astype(F32) + b.astype(F32)).astype(x.dtype)


def _rms_norm(x, g):
    x32 = x.astype(F32)
    y = x32 * lax.rsqrt(jnp.mean(jnp.square(x32), axis=-1, keepdims=True) + RMS_EPS)
    return (y * g.astype(F32)).astype(x.dtype)


def _modulate(x, shift, scale):
    return x * (1.0 + scale) + shift


def _dwconv(x, w):
    k, ch = w.shape
    return lax.conv_general_dilated(x, w[:, None, :].astype(x.dtype), window_strides=(1,),
                                    padding=[(k // 2, k // 2)],
                                    dimension_numbers=('NWC', 'WIO', 'NWC'),
                                    feature_group_count=ch)


def _axial_rope_tables(n_tok):
    rows = n_tok // GRID_W
    pos_r = jnp.repeat(jnp.arange(rows, dtype=F32), GRID_W)
    pos_c = (jnp.arange(rows * GRID_W) % GRID_W).astype(F32)
    n_freq = MLA_DR // 4
    inv = ROPE_BASE ** (-jnp.arange(n_freq, dtype=F32) / n_freq)
    ang = jnp.stack([pos_r[:, None] * inv, pos_c[:, None] * inv], axis=1)
    return jnp.cos(ang), jnp.sin(ang)


def _rope_2d(x, cos, sin):
    xs = x.reshape(x.shape[:-1] + (2, 2, MLA_DR // 4))
    x1, x2 = xs[..., 0, :], xs[..., 1, :]
    cos = cos.astype(x.dtype)
    sin = sin.astype(x.dtype)
    out = jnp.stack([x1 * cos - x2 * sin, x1 * sin + x2 * cos], axis=-2)
    return out.reshape(x.shape)


def _chunk_scan(q, k, v, log_f, s0):
    B, L, H, _ = q.shape
    dv = v.shape[-1]
    n = L // CHUNK

    def to_chunks(t):
        return t.reshape(B, n, CHUNK, H, t.shape[-1]).transpose(1, 0, 3, 2, 4)

    tri = jnp.tril(jnp.ones((CHUNK, CHUNK), dtype=bool))[:, :, None]

    def step(S, blk):
        qc, kc, vc, gc = blk
        b = jnp.cumsum(gc, axis=2)
        diff = b[:, :, :, None, :] - b[:, :, None, :, :]
        decay = jnp.exp(jnp.where(tri, diff, -jnp.inf))
        scores = jnp.einsum('bhtk,bhsk,bhtsk->bhts', qc, kc, decay)
        o = scores @ vc + jnp.einsum('bhtk,bhkv->bhtv', qc * jnp.exp(b), S)
        b_last = b[:, :, -1:, :]
        S_new = jnp.exp(b_last[:, :, 0, :])[..., None] * S + jnp.einsum('bhsk,bhsv->bhkv', kc * jnp.exp(b_last - b), vc)
        return S_new, o

    S_fin, o = lax.scan(step, s0.astype(F32), (to_chunks(q), to_chunks(k), to_chunks(v), to_chunks(log_f)))
    return o.transpose(1, 0, 3, 2, 4).reshape(B, L, H, dv), S_fin


def _flip(t):
    return jnp.flip(t, axis=1)


def _hgrn2(q_lin, f_lin_f, f_lin_b, i_lin, g_lin, lb_f, lb_b, g_norm, s0_f, s0_b):
    B, L, _ = q_lin.shape

    def heads(t):
        return t.astype(F32).reshape(B, L, HA_HEADS, -1)

    q = jax.nn.silu(heads(q_lin))
    v = heads(i_lin)

    def forget(f_lin, lb):
        lb = lb.reshape(HA_HEADS, HA_DK)
        f = lb + (1.0 - lb) * jax.nn.sigmoid(heads(f_lin))
        return jnp.log(f), 1.0 - f

    logf_f, k_f = forget(f_lin_f, lb_f)
    logf_b, k_b = forget(f_lin_b, lb_b)
    o_f, s_f = _chunk_scan(q, k_f, v, logf_f, s0_f)
    o_b, s_b = _chunk_scan(_flip(q), _flip(k_b), _flip(v), _flip(logf_b), s0_b)
    o = _rms_norm(o_f + _flip(o_b), g_norm) * jax.nn.silu(heads(g_lin))
    return o.reshape(B, L, HA_HEADS * HA_DV).astype(q_lin.dtype), s_f, s_b


def _attend(q_nope, q_rope, k_nope, k_rope, v):
    B, L, H, _ = q_nope.shape
    nb = L // Q_BLOCK
    scale = (MLA_DN + MLA_DR) ** -0.5

    def blocks(t):
        return t.reshape(B, nb, Q_BLOCK, H, t.shape[-1]).transpose(1, 0, 2, 3, 4)

    def one_block(qb):
        qn, qr = qb
        s = jnp.einsum('bqhd,bkhd->bhqk', qn, k_nope) + jnp.einsum('bqhr,bkr->bhqk', qr, k_rope)
        p = jax.nn.softmax(s.astype(F32) * scale, axis=-1)
        return jnp.einsum('bhqk,bkhd->bqhd', p.astype(v.dtype), v)

    o = lax.map(one_block, (blocks(q_nope), blocks(q_rope)))
    return o.transpose(1, 0, 2, 3, 4).reshape(B, L, H, MLA_DV)


def _mla(cq_raw, ckv_raw, kr, q_norm_g, w_uq, kv_norm_g, w_ukv, rope, ctx_ckv, ctx_kr):
    B, L, _ = cq_raw.shape
    q = (_rms_norm(cq_raw, q_norm_g) @ w_uq).reshape(B, L, MLA_HEADS, MLA_DN + MLA_DR)
    q_nope, q_rope = q[..., :MLA_DN], q[..., MLA_DN:]
    ckv = _rms_norm(ckv_raw, kv_norm_g)
    if rope is None:
        ckv_keys, kr_keys = ckv, kr
    else:
        cos, sin = rope
        q_rope = _rope_2d(q_rope, cos[:, None], sin[:, None])
        ckv_keys = jnp.concatenate([ckv, ctx_ckv.astype(ckv.dtype)], axis=1)
        kr_keys = jnp.concatenate([_rope_2d(kr, cos, sin), ctx_kr.astype(kr.dtype)], axis=1)
    kv = (ckv_keys @ w_ukv).reshape(B, ckv_keys.shape[1], MLA_HEADS, MLA_DN + MLA_DV)
    o = _attend(q_nope, q_rope, kv[..., :MLA_DN], kr_keys, kv[..., MLA_DN:])
    return o.reshape(B, L, MLA_HEADS * MLA_DV), ckv


def _ab_mixer(h, s0_f, s0_b, rope, ctx_ckv, ctx_kr, w_in, w_out, lb_f, lb_b, g_norm,
              q_norm_g, w_uq, kv_norm_g, w_ukv):
    q_a, f_f, f_b, i_a, g_a, cq, ckv_raw, kr = jnp.split(h @ w_in, AB_SPLITS, axis=-1)
    o_a, s_f, s_b = _hgrn2(q_a, f_f, f_b, i_a, g_a, lb_f, lb_b, g_norm, s0_f, s0_b)
    o_b, ckv = _mla(cq, ckv_raw, kr, q_norm_g, w_uq, kv_norm_g, w_ukv, rope, ctx_ckv, ctx_kr)
    y = jnp.concatenate([o_a, o_b], axis=-1) @ w_out
    return y, s_f, s_b, ckv, kr


def _cd_mixer(h, w_in, w_out, sc_w, cf_w, cf_b, cf_g, cf_beta):
    bg, cg, xv, ga, gb = jnp.split(h @ w_in, CD_SPLITS, axis=-1)
    y_sc = bg * _dwconv(cg * xv, sc_w)
    u = ga * jax.nn.sigmoid(gb)
    u = _layer_norm(_dwconv(u, cf_w) + cf_b, cf_g, cf_beta)
    y_cf = jax.nn.silu(u)
    return jnp.concatenate([y_sc, y_cf], axis=-1) @ w_out


def _hier_moe(h, w_group, w_expert, w_gate, w_up, w_down):
    B, L, D = h.shape
    t = h.reshape(B * L, D)
    p_group = jax.nn.softmax((t @ w_group).astype(F32), axis=-1)
    g_sel = jnp.argmax(p_group, axis=-1)
    p_g = jnp.max(p_group, axis=-1, keepdims=True)
    e_logits = (t @ w_expert).astype(F32).reshape(-1, N_GROUPS, EXP_PER_GROUP)
    e_in = jnp.einsum('tg,tge->te', jax.nn.one_hot(g_sel, N_GROUPS, dtype=F32), e_logits)
    top_v, top_i = lax.top_k(e_in, TOP_K)
    w_top = jax.nn.softmax(top_v, axis=-1) * p_g
    expert_id = g_sel[:, None] * EXP_PER_GROUP + top_i
    combine = jnp.einsum('tk,tke->te', w_top, jax.nn.one_hot(expert_id, N_EXPERTS, dtype=F32))
    hid = jax.nn.silu(jnp.einsum('td,edf->tef', t, w_gate)) * jnp.einsum('td,edf->tef', t, w_up)
    out = jnp.einsum('tef,efd->td', hid * combine.astype(h.dtype)[:, :, None], w_down)
    return out.reshape(B, L, D)


def setup_inputs(seed: int = 0) -> dict:
    key = jax.random.key(seed)
    ks = iter(jax.random.split(key, 40))

    def nrm(shape, scale):
        return scale * jax.random.normal(next(ks), shape, F32)

    return {
        'x_prompt': nrm((BATCH, SEQ, D_MODEL), 1.0),
        'x_sample': nrm((DEC_BATCH, DEC_SEQ, D_MODEL), 1.0),
        'state_hgrn_fwd': nrm((DEC_BATCH, N_EVEN, HA_HEADS, HA_DK, HA_DV), 0.5),
        'state_hgrn_bwd': nrm((DEC_BATCH, N_EVEN, HA_HEADS, HA_DK, HA_DV), 0.5),
        'cache_mla_ckv': nrm((DEC_BATCH, N_EVEN, PAST_LEN, MLA_KV_RANK), 1.0),
        'cache_mla_krope': nrm((DEC_BATCH, N_EVEN, PAST_LEN, MLA_DR), 1.0),
        'c': nrm((DEC_BATCH, D_MODEL), 1.0),
        'c_ctx': nrm((D_MODEL,), 1.0),
        'mod_w': nrm((DEPTH, D_MODEL, 6 * D_MODEL), 0.5 * D_MODEL ** -0.5),
        'mod_b': nrm((DEPTH, 6 * D_MODEL), 0.02),
        'ln_g': 1.0 + nrm((DEPTH, 2, D_MODEL), 0.05),
        'ln_b': nrm((DEPTH, 2, D_MODEL), 0.02),
        'ab_w_in': nrm((N_EVEN, D_MODEL, AB_IN), D_MODEL ** -0.5),
        'ab_w_out': nrm((N_EVEN, AB_OUT, D_MODEL), BETA * AB_OUT ** -0.5),
        'hgrn_lb_logits': nrm((N_EVEN + 1, 2, HA_W), 0.5),
        'hgrn_norm_g': 1.0 + nrm((N_EVEN, HA_DV), 0.05),
        'mla_q_norm_g': 1.0 + nrm((N_EVEN, MLA_Q_RANK), 0.05),
        'mla_w_uq': nrm((N_EVEN, MLA_Q_RANK, MLA_HEADS * (MLA_DN + MLA_DR)), MLA_Q_RANK ** -0.5),
        'mla_kv_norm_g': 1.0 + nrm((N_EVEN, MLA_KV_RANK), 0.05),
        'mla_w_ukv': nrm((N_EVEN, MLA_KV_RANK, MLA_HEADS * (MLA_DN + MLA_DV)), MLA_KV_RANK ** -0.5),
        'cd_w_in': nrm((N_ODD, D_MODEL, CD_IN), D_MODEL ** -0.5),
        'cd_w_out': nrm((N_ODD, CD_OUT, D_MODEL), BETA * CD_OUT ** -0.5),
        'sc_conv_w': nrm((N_ODD, SC_K, SC_W), SC_K ** -0.5),
        'cf_conv_w': nrm((N_ODD, CF_K, CF_W), CF_K ** -0.5),
        'cf_conv_b': nrm((N_ODD, CF_W), 0.02),
        'cf_ln_g': 1.0 + nrm((N_ODD, CF_W), 0.05),
        'cf_ln_b': nrm((N_ODD, CF_W), 0.02),
        'moe_w_group': nrm((DEPTH, D_MODEL, N_GROUPS), D_MODEL ** -0.5),
        'moe_w_expert': nrm((DEPTH, D_MODEL, N_EXPERTS), D_MODEL ** -0.5),
        'moe_w_gate': nrm((DEPTH, N_EXPERTS, D_MODEL, EXP_FF), D_MODEL ** -0.5),
        'moe_w_up': nrm((DEPTH, N_EXPERTS, D_MODEL, EXP_FF), D_MODEL ** -0.5),
        'moe_w_down': nrm((DEPTH, N_EXPERTS, EXP_FF, D_MODEL), BETA * EXP_FF ** -0.5),
    }


def reference(x_prompt, x_sample, state_hgrn_fwd, state_hgrn_bwd, cache_mla_ckv, cache_mla_krope,
              c, c_ctx, mod_w, mod_b, ln_g, ln_b, ab_w_in, ab_w_out, hgrn_lb_logits, hgrn_norm_g,
              mla_q_norm_g, mla_w_uq, mla_kv_norm_g, mla_w_ukv, cd_w_in, cd_w_out, sc_conv_w,
              cf_conv_w, cf_conv_b, cf_ln_g, cf_ln_b, moe_w_group, moe_w_expert, moe_w_gate,
              moe_w_up, moe_w_down):
    lb_all = jnp.cumsum(jax.nn.softmax(hgrn_lb_logits.astype(F32), axis=0), axis=0)[:N_EVEN]
    rope = _axial_rope_tables(x_sample.shape[1])
    bp = x_prompt.shape[0]
    xp, xs = x_prompt, x_sample
    new_sf, new_sb, new_ckv, new_kr = [], [], [], []
    for l in range(DEPTH):
        m_ctx = jnp.split(jax.nn.silu(c_ctx) @ mod_w[l] + mod_b[l], 6, axis=-1)
        m_lat = jnp.split((jax.nn.silu(c) @ mod_w[l] + mod_b[l])[:, None, :], 6, axis=-1)
        hp = _modulate(xp, m_ctx[0], m_ctx[1])
        hs = _modulate(xs, m_lat[0], m_lat[1])
        if l % 2 == 0:
            e = l // 2
            ab = (ab_w_in[e], ab_w_out[e], lb_all[e, 0], lb_all[e, 1], hgrn_norm_g[e],
                  mla_q_norm_g[e], mla_w_uq[e], mla_kv_norm_g[e], mla_w_ukv[e])
            zeros = jnp.zeros((bp, HA_HEADS, HA_DK, HA_DV), F32)
            yp, sf, sb, ckv, kr = _ab_mixer(hp, zeros, zeros, None, None, None, *ab)
            ys, _, _, _, _ = _ab_mixer(hs, state_hgrn_fwd[:, e], state_hgrn_bwd[:, e], rope,
                                       cache_mla_ckv[:, e], cache_mla_krope[:, e], *ab)
            new_sf.append(sf.astype(x_prompt.dtype))
            new_sb.append(sb.astype(x_prompt.dtype))
            new_ckv.append(ckv)
            new_kr.append(kr)
        else:
            j = l // 2
            cd = (cd_w_in[j], cd_w_out[j], sc_conv_w[j], cf_conv_w[j], cf_conv_b[j], cf_ln_g[j], cf_ln_b[j])
            yp = _cd_mixer(hp, *cd)
            ys = _cd_mixer(hs, *cd)
        xp = _layer_norm(ALPHA * xp + m_ctx[2] * yp, ln_g[l, 0], ln_b[l, 0])
        xs = _layer_norm(ALPHA * xs + m_lat[2] * ys, ln_g[l, 0], ln_b[l, 0])
        moe = (moe_w_group[l], moe_w_expert[l], moe_w_gate[l], moe_w_up[l], moe_w_down[l])
        xp = _layer_norm(ALPHA * xp + m_ctx[5] * _hier_moe(_modulate(xp, m_ctx[3], m_ctx[4]), *moe), ln_g[l, 1], ln_b[l, 1])
        xs = _layer_norm(ALPHA * xs + m_lat[5] * _hier_moe(_modulate(xs, m_lat[3], m_lat[4]), *moe), ln_g[l, 1], ln_b[l, 1])
    new_state_hgrn_fwd = jnp.stack(new_sf, axis=1)
    new_state_hgrn_bwd = jnp.stack(new_sb, axis=1)
    new_cache_mla_ckv = jnp.stack(new_ckv, axis=1)
    new_cache_mla_krope = jnp.stack(new_kr, axis=1)
    return (xp, xs, new_state_hgrn_fwd, new_state_hgrn_bwd, new_cache_mla_ckv, new_cache_mla_krope)
```

```python
import functools

import numpy as np
import jax
import jax.numpy as jnp
from jax import lax
from jax.experimental import pallas as pl
from jax.experimental.pallas import tpu as pltpu

F32 = jnp.float32
BF16 = jnp.bfloat16
HIGHEST = lax.Precision.HIGHEST

D_MODEL = 1024
DEPTH = 2
GRID_W = 64
N_EVEN = (DEPTH + 1) // 2
HA_HEADS = 4
HA_DK = 128
HA_DV = 128
HA_W = HA_HEADS * HA_DK
CHUNK = 32
MLA_HEADS = 8
MLA_DN = 64
MLA_DR = 32
MLA_DV = 64
MLA_Q_RANK = 384
MLA_KV_RANK = 256
ROPE_BASE = 10000.0
SC_W = 512
SC_K = 3
CF_W = 512
CF_K = 31
N_GROUPS = 4
EXP_PER_GROUP = 8
N_EXPERTS = N_GROUPS * EXP_PER_GROUP
EXP_FF = 128
ALPHA = (2.0 * DEPTH) ** 0.25
LN_EPS = 1e-5
RMS_EPS = 1e-6

LANES = 128
SUBLANES = 8
VMEM_LIMIT = 56 * 1024 * 1024

HEAD_PAD = LANES
MLA_SLAB = MLA_Q_RANK + MLA_KV_RANK + 2 * LANES
TOK_TILE = 256
GROUP_ROWS = 128
N_LEVELS = 5
HALO = 16


def _cparams(sem):
    return pltpu.CompilerParams(dimension_semantics=sem, vmem_limit_bytes=VMEM_LIMIT)


def _silu(x):
    return x / (1.0 + jnp.exp(-x))


def _sigmoid(x):
    return 1.0 / (1.0 + jnp.exp(-x))


def _layer_norm(r, g, b):
    mu = jnp.mean(r, axis=-1, keepdims=True)
    d = r - mu
    var = jnp.mean(d * d, axis=-1, keepdims=True)
    return d * lax.rsqrt(var + LN_EPS) * g + b


def _dot(a, b):
    return jnp.dot(a, b, preferred_element_type=F32)


def _dot_nt(a, b):
    return lax.dot_general(a, b, (((1,), (1,)), ((), ())), preferred_element_type=F32)


def _dot_tn(a, b):
    return lax.dot_general(a, b, (((0,), (0,)), ((), ())), preferred_element_type=F32)


def _mod_kernel(c_ref, w_ref, b_ref, o_ref):
    s = _silu(c_ref[...])
    o_ref[0] = jnp.dot(s, w_ref[0], precision=HIGHEST, preferred_element_type=F32) + b_ref[0]


def _mod_vectors(cvec, mod_w, mod_b):
    n_out = mod_w.shape[-1]
    tn = 1536
    return pl.pallas_call(
        _mod_kernel,
        out_shape=jax.ShapeDtypeStruct((DEPTH, SUBLANES, n_out), F32),
        grid=(DEPTH, n_out // tn),
        in_specs=[
            pl.BlockSpec((SUBLANES, D_MODEL), lambda l, j: (0, 0)),
            pl.BlockSpec((1, D_MODEL, tn), lambda l, j: (l, 0, j)),
            pl.BlockSpec((1, 1, tn), lambda l, j: (l, 0, j)),
        ],
        out_specs=pl.BlockSpec((1, SUBLANES, tn), lambda l, j: (l, 0, j)),
        compiler_params=_cparams(("arbitrary", "arbitrary")),
        name="mod_vectors",
    )(cvec, mod_w, mod_b.reshape(DEPTH, 1, n_out))


def _inproj_kernel(*refs, n_w):
    x_ref, mod_ref = refs[0], refs[1]
    w_refs = refs[2:2 + n_w]
    o_refs = refs[2 + n_w:]
    m = mod_ref[0]
    h = (x_ref[0] * (1.0 + m[1:2]) + m[0:1]).astype(BF16)
    for w_ref, o_ref in zip(w_refs, o_refs):
        o_ref[0] = _dot(h, w_ref[...])


def _inproj(x, mod, weights):
    b, l, d = x.shape
    tm = TOK_TILE
    per_batch = mod.shape[0] > 1
    mod_map = (lambda i, j: (i, 0, 0)) if per_batch else (lambda i, j: (0, 0, 0))
    in_specs = [pl.BlockSpec((1, tm, d), lambda i, j: (i, j, 0)),
                pl.BlockSpec((1, 6, d), mod_map)]
    in_specs += [pl.BlockSpec(w.shape, lambda i, j: (0, 0)) for w in weights]
    out_shape = [jax.ShapeDtypeStruct((b, l, w.shape[1]), F32) for w in weights]
    out_specs = [pl.BlockSpec((1, tm, w.shape[1]), lambda i, j: (i, j, 0)) for w in weights]
    return pl.pallas_call(
        functools.partial(_inproj_kernel, n_w=len(weights)),
        out_shape=out_shape,
        grid=(b, l // tm),
        in_specs=in_specs,
        out_specs=out_specs,
        compiler_params=_cparams(("parallel", "parallel")),
        name="modulate_inproj",
    )(x, mod, *weights)


def _hgrn_tables():
    n = GROUP_ROWS
    t = np.arange(n)[:, None]
    j = np.arange(n)[None, :]
    same_chunk = (t // CHUNK) == (j // CHUNK)
    e_f, e_b = [], []
    lv_f = np.full((n, n), -1, np.int32)
    for lvl in range(N_LEVELS):
        m = CHUNK >> (lvl + 1)
        blk0 = (t // (2 * m)) * (2 * m)
        r = blk0 + m - 1
        upper = t > r
        ef = np.where(upper, (j > r) & (j <= t), (j > t) & (j <= r))
        r2 = blk0 + m
        lower = t < r2
        eb = np.where(lower, (j >= t) & (j < r2), (j >= r2) & (j < t))
        e_f.append(ef)
        e_b.append(eb)
        s = np.arange(n)[None, :]
        same_blk = (t // (2 * m)) == (s // (2 * m))
        q_side = (t % (2 * m)) >= m
        k_side = (s % (2 * m)) < m
        lv_f[same_blk & q_side & k_side] = lvl
    e_f.append(same_chunk & (j <= t))
    e_b.append(same_chunk & (j >= t))
    lv_f[np.arange(n), np.arange(n)] = N_LEVELS
    e_f = np.concatenate(e_f, axis=0).astype(np.float32)
    e_b = np.concatenate(e_b, axis=0).astype(np.float32)
    e = np.stack([e_f, e_b])
    lv = np.stack([lv_f, lv_f.T])
    return e, lv


def _hgrn_kernel(qf_ref, vf_ref, ff_ref, qb_ref, vb_ref, fb_ref, lbl_ref, e_ref, lv_ref, s0f_ref, s0b_ref,
                 of_ref, ob_ref, sf_ref, sb_ref,
                 st_scr, qe_scr, kd_scr, v_scr, oi_scr, dec_scr, *, lt, slot, has_state):
    i = pl.program_id(1)
    n_i = pl.num_programs(1)
    n_chunks = lt // CHUNK
    n_groups = lt // GROUP_ROWS
    chunks_per_group = GROUP_ROWS // CHUNK

    @pl.when(i == 0)
    def _():
        for d, s0_ref in enumerate((s0f_ref, s0b_ref)):
            for h in range(HA_HEADS):
                if has_state:
                    st_scr[d, h] = s0_ref[0, h].T
                else:
                    st_scr[d, h] = jnp.zeros((HA_DV, HA_DK), F32)

    lg = lbl_ref[...]
    n_slots = lg.shape[0]
    mx = lg[0]
    for s in range(1, n_slots):
        mx = jnp.maximum(mx, lg[s])
    ex = [jnp.exp(lg[s] - mx) for s in range(n_slots)]
    den = ex[0]
    for s in range(1, n_slots):
        den = den + ex[s]
    num = ex[0]
    for s in range(1, slot + 1):
        num = num + ex[s]
    lb_all = num / den

    dirs = ((qf_ref, vf_ref, ff_ref), (qb_ref, vb_ref, fb_ref))
    for d, (q_ref, v_ref, f_ref) in enumerate(dirs):

        def unit(u, carry, d=d, q_ref=q_ref, v_ref=v_ref, f_ref=f_ref):
            e_mat = e_ref[d]
            lv = lv_ref[d]
            h = u // n_groups
            grp = u % n_groups
            r0 = pl.multiple_of(grp * GROUP_ROWS, GROUP_ROWS)
            c0 = pl.multiple_of(h * HA_DK, HA_DK)
            rows = pl.ds(r0, GROUP_ROWS)
            cols = pl.ds(c0, HA_DK)
            q = _silu(q_ref[0, rows, cols])
            v = v_ref[0, rows, cols].astype(BF16)
            lb = lb_all[d:d + 1]
            lb_h = jnp.zeros((1, HA_DK), F32)
            for hh in range(HA_HEADS):
                lb_h = jnp.where(h == hh, lb[:, hh * HA_DK:(hh + 1) * HA_DK], lb_h)
            f = lb_h + (1.0 - lb_h) * _sigmoid(f_ref[0, rows, cols])
            g = jnp.log(f)
            k = 1.0 - f
            g_hi = g.astype(BF16)
            g_lo = (g - g_hi.astype(F32)).astype(BF16)
            a = _dot(e_mat, g_hi) + _dot(e_mat, g_lo)
            x = jnp.exp(a)
            sc = jnp.where(lv == N_LEVELS, _dot_nt(q.astype(BF16), k.astype(BF16)), 0.0)
            for lvl in range(N_LEVELS):
                xl = x[lvl * GROUP_ROWS:(lvl + 1) * GROUP_ROWS]
                p = _dot_nt((q * xl).astype(BF16), (k * xl).astype(BF16))
                sc = jnp.where(lv == lvl, p, sc)
            oi_scr[d, rows, cols] = _dot(sc.astype(BF16), v)
            v_scr[d, rows, cols] = v
            bcum = a[N_LEVELS * GROUP_ROWS:]
            qe_scr[d, rows, cols] = (q * x[N_LEVELS * GROUP_ROWS:]).astype(BF16)
            for cc in range(chunks_per_group):
                edge = cc * CHUNK + (CHUNK - 1 if d == 0 else 0)
                tot = bcum[edge:edge + 1]
                b_c = bcum[cc * CHUNK:(cc + 1) * CHUNK]
                k_c = k[cc * CHUNK:(cc + 1) * CHUNK]
                kd = (k_c * jnp.exp(tot - b_c)).astype(BF16)
                kd_scr[d, pl.ds(r0 + cc * CHUNK, CHUNK), cols] = kd
                dec_scr[d, grp * chunks_per_group + cc, :, cols] = jnp.exp(tot)
            return carry

        lax.fori_loop(0, HA_HEADS * n_groups, unit, 0)

    out_refs = (of_ref, ob_ref)

    def chunk_step(c, carry):
        for d in range(2):
            cc = c if d == 0 else n_chunks - 1 - c
            r0 = pl.multiple_of(cc * CHUNK, CHUNK)
            rows = pl.ds(r0, CHUNK)
            for h in range(HA_HEADS):
                cols = slice(h * HA_DK, (h + 1) * HA_DK)
                st = st_scr[d, h]
                o_state = _dot_nt(qe_scr[d, rows, cols], st.astype(BF16))
                out_refs[d][0, rows, cols] = oi_scr[d, rows, cols] + o_state
                upd = _dot_tn(v_scr[d, rows, cols], kd_scr[d, rows, cols])
                st_scr[d, h] = st * dec_scr[d, cc, :, cols] + upd
        return carry

    lax.fori_loop(0, n_chunks, chunk_step, 0)

    @pl.when(i == n_i - 1)
    def _():
        for d, s_ref in enumerate((sf_ref, sb_ref)):
            for h in range(HA_HEADS):
                s_ref[0, h] = st_scr[d, h].T


def _hgrn(p_h, lb_logits, s0_f, s0_b, slot):
    b, l, _ = p_h.shape
    lt = TOK_TILE
    n_t = l // lt
    has_state = s0_f is not None
    if not has_state:
        s0_f = jnp.zeros((1, HA_HEADS, HA_DK, HA_DV), F32)
        s0_b = s0_f
    e_np, lv_np = _hgrn_tables()
    e_mat = jnp.asarray(e_np, BF16)
    lv = jnp.asarray(lv_np, jnp.int32)
    w = HA_W

    def sec(idx, rev):
        if rev:
            return pl.BlockSpec((1, lt, w), lambda bi, i: (bi, n_t - 1 - i, idx))
        return pl.BlockSpec((1, lt, w), lambda bi, i: (bi, i, idx))

    state_map = (lambda bi, i: (bi, 0, 0, 0)) if has_state else (lambda bi, i: (0, 0, 0, 0))
    in_specs = [
        sec(0, False), sec(3, False), sec(1, False),
        sec(0, True), sec(3, True), sec(2, True),
        pl.BlockSpec(lb_logits.shape, lambda bi, i: (0, 0, 0)),
        pl.BlockSpec(e_mat.shape, lambda bi, i: (0, 0, 0)),
        pl.BlockSpec(lv.shape, lambda bi, i: (0, 0, 0)),
        pl.BlockSpec((1, HA_HEADS, HA_DK, HA_DV), state_map),
        pl.BlockSpec((1, HA_HEADS, HA_DK, HA_DV), state_map),
    ]
    out_shape = [
        jax.ShapeDtypeStruct((b, l, w), F32),
        jax.ShapeDtypeStruct((b, l, w), F32),
        jax.ShapeDtypeStruct((b, HA_HEADS, HA_DK, HA_DV), F32),
        jax.ShapeDtypeStruct((b, HA_HEADS, HA_DK, HA_DV), F32),
    ]
    out_specs = [
        pl.BlockSpec((1, lt, w), lambda bi, i: (bi, i, 0)),
        pl.BlockSpec((1, lt, w), lambda bi, i: (bi, n_t - 1 - i, 0)),
        pl.BlockSpec((1, HA_HEADS, HA_DK, HA_DV), lambda bi, i: (bi, 0, 0, 0)),
        pl.BlockSpec((1, HA_HEADS, HA_DK, HA_DV), lambda bi, i: (bi, 0, 0, 0)),
    ]
    scratch = [
        pltpu.VMEM((2, HA_HEADS, HA_DV, HA_DK), F32),
        pltpu.VMEM((2, lt, w), BF16),
        pltpu.VMEM((2, lt, w), BF16),
        pltpu.VMEM((2, lt, w), BF16),
        pltpu.VMEM((2, lt, w), F32),
        pltpu.VMEM((2, lt // CHUNK, 1, w), F32),
    ]
    return pl.pallas_call(
        functools.partial(_hgrn_kernel, lt=lt, slot=slot, has_state=has_state),
        out_shape=out_shape,
        grid=(b, n_t),
        in_specs=in_specs,
        out_specs=out_specs,
        scratch_shapes=scratch,
        compiler_params=_cparams(("parallel", "arbitrary")),
        name="hgrn2_scan",
    )(p_h, p_h, p_h, p_h, p_h, p_h, lb_logits, e_mat, lv, s0_f, s0_b)


def _mla_proj_kernel(*refs, rope):
    if rope:
        pm_ref, cos_ref, sin_ref, qg_ref, kvg_ref, wq1_ref, wq2_ref, wk_ref, wv_ref = refs[:9]
        q_ref, k_ref, v_ref, ckv_ref = refs[9:]
    else:
        pm_ref, qg_ref, kvg_ref, wq1_ref, wk_ref, wv_ref = refs[:6]
        q_ref, k_ref, v_ref, ckv_ref = refs[6:]
    scale = (MLA_DN + MLA_DR) ** -0.5
    pm = pm_ref[0]
    cq = pm[:, :MLA_Q_RANK]
    cq = cq * lax.rsqrt(jnp.mean(cq * cq, axis=-1, keepdims=True) + RMS_EPS) * qg_ref[...]
    cq = cq.astype(BF16)
    ckv = pm[:, MLA_Q_RANK:MLA_Q_RANK + MLA_KV_RANK]
    ckv = ckv * lax.rsqrt(jnp.mean(ckv * ckv, axis=-1, keepdims=True) + RMS_EPS) * kvg_ref[...]
    ckv_ref[0] = ckv
    ckv = ckv.astype(BF16)
    kr0 = MLA_Q_RANK + MLA_KV_RANK
    kr = pm[:, kr0:kr0 + LANES]
    qa = _dot(cq, wq1_ref[...])
    kn = _dot(ckv, wk_ref[...])
    v_ref[0] = _dot(ckv, wv_ref[...]).astype(BF16)
    if rope:
        cos = cos_ref[...]
        sin = sin_ref[...]
        qb = _dot(cq, wq2_ref[...])
        kr = kr * cos + pm[:, kr0 + LANES:kr0 + 2 * LANES] * sin
    for h in range(MLA_HEADS):
        sl = slice(h * HEAD_PAD, (h + 1) * HEAD_PAD)
        qh = qa[:, sl]
        if rope:
            qh = qh * cos + qb[:, sl] * sin
        q_ref[0, :, sl] = (qh * scale).astype(BF16)
        k_ref[0, :, sl] = (kn[:, sl] + kr).astype(BF16)


def _mla_proj(p_m, rope_tabs, qg, kvg, wq1, wq2, wk, wv):
    b, l, _ = p_m.shape
    tm = TOK_TILE
    rope = rope_tabs is not None
    full = lambda a: pl.BlockSpec(a.shape, lambda i, j: (0,) * a.ndim)
    hw = MLA_HEADS * HEAD_PAD
    inputs = [p_m]
    in_specs = [pl.BlockSpec((1, tm, MLA_SLAB), lambda i, j: (i, j, 0))]
    if rope:
        inputs += list(rope_tabs)
        in_specs += [pl.BlockSpec((tm, LANES), lambda i, j: (j, 0))] * 2
    ws = [qg, kvg, wq1] + ([wq2] if rope else []) + [wk, wv]
    inputs += ws
    in_specs += [full(a) for a in ws]
    out_shape = [jax.ShapeDtypeStruct((b, l, hw), BF16)] * 3 + [jax.ShapeDtypeStruct((b, l, MLA_KV_RANK), F32)]
    out_specs = [pl.BlockSpec((1, tm, hw), lambda i, j: (i, j, 0))] * 3 + [
        pl.BlockSpec((1, tm, MLA_KV_RANK), lambda i, j: (i, j, 0))]
    return pl.pallas_call(
        functools.partial(_mla_proj_kernel, rope=rope),
        out_shape=out_shape,
        grid=(b, l // tm),
        in_specs=in_specs,
        out_specs=out_specs,
        compiler_params=_cparams(("parallel", "parallel")),
        name="mla_proj",
    )(*inputs)


def _mla_ctx_kernel(ckv_ref, kr_ref, place_ref, wk_ref, wv_ref, k_ref, v_ref):
    ckv = ckv_ref[0].astype(BF16)
    kr = _dot(kr_ref[0].astype(BF16), place_ref[...])
    kn = _dot(ckv, wk_ref[...])
    v_ref[0] = _dot(ckv, wv_ref[...]).astype(BF16)
    for h in range(MLA_HEADS):
        sl = slice(h * HEAD_PAD, (h + 1) * HEAD_PAD)
        k_ref[0, :, sl] = (kn[:, sl] + kr).astype(BF16)


def _mla_ctx(ctx_ckv, ctx_kr, wk, wv):
    b, lc, _ = ctx_ckv.shape
    hw = MLA_HEADS * HEAD_PAD
    place = np.zeros((MLA_DR, LANES), np.float32)
    place[np.arange(MLA_DR), MLA_DN + np.arange(MLA_DR)] = 1.0
    place = jnp.asarray(place, BF16)
    full = lambda a: pl.BlockSpec(a.shape, lambda i: (0,) * a.ndim)
    return pl.pallas_call(
        _mla_ctx_kernel,
        out_shape=[jax.ShapeDtypeStruct((b, lc, hw), BF16)] * 2,
        grid=(b,),
        in_specs=[pl.BlockSpec((1, lc, MLA_KV_RANK), lambda i: (i, 0, 0)),
                  pl.BlockSpec((1, lc, MLA_DR), lambda i: (i, 0, 0)),
                  full(place), full(wk), full(wv)],
        out_specs=[pl.BlockSpec((1, lc, hw), lambda i: (i, 0, 0))] * 2,
        compiler_params=_cparams(("parallel",)),
        name="mla_ctx_keys",
    )(ctx_ckv, ctx_kr, place, wk, wv)


def _attn_kernel(*refs, has_ctx):
    if has_ctx:
        q_ref, k_ref, v_ref, kc_ref, vc_ref, o_ref = refs
    else:
        q_ref, k_ref, v_ref, o_ref = refs
    out = None
    for hh in range(2):
        sl = slice(hh * HEAD_PAD, (hh + 1) * HEAD_PAD)
        q = q_ref[0, :, sl]
        s = _dot_nt(q, k_ref[0, :, sl])
        m = jnp.max(s, axis=-1, keepdims=True)
        if has_ctx:
            s2 = _dot_nt(q, kc_ref[0, :, sl])
            m = jnp.maximum(m, jnp.max(s2, axis=-1, keepdims=True))
        p = jnp.exp(s - m)
        den = jnp.sum(p, axis=-1, keepdims=True)
        acc = _dot(p.astype(BF16), v_ref[0, :, sl])
        if has_ctx:
            p2 = jnp.exp(s2 - m)
            den = den + jnp.sum(p2, axis=-1, keepdims=True)
            acc = acc + _dot(p2.astype(BF16), vc_ref[0, :, sl])
        o = acc / den
        out = o if out is None else out + o
    o_ref[0] = out.astype(BF16)


def _attention(q, k, v, kc, vc):
    b, l, _ = q.shape
    lk = k.shape[1]
    tq = TOK_TILE
    pw = 2 * HEAD_PAD
    has_ctx = kc is not None
    inputs = [q, k, v]
    in_specs = [pl.BlockSpec((1, tq, pw), lambda bi, p, i: (bi, i, p)),
                pl.BlockSpec((1, lk, pw), lambda bi, p, i: (bi, 0, p)),
                pl.BlockSpec((1, lk, pw), lambda bi, p, i: (bi, 0, p))]
    if has_ctx:
        lc = kc.shape[1]
        inputs += [kc, vc]
        in_specs += [pl.BlockSpec((1, lc, pw), lambda bi, p, i: (bi, 0, p))] * 2
    return pl.pallas_call(
        functools.partial(_attn_kernel, has_ctx=has_ctx),
        out_shape=jax.ShapeDtypeStruct((b, l, MLA_HEADS * MLA_DV), BF16),
        grid=(b, MLA_HEADS // 2, l // tq),
        in_specs=in_specs,
        out_specs=pl.BlockSpec((1, tq, LANES), lambda bi, p, i: (bi, i, p)),
        compiler_params=_cparams(("parallel", "parallel", "arbitrary")),
        name="mla_attention",
    )(*inputs)


def _outproj_kernel(*refs, hgrn):
    if hgrn:
        x_ref, mod_ref, of_ref, ob_ref, gl_ref, gn_ref, b_ref, w_ref, lg_ref, lb_ref, o_ref = refs
        o = of_ref[0] + ob_ref[0]
        gate = _silu(gl_ref[0])
        gn = gn_ref[...]
        parts = []
        for h in range(HA_HEADS):
            sl = slice(h * HA_DV, (h + 1) * HA_DV)
            oh = o[:, sl]
            oh = oh * lax.rsqrt(jnp.mean(oh * oh, axis=-1, keepdims=True) + RMS_EPS) * gn
            parts.append((oh * gate[:, sl]).astype(BF16))
        a = jnp.concatenate(parts, axis=-1)
    else:
        x_ref, mod_ref, a_ref, b_ref, w_ref, lg_ref, lb_ref, o_ref = refs
        a = a_ref[0]
    half = a.shape[-1]
    y = _dot(a, w_ref[:half, :]) + _dot(b_ref[0], w_ref[half:, :])
    m = mod_ref[0]
    r = ALPHA * x_ref[0] + m[2:3] * y
    o_ref[0] = _layer_norm(r, lg_ref[...], lb_ref[...])


def _outproj(x, mod, a_inputs, b_in, w, ln_g, ln_b, hgrn):
    b, l, d = x.shape
    tm = TOK_TILE
    per_batch = mod.shape[0] > 1
    mod_map = (lambda i, j: (i, 0, 0)) if per_batch else (lambda i, j: (0, 0, 0))
    row = lambda width: pl.BlockSpec((1, tm, width), lambda i, j: (i, j, 0))
    full = lambda a: pl.BlockSpec(a.shape, lambda i, j: (0,) * a.ndim)
    inputs = [x, mod]
    in_specs = [row(d), pl.BlockSpec((1, 6, d), mod_map)]
    if hgrn:
        o_f, o_b, p_h, g_norm = a_inputs
        inputs += [o_f, o_b, p_h, g_norm]
        in_specs += [row(HA_W), row(HA_W),
                     pl.BlockSpec((1, tm, HA_W), lambda i, j: (i, j, 4)), full(g_norm)]
    else:
        inputs += [a_inputs]
        in_specs += [row(a_inputs.shape[-1])]
    inputs += [b_in, w, ln_g, ln_b]
    in_specs += [row(b_in.shape[-1]), full(w), full(ln_g), full(ln_b)]
    return pl.pallas_call(
        functools.partial(_outproj_kernel, hgrn=hgrn),
        out_shape=jax.ShapeDtypeStruct((b, l, d), F32),
        grid=(b, l // tm),
        in_specs=in_specs,
        out_specs=row(d),
        compiler_params=_cparams(("parallel", "parallel")),
        name="outproj_ln",
    )(*inputs)


def _conv_kernel(pm_ref, pp_ref, pn_ref, scw_ref, cfw_ref, cfb_ref, cfg_ref, cfbeta_ref,
                 ysc_ref, ycf_ref, ext_sc, ext_cf, *, lt):
    i = pl.program_id(1)
    n_i = pl.num_programs(1)
    w = SC_W

    def sc_in(p):
        return p[:, w:2 * w] * p[:, 2 * w:3 * w]

    def cf_in(p):
        return p[:, 3 * w:3 * w + CF_W] * _sigmoid(p[:, 3 * w + CF_W:3 * w + 2 * CF_W])

    pm = pm_ref[0]
    pp = pp_ref[0]
    pn = pn_ref[0]
    has_prev = i > 0
    has_next = i < n_i - 1
    ext_sc[0:HALO, :] = jnp.where(has_prev, sc_in(pp), 0.0)
    ext_sc[HALO:HALO + lt, :] = sc_in(pm)
    ext_sc[HALO + lt:, :] = jnp.where(has_next, sc_in(pn), 0.0)
    ext_cf[0:HALO, :] = jnp.where(has_prev, cf_in(pp), 0.0)
    ext_cf[HALO:HALO + lt, :] = cf_in(pm)
    ext_cf[HALO + lt:, :] = jnp.where(has_next, cf_in(pn), 0.0)

    rb = 32
    for r in range(lt // rb):
        base = HALO + r * rb
        acc = None
        for j in range(SC_K):
            term = scw_ref[j:j + 1, :] * ext_sc[base - SC_K // 2 + j:base - SC_K // 2 + j + rb, :]
            acc = term if acc is None else acc + term
        ysc_ref[0, r * rb:(r + 1) * rb, :] = (pm[r * rb:(r + 1) * rb, 0:w] * acc).astype(BF16)
        acc = None
        for j in range(CF_K):
            term = cfw_ref[j:j + 1, :] * ext_cf[base - CF_K // 2 + j:base - CF_K // 2 + j + rb, :]
            acc = term if acc is None else acc + term
        u = _layer_norm(acc + cfb_ref[...], cfg_ref[...], cfbeta_ref[...])
        ycf_ref[0, r * rb:(r + 1) * rb, :] = _silu(u).astype(BF16)


def _conv_mixers(p1, sc_w, cf_w, cf_b, cf_g, cf_beta):
    b, l, width = p1.shape
    lt = TOK_TILE
    hb = lt // HALO
    n_h = l // HALO
    full = lambda a: pl.BlockSpec(a.shape, lambda bi, i: (0,) * a.ndim)
    return pl.pallas_call(
        functools.partial(_conv_kernel, lt=lt),
        out_shape=[jax.ShapeDtypeStruct((b, l, SC_W), BF16), jax.ShapeDtypeStruct((b, l, CF_W), BF16)],
        grid=(b, l // lt),
        in_specs=[
            pl.BlockSpec((1, lt, width), lambda bi, i: (bi, i, 0)),
            pl.BlockSpec((1, HALO, width), lambda bi, i: (bi, jnp.maximum(i * hb - 1, 0), 0)),
            pl.BlockSpec((1, HALO, width), lambda bi, i: (bi, jnp.minimum((i + 1) * hb, n_h - 1), 0)),
            full(sc_w), full(cf_w), full(cf_b), full(cf_g), full(cf_beta),
        ],
        out_specs=[pl.BlockSpec((1, lt, SC_W), lambda bi, i: (bi, i, 0)),
                   pl.BlockSpec((1, lt, CF_W), lambda bi, i: (bi, i, 0))],
        scratch_shapes=[pltpu.VMEM((lt + 2 * HALO, SC_W), F32), pltpu.VMEM((lt + 2 * HALO, CF_W), F32)],
        compiler_params=_cparams(("parallel", "parallel")),
        name="conv_mixers",
    )(p1, p1, p1, sc_w, cf_w, cf_b, cf_g, cf_beta)


def _route(logits):
    lane = lax.broadcasted_iota(jnp.int32, logits.shape, 1).astype(F32)
    neg = -jnp.inf
    big = float(4 * LANES)
    is_group = (lane >= N_EXPERTS) & (lane < N_EXPERTS + N_GROUPS)
    gl = jnp.where(is_group, logits, neg)
    gmax = jnp.max(gl, axis=-1, keepdims=True)
    gsum = jnp.sum(jnp.exp(gl - gmax), axis=-1, keepdims=True)
    p_g = 1.0 / gsum
    g_sel = jnp.min(jnp.where(gl == gmax, lane, big), axis=-1, keepdims=True) - N_EXPERTS
    in_group = (lane >= g_sel * EXP_PER_GROUP) & (lane < (g_sel + 1) * EXP_PER_GROUP)
    el = jnp.where(in_group, logits, neg)
    v1 = jnp.max(el, axis=-1, keepdims=True)
    i1 = jnp.min(jnp.where(el == v1, lane, big), axis=-1, keepdims=True)
    el2 = jnp.where(lane == i1, neg, el)
    v2 = jnp.max(el2, axis=-1, keepdims=True)
    i2 = jnp.min(jnp.where(el2 == v2, lane, big), axis=-1, keepdims=True)
    e2 = jnp.exp(v2 - v1)
    w1 = p_g / (1.0 + e2)
    w2 = p_g * e2 / (1.0 + e2)
    return jnp.where(lane == i1, w1, 0.0) + jnp.where(lane == i2, w2, 0.0)


def _moe_kernel(x_ref, mod_ref, wr_ref, exp_ref, wg_ref, wu_ref, wd_ref, lg_ref, lb_ref, o_ref,
                h_scr, comb_scr, acc_scr):
    j = pl.program_id(2)
    n_j = pl.num_programs(2)
    m = mod_ref[0]

    @pl.when(j == 0)
    def _():
        h = x_ref[0] * (1.0 + m[4:5]) + m[3:4]
        logits = jnp.dot(h, wr_ref[...], precision=HIGHEST, preferred_element_type=F32)
        comb_scr[...] = _route(logits)
        h_scr[...] = h.astype(BF16)
        acc_scr[...] = jnp.zeros_like(acc_scr)

    h = h_scr[...]
    comb = comb_scr[...]
    c_hi = comb.astype(BF16)
    rem = comb - c_hi.astype(F32)
    c_mid = rem.astype(BF16)
    c_lo = (rem - c_mid.astype(F32)).astype(BF16)
    ex = exp_ref[0]
    combx = _dot(c_hi, ex) + _dot(c_mid, ex) + _dot(c_lo, ex)
    hid = _silu(_dot(h, wg_ref[...])) * _dot(h, wu_ref[...])
    acc_scr[...] += _dot((hid * combx).astype(BF16), wd_ref[...])

    @pl.when(j == n_j - 1)
    def _():
        r = ALPHA * x_ref[0] + m[5:6] * acc_scr[...]
        o_ref[0] = _layer_norm(r, lg_ref[...], lb_ref[...])


def _moe(x, mod, wr, wg, wu, wd, ln_g, ln_b):
    b, l, d = x.shape
    tm = 512 if l % 512 == 0 else TOK_TILE
    n_ch = N_GROUPS
    cw = EXP_PER_GROUP * EXP_FF
    per_batch = mod.shape[0] > 1
    mod_map = (lambda i, t, j: (i, 0, 0)) if per_batch else (lambda i, t, j: (0, 0, 0))
    expand = np.zeros((n_ch, LANES, cw), np.float32)
    for c in range(n_ch):
        for e in range(EXP_PER_GROUP):
            expand[c, c * EXP_PER_GROUP + e, e * EXP_FF:(e + 1) * EXP_FF] = 1.0
    expand = jnp.asarray(expand, BF16)
    full = lambda a: pl.BlockSpec(a.shape, lambda i, t, j: (0,) * a.ndim)
    return pl.pallas_call(
        _moe_kernel,
        out_shape=jax.ShapeDtypeStruct((b, l, d), F32),
        grid=(b, l // tm, n_ch),
        in_specs=[
            pl.BlockSpec((1, tm, d), lambda i, t, j: (i, t, 0)),
            pl.BlockSpec((1, 6, d), mod_map),
            full(wr),
            pl.BlockSpec((1, LANES, cw), lambda i, t, j: (j, 0, 0)),
            pl.BlockSpec((d, cw), lambda i, t, j: (0, j)),
            pl.BlockSpec((d, cw), lambda i, t, j: (0, j)),
            pl.BlockSpec((cw, d), lambda i, t, j: (j, 0)),
            full(ln_g), full(ln_b),
        ],
        out_specs=pl.BlockSpec((1, tm, d), lambda i, t, j: (i, t, 0)),
        scratch_shapes=[pltpu.VMEM((tm, d), BF16), pltpu.VMEM((tm, LANES), F32), pltpu.VMEM((tm, d), F32)],
        compiler_params=_cparams(("parallel", "parallel", "arbitrary")),
        name="hier_moe_ln",
    )(x, mod, wr, expand, wg, wu, wd, ln_g, ln_b)


def _rope_swap_perm():
    idx = np.arange(MLA_DR)
    return idx ^ (MLA_DR // 4)


def _rope_tables(n_tok):
    rows = n_tok // GRID_W
    pos_r = jnp.repeat(jnp.arange(rows, dtype=F32), GRID_W)
    pos_c = (jnp.arange(rows * GRID_W) % GRID_W).astype(F32)
    n_freq = MLA_DR // 4
    inv = ROPE_BASE ** (-jnp.arange(n_freq, dtype=F32) / n_freq)
    ang = jnp.stack([pos_r[:, None] * inv, pos_c[:, None] * inv], axis=1)
    cos, sin = jnp.cos(ang), jnp.sin(ang)
    cos32 = jnp.stack([cos, cos], axis=2).reshape(n_tok, MLA_DR)
    sin32 = jnp.stack([-sin, sin], axis=2).reshape(n_tok, MLA_DR)
    pad_hi = LANES - MLA_DN - MLA_DR
    cos_t = jnp.concatenate([jnp.ones((n_tok, MLA_DN), F32), cos32, jnp.zeros((n_tok, pad_hi), F32)], axis=1)
    sin_t = jnp.concatenate([jnp.zeros((n_tok, MLA_DN), F32), sin32, jnp.zeros((n_tok, pad_hi), F32)], axis=1)
    return cos_t, sin_t


def _ab_weights(w_in, w_uq, w_ukv):
    perm = _rope_swap_perm()
    pad_hi = LANES - MLA_DN - MLA_DR
    w_h = w_in[:, :5 * HA_W]
    cq = w_in[:, 5 * HA_W:5 * HA_W + MLA_Q_RANK]
    ckv = w_in[:, 5 * HA_W + MLA_Q_RANK:5 * HA_W + MLA_Q_RANK + MLA_KV_RANK]
    kr = w_in[:, 5 * HA_W + MLA_Q_RANK + MLA_KV_RANK:]
    d = w_in.shape[0]
    z_lo = jnp.zeros((d, MLA_DN), F32)
    z_hi = jnp.zeros((d, pad_hi), F32)
    w_m = jnp.concatenate([cq, ckv, z_lo, kr, z_hi, z_lo, kr[:, perm], z_hi], axis=1)
    uq = w_uq.reshape(MLA_Q_RANK, MLA_HEADS, MLA_DN + MLA_DR)
    q_nope, q_rope = uq[..., :MLA_DN], uq[..., MLA_DN:]
    zq_hi = jnp.zeros((MLA_Q_RANK, MLA_HEADS, pad_hi), F32)
    wq1 = jnp.concatenate([q_nope, q_rope, zq_hi], axis=-1).reshape(MLA_Q_RANK, -1)
    wq2 = jnp.concatenate([jnp.zeros_like(q_nope), q_rope[..., perm], zq_hi], axis=-1).reshape(MLA_Q_RANK, -1)
    ukv = w_ukv.reshape(MLA_KV_RANK, MLA_HEADS, MLA_DN + MLA_DV)
    k_nope, v = ukv[..., :MLA_DN], ukv[..., MLA_DN:]
    wk = jnp.concatenate([k_nope, jnp.zeros((MLA_KV_RANK, MLA_HEADS, LANES - MLA_DN), F32)], axis=-1)
    wk = wk.reshape(MLA_KV_RANK, -1)
    v_pairs = v.reshape(MLA_KV_RANK, MLA_HEADS // 2, 2, MLA_DV)
    zv = jnp.zeros_like(v_pairs[:, :, 0])
    wv = jnp.stack([jnp.concatenate([v_pairs[:, :, 0], zv], axis=-1),
                    jnp.concatenate([zv, v_pairs[:, :, 1]], axis=-1)], axis=2).reshape(MLA_KV_RANK, -1)
    bf = lambda a: a.astype(BF16)
    return bf(w_h), bf(w_m), bf(wq1), bf(wq2), bf(wk), bf(wv)


def _moe_weights(w_group, w_expert, w_gate, w_up, w_down):
    d = w_group.shape[0]
    wr = jnp.concatenate([w_expert, w_group, jnp.zeros((d, LANES - N_EXPERTS - N_GROUPS), F32)], axis=1)
    wg = jnp.transpose(w_gate, (1, 0, 2)).reshape(d, N_EXPERTS * EXP_FF).astype(BF16)
    wu = jnp.transpose(w_up, (1, 0, 2)).reshape(d, N_EXPERTS * EXP_FF).astype(BF16)
    wd = w_down.reshape(N_EXPERTS * EXP_FF, d).astype(BF16)
    return wr, wg, wu, wd


def kernel(x_prompt, x_sample, state_hgrn_fwd, state_hgrn_bwd, cache_mla_ckv, cache_mla_krope, c, c_ctx, mod_w, mod_b, ln_g, ln_b, ab_w_in, ab_w_out, hgrn_lb_logits, hgrn_norm_g, mla_q_norm_g, mla_w_uq, mla_kv_norm_g, mla_w_ukv, cd_w_in, cd_w_out, sc_conv_w, cf_conv_w, cf_conv_b, cf_ln_g, cf_ln_b, moe_w_group, moe_w_expert, moe_w_gate, moe_w_up, moe_w_down):
    dec_b = x_sample.shape[0]
    d = D_MODEL
    cvec = jnp.concatenate([c, c_ctx[None, :], jnp.zeros((SUBLANES - dec_b - 1, d), F32)], axis=0)
    mods = _mod_vectors(cvec, mod_w, mod_b)
    rope_tabs = _rope_tables(x_sample.shape[1])
    xp, xs = x_prompt, x_sample
    new_sf = new_sb = new_ckv = new_kr = None
    for l in range(DEPTH):
        mod_lat = mods[l, :dec_b].reshape(dec_b, 6, d)
        mod_ctx = mods[l, dec_b:dec_b + 1].reshape(1, 6, d)
        row = lambda a: a.reshape(1, -1)
        if l % 2 == 0:
            e = l // 2
            w_h, w_m, wq1, wq2, wk, wv = _ab_weights(ab_w_in[e], mla_w_uq[e], mla_w_ukv[e])
            w_out = ab_w_out[e].astype(BF16)
            qg, kvg, gn = row(mla_q_norm_g[e]), row(mla_kv_norm_g[e]), row(hgrn_norm_g[e])
            ph_p, pm_p = _inproj(xp, mod_ctx, [w_h, w_m])
            of_p, ob_p, sf, sb = _hgrn(ph_p, hgrn_lb_logits, None, None, e)
            q_p, k_p, v_p, ckv_p = _mla_proj(pm_p, None, qg, kvg, wq1, None, wk, wv)
            om_p = _attention(q_p, k_p, v_p, None, None)
            xp = _outproj(xp, mod_ctx, (of_p, ob_p, ph_p, gn), om_p, w_out, row(ln_g[l, 0]), row(ln_b[l, 0]), True)
            ph_s, pm_s = _inproj(xs, mod_lat, [w_h, w_m])
            of_s, ob_s, _, _ = _hgrn(ph_s, hgrn_lb_logits, state_hgrn_fwd[:, e], state_hgrn_bwd[:, e], e)
            q_s, k_s, v_s, _ = _mla_proj(pm_s, rope_tabs, qg, kvg, wq1, wq2, wk, wv)
            kc, vc = _mla_ctx(cache_mla_ckv[:, e], cache_mla_krope[:, e], wk, wv)
            om_s = _attention(q_s, k_s, v_s, kc, vc)
            xs = _outproj(xs, mod_lat, (of_s, ob_s, ph_s, gn), om_s, w_out, row(ln_g[l, 0]), row(ln_b[l, 0]), True)
            kr0 = MLA_Q_RANK + MLA_KV_RANK + MLA_DN
            new_sf, new_sb, new_ckv = sf, sb, ckv_p
            new_kr = pm_p[:, :, kr0:kr0 + MLA_DR]
        else:
            jx = l // 2
            w1 = cd_w_in[jx].astype(BF16)
            w_out = cd_w_out[jx].astype(BF16)
            cd = (sc_conv_w[jx], cf_conv_w[jx], row(cf_conv_b[jx]), row(cf_ln_g[jx]), row(cf_ln_b[jx]))
            (p1_p,) = _inproj(xp, mod_ctx, [w1])
            ysc_p, ycf_p = _conv_mixers(p1_p, *cd)
            xp = _outproj(xp, mod_ctx, ysc_p, ycf_p, w_out, row(ln_g[l, 0]), row(ln_b[l, 0]), False)
            (p1_s,) = _inproj(xs, mod_lat, [w1])
            ysc_s, ycf_s = _conv_mixers(p1_s, *cd)
            xs = _outproj(xs, mod_lat, ysc_s, ycf_s, w_out, row(ln_g[l, 0]), row(ln_b[l, 0]), False)
        wr, wg, wu, wd = _moe_weights(moe_w_group[l], moe_w_expert[l], moe_w_gate[l], moe_w_up[l], moe_w_down[l])
        xp = _moe(xp, mod_ctx, wr, wg, wu, wd, row(ln_g[l, 1]), row(ln_b[l, 1]))
        xs = _moe(xs, mod_lat, wr, wg, wu, wd, row(ln_g[l, 1]), row(ln_b[l, 1]))
    return (xp, xs, new_sf[:, None], new_sb[:, None], new_ckv[:, None], new_kr[:, None])
```

```python
import functools

import numpy as np
import jax
import jax.numpy as jnp
from jax import lax
from jax.experimental import pallas as pl
from jax.experimental.pallas import tpu as pltpu

F32 = jnp.float32
BF16 = jnp.bfloat16
HIGHEST = lax.Precision.HIGHEST

D_MODEL = 1024
DEPTH = 2
GRID_W = 64
N_EVEN = (DEPTH + 1) // 2
HA_HEADS = 4
HA_DK = 128
HA_DV = 128
HA_W = HA_HEADS * HA_DK
CHUNK = 32
MLA_HEADS = 8
MLA_DN = 64
MLA_DR = 32
MLA_DV = 64
MLA_Q_RANK = 384
MLA_KV_RANK = 256
ROPE_BASE = 10000.0
SC_W = 512
SC_K = 3
CF_W = 512
CF_K = 31
N_GROUPS = 4
EXP_PER_GROUP = 8
N_EXPERTS = N_GROUPS * EXP_PER_GROUP
EXP_FF = 128
ALPHA = (2.0 * DEPTH) ** 0.25
LN_EPS = 1e-5
RMS_EPS = 1e-6

LANES = 128
SUBLANES = 8
VMEM_LIMIT = 56 * 1024 * 1024

HEAD_PAD = LANES
MLA_SLAB = MLA_Q_RANK + MLA_KV_RANK + 2 * LANES
TOK_TILE = 256
GROUP_ROWS = 128
N_LEVELS = 5
SAFE_LOG_DECAY = 60.0
HALO = 16


def _cparams(sem):
    return pltpu.CompilerParams(dimension_semantics=sem, vmem_limit_bytes=VMEM_LIMIT)


def _silu(x):
    return x / (1.0 + jnp.exp(-x))


def _sigmoid(x):
    return 1.0 / (1.0 + jnp.exp(-x))


def _layer_norm(r, g, b):
    mu = jnp.mean(r, axis=-1, keepdims=True)
    d = r - mu
    var = jnp.mean(d * d, axis=-1, keepdims=True)
    return d * lax.rsqrt(var + LN_EPS) * g + b


def _dot(a, b):
    return jnp.dot(a, b, preferred_element_type=F32)


def _dot_nt(a, b):
    return lax.dot_general(a, b, (((1,), (1,)), ((), ())), preferred_element_type=F32)


def _dot_tn(a, b):
    return lax.dot_general(a, b, (((0,), (0,)), ((), ())), preferred_element_type=F32)


def _mod_kernel(c_ref, w_ref, b_ref, o_ref):
    s = _silu(c_ref[...])
    o_ref[0] = jnp.dot(s, w_ref[0], precision=HIGHEST, preferred_element_type=F32) + b_ref[0]


def _mod_vectors(cvec, mod_w, mod_b):
    n_out = mod_w.shape[-1]
    tn = 1536
    return pl.pallas_call(
        _mod_kernel,
        out_shape=jax.ShapeDtypeStruct((DEPTH, SUBLANES, n_out), F32),
        grid=(DEPTH, n_out // tn),
        in_specs=[
            pl.BlockSpec((SUBLANES, D_MODEL), lambda l, j: (0, 0)),
            pl.BlockSpec((1, D_MODEL, tn), lambda l, j: (l, 0, j)),
            pl.BlockSpec((1, 1, tn), lambda l, j: (l, 0, j)),
        ],
        out_specs=pl.BlockSpec((1, SUBLANES, tn), lambda l, j: (l, 0, j)),
        compiler_params=_cparams(("arbitrary", "arbitrary")),
        name="mod_vectors",
    )(cvec, mod_w, mod_b.reshape(DEPTH, 1, n_out))


def _inproj_kernel(*refs, n_w):
    x_ref, mod_ref = refs[0], refs[1]
    w_refs = refs[2:2 + n_w]
    o_refs = refs[2 + n_w:]
    m = mod_ref[0]
    h = (x_ref[0] * (1.0 + m[1:2]) + m[0:1]).astype(BF16)
    for w_ref, o_ref in zip(w_refs, o_refs):
        o_ref[0] = _dot(h, w_ref[...])


def _inproj(x, mod, weights):
    b, l, d = x.shape
    tm = TOK_TILE
    per_batch = mod.shape[0] > 1
    mod_map = (lambda i, j: (i, 0, 0)) if per_batch else (lambda i, j: (0, 0, 0))
    in_specs = [pl.BlockSpec((1, tm, d), lambda i, j: (i, j, 0)),
                pl.BlockSpec((1, 6, d), mod_map)]
    in_specs += [pl.BlockSpec(w.shape, lambda i, j: (0, 0)) for w in weights]
    out_shape = [jax.ShapeDtypeStruct((b, l, w.shape[1]), F32) for w in weights]
    out_specs = [pl.BlockSpec((1, tm, w.shape[1]), lambda i, j: (i, j, 0)) for w in weights]
    return pl.pallas_call(
        functools.partial(_inproj_kernel, n_w=len(weights)),
        out_shape=out_shape,
        grid=(b, l // tm),
        in_specs=in_specs,
        out_specs=out_specs,
        compiler_params=_cparams(("parallel", "parallel")),
        name="modulate_inproj",
    )(x, mod, *weights)


def _hgrn_tables():
    n = GROUP_ROWS
    t = np.arange(n)[:, None]
    j = np.arange(n)[None, :]
    same_chunk = (t // CHUNK) == (j // CHUNK)
    e_f, e_b = [], []
    lv_f = np.full((n, n), -1, np.int32)
    for lvl in range(N_LEVELS):
        m = CHUNK >> (lvl + 1)
        blk0 = (t // (2 * m)) * (2 * m)
        r = blk0 + m - 1
        upper = t > r
        ef = np.where(upper, (j > r) & (j <= t), (j > t) & (j <= r))
        r2 = blk0 + m
        lower = t < r2
        eb = np.where(lower, (j >= t) & (j < r2), (j >= r2) & (j < t))
        e_f.append(ef)
        e_b.append(eb)
        s = np.arange(n)[None, :]
        same_blk = (t // (2 * m)) == (s // (2 * m))
        q_side = (t % (2 * m)) >= m
        k_side = (s % (2 * m)) < m
        lv_f[same_blk & q_side & k_side] = lvl
    lv_f[np.arange(n), np.arange(n)] = N_LEVELS
    e_f = np.concatenate(e_f, axis=0).astype(np.float32)
    e_b = np.concatenate(e_b, axis=0).astype(np.float32)
    e = np.stack([e_f, e_b])
    cum = np.stack([same_chunk & (j <= t), same_chunk & (j >= t)]).astype(np.float32)
    lv = np.stack([lv_f, lv_f.T])
    return e, cum, lv


def _hgrn_kernel(qf_ref, vf_ref, ff_ref, qb_ref, vb_ref, fb_ref, lbl_ref, e_ref, cum_ref, lv_ref, s0f_ref, s0b_ref,
                 of_ref, ob_ref, sf_ref, sb_ref,
                 st_scr, qe_scr, kd_scr, sc_scr, v_scr, oi_scr, dec_scr, *, lt, slot, has_state):
    i = pl.program_id(1)
    n_i = pl.num_programs(1)
    n_chunks = lt // CHUNK
    n_groups = lt // GROUP_ROWS
    chunks_per_group = GROUP_ROWS // CHUNK

    @pl.when(i == 0)
    def _():
        for d, s0_ref in enumerate((s0f_ref, s0b_ref)):
            for h in range(HA_HEADS):
                if has_state:
                    st_scr[d, h] = s0_ref[0, h].T
                else:
                    st_scr[d, h] = jnp.zeros((HA_DV, HA_DK), F32)

    lg = lbl_ref[...]
    n_slots = lg.shape[0]
    mx = lg[0]
    for s in range(1, n_slots):
        mx = jnp.maximum(mx, lg[s])
    ex = [jnp.exp(lg[s] - mx) for s in range(n_slots)]
    den = ex[0]
    for s in range(1, n_slots):
        den = den + ex[s]
    num = ex[0]
    for s in range(1, slot + 1):
        num = num + ex[s]
    lb_all = num / den

    dirs = ((qf_ref, vf_ref, ff_ref), (qb_ref, vb_ref, fb_ref))
    head_cols = [slice(h * HA_DK, (h + 1) * HA_DK) for h in range(HA_HEADS)]

    def gates(d, rows, cols):
        q_ref, _, f_ref = dirs[d]
        q = _silu(q_ref[0, rows, cols])
        lb_h = lb_all[d:d + 1, cols]
        f = lb_h + (1.0 - lb_h) * _sigmoid(f_ref[0, rows, cols])
        g = jnp.log(f)
        g_hi = g.astype(BF16)
        g_lo = (g - g_hi.astype(F32)).astype(BF16)
        return q, 1.0 - f, g_hi, g_lo

    tot_min = jnp.zeros((1, HA_DK), F32)
    for d in range(2):

        def group_step(grp, tot_min, d=d):
            r0 = pl.multiple_of(grp * GROUP_ROWS, GROUP_ROWS)
            rows = pl.ds(r0, GROUP_ROWS)
            cum = cum_ref[d]
            lv = lv_ref[d]
            for cols in head_cols:
                q, k, g_hi, g_lo = gates(d, rows, cols)
                bcum = _dot(cum, g_hi) + _dot(cum, g_lo)
                v_scr[d, rows, cols] = dirs[d][1][0, rows, cols].astype(BF16)
                qe = (q * jnp.exp(bcum)).astype(BF16)
                qe_scr[d, rows, cols] = qe
                p = _dot_nt(qe, (k * jnp.exp(-bcum)).astype(BF16))
                sc_scr[d, rows, cols] = jnp.where(lv >= 0, p, 0.0).astype(BF16)
                for cc in range(chunks_per_group):
                    edge = cc * CHUNK + (CHUNK - 1 if d == 0 else 0)
                    tot = bcum[edge:edge + 1]
                    b_c = bcum[cc * CHUNK:(cc + 1) * CHUNK]
                    k_c = k[cc * CHUNK:(cc + 1) * CHUNK]
                    kd = (k_c * jnp.exp(tot - b_c)).astype(BF16)
                    kd_scr[d, pl.ds(r0 + cc * CHUNK, CHUNK), cols] = kd
                    dec_scr[d, grp * chunks_per_group + cc, :, cols] = jnp.exp(tot)
                    tot_min = jnp.minimum(tot_min, tot)
            return tot_min

        tot_min = lax.fori_loop(0, n_groups, group_step, tot_min)

    def one_factor():
        for d in range(2):
            for grp in range(n_groups):
                rows = slice(grp * GROUP_ROWS, (grp + 1) * GROUP_ROWS)
                for cols in head_cols:
                    oi_scr[d, rows, cols] = _dot(sc_scr[d, rows, cols], v_scr[d, rows, cols])

    def per_level():
        for d in range(2):

            def group_step(grp, carry, d=d):
                rows = pl.ds(pl.multiple_of(grp * GROUP_ROWS, GROUP_ROWS), GROUP_ROWS)
                lv = lv_ref[d]
                for cols in head_cols:
                    q, k, g_hi, g_lo = gates(d, rows, cols)
                    x = jnp.exp(_dot(e_ref[d], g_hi) + _dot(e_ref[d], g_lo))
                    sc = jnp.where(lv == N_LEVELS, _dot_nt(q.astype(BF16), k.astype(BF16)), 0.0)
                    for lvl in range(N_LEVELS):
                        xl = x[lvl * GROUP_ROWS:(lvl + 1) * GROUP_ROWS]
                        p = _dot_nt((q * xl).astype(BF16), (k * xl).astype(BF16))
                        sc = jnp.where(lv == lvl, p, sc)
                    oi_scr[d, rows, cols] = _dot(sc.astype(BF16), v_scr[d, rows, cols])
                return carry

            lax.fori_loop(0, n_groups, group_step, 0)

    lax.cond(jnp.min(tot_min) > -SAFE_LOG_DECAY, one_factor, per_level)

    out_refs = (of_ref, ob_ref)

    def chunk_step(c, carry):
        for d in range(2):
            cc = c if d == 0 else n_chunks - 1 - c
            r0 = pl.multiple_of(cc * CHUNK, CHUNK)
            rows = pl.ds(r0, CHUNK)
            for h in range(HA_HEADS):
                cols = slice(h * HA_DK, (h + 1) * HA_DK)
                st = st_scr[d, h]
                o_state = _dot_nt(qe_scr[d, rows, cols], st.astype(BF16))
                out_refs[d][0, rows, cols] = oi_scr[d, rows, cols] + o_state
                upd = _dot_tn(v_scr[d, rows, cols], kd_scr[d, rows, cols])
                st_scr[d, h] = st * dec_scr[d, cc, :, cols] + upd
        return carry

    lax.fori_loop(0, n_chunks, chunk_step, 0)

    @pl.when(i == n_i - 1)
    def _():
        for d, s_ref in enumerate((sf_ref, sb_ref)):
            for h in range(HA_HEADS):
                s_ref[0, h] = st_scr[d, h].T


def _hgrn(p_h, lb_logits, s0_f, s0_b, slot):
    b, l, _ = p_h.shape
    lt = TOK_TILE
    n_t = l // lt
    has_state = s0_f is not None
    if not has_state:
        s0_f = jnp.zeros((1, HA_HEADS, HA_DK, HA_DV), F32)
        s0_b = s0_f
    e_np, cum_np, lv_np = _hgrn_tables()
    e_mat = jnp.asarray(e_np, BF16)
    cum = jnp.asarray(cum_np, BF16)
    lv = jnp.asarray(lv_np, jnp.int32)
    w = HA_W

    def sec(idx, rev):
        if rev:
            return pl.BlockSpec((1, lt, w), lambda bi, i: (bi, n_t - 1 - i, idx))
        return pl.BlockSpec((1, lt, w), lambda bi, i: (bi, i, idx))

    state_map = (lambda bi, i: (bi, 0, 0, 0)) if has_state else (lambda bi, i: (0, 0, 0, 0))
    in_specs = [
        sec(0, False), sec(3, False), sec(1, False),
        sec(0, True), sec(3, True), sec(2, True),
        pl.BlockSpec(lb_logits.shape, lambda bi, i: (0, 0, 0)),
        pl.BlockSpec(e_mat.shape, lambda bi, i: (0, 0, 0)),
        pl.BlockSpec(cum.shape, lambda bi, i: (0, 0, 0)),
        pl.BlockSpec(lv.shape, lambda bi, i: (0, 0, 0)),
        pl.BlockSpec((1, HA_HEADS, HA_DK, HA_DV), state_map),
        pl.BlockSpec((1, HA_HEADS, HA_DK, HA_DV), state_map),
    ]
    out_shape = [
        jax.ShapeDtypeStruct((b, l, w), F32),
        jax.ShapeDtypeStruct((b, l, w), F32),
        jax.ShapeDtypeStruct((b, HA_HEADS, HA_DK, HA_DV), F32),
        jax.ShapeDtypeStruct((b, HA_HEADS, HA_DK, HA_DV), F32),
    ]
    out_specs = [
        pl.BlockSpec((1, lt, w), lambda bi, i: (bi, i, 0)),
        pl.BlockSpec((1, lt, w), lambda bi, i: (bi, n_t - 1 - i, 0)),
        pl.BlockSpec((1, HA_HEADS, HA_DK, HA_DV), lambda bi, i: (bi, 0, 0, 0)),
        pl.BlockSpec((1, HA_HEADS, HA_DK, HA_DV), lambda bi, i: (bi, 0, 0, 0)),
    ]
    scratch = [
        pltpu.VMEM((2, HA_HEADS, HA_DV, HA_DK), F32),
        pltpu.VMEM((2, lt, w), BF16),
        pltpu.VMEM((2, lt, w), BF16),
        pltpu.VMEM((2, lt, w), BF16),
        pltpu.VMEM((2, lt, w), BF16),
        pltpu.VMEM((2, lt, w), F32),
        pltpu.VMEM((2, lt // CHUNK, 1, w), F32),
    ]
    return pl.pallas_call(
        functools.partial(_hgrn_kernel, lt=lt, slot=slot, has_state=has_state),
        out_shape=out_shape,
        grid=(b, n_t),
        in_specs=in_specs,
        out_specs=out_specs,
        scratch_shapes=scratch,
        compiler_params=_cparams(("parallel", "arbitrary")),
        name="hgrn2_scan",
    )(p_h, p_h, p_h, p_h, p_h, p_h, lb_logits, e_mat, cum, lv, s0_f, s0_b)


def _mla_proj_kernel(*refs, rope):
    if rope:
        pm_ref, cos_ref, sin_ref, qg_ref, kvg_ref, wq1_ref, wq2_ref, wk_ref, wv_ref = refs[:9]
        q_ref, k_ref, v_ref, ckv_ref = refs[9:]
    else:
        pm_ref, qg_ref, kvg_ref, wq1_ref, wk_ref, wv_ref = refs[:6]
        q_ref, k_ref, v_ref, ckv_ref = refs[6:]
    scale = (MLA_DN + MLA_DR) ** -0.5
    pm = pm_ref[0]
    cq = pm[:, :MLA_Q_RANK]
    cq = cq * lax.rsqrt(jnp.mean(cq * cq, axis=-1, keepdims=True) + RMS_EPS) * qg_ref[...]
    cq = cq.astype(BF16)
    ckv = pm[:, MLA_Q_RANK:MLA_Q_RANK + MLA_KV_RANK]
    ckv = ckv * lax.rsqrt(jnp.mean(ckv * ckv, axis=-1, keepdims=True) + RMS_EPS) * kvg_ref[...]
    ckv_ref[0] = ckv
    ckv = ckv.astype(BF16)
    kr0 = MLA_Q_RANK + MLA_KV_RANK
    kr = pm[:, kr0:kr0 + LANES]
    qa = _dot(cq, wq1_ref[...])
    kn = _dot(ckv, wk_ref[...])
    v_ref[0] = _dot(ckv, wv_ref[...]).astype(BF16)
    if rope:
        cos = cos_ref[...]
        sin = sin_ref[...]
        qb = _dot(cq, wq2_ref[...])
        kr = kr * cos + pm[:, kr0 + LANES:kr0 + 2 * LANES] * sin
    for h in range(MLA_HEADS):
        sl = slice(h * HEAD_PAD, (h + 1) * HEAD_PAD)
        qh = qa[:, sl]
        if rope:
            qh = qh * cos + qb[:, sl] * sin
        q_ref[0, :, sl] = (qh * scale).astype(BF16)
        k_ref[0, :, sl] = (kn[:, sl] + kr).astype(BF16)


def _mla_proj(p_m, rope_tabs, qg, kvg, wq1, wq2, wk, wv):
    b, l, _ = p_m.shape
    tm = TOK_TILE
    rope = rope_tabs is not None
    full = lambda a: pl.BlockSpec(a.shape, lambda i, j: (0,) * a.ndim)
    hw = MLA_HEADS * HEAD_PAD
    inputs = [p_m]
    in_specs = [pl.BlockSpec((1, tm, MLA_SLAB), lambda i, j: (i, j, 0))]
    if rope:
        inputs += list(rope_tabs)
        in_specs += [pl.BlockSpec((tm, LANES), lambda i, j: (j, 0))] * 2
    ws = [qg, kvg, wq1] + ([wq2] if rope else []) + [wk, wv]
    inputs += ws
    in_specs += [full(a) for a in ws]
    out_shape = [jax.ShapeDtypeStruct((b, l, hw), BF16)] * 3 + [jax.ShapeDtypeStruct((b, l, MLA_KV_RANK), F32)]
    out_specs = [pl.BlockSpec((1, tm, hw), lambda i, j: (i, j, 0))] * 3 + [
        pl.BlockSpec((1, tm, MLA_KV_RANK), lambda i, j: (i, j, 0))]
    return pl.pallas_call(
        functools.partial(_mla_proj_kernel, rope=rope),
        out_shape=out_shape,
        grid=(b, l // tm),
        in_specs=in_specs,
        out_specs=out_specs,
        compiler_params=_cparams(("parallel", "parallel")),
        name="mla_proj",
    )(*inputs)


def _mla_ctx_kernel(ckv_ref, kr_ref, place_ref, wk_ref, wv_ref, k_ref, v_ref):
    ckv = ckv_ref[0].astype(BF16)
    kr = _dot(kr_ref[0].astype(BF16), place_ref[...])
    kn = _dot(ckv, wk_ref[...])
    v_ref[0] = _dot(ckv, wv_ref[...]).astype(BF16)
    for h in range(MLA_HEADS):
        sl = slice(h * HEAD_PAD, (h + 1) * HEAD_PAD)
        k_ref[0, :, sl] = (kn[:, sl] + kr).astype(BF16)


def _mla_ctx(ctx_ckv, ctx_kr, wk, wv):
    b, lc, _ = ctx_ckv.shape
    hw = MLA_HEADS * HEAD_PAD
    place = np.zeros((MLA_DR, LANES), np.float32)
    place[np.arange(MLA_DR), MLA_DN + np.arange(MLA_DR)] = 1.0
    place = jnp.asarray(place, BF16)
    full = lambda a: pl.BlockSpec(a.shape, lambda i: (0,) * a.ndim)
    return pl.pallas_call(
        _mla_ctx_kernel,
        out_shape=[jax.ShapeDtypeStruct((b, lc, hw), BF16)] * 2,
        grid=(b,),
        in_specs=[pl.BlockSpec((1, lc, MLA_KV_RANK), lambda i: (i, 0, 0)),
                  pl.BlockSpec((1, lc, MLA_DR), lambda i: (i, 0, 0)),
                  full(place), full(wk), full(wv)],
        out_specs=[pl.BlockSpec((1, lc, hw), lambda i: (i, 0, 0))] * 2,
        compiler_params=_cparams(("parallel",)),
        name="mla_ctx_keys",
    )(ctx_ckv, ctx_kr, place, wk, wv)


def _attn_kernel(*refs, has_ctx):
    if has_ctx:
        q_ref, k_ref, v_ref, kc_ref, vc_ref, o_ref = refs
    else:
        q_ref, k_ref, v_ref, o_ref = refs
    out = None
    for hh in range(2):
        sl = slice(hh * HEAD_PAD, (hh + 1) * HEAD_PAD)
        q = q_ref[0, :, sl]
        s = _dot_nt(q, k_ref[0, :, sl])
        m = jnp.max(s, axis=-1, keepdims=True)
        if has_ctx:
            s2 = _dot_nt(q, kc_ref[0, :, sl])
            m = jnp.maximum(m, jnp.max(s2, axis=-1, keepdims=True))
        p = jnp.exp(s - m)
        den = jnp.sum(p, axis=-1, keepdims=True)
        acc = _dot(p.astype(BF16), v_ref[0, :, sl])
        if has_ctx:
            p2 = jnp.exp(s2 - m)
            den = den + jnp.sum(p2, axis=-1, keepdims=True)
            acc = acc + _dot(p2.astype(BF16), vc_ref[0, :, sl])
        o = acc / den
        out = o if out is None else out + o
    o_ref[0] = out.astype(BF16)


def _attention(q, k, v, kc, vc):
    b, l, _ = q.shape
    lk = k.shape[1]
    tq = TOK_TILE
    pw = 2 * HEAD_PAD
    has_ctx = kc is not None
    inputs = [q, k, v]
    in_specs = [pl.BlockSpec((1, tq, pw), lambda bi, p, i: (bi, i, p)),
                pl.BlockSpec((1, lk, pw), lambda bi, p, i: (bi, 0, p)),
                pl.BlockSpec((1, lk, pw), lambda bi, p, i: (bi, 0, p))]
    if has_ctx:
        lc = kc.shape[1]
        inputs += [kc, vc]
        in_specs += [pl.BlockSpec((1, lc, pw), lambda bi, p, i: (bi, 0, p))] * 2
    return pl.pallas_call(
        functools.partial(_attn_kernel, has_ctx=has_ctx),
        out_shape=jax.ShapeDtypeStruct((b, l, MLA_HEADS * MLA_DV), BF16),
        grid=(b, MLA_HEADS // 2, l // tq),
        in_specs=in_specs,
        out_specs=pl.BlockSpec((1, tq, LANES), lambda bi, p, i: (bi, i, p)),
        compiler_params=_cparams(("parallel", "parallel", "arbitrary")),
        name="mla_attention",
    )(*inputs)


def _outproj_kernel(*refs, hgrn):
    if hgrn:
        x_ref, mod_ref, of_ref, ob_ref, gl_ref, gn_ref, b_ref, w_ref, lg_ref, lb_ref, o_ref = refs
        o = of_ref[0] + ob_ref[0]
        gate = _silu(gl_ref[0])
        gn = gn_ref[...]
        parts = []
        for h in range(HA_HEADS):
            sl = slice(h * HA_DV, (h + 1) * HA_DV)
            oh = o[:, sl]
            oh = oh * lax.rsqrt(jnp.mean(oh * oh, axis=-1, keepdims=True) + RMS_EPS) * gn
            parts.append((oh * gate[:, sl]).astype(BF16))
        a = jnp.concatenate(parts, axis=-1)
    else:
        x_ref, mod_ref, a_ref, b_ref, w_ref, lg_ref, lb_ref, o_ref = refs
        a = a_ref[0]
    half = a.shape[-1]
    y = _dot(a, w_ref[:half, :]) + _dot(b_ref[0], w_ref[half:, :])
    m = mod_ref[0]
    r = ALPHA * x_ref[0] + m[2:3] * y
    o_ref[0] = _layer_norm(r, lg_ref[...], lb_ref[...])


def _outproj(x, mod, a_inputs, b_in, w, ln_g, ln_b, hgrn):
    b, l, d = x.shape
    tm = TOK_TILE
    per_batch = mod.shape[0] > 1
    mod_map = (lambda i, j: (i, 0, 0)) if per_batch else (lambda i, j: (0, 0, 0))
    row = lambda width: pl.BlockSpec((1, tm, width), lambda i, j: (i, j, 0))
    full = lambda a: pl.BlockSpec(a.shape, lambda i, j: (0,) * a.ndim)
    inputs = [x, mod]
    in_specs = [row(d), pl.BlockSpec((1, 6, d), mod_map)]
    if hgrn:
        o_f, o_b, p_h, g_norm = a_inputs
        inputs += [o_f, o_b, p_h, g_norm]
        in_specs += [row(HA_W), row(HA_W),
                     pl.BlockSpec((1, tm, HA_W), lambda i, j: (i, j, 4)), full(g_norm)]
    else:
        inputs += [a_inputs]
        in_specs += [row(a_inputs.shape[-1])]
    inputs += [b_in, w, ln_g, ln_b]
    in_specs += [row(b_in.shape[-1]), full(w), full(ln_g), full(ln_b)]
    return pl.pallas_call(
        functools.partial(_outproj_kernel, hgrn=hgrn),
        out_shape=jax.ShapeDtypeStruct((b, l, d), F32),
        grid=(b, l // tm),
        in_specs=in_specs,
        out_specs=row(d),
        compiler_params=_cparams(("parallel", "parallel")),
        name="outproj_ln",
    )(*inputs)


def _conv_kernel(pm_ref, pp_ref, pn_ref, scw_ref, cfw_ref, cfb_ref, cfg_ref, cfbeta_ref,
                 ysc_ref, ycf_ref, ext_sc, ext_cf, *, lt):
    i = pl.program_id(1)
    n_i = pl.num_programs(1)
    w = SC_W

    def sc_in(p):
        return p[:, w:2 * w] * p[:, 2 * w:3 * w]

    def cf_in(p):
        return p[:, 3 * w:3 * w + CF_W] * _sigmoid(p[:, 3 * w + CF_W:3 * w + 2 * CF_W])

    pm = pm_ref[0]
    pp = pp_ref[0]
    pn = pn_ref[0]
    has_prev = i > 0
    has_next = i < n_i - 1
    ext_sc[0:HALO, :] = jnp.where(has_prev, sc_in(pp), 0.0)
    ext_sc[HALO:HALO + lt, :] = sc_in(pm)
    ext_sc[HALO + lt:, :] = jnp.where(has_next, sc_in(pn), 0.0)
    ext_cf[0:HALO, :] = jnp.where(has_prev, cf_in(pp), 0.0)
    ext_cf[HALO:HALO + lt, :] = cf_in(pm)
    ext_cf[HALO + lt:, :] = jnp.where(has_next, cf_in(pn), 0.0)

    rb = 32
    for r in range(lt // rb):
        base = HALO + r * rb
        acc = None
        for j in range(SC_K):
            term = scw_ref[j:j + 1, :] * ext_sc[base - SC_K // 2 + j:base - SC_K // 2 + j + rb, :]
            acc = term if acc is None else acc + term
        ysc_ref[0, r * rb:(r + 1) * rb, :] = (pm[r * rb:(r + 1) * rb, 0:w] * acc).astype(BF16)
        acc = None
        for j in range(CF_K):
            term = cfw_ref[j:j + 1, :] * ext_cf[base - CF_K // 2 + j:base - CF_K // 2 + j + rb, :]
            acc = term if acc is None else acc + term
        u = _layer_norm(acc + cfb_ref[...], cfg_ref[...], cfbeta_ref[...])
        ycf_ref[0, r * rb:(r + 1) * rb, :] = _silu(u).astype(BF16)


def _conv_mixers(p1, sc_w, cf_w, cf_b, cf_g, cf_beta):
    b, l, width = p1.shape
    lt = TOK_TILE
    hb = lt // HALO
    n_h = l // HALO
    full = lambda a: pl.BlockSpec(a.shape, lambda bi, i: (0,) * a.ndim)
    return pl.pallas_call(
        functools.partial(_conv_kernel, lt=lt),
        out_shape=[jax.ShapeDtypeStruct((b, l, SC_W), BF16), jax.ShapeDtypeStruct((b, l, CF_W), BF16)],
        grid=(b, l // lt),
        in_specs=[
            pl.BlockSpec((1, lt, width), lambda bi, i: (bi, i, 0)),
            pl.BlockSpec((1, HALO, width), lambda bi, i: (bi, jnp.maximum(i * hb - 1, 0), 0)),
            pl.BlockSpec((1, HALO, width), lambda bi, i: (bi, jnp.minimum((i + 1) * hb, n_h - 1), 0)),
            full(sc_w), full(cf_w), full(cf_b), full(cf_g), full(cf_beta),
        ],
        out_specs=[pl.BlockSpec((1, lt, SC_W), lambda bi, i: (bi, i, 0)),
                   pl.BlockSpec((1, lt, CF_W), lambda bi, i: (bi, i, 0))],
        scratch_shapes=[pltpu.VMEM((lt + 2 * HALO, SC_W), F32), pltpu.VMEM((lt + 2 * HALO, CF_W), F32)],
        compiler_params=_cparams(("parallel", "parallel")),
        name="conv_mixers",
    )(p1, p1, p1, sc_w, cf_w, cf_b, cf_g, cf_beta)


def _route(logits):
    lane = lax.broadcasted_iota(jnp.int32, logits.shape, 1).astype(F32)
    neg = -jnp.inf
    big = float(4 * LANES)
    is_group = (lane >= N_EXPERTS) & (lane < N_EXPERTS + N_GROUPS)
    gl = jnp.where(is_group, logits, neg)
    gmax = jnp.max(gl, axis=-1, keepdims=True)
    gsum = jnp.sum(jnp.exp(gl - gmax), axis=-1, keepdims=True)
    p_g = 1.0 / gsum
    g_sel = jnp.min(jnp.where(gl == gmax, lane, big), axis=-1, keepdims=True) - N_EXPERTS
    in_group = (lane >= g_sel * EXP_PER_GROUP) & (lane < (g_sel + 1) * EXP_PER_GROUP)
    el = jnp.where(in_group, logits, neg)
    v1 = jnp.max(el, axis=-1, keepdims=True)
    i1 = jnp.min(jnp.where(el == v1, lane, big), axis=-1, keepdims=True)
    el2 = jnp.where(lane == i1, neg, el)
    v2 = jnp.max(el2, axis=-1, keepdims=True)
    i2 = jnp.min(jnp.where(el2 == v2, lane, big), axis=-1, keepdims=True)
    e2 = jnp.exp(v2 - v1)
    w1 = p_g / (1.0 + e2)
    w2 = p_g * e2 / (1.0 + e2)
    return jnp.where(lane == i1, w1, 0.0) + jnp.where(lane == i2, w2, 0.0)


def _moe_kernel(x_ref, mod_ref, wr_ref, exp_ref, wg_ref, wu_ref, wd_ref, lg_ref, lb_ref, o_ref,
                h_scr, comb_scr, acc_scr):
    j = pl.program_id(2)
    n_j = pl.num_programs(2)
    m = mod_ref[0]

    @pl.when(j == 0)
    def _():
        h = x_ref[0] * (1.0 + m[4:5]) + m[3:4]
        logits = jnp.dot(h, wr_ref[...], precision=HIGHEST, preferred_element_type=F32)
        comb_scr[...] = _route(logits)
        h_scr[...] = h.astype(BF16)
        acc_scr[...] = jnp.zeros_like(acc_scr)

    h = h_scr[...]
    comb = comb_scr[...]
    c_hi = comb.astype(BF16)
    rem = comb - c_hi.astype(F32)
    c_mid = rem.astype(BF16)
    c_lo = (rem - c_mid.astype(F32)).astype(BF16)
    ex = exp_ref[0]
    combx = _dot(c_hi, ex) + _dot(c_mid, ex) + _dot(c_lo, ex)
    hid = _silu(_dot(h, wg_ref[...])) * _dot(h, wu_ref[...])
    acc_scr[...] += _dot((hid * combx).astype(BF16), wd_ref[...])

    @pl.when(j == n_j - 1)
    def _():
        r = ALPHA * x_ref[0] + m[5:6] * acc_scr[...]
        o_ref[0] = _layer_norm(r, lg_ref[...], lb_ref[...])


def _moe(x, mod, wr, wg, wu, wd, ln_g, ln_b):
    b, l, d = x.shape
    tm = 512 if l % 512 == 0 else TOK_TILE
    n_ch = N_GROUPS
    cw = EXP_PER_GROUP * EXP_FF
    per_batch = mod.shape[0] > 1
    mod_map = (lambda i, t, j: (i, 0, 0)) if per_batch else (lambda i, t, j: (0, 0, 0))
    expand = np.zeros((n_ch, LANES, cw), np.float32)
    for c in range(n_ch):
        for e in range(EXP_PER_GROUP):
            expand[c, c * EXP_PER_GROUP + e, e * EXP_FF:(e + 1) * EXP_FF] = 1.0
    expand = jnp.asarray(expand, BF16)
    full = lambda a: pl.BlockSpec(a.shape, lambda i, t, j: (0,) * a.ndim)
    return pl.pallas_call(
        _moe_kernel,
        out_shape=jax.ShapeDtypeStruct((b, l, d), F32),
        grid=(b, l // tm, n_ch),
        in_specs=[
            pl.BlockSpec((1, tm, d), lambda i, t, j: (i, t, 0)),
            pl.BlockSpec((1, 6, d), mod_map),
            full(wr),
            pl.BlockSpec((1, LANES, cw), lambda i, t, j: (j, 0, 0)),
            pl.BlockSpec((d, cw), lambda i, t, j: (0, j)),
            pl.BlockSpec((d, cw), lambda i, t, j: (0, j)),
            pl.BlockSpec((cw, d), lambda i, t, j: (j, 0)),
            full(ln_g), full(ln_b),
        ],
        out_specs=pl.BlockSpec((1, tm, d), lambda i, t, j: (i, t, 0)),
        scratch_shapes=[pltpu.VMEM((tm, d), BF16), pltpu.VMEM((tm, LANES), F32), pltpu.VMEM((tm, d), F32)],
        compiler_params=_cparams(("parallel", "parallel", "arbitrary")),
        name="hier_moe_ln",
    )(x, mod, wr, expand, wg, wu, wd, ln_g, ln_b)


def _rope_swap_perm():
    idx = np.arange(MLA_DR)
    return idx ^ (MLA_DR // 4)


def _rope_tables(n_tok):
    rows = n_tok // GRID_W
    pos_r = jnp.repeat(jnp.arange(rows, dtype=F32), GRID_W)
    pos_c = (jnp.arange(rows * GRID_W) % GRID_W).astype(F32)
    n_freq = MLA_DR // 4
    inv = ROPE_BASE ** (-jnp.arange(n_freq, dtype=F32) / n_freq)
    ang = jnp.stack([pos_r[:, None] * inv, pos_c[:, None] * inv], axis=1)
    cos, sin = jnp.cos(ang), jnp.sin(ang)
    cos32 = jnp.stack([cos, cos], axis=2).reshape(n_tok, MLA_DR)
    sin32 = jnp.stack([-sin, sin], axis=2).reshape(n_tok, MLA_DR)
    pad_hi = LANES - MLA_DN - MLA_DR
    cos_t = jnp.concatenate([jnp.ones((n_tok, MLA_DN), F32), cos32, jnp.zeros((n_tok, pad_hi), F32)], axis=1)
    sin_t = jnp.concatenate([jnp.zeros((n_tok, MLA_DN), F32), sin32, jnp.zeros((n_tok, pad_hi), F32)], axis=1)
    return cos_t, sin_t


def _ab_weights(w_in, w_uq, w_ukv):
    perm = _rope_swap_perm()
    pad_hi = LANES - MLA_DN - MLA_DR
    w_h = w_in[:, :5 * HA_W]
    cq = w_in[:, 5 * HA_W:5 * HA_W + MLA_Q_RANK]
    ckv = w_in[:, 5 * HA_W + MLA_Q_RANK:5 * HA_W + MLA_Q_RANK + MLA_KV_RANK]
    kr = w_in[:, 5 * HA_W + MLA_Q_RANK + MLA_KV_RANK:]
    d = w_in.shape[0]
    z_lo = jnp.zeros((d, MLA_DN), F32)
    z_hi = jnp.zeros((d, pad_hi), F32)
    w_m = jnp.concatenate([cq, ckv, z_lo, kr, z_hi, z_lo, kr[:, perm], z_hi], axis=1)
    uq = w_uq.reshape(MLA_Q_RANK, MLA_HEADS, MLA_DN + MLA_DR)
    q_nope, q_rope = uq[..., :MLA_DN], uq[..., MLA_DN:]
    zq_hi = jnp.zeros((MLA_Q_RANK, MLA_HEADS, pad_hi), F32)
    wq1 = jnp.concatenate([q_nope, q_rope, zq_hi], axis=-1).reshape(MLA_Q_RANK, -1)
    wq2 = jnp.concatenate([jnp.zeros_like(q_nope), q_rope[..., perm], zq_hi], axis=-1).reshape(MLA_Q_RANK, -1)
    ukv = w_ukv.reshape(MLA_KV_RANK, MLA_HEADS, MLA_DN + MLA_DV)
    k_nope, v = ukv[..., :MLA_DN], ukv[..., MLA_DN:]
    wk = jnp.concatenate([k_nope, jnp.zeros((MLA_KV_RANK, MLA_HEADS, LANES - MLA_DN), F32)], axis=-1)
    wk = wk.reshape(MLA_KV_RANK, -1)
    v_pairs = v.reshape(MLA_KV_RANK, MLA_HEADS // 2, 2, MLA_DV)
    zv = jnp.zeros_like(v_pairs[:, :, 0])
    wv = jnp.stack([jnp.concatenate([v_pairs[:, :, 0], zv], axis=-1),
                    jnp.concatenate([zv, v_pairs[:, :, 1]], axis=-1)], axis=2).reshape(MLA_KV_RANK, -1)
    bf = lambda a: a.astype(BF16)
    return bf(w_h), bf(w_m), bf(wq1), bf(wq2), bf(wk), bf(wv)


def _moe_weights(w_group, w_expert, w_gate, w_up, w_down):
    d = w_group.shape[0]
    wr = jnp.concatenate([w_expert, w_group, jnp.zeros((d, LANES - N_EXPERTS - N_GROUPS), F32)], axis=1)
    wg = jnp.transpose(w_gate, (1, 0, 2)).reshape(d, N_EXPERTS * EXP_FF).astype(BF16)
    wu = jnp.transpose(w_up, (1, 0, 2)).reshape(d, N_EXPERTS * EXP_FF).astype(BF16)
    wd = w_down.reshape(N_EXPERTS * EXP_FF, d).astype(BF16)
    return wr, wg, wu, wd


def kernel(x_prompt, x_sample, state_hgrn_fwd, state_hgrn_bwd, cache_mla_ckv, cache_mla_krope, c, c_ctx, mod_w, mod_b, ln_g, ln_b, ab_w_in, ab_w_out, hgrn_lb_logits, hgrn_norm_g, mla_q_norm_g, mla_w_uq, mla_kv_norm_g, mla_w_ukv, cd_w_in, cd_w_out, sc_conv_w, cf_conv_w, cf_conv_b, cf_ln_g, cf_ln_b, moe_w_group, moe_w_expert, moe_w_gate, moe_w_up, moe_w_down):
    dec_b = x_sample.shape[0]
    d = D_MODEL
    cvec = jnp.concatenate([c, c_ctx[None, :], jnp.zeros((SUBLANES - dec_b - 1, d), F32)], axis=0)
    mods = _mod_vectors(cvec, mod_w, mod_b)
    rope_tabs = _rope_tables(x_sample.shape[1])
    xp, xs = x_prompt, x_sample
    new_sf = new_sb = new_ckv = new_kr = None
    for l in range(DEPTH):
        mod_lat = mods[l, :dec_b].reshape(dec_b, 6, d)
        mod_ctx = mods[l, dec_b:dec_b + 1].reshape(1, 6, d)
        row = lambda a: a.reshape(1, -1)
        if l % 2 == 0:
            e = l // 2
            w_h, w_m, wq1, wq2, wk, wv = _ab_weights(ab_w_in[e], mla_w_uq[e], mla_w_ukv[e])
            w_out = ab_w_out[e].astype(BF16)
            qg, kvg, gn = row(mla_q_norm_g[e]), row(mla_kv_norm_g[e]), row(hgrn_norm_g[e])
            ph_p, pm_p = _inproj(xp, mod_ctx, [w_h, w_m])
            of_p, ob_p, sf, sb = _hgrn(ph_p, hgrn_lb_logits, None, None, e)
            q_p, k_p, v_p, ckv_p = _mla_proj(pm_p, None, qg, kvg, wq1, None, wk, wv)
            om_p = _attention(q_p, k_p, v_p, None, None)
            xp = _outproj(xp, mod_ctx, (of_p, ob_p, ph_p, gn), om_p, w_out, row(ln_g[l, 0]), row(ln_b[l, 0]), True)
            ph_s, pm_s = _inproj(xs, mod_lat, [w_h, w_m])
            of_s, ob_s, _, _ = _hgrn(ph_s, hgrn_lb_logits, state_hgrn_fwd[:, e], state_hgrn_bwd[:, e], e)
            q_s, k_s, v_s, _ = _mla_proj(pm_s, rope_tabs, qg, kvg, wq1, wq2, wk, wv)
            kc, vc = _mla_ctx(cache_mla_ckv[:, e], cache_mla_krope[:, e], wk, wv)
            om_s = _attention(q_s, k_s, v_s, kc, vc)
            xs = _outproj(xs, mod_lat, (of_s, ob_s, ph_s, gn), om_s, w_out, row(ln_g[l, 0]), row(ln_b[l, 0]), True)
            kr0 = MLA_Q_RANK + MLA_KV_RANK + MLA_DN
            new_sf, new_sb, new_ckv = sf, sb, ckv_p
            new_kr = pm_p[:, :, kr0:kr0 + MLA_DR]
        else:
            jx = l // 2
            w1 = cd_w_in[jx].astype(BF16)
            w_out = cd_w_out[jx].astype(BF16)
            cd = (sc_conv_w[jx], cf_conv_w[jx], row(cf_conv_b[jx]), row(cf_ln_g[jx]), row(cf_ln_b[jx]))
            (p1_p,) = _inproj(xp, mod_ctx, [w1])
            ysc_p, ycf_p = _conv_mixers(p1_p, *cd)
            xp = _outproj(xp, mod_ctx, ysc_p, ycf_p, w_out, row(ln_g[l, 0]), row(ln_b[l, 0]), False)
            (p1_s,) = _inproj(xs, mod_lat, [w1])
            ysc_s, ycf_s = _conv_mixers(p1_s, *cd)
            xs = _outproj(xs, mod_lat, ysc_s, ycf_s, w_out, row(ln_g[l, 0]), row(ln_b[l, 0]), False)
        wr, wg, wu, wd = _moe_weights(moe_w_group[l], moe_w_expert[l], moe_w_gate[l], moe_w_up[l], moe_w_down[l])
        xp = _moe(xp, mod_ctx, wr, wg, wu, wd, row(ln_g[l, 1]), row(ln_b[l, 1]))
        xs = _moe(xs, mod_lat, wr, wg, wu, wd, row(ln_g[l, 1]), row(ln_b[l, 1]))
    return (xp, xs, new_sf[:, None], new_sb[:, None], new_ckv[:, None], new_kr[:, None])
```

```python
import functools

import numpy as np
import jax
import jax.numpy as jnp
from jax import lax
from jax.experimental import pallas as pl
from jax.experimental.pallas import tpu as pltpu

F32 = jnp.float32
BF16 = jnp.bfloat16
HIGHEST = lax.Precision.HIGHEST

D_MODEL = 1024
DEPTH = 2
GRID_W = 64
N_EVEN = (DEPTH + 1) // 2
HA_HEADS = 4
HA_DK = 128
HA_DV = 128
HA_W = HA_HEADS * HA_DK
CHUNK = 32
MLA_HEADS = 8
MLA_DN = 64
MLA_DR = 32
MLA_DV = 64
MLA_Q_RANK = 384
MLA_KV_RANK = 256
ROPE_BASE = 10000.0
SC_W = 512
SC_K = 3
CF_W = 512
CF_K = 31
N_GROUPS = 4
EXP_PER_GROUP = 8
N_EXPERTS = N_GROUPS * EXP_PER_GROUP
EXP_FF = 128
ALPHA = (2.0 * DEPTH) ** 0.25
LN_EPS = 1e-5
RMS_EPS = 1e-6

LANES = 128
SUBLANES = 8
VMEM_LIMIT = 56 * 1024 * 1024

HEAD_PAD = LANES
MLA_SLAB = MLA_Q_RANK + MLA_KV_RANK + 2 * LANES
TOK_TILE = 256
GROUP_ROWS = 128
N_LEVELS = 5
SAFE_LOG_DECAY = 60.0
MOE_TILE = 512
MOE_BLOCK = 128
HALO = 16


def _cparams(sem):
    return pltpu.CompilerParams(dimension_semantics=sem, vmem_limit_bytes=VMEM_LIMIT)


def _silu(x):
    return x / (1.0 + jnp.exp(-x))


def _sigmoid(x):
    return 1.0 / (1.0 + jnp.exp(-x))


def _layer_norm(r, g, b):
    mu = jnp.mean(r, axis=-1, keepdims=True)
    d = r - mu
    var = jnp.mean(d * d, axis=-1, keepdims=True)
    return d * lax.rsqrt(var + LN_EPS) * g + b


def _dot(a, b):
    return jnp.dot(a, b, preferred_element_type=F32)


def _dot_nt(a, b):
    return lax.dot_general(a, b, (((1,), (1,)), ((), ())), preferred_element_type=F32)


def _dot_tn(a, b):
    return lax.dot_general(a, b, (((0,), (0,)), ((), ())), preferred_element_type=F32)


def _mod_kernel(c_ref, w_ref, b_ref, o_ref):
    s = _silu(c_ref[...])
    o_ref[0] = jnp.dot(s, w_ref[0], precision=HIGHEST, preferred_element_type=F32) + b_ref[0]


def _mod_vectors(cvec, mod_w, mod_b):
    n_out = mod_w.shape[-1]
    tn = 1536
    return pl.pallas_call(
        _mod_kernel,
        out_shape=jax.ShapeDtypeStruct((DEPTH, SUBLANES, n_out), F32),
        grid=(DEPTH, n_out // tn),
        in_specs=[
            pl.BlockSpec((SUBLANES, D_MODEL), lambda l, j: (0, 0)),
            pl.BlockSpec((1, D_MODEL, tn), lambda l, j: (l, 0, j)),
            pl.BlockSpec((1, 1, tn), lambda l, j: (l, 0, j)),
        ],
        out_specs=pl.BlockSpec((1, SUBLANES, tn), lambda l, j: (l, 0, j)),
        compiler_params=_cparams(("arbitrary", "arbitrary")),
        name="mod_vectors",
    )(cvec, mod_w, mod_b.reshape(DEPTH, 1, n_out))


def _inproj_kernel(*refs, n_w):
    x_ref, mod_ref = refs[0], refs[1]
    w_refs = refs[2:2 + n_w]
    o_refs = refs[2 + n_w:]
    m = mod_ref[0]
    h = (x_ref[0] * (1.0 + m[1:2]) + m[0:1]).astype(BF16)
    for w_ref, o_ref in zip(w_refs, o_refs):
        o_ref[0] = _dot(h, w_ref[...])


def _inproj(x, mod, weights):
    b, l, d = x.shape
    tm = TOK_TILE
    per_batch = mod.shape[0] > 1
    mod_map = (lambda i, j: (i, 0, 0)) if per_batch else (lambda i, j: (0, 0, 0))
    in_specs = [pl.BlockSpec((1, tm, d), lambda i, j: (i, j, 0)),
                pl.BlockSpec((1, 6, d), mod_map)]
    in_specs += [pl.BlockSpec(w.shape, lambda i, j: (0, 0)) for w in weights]
    out_shape = [jax.ShapeDtypeStruct((b, l, w.shape[1]), F32) for w in weights]
    out_specs = [pl.BlockSpec((1, tm, w.shape[1]), lambda i, j: (i, j, 0)) for w in weights]
    return pl.pallas_call(
        functools.partial(_inproj_kernel, n_w=len(weights)),
        out_shape=out_shape,
        grid=(b, l // tm),
        in_specs=in_specs,
        out_specs=out_specs,
        compiler_params=_cparams(("parallel", "parallel")),
        name="modulate_inproj",
    )(x, mod, *weights)


def _hgrn_tables():
    n = GROUP_ROWS
    t = np.arange(n)[:, None]
    j = np.arange(n)[None, :]
    same_chunk = (t // CHUNK) == (j // CHUNK)
    e_f, e_b = [], []
    lv_f = np.full((n, n), -1, np.int32)
    for lvl in range(N_LEVELS):
        m = CHUNK >> (lvl + 1)
        blk0 = (t // (2 * m)) * (2 * m)
        r = blk0 + m - 1
        upper = t > r
        ef = np.where(upper, (j > r) & (j <= t), (j > t) & (j <= r))
        r2 = blk0 + m
        lower = t < r2
        eb = np.where(lower, (j >= t) & (j < r2), (j >= r2) & (j < t))
        e_f.append(ef)
        e_b.append(eb)
        s = np.arange(n)[None, :]
        same_blk = (t // (2 * m)) == (s // (2 * m))
        q_side = (t % (2 * m)) >= m
        k_side = (s % (2 * m)) < m
        lv_f[same_blk & q_side & k_side] = lvl
    lv_f[np.arange(n), np.arange(n)] = N_LEVELS
    e_f = np.concatenate(e_f, axis=0).astype(np.float32)
    e_b = np.concatenate(e_b, axis=0).astype(np.float32)
    e = np.stack([e_f, e_b])
    cum = np.stack([same_chunk & (j <= t), same_chunk & (j >= t)]).astype(np.float32)
    lv = np.stack([lv_f, lv_f.T])
    return e, cum, lv


def _hgrn_kernel(qf_ref, vf_ref, ff_ref, qb_ref, vb_ref, fb_ref, lbl_ref, e_ref, cum_ref, lv_ref, s0f_ref, s0b_ref,
                 of_ref, ob_ref, sf_ref, sb_ref,
                 st_scr, qe_scr, kd_scr, sc_scr, v_scr, oi_scr, dec_scr, *, lt, slot, has_state):
    i = pl.program_id(1)
    n_i = pl.num_programs(1)
    n_chunks = lt // CHUNK
    n_groups = lt // GROUP_ROWS
    chunks_per_group = GROUP_ROWS // CHUNK

    @pl.when(i == 0)
    def _():
        for d, s0_ref in enumerate((s0f_ref, s0b_ref)):
            for h in range(HA_HEADS):
                if has_state:
                    st_scr[d, h] = s0_ref[0, h].T
                else:
                    st_scr[d, h] = jnp.zeros((HA_DV, HA_DK), F32)

    lg = lbl_ref[...]
    n_slots = lg.shape[0]
    mx = lg[0]
    for s in range(1, n_slots):
        mx = jnp.maximum(mx, lg[s])
    ex = [jnp.exp(lg[s] - mx) for s in range(n_slots)]
    den = ex[0]
    for s in range(1, n_slots):
        den = den + ex[s]
    num = ex[0]
    for s in range(1, slot + 1):
        num = num + ex[s]
    lb_all = num / den

    dirs = ((qf_ref, vf_ref, ff_ref), (qb_ref, vb_ref, fb_ref))
    head_cols = [slice(h * HA_DK, (h + 1) * HA_DK) for h in range(HA_HEADS)]

    def gates(d, rows, cols):
        q_ref, _, f_ref = dirs[d]
        q = _silu(q_ref[0, rows, cols])
        lb_h = lb_all[d:d + 1, cols]
        f = lb_h + (1.0 - lb_h) * _sigmoid(f_ref[0, rows, cols])
        g = jnp.log(f)
        g_hi = g.astype(BF16)
        g_lo = (g - g_hi.astype(F32)).astype(BF16)
        return q, 1.0 - f, g_hi, g_lo

    tot_min = jnp.zeros((1, HA_DK), F32)
    for d in range(2):

        def group_step(grp, tot_min, d=d):
            r0 = pl.multiple_of(grp * GROUP_ROWS, GROUP_ROWS)
            rows = pl.ds(r0, GROUP_ROWS)
            cum = cum_ref[d]
            lv = lv_ref[d]
            for cols in head_cols:
                q, k, g_hi, g_lo = gates(d, rows, cols)
                bcum = _dot(cum, g_hi) + _dot(cum, g_lo)
                v_scr[d, rows, cols] = dirs[d][1][0, rows, cols].astype(BF16)
                qe = (q * jnp.exp(bcum)).astype(BF16)
                qe_scr[d, rows, cols] = qe
                p = _dot_nt(qe, (k * jnp.exp(-bcum)).astype(BF16))
                sc_scr[d, rows, cols] = jnp.where(lv >= 0, p, 0.0).astype(BF16)
                for cc in range(chunks_per_group):
                    edge = cc * CHUNK + (CHUNK - 1 if d == 0 else 0)
                    tot = bcum[edge:edge + 1]
                    b_c = bcum[cc * CHUNK:(cc + 1) * CHUNK]
                    k_c = k[cc * CHUNK:(cc + 1) * CHUNK]
                    kd = (k_c * jnp.exp(tot - b_c)).astype(BF16)
                    kd_scr[d, pl.ds(r0 + cc * CHUNK, CHUNK), cols] = kd
                    dec_scr[d, grp * chunks_per_group + cc, :, cols] = jnp.exp(tot)
                    tot_min = jnp.minimum(tot_min, tot)
            return tot_min

        tot_min = lax.fori_loop(0, n_groups, group_step, tot_min)

    def one_factor():
        for d in range(2):
            for grp in range(n_groups):
                rows = slice(grp * GROUP_ROWS, (grp + 1) * GROUP_ROWS)
                for cols in head_cols:
                    oi_scr[d, rows, cols] = _dot(sc_scr[d, rows, cols], v_scr[d, rows, cols])

    def per_level():
        for d in range(2):

            def group_step(grp, carry, d=d):
                rows = pl.ds(pl.multiple_of(grp * GROUP_ROWS, GROUP_ROWS), GROUP_ROWS)
                lv = lv_ref[d]
                for cols in head_cols:
                    q, k, g_hi, g_lo = gates(d, rows, cols)
                    x = jnp.exp(_dot(e_ref[d], g_hi) + _dot(e_ref[d], g_lo))
                    sc = jnp.where(lv == N_LEVELS, _dot_nt(q.astype(BF16), k.astype(BF16)), 0.0)
                    for lvl in range(N_LEVELS):
                        xl = x[lvl * GROUP_ROWS:(lvl + 1) * GROUP_ROWS]
                        p = _dot_nt((q * xl).astype(BF16), (k * xl).astype(BF16))
                        sc = jnp.where(lv == lvl, p, sc)
                    oi_scr[d, rows, cols] = _dot(sc.astype(BF16), v_scr[d, rows, cols])
                return carry

            lax.fori_loop(0, n_groups, group_step, 0)

    lax.cond(jnp.min(tot_min) > -SAFE_LOG_DECAY, one_factor, per_level)

    out_refs = (of_ref, ob_ref)

    def chunk_step(c, carry):
        for d in range(2):
            cc = c if d == 0 else n_chunks - 1 - c
            r0 = pl.multiple_of(cc * CHUNK, CHUNK)
            rows = pl.ds(r0, CHUNK)
            for h in range(HA_HEADS):
                cols = slice(h * HA_DK, (h + 1) * HA_DK)
                st = st_scr[d, h]
                o_state = _dot_nt(qe_scr[d, rows, cols], st.astype(BF16))
                out_refs[d][0, rows, cols] = oi_scr[d, rows, cols] + o_state
                upd = _dot_tn(v_scr[d, rows, cols], kd_scr[d, rows, cols])
                st_scr[d, h] = st * dec_scr[d, cc, :, cols] + upd
        return carry

    lax.fori_loop(0, n_chunks, chunk_step, 0)

    @pl.when(i == n_i - 1)
    def _():
        for d, s_ref in enumerate((sf_ref, sb_ref)):
            for h in range(HA_HEADS):
                s_ref[0, h] = st_scr[d, h].T


def _hgrn(p_h, lb_logits, s0_f, s0_b, slot):
    b, l, _ = p_h.shape
    lt = TOK_TILE
    n_t = l // lt
    has_state = s0_f is not None
    if not has_state:
        s0_f = jnp.zeros((1, HA_HEADS, HA_DK, HA_DV), F32)
        s0_b = s0_f
    e_np, cum_np, lv_np = _hgrn_tables()
    e_mat = jnp.asarray(e_np, BF16)
    cum = jnp.asarray(cum_np, BF16)
    lv = jnp.asarray(lv_np, jnp.int32)
    w = HA_W

    def sec(idx, rev):
        if rev:
            return pl.BlockSpec((1, lt, w), lambda bi, i: (bi, n_t - 1 - i, idx))
        return pl.BlockSpec((1, lt, w), lambda bi, i: (bi, i, idx))

    state_map = (lambda bi, i: (bi, 0, 0, 0)) if has_state else (lambda bi, i: (0, 0, 0, 0))
    in_specs = [
        sec(0, False), sec(3, False), sec(1, False),
        sec(0, True), sec(3, True), sec(2, True),
        pl.BlockSpec(lb_logits.shape, lambda bi, i: (0, 0, 0)),
        pl.BlockSpec(e_mat.shape, lambda bi, i: (0, 0, 0)),
        pl.BlockSpec(cum.shape, lambda bi, i: (0, 0, 0)),
        pl.BlockSpec(lv.shape, lambda bi, i: (0, 0, 0)),
        pl.BlockSpec((1, HA_HEADS, HA_DK, HA_DV), state_map),
        pl.BlockSpec((1, HA_HEADS, HA_DK, HA_DV), state_map),
    ]
    out_shape = [
        jax.ShapeDtypeStruct((b, l, w), F32),
        jax.ShapeDtypeStruct((b, l, w), F32),
        jax.ShapeDtypeStruct((b, HA_HEADS, HA_DK, HA_DV), F32),
        jax.ShapeDtypeStruct((b, HA_HEADS, HA_DK, HA_DV), F32),
    ]
    out_specs = [
        pl.BlockSpec((1, lt, w), lambda bi, i: (bi, i, 0)),
        pl.BlockSpec((1, lt, w), lambda bi, i: (bi, n_t - 1 - i, 0)),
        pl.BlockSpec((1, HA_HEADS, HA_DK, HA_DV), lambda bi, i: (bi, 0, 0, 0)),
        pl.BlockSpec((1, HA_HEADS, HA_DK, HA_DV), lambda bi, i: (bi, 0, 0, 0)),
    ]
    scratch = [
        pltpu.VMEM((2, HA_HEADS, HA_DV, HA_DK), F32),
        pltpu.VMEM((2, lt, w), BF16),
        pltpu.VMEM((2, lt, w), BF16),
        pltpu.VMEM((2, lt, w), BF16),
        pltpu.VMEM((2, lt, w), BF16),
        pltpu.VMEM((2, lt, w), F32),
        pltpu.VMEM((2, lt // CHUNK, 1, w), F32),
    ]
    return pl.pallas_call(
        functools.partial(_hgrn_kernel, lt=lt, slot=slot, has_state=has_state),
        out_shape=out_shape,
        grid=(b, n_t),
        in_specs=in_specs,
        out_specs=out_specs,
        scratch_shapes=scratch,
        compiler_params=_cparams(("parallel", "arbitrary")),
        name="hgrn2_scan",
    )(p_h, p_h, p_h, p_h, p_h, p_h, lb_logits, e_mat, cum, lv, s0_f, s0_b)


def _mla_proj_kernel(*refs, rope):
    if rope:
        pm_ref, cos_ref, sin_ref, qg_ref, kvg_ref, wq1_ref, wq2_ref, wk_ref, wv_ref = refs[:9]
        q_ref, k_ref, v_ref, ckv_ref = refs[9:]
    else:
        pm_ref, qg_ref, kvg_ref, wq1_ref, wk_ref, wv_ref = refs[:6]
        q_ref, k_ref, v_ref, ckv_ref = refs[6:]
    scale = (MLA_DN + MLA_DR) ** -0.5
    pm = pm_ref[0]
    cq = pm[:, :MLA_Q_RANK]
    cq = cq * lax.rsqrt(jnp.mean(cq * cq, axis=-1, keepdims=True) + RMS_EPS) * qg_ref[...]
    cq = cq.astype(BF16)
    ckv = pm[:, MLA_Q_RANK:MLA_Q_RANK + MLA_KV_RANK]
    ckv = ckv * lax.rsqrt(jnp.mean(ckv * ckv, axis=-1, keepdims=True) + RMS_EPS) * kvg_ref[...]
    ckv_ref[0] = ckv
    ckv = ckv.astype(BF16)
    kr0 = MLA_Q_RANK + MLA_KV_RANK
    kr = pm[:, kr0:kr0 + LANES]
    qa = _dot(cq, wq1_ref[...])
    kn = _dot(ckv, wk_ref[...])
    v_ref[0] = _dot(ckv, wv_ref[...]).astype(BF16)
    if rope:
        cos = cos_ref[...]
        sin = sin_ref[...]
        qb = _dot(cq, wq2_ref[...])
        kr = kr * cos + pm[:, kr0 + LANES:kr0 + 2 * LANES] * sin
    for h in range(MLA_HEADS):
        sl = slice(h * HEAD_PAD, (h + 1) * HEAD_PAD)
        qh = qa[:, sl]
        if rope:
            qh = qh * cos + qb[:, sl] * sin
        q_ref[0, :, sl] = (qh * scale).astype(BF16)
        k_ref[0, :, sl] = (kn[:, sl] + kr).astype(BF16)


def _mla_proj(p_m, rope_tabs, qg, kvg, wq1, wq2, wk, wv):
    b, l, _ = p_m.shape
    tm = TOK_TILE
    rope = rope_tabs is not None
    full = lambda a: pl.BlockSpec(a.shape, lambda i, j: (0,) * a.ndim)
    hw = MLA_HEADS * HEAD_PAD
    inputs = [p_m]
    in_specs = [pl.BlockSpec((1, tm, MLA_SLAB), lambda i, j: (i, j, 0))]
    if rope:
        inputs += list(rope_tabs)
        in_specs += [pl.BlockSpec((tm, LANES), lambda i, j: (j, 0))] * 2
    ws = [qg, kvg, wq1] + ([wq2] if rope else []) + [wk, wv]
    inputs += ws
    in_specs += [full(a) for a in ws]
    out_shape = [jax.ShapeDtypeStruct((b, l, hw), BF16)] * 3 + [jax.ShapeDtypeStruct((b, l, MLA_KV_RANK), F32)]
    out_specs = [pl.BlockSpec((1, tm, hw), lambda i, j: (i, j, 0))] * 3 + [
        pl.BlockSpec((1, tm, MLA_KV_RANK), lambda i, j: (i, j, 0))]
    return pl.pallas_call(
        functools.partial(_mla_proj_kernel, rope=rope),
        out_shape=out_shape,
        grid=(b, l // tm),
        in_specs=in_specs,
        out_specs=out_specs,
        compiler_params=_cparams(("parallel", "parallel")),
        name="mla_proj",
    )(*inputs)


def _mla_ctx_kernel(ckv_ref, kr_ref, place_ref, wk_ref, wv_ref, k_ref, v_ref):
    ckv = ckv_ref[0].astype(BF16)
    kr = _dot(kr_ref[0].astype(BF16), place_ref[...])
    kn = _dot(ckv, wk_ref[...])
    v_ref[0] = _dot(ckv, wv_ref[...]).astype(BF16)
    for h in range(MLA_HEADS):
        sl = slice(h * HEAD_PAD, (h + 1) * HEAD_PAD)
        k_ref[0, :, sl] = (kn[:, sl] + kr).astype(BF16)


def _mla_ctx(ctx_ckv, ctx_kr, wk, wv):
    b, lc, _ = ctx_ckv.shape
    hw = MLA_HEADS * HEAD_PAD
    place = np.zeros((MLA_DR, LANES), np.float32)
    place[np.arange(MLA_DR), MLA_DN + np.arange(MLA_DR)] = 1.0
    place = jnp.asarray(place, BF16)
    full = lambda a: pl.BlockSpec(a.shape, lambda i: (0,) * a.ndim)
    return pl.pallas_call(
        _mla_ctx_kernel,
        out_shape=[jax.ShapeDtypeStruct((b, lc, hw), BF16)] * 2,
        grid=(b,),
        in_specs=[pl.BlockSpec((1, lc, MLA_KV_RANK), lambda i: (i, 0, 0)),
                  pl.BlockSpec((1, lc, MLA_DR), lambda i: (i, 0, 0)),
                  full(place), full(wk), full(wv)],
        out_specs=[pl.BlockSpec((1, lc, hw), lambda i: (i, 0, 0))] * 2,
        compiler_params=_cparams(("parallel",)),
        name="mla_ctx_keys",
    )(ctx_ckv, ctx_kr, place, wk, wv)


def _attn_kernel(*refs, has_ctx):
    if has_ctx:
        q_ref, k_ref, v_ref, kc_ref, vc_ref, o_ref = refs
    else:
        q_ref, k_ref, v_ref, o_ref = refs
    out = None
    for hh in range(2):
        sl = slice(hh * HEAD_PAD, (hh + 1) * HEAD_PAD)
        q = q_ref[0, :, sl]
        s = _dot_nt(q, k_ref[0, :, sl])
        m = jnp.max(s, axis=-1, keepdims=True)
        if has_ctx:
            s2 = _dot_nt(q, kc_ref[0, :, sl])
            m = jnp.maximum(m, jnp.max(s2, axis=-1, keepdims=True))
        p = jnp.exp(s - m)
        den = jnp.sum(p, axis=-1, keepdims=True)
        acc = _dot(p.astype(BF16), v_ref[0, :, sl])
        if has_ctx:
            p2 = jnp.exp(s2 - m)
            den = den + jnp.sum(p2, axis=-1, keepdims=True)
            acc = acc + _dot(p2.astype(BF16), vc_ref[0, :, sl])
        o = acc / den
        out = o if out is None else out + o
    o_ref[0] = out.astype(BF16)


def _attention(q, k, v, kc, vc):
    b, l, _ = q.shape
    lk = k.shape[1]
    tq = TOK_TILE
    pw = 2 * HEAD_PAD
    has_ctx = kc is not None
    inputs = [q, k, v]
    in_specs = [pl.BlockSpec((1, tq, pw), lambda bi, p, i: (bi, i, p)),
                pl.BlockSpec((1, lk, pw), lambda bi, p, i: (bi, 0, p)),
                pl.BlockSpec((1, lk, pw), lambda bi, p, i: (bi, 0, p))]
    if has_ctx:
        lc = kc.shape[1]
        inputs += [kc, vc]
        in_specs += [pl.BlockSpec((1, lc, pw), lambda bi, p, i: (bi, 0, p))] * 2
    return pl.pallas_call(
        functools.partial(_attn_kernel, has_ctx=has_ctx),
        out_shape=jax.ShapeDtypeStruct((b, l, MLA_HEADS * MLA_DV), BF16),
        grid=(b, MLA_HEADS // 2, l // tq),
        in_specs=in_specs,
        out_specs=pl.BlockSpec((1, tq, LANES), lambda bi, p, i: (bi, i, p)),
        compiler_params=_cparams(("parallel", "parallel", "arbitrary")),
        name="mla_attention",
    )(*inputs)


def _outproj_kernel(*refs, hgrn):
    if hgrn:
        x_ref, mod_ref, of_ref, ob_ref, gl_ref, gn_ref, b_ref, w_ref, lg_ref, lb_ref, o_ref = refs
        o = of_ref[0] + ob_ref[0]
        gate = _silu(gl_ref[0])
        gn = gn_ref[...]
        parts = []
        for h in range(HA_HEADS):
            sl = slice(h * HA_DV, (h + 1) * HA_DV)
            oh = o[:, sl]
            oh = oh * lax.rsqrt(jnp.mean(oh * oh, axis=-1, keepdims=True) + RMS_EPS) * gn
            parts.append((oh * gate[:, sl]).astype(BF16))
        a = jnp.concatenate(parts, axis=-1)
    else:
        x_ref, mod_ref, a_ref, b_ref, w_ref, lg_ref, lb_ref, o_ref = refs
        a = a_ref[0]
    half = a.shape[-1]
    y = _dot(a, w_ref[:half, :]) + _dot(b_ref[0], w_ref[half:, :])
    m = mod_ref[0]
    r = ALPHA * x_ref[0] + m[2:3] * y
    o_ref[0] = _layer_norm(r, lg_ref[...], lb_ref[...])


def _outproj(x, mod, a_inputs, b_in, w, ln_g, ln_b, hgrn):
    b, l, d = x.shape
    tm = TOK_TILE
    per_batch = mod.shape[0] > 1
    mod_map = (lambda i, j: (i, 0, 0)) if per_batch else (lambda i, j: (0, 0, 0))
    row = lambda width: pl.BlockSpec((1, tm, width), lambda i, j: (i, j, 0))
    full = lambda a: pl.BlockSpec(a.shape, lambda i, j: (0,) * a.ndim)
    inputs = [x, mod]
    in_specs = [row(d), pl.BlockSpec((1, 6, d), mod_map)]
    if hgrn:
        o_f, o_b, p_h, g_norm = a_inputs
        inputs += [o_f, o_b, p_h, g_norm]
        in_specs += [row(HA_W), row(HA_W),
                     pl.BlockSpec((1, tm, HA_W), lambda i, j: (i, j, 4)), full(g_norm)]
    else:
        inputs += [a_inputs]
        in_specs += [row(a_inputs.shape[-1])]
    inputs += [b_in, w, ln_g, ln_b]
    in_specs += [row(b_in.shape[-1]), full(w), full(ln_g), full(ln_b)]
    return pl.pallas_call(
        functools.partial(_outproj_kernel, hgrn=hgrn),
        out_shape=jax.ShapeDtypeStruct((b, l, d), F32),
        grid=(b, l // tm),
        in_specs=in_specs,
        out_specs=row(d),
        compiler_params=_cparams(("parallel", "parallel")),
        name="outproj_ln",
    )(*inputs)


def _conv_kernel(pm_ref, pp_ref, pn_ref, scw_ref, cfw_ref, cfb_ref, cfg_ref, cfbeta_ref,
                 ysc_ref, ycf_ref, ext_sc, ext_cf, *, lt):
    i = pl.program_id(1)
    n_i = pl.num_programs(1)
    w = SC_W

    def sc_in(p):
        return p[:, w:2 * w] * p[:, 2 * w:3 * w]

    def cf_in(p):
        return p[:, 3 * w:3 * w + CF_W] * _sigmoid(p[:, 3 * w + CF_W:3 * w + 2 * CF_W])

    pm = pm_ref[0]
    pp = pp_ref[0]
    pn = pn_ref[0]
    has_prev = i > 0
    has_next = i < n_i - 1
    ext_sc[0:HALO, :] = jnp.where(has_prev, sc_in(pp), 0.0)
    ext_sc[HALO:HALO + lt, :] = sc_in(pm)
    ext_sc[HALO + lt:, :] = jnp.where(has_next, sc_in(pn), 0.0)
    ext_cf[0:HALO, :] = jnp.where(has_prev, cf_in(pp), 0.0)
    ext_cf[HALO:HALO + lt, :] = cf_in(pm)
    ext_cf[HALO + lt:, :] = jnp.where(has_next, cf_in(pn), 0.0)

    rb = 32
    for r in range(lt // rb):
        base = HALO + r * rb
        acc = None
        for j in range(SC_K):
            term = scw_ref[j:j + 1, :] * ext_sc[base - SC_K // 2 + j:base - SC_K // 2 + j + rb, :]
            acc = term if acc is None else acc + term
        ysc_ref[0, r * rb:(r + 1) * rb, :] = (pm[r * rb:(r + 1) * rb, 0:w] * acc).astype(BF16)
        acc = None
        for j in range(CF_K):
            term = cfw_ref[j:j + 1, :] * ext_cf[base - CF_K // 2 + j:base - CF_K // 2 + j + rb, :]
            acc = term if acc is None else acc + term
        u = _layer_norm(acc + cfb_ref[...], cfg_ref[...], cfbeta_ref[...])
        ycf_ref[0, r * rb:(r + 1) * rb, :] = _silu(u).astype(BF16)


def _conv_mixers(p1, sc_w, cf_w, cf_b, cf_g, cf_beta):
    b, l, width = p1.shape
    lt = TOK_TILE
    hb = lt // HALO
    n_h = l // HALO
    full = lambda a: pl.BlockSpec(a.shape, lambda bi, i: (0,) * a.ndim)
    return pl.pallas_call(
        functools.partial(_conv_kernel, lt=lt),
        out_shape=[jax.ShapeDtypeStruct((b, l, SC_W), BF16), jax.ShapeDtypeStruct((b, l, CF_W), BF16)],
        grid=(b, l // lt),
        in_specs=[
            pl.BlockSpec((1, lt, width), lambda bi, i: (bi, i, 0)),
            pl.BlockSpec((1, HALO, width), lambda bi, i: (bi, jnp.maximum(i * hb - 1, 0), 0)),
            pl.BlockSpec((1, HALO, width), lambda bi, i: (bi, jnp.minimum((i + 1) * hb, n_h - 1), 0)),
            full(sc_w), full(cf_w), full(cf_b), full(cf_g), full(cf_beta),
        ],
        out_specs=[pl.BlockSpec((1, lt, SC_W), lambda bi, i: (bi, i, 0)),
                   pl.BlockSpec((1, lt, CF_W), lambda bi, i: (bi, i, 0))],
        scratch_shapes=[pltpu.VMEM((lt + 2 * HALO, SC_W), F32), pltpu.VMEM((lt + 2 * HALO, CF_W), F32)],
        compiler_params=_cparams(("parallel", "parallel")),
        name="conv_mixers",
    )(p1, p1, p1, sc_w, cf_w, cf_b, cf_g, cf_beta)


def _route_t(lt):
    t = lt.shape[1]
    row = lax.broadcasted_iota(jnp.int32, (SUBLANES, t), 0).astype(F32)
    neg = -jnp.inf
    big = float(LANES)
    gl = jnp.where(row < N_GROUPS, lt[N_EXPERTS:N_EXPERTS + SUBLANES], neg)
    gmax = jnp.max(gl, axis=0, keepdims=True)
    p_g = 1.0 / jnp.sum(jnp.exp(gl - gmax), axis=0, keepdims=True)
    g_sel = jnp.min(jnp.where(gl == gmax, row, big), axis=0, keepdims=True)
    el = lt[0:EXP_PER_GROUP]
    for gi in range(1, N_GROUPS):
        el = jnp.where(g_sel == gi, lt[gi * EXP_PER_GROUP:(gi + 1) * EXP_PER_GROUP], el)
    v1 = jnp.max(el, axis=0, keepdims=True)
    i1 = jnp.min(jnp.where(el == v1, row, big), axis=0, keepdims=True)
    el2 = jnp.where(row == i1, neg, el)
    v2 = jnp.max(el2, axis=0, keepdims=True)
    i2 = jnp.min(jnp.where(el2 == v2, row, big), axis=0, keepdims=True)
    e2 = jnp.exp(v2 - v1)
    w1 = p_g / (1.0 + e2)
    w2 = p_g * e2 / (1.0 + e2)
    comb = jnp.where(row == i1, w1, 0.0) + jnp.where(row == i2, w2, 0.0)
    onehot = jnp.where(row == g_sel, 1.0, 0.0)
    return onehot, comb


def _moe_kernel(x_ref, mod_ref, wrh_ref, wrl_ref, tri_ref, exp_ref, wg_ref, wu_ref, wd_ref, lg_ref, lb_ref, o_ref,
                hb_scr, hs_scr, os_scr, pt_scr, ch_scr, cm_scr, blk_smem):
    g = pl.program_id(2)
    n_g = pl.num_programs(2)
    m = mod_ref[0]
    tm, cap = pt_scr.shape

    @pl.when(g == 0)
    def _():
        h = x_ref[0] * (1.0 + m[4:5]) + m[3:4]
        h_hi = h.astype(BF16)
        hb_scr[...] = h_hi
        h_lo = (h - h_hi.astype(F32)).astype(BF16)
        logits = _dot(h_hi, wrh_ref[...]) + _dot(h_lo, wrh_ref[...]) + _dot(h_hi, wrl_ref[...])
        onehot, comb = _route_t(logits.T)
        rank = _dot(onehot.astype(BF16), tri_ref[...])
        cnt = jnp.sum(onehot, axis=1, keepdims=True)
        padded = jnp.floor((cnt + (MOE_BLOCK - 1)) * (1.0 / MOE_BLOCK)) * MOE_BLOCK
        start = jnp.zeros((1, 1), F32)
        dest = jnp.zeros((1, tm), F32)
        for gi in range(N_GROUPS):
            dest = dest + onehot[gi:gi + 1] * (start + rank[gi:gi + 1])
            blk_smem[gi] = (start[0, 0] * (1.0 / MOE_BLOCK)).astype(jnp.int32)
            blk_smem[N_GROUPS + gi] = (padded[gi, 0] * (1.0 / MOE_BLOCK)).astype(jnp.int32)
            start = start + padded[gi:gi + 1]
        aux = jnp.concatenate([jnp.broadcast_to(dest, (SUBLANES, tm)), comb,
                               jnp.zeros((LANES - 2 * SUBLANES, tm), F32)], axis=0).T
        lane = lax.broadcasted_iota(jnp.int32, (tm, LANES), 1)
        comb_tok = jnp.where((lane >= SUBLANES) & (lane < 2 * SUBLANES), aux, 0.0)
        c_hi = comb_tok.astype(BF16)
        c_mid = (comb_tok - c_hi.astype(F32)).astype(BF16)
        slot_t = lax.broadcasted_iota(jnp.int32, (tm, cap), 1).astype(F32)
        pt_scr[...] = jnp.where(slot_t == aux[:, 0:1], 1.0, 0.0).astype(BF16)
        slot = lax.broadcasted_iota(jnp.int32, (cap, tm), 0).astype(F32)
        perm = jnp.where(slot == dest, 1.0, 0.0).astype(BF16)
        hs_scr[...] = _dot(perm, hb_scr[...]).astype(BF16)
        ch_scr[...] = _dot(perm, c_hi).astype(BF16)
        cm_scr[...] = _dot(perm, c_mid).astype(BF16)
        os_scr[...] = jnp.zeros_like(os_scr)

    first = blk_smem[g]
    count = blk_smem[N_GROUPS + g]

    def block(i, carry):
        r0 = pl.multiple_of((first + i) * MOE_BLOCK, MOE_BLOCK)
        rows = pl.ds(r0, MOE_BLOCK)
        hs = hs_scr[rows, :]
        ex = exp_ref[...]
        combx = _dot(ch_scr[rows, :], ex) + _dot(cm_scr[rows, :], ex)
        hid = _silu(_dot(hs, wg_ref[...])) * _dot(hs, wu_ref[...])
        os_scr[rows, :] = _dot((hid * combx).astype(BF16), wd_ref[...]).astype(BF16)
        return carry

    lax.fori_loop(0, count, block, 0)

    @pl.when(g == n_g - 1)
    def _():
        y = _dot(pt_scr[...], os_scr[...])
        r = ALPHA * x_ref[0] + m[5:6] * y
        o_ref[0] = _layer_norm(r, lg_ref[...], lb_ref[...])


def _moe(x, mod, wr, wg, wu, wd, ln_g, ln_b):
    b, l, d = x.shape
    tm = MOE_TILE
    wrh, wrl = wr
    cap = (tm + N_GROUPS * (MOE_BLOCK - 1)) // MOE_BLOCK * MOE_BLOCK
    cw = EXP_PER_GROUP * EXP_FF
    per_batch = mod.shape[0] > 1
    mod_map = (lambda i, t, j: (i, 0, 0)) if per_batch else (lambda i, t, j: (0, 0, 0))
    expand = np.zeros((LANES, cw), np.float32)
    for e in range(EXP_PER_GROUP):
        expand[SUBLANES + e, e * EXP_FF:(e + 1) * EXP_FF] = 1.0
    expand = jnp.asarray(expand, BF16)
    tri = jnp.asarray(np.triu(np.ones((tm, tm), np.float32), k=1), BF16)
    full = lambda a: pl.BlockSpec(a.shape, lambda i, t, j: (0,) * a.ndim)
    return pl.pallas_call(
        _moe_kernel,
        out_shape=jax.ShapeDtypeStruct((b, l, d), F32),
        grid=(b, l // tm, N_GROUPS),
        in_specs=[
            pl.BlockSpec((1, tm, d), lambda i, t, j: (i, t, 0)),
            pl.BlockSpec((1, 6, d), mod_map),
            full(wrh), full(wrl), full(tri), full(expand),
            pl.BlockSpec((d, cw), lambda i, t, j: (0, j)),
            pl.BlockSpec((d, cw), lambda i, t, j: (0, j)),
            pl.BlockSpec((cw, d), lambda i, t, j: (j, 0)),
            full(ln_g), full(ln_b),
        ],
        out_specs=pl.BlockSpec((1, tm, d), lambda i, t, j: (i, t, 0)),
        scratch_shapes=[pltpu.VMEM((tm, d), BF16), pltpu.VMEM((cap, d), BF16), pltpu.VMEM((cap, d), BF16),
                        pltpu.VMEM((tm, cap), BF16), pltpu.VMEM((cap, LANES), BF16), pltpu.VMEM((cap, LANES), BF16),
                        pltpu.SMEM((2 * N_GROUPS,), jnp.int32)],
        compiler_params=_cparams(("parallel", "parallel", "arbitrary")),
        name="hier_moe_ln",
    )(x, mod, wrh, wrl, tri, expand, wg, wu, wd, ln_g, ln_b)


def _rope_swap_perm():
    idx = np.arange(MLA_DR)
    return idx ^ (MLA_DR // 4)


def _rope_tables(n_tok):
    rows = n_tok // GRID_W
    pos_r = jnp.repeat(jnp.arange(rows, dtype=F32), GRID_W)
    pos_c = (jnp.arange(rows * GRID_W) % GRID_W).astype(F32)
    n_freq = MLA_DR // 4
    inv = ROPE_BASE ** (-jnp.arange(n_freq, dtype=F32) / n_freq)
    ang = jnp.stack([pos_r[:, None] * inv, pos_c[:, None] * inv], axis=1)
    cos, sin = jnp.cos(ang), jnp.sin(ang)
    cos32 = jnp.stack([cos, cos], axis=2).reshape(n_tok, MLA_DR)
    sin32 = jnp.stack([-sin, sin], axis=2).reshape(n_tok, MLA_DR)
    pad_hi = LANES - MLA_DN - MLA_DR
    cos_t = jnp.concatenate([jnp.ones((n_tok, MLA_DN), F32), cos32, jnp.zeros((n_tok, pad_hi), F32)], axis=1)
    sin_t = jnp.concatenate([jnp.zeros((n_tok, MLA_DN), F32), sin32, jnp.zeros((n_tok, pad_hi), F32)], axis=1)
    return cos_t, sin_t


def _ab_weights(w_in, w_uq, w_ukv):
    perm = _rope_swap_perm()
    pad_hi = LANES - MLA_DN - MLA_DR
    w_h = w_in[:, :5 * HA_W]
    cq = w_in[:, 5 * HA_W:5 * HA_W + MLA_Q_RANK]
    ckv = w_in[:, 5 * HA_W + MLA_Q_RANK:5 * HA_W + MLA_Q_RANK + MLA_KV_RANK]
    kr = w_in[:, 5 * HA_W + MLA_Q_RANK + MLA_KV_RANK:]
    d = w_in.shape[0]
    z_lo = jnp.zeros((d, MLA_DN), F32)
    z_hi = jnp.zeros((d, pad_hi), F32)
    w_m = jnp.concatenate([cq, ckv, z_lo, kr, z_hi, z_lo, kr[:, perm], z_hi], axis=1)
    uq = w_uq.reshape(MLA_Q_RANK, MLA_HEADS, MLA_DN + MLA_DR)
    q_nope, q_rope = uq[..., :MLA_DN], uq[..., MLA_DN:]
    zq_hi = jnp.zeros((MLA_Q_RANK, MLA_HEADS, pad_hi), F32)
    wq1 = jnp.concatenate([q_nope, q_rope, zq_hi], axis=-1).reshape(MLA_Q_RANK, -1)
    wq2 = jnp.concatenate([jnp.zeros_like(q_nope), q_rope[..., perm], zq_hi], axis=-1).reshape(MLA_Q_RANK, -1)
    ukv = w_ukv.reshape(MLA_KV_RANK, MLA_HEADS, MLA_DN + MLA_DV)
    k_nope, v = ukv[..., :MLA_DN], ukv[..., MLA_DN:]
    wk = jnp.concatenate([k_nope, jnp.zeros((MLA_KV_RANK, MLA_HEADS, LANES - MLA_DN), F32)], axis=-1)
    wk = wk.reshape(MLA_KV_RANK, -1)
    v_pairs = v.reshape(MLA_KV_RANK, MLA_HEADS // 2, 2, MLA_DV)
    zv = jnp.zeros_like(v_pairs[:, :, 0])
    wv = jnp.stack([jnp.concatenate([v_pairs[:, :, 0], zv], axis=-1),
                    jnp.concatenate([zv, v_pairs[:, :, 1]], axis=-1)], axis=2).reshape(MLA_KV_RANK, -1)
    bf = lambda a: a.astype(BF16)
    return bf(w_h), bf(w_m), bf(wq1), bf(wq2), bf(wk), bf(wv)


def _moe_weights(w_group, w_expert, w_gate, w_up, w_down):
    d = w_group.shape[0]
    wr = jnp.concatenate([w_expert, w_group, jnp.zeros((d, LANES - N_EXPERTS - N_GROUPS), F32)], axis=1)
    wr_hi = wr.astype(BF16)
    wr = (wr_hi, (wr - wr_hi.astype(F32)).astype(BF16))
    wg = jnp.transpose(w_gate, (1, 0, 2)).reshape(d, N_EXPERTS * EXP_FF).astype(BF16)
    wu = jnp.transpose(w_up, (1, 0, 2)).reshape(d, N_EXPERTS * EXP_FF).astype(BF16)
    wd = w_down.reshape(N_EXPERTS * EXP_FF, d).astype(BF16)
    return wr, wg, wu, wd


def kernel(x_prompt, x_sample, state_hgrn_fwd, state_hgrn_bwd, cache_mla_ckv, cache_mla_krope, c, c_ctx, mod_w, mod_b, ln_g, ln_b, ab_w_in, ab_w_out, hgrn_lb_logits, hgrn_norm_g, mla_q_norm_g, mla_w_uq, mla_kv_norm_g, mla_w_ukv, cd_w_in, cd_w_out, sc_conv_w, cf_conv_w, cf_conv_b, cf_ln_g, cf_ln_b, moe_w_group, moe_w_expert, moe_w_gate, moe_w_up, moe_w_down):
    dec_b = x_sample.shape[0]
    d = D_MODEL
    cvec = jnp.concatenate([c, c_ctx[None, :], jnp.zeros((SUBLANES - dec_b - 1, d), F32)], axis=0)
    mods = _mod_vectors(cvec, mod_w, mod_b)
    rope_tabs = _rope_tables(x_sample.shape[1])
    xp, xs = x_prompt, x_sample
    new_sf = new_sb = new_ckv = new_kr = None
    for l in range(DEPTH):
        mod_lat = mods[l, :dec_b].reshape(dec_b, 6, d)
        mod_ctx = mods[l, dec_b:dec_b + 1].reshape(1, 6, d)
        row = lambda a: a.reshape(1, -1)
        if l % 2 == 0:
            e = l // 2
            w_h, w_m, wq1, wq2, wk, wv = _ab_weights(ab_w_in[e], mla_w_uq[e], mla_w_ukv[e])
            w_out = ab_w_out[e].astype(BF16)
            qg, kvg, gn = row(mla_q_norm_g[e]), row(mla_kv_norm_g[e]), row(hgrn_norm_g[e])
            ph_p, pm_p = _inproj(xp, mod_ctx, [w_h, w_m])
            of_p, ob_p, sf, sb = _hgrn(ph_p, hgrn_lb_logits, None, None, e)
            q_p, k_p, v_p, ckv_p = _mla_proj(pm_p, None, qg, kvg, wq1, None, wk, wv)
            om_p = _attention(q_p, k_p, v_p, None, None)
            xp = _outproj(xp, mod_ctx, (of_p, ob_p, ph_p, gn), om_p, w_out, row(ln_g[l, 0]), row(ln_b[l, 0]), True)
            ph_s, pm_s = _inproj(xs, mod_lat, [w_h, w_m])
            of_s, ob_s, _, _ = _hgrn(ph_s, hgrn_lb_logits, state_hgrn_fwd[:, e], state_hgrn_bwd[:, e], e)
            q_s, k_s, v_s, _ = _mla_proj(pm_s, rope_tabs, qg, kvg, wq1, wq2, wk, wv)
            kc, vc = _mla_ctx(cache_mla_ckv[:, e], cache_mla_krope[:, e], wk, wv)
            om_s = _attention(q_s, k_s, v_s, kc, vc)
            xs = _outproj(xs, mod_lat, (of_s, ob_s, ph_s, gn), om_s, w_out, row(ln_g[l, 0]), row(ln_b[l, 0]), True)
            kr0 = MLA_Q_RANK + MLA_KV_RANK + MLA_DN
            new_sf, new_sb, new_ckv = sf, sb, ckv_p
            new_kr = pm_p[:, :, kr0:kr0 + MLA_DR]
        else:
            jx = l // 2
            w1 = cd_w_in[jx].astype(BF16)
            w_out = cd_w_out[jx].astype(BF16)
            cd = (sc_conv_w[jx], cf_conv_w[jx], row(cf_conv_b[jx]), row(cf_ln_g[jx]), row(cf_ln_b[jx]))
            (p1_p,) = _inproj(xp, mod_ctx, [w1])
            ysc_p, ycf_p = _conv_mixers(p1_p, *cd)
            xp = _outproj(xp, mod_ctx, ysc_p, ycf_p, w_out, row(ln_g[l, 0]), row(ln_b[l, 0]), False)
            (p1_s,) = _inproj(xs, mod_lat, [w1])
            ysc_s, ycf_s = _conv_mixers(p1_s, *cd)
            xs = _outproj(xs, mod_lat, ysc_s, ycf_s, w_out, row(ln_g[l, 0]), row(ln_b[l, 0]), False)
        wr, wg, wu, wd = _moe_weights(moe_w_group[l], moe_w_expert[l], moe_w_gate[l], moe_w_up[l], moe_w_down[l])
        xp_t = xp.reshape(-1, MOE_TILE, d)
        xp = _moe(xp_t, mod_ctx, wr, wg, wu, wd, row(ln_g[l, 1]), row(ln_b[l, 1])).reshape(xp.shape)
        xs = _moe(xs, mod_lat, wr, wg, wu, wd, row(ln_g[l, 1]), row(ln_b[l, 1]))
    return (xp, xs, new_sf[:, None], new_sb[:, None], new_ckv[:, None], new_kr[:, None])
```

```python
import functools

import numpy as np
import jax
import jax.numpy as jnp
from jax import lax
from jax.experimental import pallas as pl
from jax.experimental.pallas import tpu as pltpu

F32 = jnp.float32
BF16 = jnp.bfloat16
HIGHEST = lax.Precision.HIGHEST

D_MODEL = 1024
DEPTH = 2
GRID_W = 64
N_EVEN = (DEPTH + 1) // 2
HA_HEADS = 4
HA_DK = 128
HA_DV = 128
HA_W = HA_HEADS * HA_DK
CHUNK = 32
MLA_HEADS = 8
MLA_DN = 64
MLA_DR = 32
MLA_DV = 64
MLA_Q_RANK = 384
MLA_KV_RANK = 256
ROPE_BASE = 10000.0
SC_W = 512
SC_K = 3
CF_W = 512
CF_K = 31
N_GROUPS = 4
EXP_PER_GROUP = 8
N_EXPERTS = N_GROUPS * EXP_PER_GROUP
EXP_FF = 128
ALPHA = (2.0 * DEPTH) ** 0.25
LOG2_E = 1.4426950408889634
LN_EPS = 1e-5
RMS_EPS = 1e-6

LANES = 128
SUBLANES = 8
VMEM_LIMIT = 56 * 1024 * 1024

HEAD_PAD = LANES
MLA_SLAB = MLA_Q_RANK + MLA_KV_RANK + 2 * LANES
TOK_TILE = 256
GROUP_ROWS = 128
N_LEVELS = 5
SAFE_LOG_DECAY = 60.0
ATTN_KEY_BLOCK = 512
MOE_TILE = 512
MOE_BLOCK = 128
HALO = 16


def _cparams(sem):
    return pltpu.CompilerParams(dimension_semantics=sem, vmem_limit_bytes=VMEM_LIMIT)


def _silu(x):
    return x / (1.0 + jnp.exp(-x))


def _sigmoid(x):
    return 1.0 / (1.0 + jnp.exp(-x))


def _layer_norm(r, g, b):
    mu = jnp.mean(r, axis=-1, keepdims=True)
    d = r - mu
    var = jnp.mean(d * d, axis=-1, keepdims=True)
    return d * lax.rsqrt(var + LN_EPS) * g + b


def _dot(a, b):
    return jnp.dot(a, b, preferred_element_type=F32)


def _dot_nt(a, b):
    return lax.dot_general(a, b, (((1,), (1,)), ((), ())), preferred_element_type=F32)


def _dot_tn(a, b):
    return lax.dot_general(a, b, (((0,), (0,)), ((), ())), preferred_element_type=F32)


def _mod_kernel(c_ref, w_ref, b_ref, o_ref):
    s = _silu(c_ref[...])
    o_ref[0] = jnp.dot(s, w_ref[0], precision=HIGHEST, preferred_element_type=F32) + b_ref[0]


def _mod_vectors(cvec, mod_w, mod_b):
    n_out = mod_w.shape[-1]
    tn = 1536
    return pl.pallas_call(
        _mod_kernel,
        out_shape=jax.ShapeDtypeStruct((DEPTH, SUBLANES, n_out), F32),
        grid=(DEPTH, n_out // tn),
        in_specs=[
            pl.BlockSpec((SUBLANES, D_MODEL), lambda l, j: (0, 0)),
            pl.BlockSpec((1, D_MODEL, tn), lambda l, j: (l, 0, j)),
            pl.BlockSpec((1, 1, tn), lambda l, j: (l, 0, j)),
        ],
        out_specs=pl.BlockSpec((1, SUBLANES, tn), lambda l, j: (l, 0, j)),
        compiler_params=_cparams(("arbitrary", "arbitrary")),
        name="mod_vectors",
    )(cvec, mod_w, mod_b.reshape(DEPTH, 1, n_out))


def _inproj_kernel(*refs, n_w):
    x_ref, mod_ref = refs[0], refs[1]
    w_refs = refs[2:2 + n_w]
    o_refs = refs[2 + n_w:]
    m = mod_ref[0]
    h = (x_ref[0] * (1.0 + m[1:2]) + m[0:1]).astype(BF16)
    for w_ref, o_ref in zip(w_refs, o_refs):
        o_ref[0] = _dot(h, w_ref[...])


def _inproj(x, mod, weights):
    b, l, d = x.shape
    tm = TOK_TILE
    per_batch = mod.shape[0] > 1
    mod_map = (lambda i, j: (i, 0, 0)) if per_batch else (lambda i, j: (0, 0, 0))
    in_specs = [pl.BlockSpec((1, tm, d), lambda i, j: (i, j, 0)),
                pl.BlockSpec((1, 6, d), mod_map)]
    in_specs += [pl.BlockSpec(w.shape, lambda i, j: (0, 0)) for w in weights]
    out_shape = [jax.ShapeDtypeStruct((b, l, w.shape[1]), F32) for w in weights]
    out_specs = [pl.BlockSpec((1, tm, w.shape[1]), lambda i, j: (i, j, 0)) for w in weights]
    return pl.pallas_call(
        functools.partial(_inproj_kernel, n_w=len(weights)),
        out_shape=out_shape,
        grid=(b, l // tm),
        in_specs=in_specs,
        out_specs=out_specs,
        compiler_params=_cparams(("parallel", "parallel")),
        name="modulate_inproj",
    )(x, mod, *weights)


def _hgrn_tables():
    n = GROUP_ROWS
    t = np.arange(n)[:, None]
    j = np.arange(n)[None, :]
    same_chunk = (t // CHUNK) == (j // CHUNK)
    e_f, e_b = [], []
    lv_f = np.full((n, n), -1, np.int32)
    for lvl in range(N_LEVELS):
        m = CHUNK >> (lvl + 1)
        blk0 = (t // (2 * m)) * (2 * m)
        r = blk0 + m - 1
        upper = t > r
        ef = np.where(upper, (j > r) & (j <= t), (j > t) & (j <= r))
        r2 = blk0 + m
        lower = t < r2
        eb = np.where(lower, (j >= t) & (j < r2), (j >= r2) & (j < t))
        e_f.append(ef)
        e_b.append(eb)
        s = np.arange(n)[None, :]
        same_blk = (t // (2 * m)) == (s // (2 * m))
        q_side = (t % (2 * m)) >= m
        k_side = (s % (2 * m)) < m
        lv_f[same_blk & q_side & k_side] = lvl
    lv_f[np.arange(n), np.arange(n)] = N_LEVELS
    e_f = np.concatenate(e_f, axis=0).astype(np.float32)
    e_b = np.concatenate(e_b, axis=0).astype(np.float32)
    e = np.stack([e_f, e_b])
    cum = np.stack([same_chunk & (j <= t), same_chunk & (j >= t)]).astype(np.float32)
    lv = np.stack([lv_f, lv_f.T])
    return e, cum, lv


def _hgrn_kernel(qf_ref, vf_ref, ff_ref, qb_ref, vb_ref, fb_ref, lbl_ref, e_ref, cum_ref, lv_ref, s0f_ref, s0b_ref,
                 of_ref, ob_ref, sf_ref, sb_ref,
                 st_scr, qe_scr, kd_scr, sc_scr, v_scr, oi_scr, dec_scr, *, lt, slot, has_state):
    i = pl.program_id(1)
    n_i = pl.num_programs(1)
    n_chunks = lt // CHUNK
    n_groups = lt // GROUP_ROWS
    chunks_per_group = GROUP_ROWS // CHUNK

    @pl.when(i == 0)
    def _():
        for d, s0_ref in enumerate((s0f_ref, s0b_ref)):
            for h in range(HA_HEADS):
                if has_state:
                    st_scr[d, h] = s0_ref[0, h].T
                else:
                    st_scr[d, h] = jnp.zeros((HA_DV, HA_DK), F32)

    lg = lbl_ref[...]
    n_slots = lg.shape[0]
    mx = lg[0]
    for s in range(1, n_slots):
        mx = jnp.maximum(mx, lg[s])
    ex = [jnp.exp(lg[s] - mx) for s in range(n_slots)]
    den = ex[0]
    for s in range(1, n_slots):
        den = den + ex[s]
    num = ex[0]
    for s in range(1, slot + 1):
        num = num + ex[s]
    lb_all = num / den

    dirs = ((qf_ref, vf_ref, ff_ref), (qb_ref, vb_ref, fb_ref))
    head_cols = [slice(h * HA_DK, (h + 1) * HA_DK) for h in range(HA_HEADS)]

    def gates(d, rows, cols):
        q_ref, _, f_ref = dirs[d]
        q = _silu(q_ref[0, rows, cols])
        lb_h = lb_all[d:d + 1, cols]
        f = lb_h + (1.0 - lb_h) * _sigmoid(f_ref[0, rows, cols])
        g = jnp.log(f)
        g_hi = g.astype(BF16)
        g_lo = (g - g_hi.astype(F32)).astype(BF16)
        return q, 1.0 - f, g_hi, g_lo

    tot_min = jnp.zeros((1, HA_DK), F32)
    for d in range(2):

        def group_step(grp, tot_min, d=d):
            r0 = pl.multiple_of(grp * GROUP_ROWS, GROUP_ROWS)
            rows = pl.ds(r0, GROUP_ROWS)
            cum = cum_ref[d]
            lv = lv_ref[d]
            for cols in head_cols:
                q, k, g_hi, g_lo = gates(d, rows, cols)
                bcum = _dot(cum, g_hi) + _dot(cum, g_lo)
                v_scr[d, rows, cols] = dirs[d][1][0, rows, cols].astype(BF16)
                qe = (q * jnp.exp(bcum)).astype(BF16)
                qe_scr[d, rows, cols] = qe
                p = _dot_nt(qe, (k * jnp.exp(-bcum)).astype(BF16))
                sc_scr[d, rows, cols] = jnp.where(lv >= 0, p, 0.0).astype(BF16)
                for cc in range(chunks_per_group):
                    edge = cc * CHUNK + (CHUNK - 1 if d == 0 else 0)
                    tot = bcum[edge:edge + 1]
                    b_c = bcum[cc * CHUNK:(cc + 1) * CHUNK]
                    k_c = k[cc * CHUNK:(cc + 1) * CHUNK]
                    kd = (k_c * jnp.exp(tot - b_c)).astype(BF16)
                    kd_scr[d, pl.ds(r0 + cc * CHUNK, CHUNK), cols] = kd
                    dec_scr[d, grp * chunks_per_group + cc, :, cols] = jnp.exp(tot)
                    tot_min = jnp.minimum(tot_min, tot)
            return tot_min

        tot_min = lax.fori_loop(0, n_groups, group_step, tot_min)

    def one_factor():
        for d in range(2):
            for grp in range(n_groups):
                rows = slice(grp * GROUP_ROWS, (grp + 1) * GROUP_ROWS)
                for cols in head_cols:
                    oi_scr[d, rows, cols] = _dot(sc_scr[d, rows, cols], v_scr[d, rows, cols])

    def per_level():
        for d in range(2):

            def group_step(grp, carry, d=d):
                rows = pl.ds(pl.multiple_of(grp * GROUP_ROWS, GROUP_ROWS), GROUP_ROWS)
                lv = lv_ref[d]
                for cols in head_cols:
                    q, k, g_hi, g_lo = gates(d, rows, cols)
                    x = jnp.exp(_dot(e_ref[d], g_hi) + _dot(e_ref[d], g_lo))
                    sc = jnp.where(lv == N_LEVELS, _dot_nt(q.astype(BF16), k.astype(BF16)), 0.0)
                    for lvl in range(N_LEVELS):
                        xl = x[lvl * GROUP_ROWS:(lvl + 1) * GROUP_ROWS]
                        p = _dot_nt((q * xl).astype(BF16), (k * xl).astype(BF16))
                        sc = jnp.where(lv == lvl, p, sc)
                    oi_scr[d, rows, cols] = _dot(sc.astype(BF16), v_scr[d, rows, cols])
                return carry

            lax.fori_loop(0, n_groups, group_step, 0)

    lax.cond(jnp.min(tot_min) > -SAFE_LOG_DECAY, one_factor, per_level)

    out_refs = (of_ref, ob_ref)

    def chunk_step(c, carry):
        for d in range(2):
            cc = c if d == 0 else n_chunks - 1 - c
            r0 = pl.multiple_of(cc * CHUNK, CHUNK)
            rows = pl.ds(r0, CHUNK)
            for h in range(HA_HEADS):
                cols = slice(h * HA_DK, (h + 1) * HA_DK)
                st = st_scr[d, h]
                o_state = _dot_nt(qe_scr[d, rows, cols], st.astype(BF16))
                out_refs[d][0, rows, cols] = oi_scr[d, rows, cols] + o_state
                upd = _dot_tn(v_scr[d, rows, cols], kd_scr[d, rows, cols])
                st_scr[d, h] = st * dec_scr[d, cc, :, cols] + upd
        return carry

    lax.fori_loop(0, n_chunks, chunk_step, 0)

    @pl.when(i == n_i - 1)
    def _():
        for d, s_ref in enumerate((sf_ref, sb_ref)):
            for h in range(HA_HEADS):
                s_ref[0, h] = st_scr[d, h].T


def _hgrn(p_h, lb_logits, s0_f, s0_b, slot):
    b, l, _ = p_h.shape
    lt = TOK_TILE
    n_t = l // lt
    has_state = s0_f is not None
    if not has_state:
        s0_f = jnp.zeros((1, HA_HEADS, HA_DK, HA_DV), F32)
        s0_b = s0_f
    e_np, cum_np, lv_np = _hgrn_tables()
    e_mat = jnp.asarray(e_np, BF16)
    cum = jnp.asarray(cum_np, BF16)
    lv = jnp.asarray(lv_np, jnp.int32)
    w = HA_W

    def sec(idx, rev):
        if rev:
            return pl.BlockSpec((1, lt, w), lambda bi, i: (bi, n_t - 1 - i, idx))
        return pl.BlockSpec((1, lt, w), lambda bi, i: (bi, i, idx))

    state_map = (lambda bi, i: (bi, 0, 0, 0)) if has_state else (lambda bi, i: (0, 0, 0, 0))
    in_specs = [
        sec(0, False), sec(3, False), sec(1, False),
        sec(0, True), sec(3, True), sec(2, True),
        pl.BlockSpec(lb_logits.shape, lambda bi, i: (0, 0, 0)),
        pl.BlockSpec(e_mat.shape, lambda bi, i: (0, 0, 0)),
        pl.BlockSpec(cum.shape, lambda bi, i: (0, 0, 0)),
        pl.BlockSpec(lv.shape, lambda bi, i: (0, 0, 0)),
        pl.BlockSpec((1, HA_HEADS, HA_DK, HA_DV), state_map),
        pl.BlockSpec((1, HA_HEADS, HA_DK, HA_DV), state_map),
    ]
    out_shape = [
        jax.ShapeDtypeStruct((b, l, w), F32),
        jax.ShapeDtypeStruct((b, l, w), F32),
        jax.ShapeDtypeStruct((b, HA_HEADS, HA_DK, HA_DV), F32),
        jax.ShapeDtypeStruct((b, HA_HEADS, HA_DK, HA_DV), F32),
    ]
    out_specs = [
        pl.BlockSpec((1, lt, w), lambda bi, i: (bi, i, 0)),
        pl.BlockSpec((1, lt, w), lambda bi, i: (bi, n_t - 1 - i, 0)),
        pl.BlockSpec((1, HA_HEADS, HA_DK, HA_DV), lambda bi, i: (bi, 0, 0, 0)),
        pl.BlockSpec((1, HA_HEADS, HA_DK, HA_DV), lambda bi, i: (bi, 0, 0, 0)),
    ]
    scratch = [
        pltpu.VMEM((2, HA_HEADS, HA_DV, HA_DK), F32),
        pltpu.VMEM((2, lt, w), BF16),
        pltpu.VMEM((2, lt, w), BF16),
        pltpu.VMEM((2, lt, w), BF16),
        pltpu.VMEM((2, lt, w), BF16),
        pltpu.VMEM((2, lt, w), F32),
        pltpu.VMEM((2, lt // CHUNK, 1, w), F32),
    ]
    return pl.pallas_call(
        functools.partial(_hgrn_kernel, lt=lt, slot=slot, has_state=has_state),
        out_shape=out_shape,
        grid=(b, n_t),
        in_specs=in_specs,
        out_specs=out_specs,
        scratch_shapes=scratch,
        compiler_params=_cparams(("parallel", "arbitrary")),
        name="hgrn2_scan",
    )(p_h, p_h, p_h, p_h, p_h, p_h, lb_logits, e_mat, cum, lv, s0_f, s0_b)


def _mla_proj_kernel(*refs, rope):
    if rope:
        pm_ref, cos_ref, sin_ref, qg_ref, kvg_ref, wq1_ref, wq2_ref, wk_ref, wv_ref = refs[:9]
        q_ref, k_ref, v_ref, ckv_ref = refs[9:]
    else:
        pm_ref, qg_ref, kvg_ref, wq1_ref, wk_ref, wv_ref = refs[:6]
        q_ref, k_ref, v_ref, ckv_ref = refs[6:]
    scale = (MLA_DN + MLA_DR) ** -0.5 * LOG2_E
    pm = pm_ref[0]
    cq = pm[:, :MLA_Q_RANK]
    cq = cq * lax.rsqrt(jnp.mean(cq * cq, axis=-1, keepdims=True) + RMS_EPS) * qg_ref[...]
    cq = cq.astype(BF16)
    ckv = pm[:, MLA_Q_RANK:MLA_Q_RANK + MLA_KV_RANK]
    ckv = ckv * lax.rsqrt(jnp.mean(ckv * ckv, axis=-1, keepdims=True) + RMS_EPS) * kvg_ref[...]
    ckv_ref[0] = ckv
    ckv = ckv.astype(BF16)
    kr0 = MLA_Q_RANK + MLA_KV_RANK
    kr = pm[:, kr0:kr0 + LANES]
    qa = _dot(cq, wq1_ref[...])
    kn = _dot(ckv, wk_ref[...])
    v_ref[0] = _dot_nt(wv_ref[...], ckv).astype(BF16)
    if rope:
        cos = cos_ref[...]
        sin = sin_ref[...]
        qb = _dot(cq, wq2_ref[...])
        kr = kr * cos + pm[:, kr0 + LANES:kr0 + 2 * LANES] * sin
    for h in range(MLA_HEADS):
        sl = slice(h * HEAD_PAD, (h + 1) * HEAD_PAD)
        qh = qa[:, sl]
        if rope:
            qh = qh * cos + qb[:, sl] * sin
        q_ref[0, :, sl] = (qh * scale).astype(BF16)
        k_ref[0, :, sl] = (kn[:, sl] + kr).astype(BF16)


def _mla_proj(p_m, rope_tabs, qg, kvg, wq1, wq2, wk, wv):
    b, l, _ = p_m.shape
    tm = TOK_TILE
    rope = rope_tabs is not None
    full = lambda a: pl.BlockSpec(a.shape, lambda i, j: (0,) * a.ndim)
    hw = MLA_HEADS * HEAD_PAD
    inputs = [p_m]
    in_specs = [pl.BlockSpec((1, tm, MLA_SLAB), lambda i, j: (i, j, 0))]
    if rope:
        inputs += list(rope_tabs)
        in_specs += [pl.BlockSpec((tm, LANES), lambda i, j: (j, 0))] * 2
    ws = [qg, kvg, wq1] + ([wq2] if rope else []) + [wk, wv]
    inputs += ws
    in_specs += [full(a) for a in ws]
    vw = MLA_HEADS * MLA_DV
    out_shape = [jax.ShapeDtypeStruct((b, l, hw), BF16)] * 2 + [
        jax.ShapeDtypeStruct((b, vw, l), BF16), jax.ShapeDtypeStruct((b, l, MLA_KV_RANK), F32)]
    out_specs = [pl.BlockSpec((1, tm, hw), lambda i, j: (i, j, 0))] * 2 + [
        pl.BlockSpec((1, vw, tm), lambda i, j: (i, 0, j)),
        pl.BlockSpec((1, tm, MLA_KV_RANK), lambda i, j: (i, j, 0))]
    return pl.pallas_call(
        functools.partial(_mla_proj_kernel, rope=rope),
        out_shape=out_shape,
        grid=(b, l // tm),
        in_specs=in_specs,
        out_specs=out_specs,
        compiler_params=_cparams(("parallel", "parallel")),
        name="mla_proj",
    )(*inputs)


def _mla_ctx_kernel(ckv_ref, kr_ref, place_ref, wk_ref, wv_ref, k_ref, v_ref):
    ckv = ckv_ref[0].astype(BF16)
    kr = _dot(kr_ref[0].astype(BF16), place_ref[...])
    kn = _dot(ckv, wk_ref[...])
    v_ref[0] = _dot_nt(wv_ref[...], ckv).astype(BF16)
    for h in range(MLA_HEADS):
        sl = slice(h * HEAD_PAD, (h + 1) * HEAD_PAD)
        k_ref[0, :, sl] = (kn[:, sl] + kr).astype(BF16)


def _mla_ctx(ctx_ckv, ctx_kr, wk, wv):
    b, lc, _ = ctx_ckv.shape
    hw = MLA_HEADS * HEAD_PAD
    vw = MLA_HEADS * MLA_DV
    place = np.zeros((MLA_DR, LANES), np.float32)
    place[np.arange(MLA_DR), MLA_DN + np.arange(MLA_DR)] = 1.0
    place = jnp.asarray(place, BF16)
    full = lambda a: pl.BlockSpec(a.shape, lambda i: (0,) * a.ndim)
    return pl.pallas_call(
        _mla_ctx_kernel,
        out_shape=[jax.ShapeDtypeStruct((b, lc, hw), BF16), jax.ShapeDtypeStruct((b, vw, lc), BF16)],
        grid=(b,),
        in_specs=[pl.BlockSpec((1, lc, MLA_KV_RANK), lambda i: (i, 0, 0)),
                  pl.BlockSpec((1, lc, MLA_DR), lambda i: (i, 0, 0)),
                  full(place), full(wk), full(wv)],
        out_specs=[pl.BlockSpec((1, lc, hw), lambda i: (i, 0, 0)), pl.BlockSpec((1, vw, lc), lambda i: (i, 0, 0))],
        compiler_params=_cparams(("parallel",)),
        name="mla_ctx_keys",
    )(ctx_ckv, ctx_kr, place, wk, wv)


def _attn_kernel(*refs, has_ctx):
    s_scr = refs[-1]
    refs = refs[:-1]
    if has_ctx:
        q_ref, k_ref, vt_ref, kc_ref, vtc_ref, o_ref = refs
    else:
        q_ref, k_ref, vt_ref, o_ref = refs
    tq = q_ref.shape[1]
    lk = k_ref.shape[1]
    tk = min(ATTN_KEY_BLOCK, lk)
    n_blk = lk // tk
    slabs = [slice(hh * HEAD_PAD, (hh + 1) * HEAD_PAD) for hh in range(2)]
    vrows = [slice(hh * MLA_DV, (hh + 1) * MLA_DV) for hh in range(2)]
    q_t = [q_ref[0, :, sl].astype(F32).T.astype(BF16) for sl in slabs]

    def fold(a, op):
        return op(a.reshape(a.shape[0] // SUBLANES, SUBLANES, tq), axis=0)

    def scores(hh, k_blk, rows, m_run):
        s = _dot(k_blk, q_t[hh])
        s_scr[hh, rows, :] = s
        return jnp.maximum(m_run, fold(s, jnp.max))

    def weigh(hh, vt_blk, rows, m, acc):
        p = jnp.exp2(s_scr[hh, rows, :] - m).astype(BF16)
        lhs = jnp.concatenate([vt_blk, jnp.ones((2 * SUBLANES, vt_blk.shape[1]), BF16)], axis=0)
        return acc + _dot(lhs, p)

    def pass1(j, m_runs):
        rows = pl.ds(pl.multiple_of(j * tk, tk), tk)
        return tuple(scores(hh, k_ref[0, rows, slabs[hh]], rows, m_runs[hh]) for hh in range(2))

    m_runs = (jnp.full((SUBLANES, tq), -jnp.inf, F32),) * 2
    m_runs = lax.fori_loop(0, n_blk, pass1, m_runs, unroll=min(2, n_blk))
    if has_ctx:
        crow = slice(lk, lk + kc_ref.shape[1])
        m_runs = tuple(scores(hh, kc_ref[0, :, slabs[hh]], crow, m_runs[hh]) for hh in range(2))
    ms = [jnp.max(mr, axis=0, keepdims=True) for mr in m_runs]

    def pass2(j, carries):
        rows = pl.ds(pl.multiple_of(j * tk, tk), tk)
        return tuple(weigh(hh, vt_ref[0, vrows[hh], rows], rows, ms[hh], carries[hh]) for hh in range(2))

    carries = (jnp.zeros((MLA_DV + 2 * SUBLANES, tq), F32),) * 2
    carries = lax.fori_loop(0, n_blk, pass2, carries, unroll=min(2, n_blk))
    if has_ctx:
        carries = tuple(weigh(hh, vtc_ref[0, vrows[hh], :], crow, ms[hh], carries[hh]) for hh in range(2))
    outs = [acc[:MLA_DV] / acc[MLA_DV:MLA_DV + 1] for acc in carries]
    o_ref[0] = jnp.concatenate(outs, axis=0).T.astype(BF16)


def _attention(q, k, vt, kc, vtc):
    b, l, _ = q.shape
    lk = k.shape[1]
    tq = TOK_TILE
    pw = 2 * HEAD_PAD
    vpw = 2 * MLA_DV
    has_ctx = kc is not None
    inputs = [q, k, vt]
    in_specs = [pl.BlockSpec((1, tq, pw), lambda bi, p, i: (bi, i, p)),
                pl.BlockSpec((1, lk, pw), lambda bi, p, i: (bi, 0, p)),
                pl.BlockSpec((1, vpw, lk), lambda bi, p, i: (bi, p, 0))]
    if has_ctx:
        lc = kc.shape[1]
        inputs += [kc, vtc]
        in_specs += [pl.BlockSpec((1, lc, pw), lambda bi, p, i: (bi, 0, p)),
                     pl.BlockSpec((1, vpw, lc), lambda bi, p, i: (bi, p, 0))]
    return pl.pallas_call(
        functools.partial(_attn_kernel, has_ctx=has_ctx),
        out_shape=jax.ShapeDtypeStruct((b, l, MLA_HEADS * MLA_DV), BF16),
        grid=(b, MLA_HEADS // 2, l // tq),
        in_specs=in_specs,
        out_specs=pl.BlockSpec((1, tq, LANES), lambda bi, p, i: (bi, i, p)),
        scratch_shapes=[pltpu.VMEM((2, lk + (kc.shape[1] if has_ctx else 0), tq), F32)],
        compiler_params=_cparams(("parallel", "parallel", "arbitrary")),
        name="mla_attention",
    )(*inputs)


def _outproj_kernel(*refs, hgrn):
    if hgrn:
        x_ref, mod_ref, of_ref, ob_ref, gl_ref, gn_ref, b_ref, w_ref, lg_ref, lb_ref, o_ref = refs
        o = of_ref[0] + ob_ref[0]
        gate = _silu(gl_ref[0])
        gn = gn_ref[...]
        parts = []
        for h in range(HA_HEADS):
            sl = slice(h * HA_DV, (h + 1) * HA_DV)
            oh = o[:, sl]
            oh = oh * lax.rsqrt(jnp.mean(oh * oh, axis=-1, keepdims=True) + RMS_EPS) * gn
            parts.append((oh * gate[:, sl]).astype(BF16))
        a = jnp.concatenate(parts, axis=-1)
    else:
        x_ref, mod_ref, a_ref, b_ref, w_ref, lg_ref, lb_ref, o_ref = refs
        a = a_ref[0]
    half = a.shape[-1]
    y = _dot(a, w_ref[:half, :]) + _dot(b_ref[0], w_ref[half:, :])
    m = mod_ref[0]
    r = ALPHA * x_ref[0] + m[2:3] * y
    o_ref[0] = _layer_norm(r, lg_ref[...], lb_ref[...])


def _outproj(x, mod, a_inputs, b_in, w, ln_g, ln_b, hgrn):
    b, l, d = x.shape
    tm = TOK_TILE
    per_batch = mod.shape[0] > 1
    mod_map = (lambda i, j: (i, 0, 0)) if per_batch else (lambda i, j: (0, 0, 0))
    row = lambda width: pl.BlockSpec((1, tm, width), lambda i, j: (i, j, 0))
    full = lambda a: pl.BlockSpec(a.shape, lambda i, j: (0,) * a.ndim)
    inputs = [x, mod]
    in_specs = [row(d), pl.BlockSpec((1, 6, d), mod_map)]
    if hgrn:
        o_f, o_b, p_h, g_norm = a_inputs
        inputs += [o_f, o_b, p_h, g_norm]
        in_specs += [row(HA_W), row(HA_W),
                     pl.BlockSpec((1, tm, HA_W), lambda i, j: (i, j, 4)), full(g_norm)]
    else:
        inputs += [a_inputs]
        in_specs += [row(a_inputs.shape[-1])]
    inputs += [b_in, w, ln_g, ln_b]
    in_specs += [row(b_in.shape[-1]), full(w), full(ln_g), full(ln_b)]
    return pl.pallas_call(
        functools.partial(_outproj_kernel, hgrn=hgrn),
        out_shape=jax.ShapeDtypeStruct((b, l, d), F32),
        grid=(b, l // tm),
        in_specs=in_specs,
        out_specs=row(d),
        compiler_params=_cparams(("parallel", "parallel")),
        name="outproj_ln",
    )(*inputs)


def _conv_kernel(pm_ref, pp_ref, pn_ref, scw_ref, cfw_ref, cfb_ref, cfg_ref, cfbeta_ref,
                 ysc_ref, ycf_ref, ext_sc, ext_cf, *, lt):
    i = pl.program_id(1)
    n_i = pl.num_programs(1)
    w = SC_W

    def sc_in(p):
        return p[:, w:2 * w] * p[:, 2 * w:3 * w]

    def cf_in(p):
        return p[:, 3 * w:3 * w + CF_W] * _sigmoid(p[:, 3 * w + CF_W:3 * w + 2 * CF_W])

    pm = pm_ref[0]
    pp = pp_ref[0]
    pn = pn_ref[0]
    has_prev = i > 0
    has_next = i < n_i - 1
    ext_sc[0:HALO, :] = jnp.where(has_prev, sc_in(pp), 0.0)
    ext_sc[HALO:HALO + lt, :] = sc_in(pm)
    ext_sc[HALO + lt:, :] = jnp.where(has_next, sc_in(pn), 0.0)
    ext_cf[0:HALO, :] = jnp.where(has_prev, cf_in(pp), 0.0)
    ext_cf[HALO:HALO + lt, :] = cf_in(pm)
    ext_cf[HALO + lt:, :] = jnp.where(has_next, cf_in(pn), 0.0)

    rb = 32
    for r in range(lt // rb):
        base = HALO + r * rb
        acc = None
        for j in range(SC_K):
            term = scw_ref[j:j + 1, :] * ext_sc[base - SC_K // 2 + j:base - SC_K // 2 + j + rb, :]
            acc = term if acc is None else acc + term
        ysc_ref[0, r * rb:(r + 1) * rb, :] = (pm[r * rb:(r + 1) * rb, 0:w] * acc).astype(BF16)
        acc = None
        for j in range(CF_K):
            term = cfw_ref[j:j + 1, :] * ext_cf[base - CF_K // 2 + j:base - CF_K // 2 + j + rb, :]
            acc = term if acc is None else acc + term
        u = _layer_norm(acc + cfb_ref[...], cfg_ref[...], cfbeta_ref[...])
        ycf_ref[0, r * rb:(r + 1) * rb, :] = _silu(u).astype(BF16)


def _conv_mixers(p1, sc_w, cf_w, cf_b, cf_g, cf_beta):
    b, l, width = p1.shape
    lt = TOK_TILE
    hb = lt // HALO
    n_h = l // HALO
    full = lambda a: pl.BlockSpec(a.shape, lambda bi, i: (0,) * a.ndim)
    return pl.pallas_call(
        functools.partial(_conv_kernel, lt=lt),
        out_shape=[jax.ShapeDtypeStruct((b, l, SC_W), BF16), jax.ShapeDtypeStruct((b, l, CF_W), BF16)],
        grid=(b, l // lt),
        in_specs=[
            pl.BlockSpec((1, lt, width), lambda bi, i: (bi, i, 0)),
            pl.BlockSpec((1, HALO, width), lambda bi, i: (bi, jnp.maximum(i * hb - 1, 0), 0)),
            pl.BlockSpec((1, HALO, width), lambda bi, i: (bi, jnp.minimum((i + 1) * hb, n_h - 1), 0)),
            full(sc_w), full(cf_w), full(cf_b), full(cf_g), full(cf_beta),
        ],
        out_specs=[pl.BlockSpec((1, lt, SC_W), lambda bi, i: (bi, i, 0)),
                   pl.BlockSpec((1, lt, CF_W), lambda bi, i: (bi, i, 0))],
        scratch_shapes=[pltpu.VMEM((lt + 2 * HALO, SC_W), F32), pltpu.VMEM((lt + 2 * HALO, CF_W), F32)],
        compiler_params=_cparams(("parallel", "parallel")),
        name="conv_mixers",
    )(p1, p1, p1, sc_w, cf_w, cf_b, cf_g, cf_beta)


def _route_t(lt):
    t = lt.shape[1]
    row = lax.broadcasted_iota(jnp.int32, (SUBLANES, t), 0).astype(F32)
    neg = -jnp.inf
    big = float(LANES)
    gl = jnp.where(row < N_GROUPS, lt[N_EXPERTS:N_EXPERTS + SUBLANES], neg)
    gmax = jnp.max(gl, axis=0, keepdims=True)
    p_g = 1.0 / jnp.sum(jnp.exp(gl - gmax), axis=0, keepdims=True)
    g_sel = jnp.min(jnp.where(gl == gmax, row, big), axis=0, keepdims=True)
    el = lt[0:EXP_PER_GROUP]
    for gi in range(1, N_GROUPS):
        el = jnp.where(g_sel == gi, lt[gi * EXP_PER_GROUP:(gi + 1) * EXP_PER_GROUP], el)
    v1 = jnp.max(el, axis=0, keepdims=True)
    i1 = jnp.min(jnp.where(el == v1, row, big), axis=0, keepdims=True)
    el2 = jnp.where(row == i1, neg, el)
    v2 = jnp.max(el2, axis=0, keepdims=True)
    i2 = jnp.min(jnp.where(el2 == v2, row, big), axis=0, keepdims=True)
    e2 = jnp.exp(v2 - v1)
    w1 = p_g / (1.0 + e2)
    w2 = p_g * e2 / (1.0 + e2)
    comb = jnp.where(row == i1, w1, 0.0) + jnp.where(row == i2, w2, 0.0)
    onehot = jnp.where(row == g_sel, 1.0, 0.0)
    return onehot, comb


def _moe_kernel(x_ref, mod_ref, wr_ref, tri_ref, exp_ref, wg_ref, wu_ref, wd_ref, lg_ref, lb_ref, o_ref,
                hb_scr, hs_scr, os_scr, pt_scr, cs_scr):
    m = mod_ref[0]
    tm, cap = pt_scr.shape
    d = x_ref.shape[-1]
    half = d // 2
    cw = EXP_PER_GROUP * EXP_FF

    h = x_ref[0] * (1.0 + m[4:5]) + m[3:4]
    h_hi = h.astype(BF16)
    hb_scr[...] = h_hi
    h_lo = (h - h_hi.astype(F32)).astype(BF16)
    l2 = _dot(h_hi, wr_ref[...])
    logits = l2[:, :LANES] + l2[:, LANES:] + _dot(h_lo, wr_ref[:, :LANES])
    onehot, comb = _route_t(logits.T)
    rank = _dot(onehot.astype(BF16), tri_ref[...])
    cnt = jnp.sum(onehot, axis=1, keepdims=True)
    padded = jnp.floor((cnt + (MOE_BLOCK - 1)) * (1.0 / MOE_BLOCK)) * MOE_BLOCK
    start = jnp.zeros((1, 1), F32)
    dest = jnp.zeros((1, tm), F32)
    first, count = [], []
    for gi in range(N_GROUPS):
        dest = dest + onehot[gi:gi + 1] * (start + rank[gi:gi + 1])
        first.append((start[0, 0] * (1.0 / MOE_BLOCK)).astype(jnp.int32))
        count.append((padded[gi, 0] * (1.0 / MOE_BLOCK)).astype(jnp.int32))
        start = start + padded[gi:gi + 1]
    aux = jnp.concatenate([jnp.broadcast_to(dest, (SUBLANES, tm)), comb,
                           jnp.zeros((LANES - 2 * SUBLANES, tm), F32)], axis=0).T
    lane = lax.broadcasted_iota(jnp.int32, (tm, LANES), 1)
    comb_tok = jnp.where((lane >= SUBLANES) & (lane < 2 * SUBLANES), aux, 0.0)
    c_hi = comb_tok.astype(BF16)
    c_mid = (comb_tok - c_hi.astype(F32)).astype(BF16)
    slot_t = lax.broadcasted_iota(jnp.int32, (tm, cap), 1).astype(F32)
    pt_scr[...] = jnp.where(slot_t == aux[:, 0:1], 1.0, 0.0).astype(BF16)
    slot = lax.broadcasted_iota(jnp.int32, (cap, tm), 0).astype(F32)
    perm = jnp.where(slot == dest, 1.0, 0.0).astype(BF16)
    hs_scr[:, :half] = _dot(perm, hb_scr[:, :half]).astype(BF16)
    hs_scr[:, half:] = _dot(perm, hb_scr[:, half:]).astype(BF16)
    cs_scr[...] = _dot(perm, jnp.concatenate([c_hi, c_mid], axis=1)).astype(BF16)
    os_scr[...] = jnp.zeros_like(os_scr)

    for gi in range(N_GROUPS):
        cols = slice(gi * cw, (gi + 1) * cw)

        def block(i, carry, gi=gi, cols=cols):
            r0 = pl.multiple_of((first[gi] + i) * MOE_BLOCK, MOE_BLOCK)
            rows = pl.ds(r0, MOE_BLOCK)
            hs = hs_scr[rows, :]
            ex = exp_ref[...]
            combx = _dot(cs_scr[rows, :LANES], ex) + _dot(cs_scr[rows, LANES:], ex)
            hid = _silu(_dot(hs, wg_ref[:, cols])) * _dot(hs, wu_ref[:, cols])
            os_scr[rows, :] = _dot((hid * combx).astype(BF16), wd_ref[cols, :]).astype(BF16)
            return carry

        lax.fori_loop(0, count[gi], block, 0)

    y = jnp.concatenate([_dot(pt_scr[...], os_scr[:, :half]), _dot(pt_scr[...], os_scr[:, half:])], axis=1)
    r = ALPHA * x_ref[0] + m[5:6] * y
    o_ref[0] = _layer_norm(r, lg_ref[...], lb_ref[...])


def _moe(x, mod, wr, wg, wu, wd, ln_g, ln_b):
    b, l, d = x.shape
    tm = MOE_TILE
    cap = (tm + N_GROUPS * (MOE_BLOCK - 1)) // MOE_BLOCK * MOE_BLOCK
    cw = EXP_PER_GROUP * EXP_FF
    per_batch = mod.shape[0] > 1
    mod_map = (lambda i, t: (i, 0, 0)) if per_batch else (lambda i, t: (0, 0, 0))
    expand = np.zeros((LANES, cw), np.float32)
    for e in range(EXP_PER_GROUP):
        expand[SUBLANES + e, e * EXP_FF:(e + 1) * EXP_FF] = 1.0
    expand = jnp.asarray(expand, BF16)
    tri = jnp.asarray(np.triu(np.ones((tm, tm), np.float32), k=1), BF16)
    full = lambda a: pl.BlockSpec(a.shape, lambda i, t: (0,) * a.ndim, pipeline_mode=pl.Buffered(1))
    return pl.pallas_call(
        _moe_kernel,
        out_shape=jax.ShapeDtypeStruct((b, l, d), F32),
        grid=(b, l // tm),
        in_specs=[
            pl.BlockSpec((1, tm, d), lambda i, t: (i, t, 0)),
            pl.BlockSpec((1, 6, d), mod_map),
            full(wr), full(tri), full(expand), full(wg), full(wu), full(wd), full(ln_g), full(ln_b),
        ],
        out_specs=pl.BlockSpec((1, tm, d), lambda i, t: (i, t, 0)),
        scratch_shapes=[pltpu.VMEM((tm, d), BF16), pltpu.VMEM((cap, d), BF16), pltpu.VMEM((cap, d), BF16),
                        pltpu.VMEM((tm, cap), BF16), pltpu.VMEM((cap, 2 * LANES), BF16)],
        compiler_params=_cparams(("arbitrary", "arbitrary")),
        name="hier_moe_ln",
    )(x, mod, wr, tri, expand, wg, wu, wd, ln_g, ln_b)


def _rope_swap_perm():
    idx = np.arange(MLA_DR)
    return idx ^ (MLA_DR // 4)


def _rope_tables(n_tok):
    rows = n_tok // GRID_W
    pos_r = jnp.repeat(jnp.arange(rows, dtype=F32), GRID_W)
    pos_c = (jnp.arange(rows * GRID_W) % GRID_W).astype(F32)
    n_freq = MLA_DR // 4
    inv = ROPE_BASE ** (-jnp.arange(n_freq, dtype=F32) / n_freq)
    ang = jnp.stack([pos_r[:, None] * inv, pos_c[:, None] * inv], axis=1)
    cos, sin = jnp.cos(ang), jnp.sin(ang)
    cos32 = jnp.stack([cos, cos], axis=2).reshape(n_tok, MLA_DR)
    sin32 = jnp.stack([-sin, sin], axis=2).reshape(n_tok, MLA_DR)
    pad_hi = LANES - MLA_DN - MLA_DR
    cos_t = jnp.concatenate([jnp.ones((n_tok, MLA_DN), F32), cos32, jnp.zeros((n_tok, pad_hi), F32)], axis=1)
    sin_t = jnp.concatenate([jnp.zeros((n_tok, MLA_DN), F32), sin32, jnp.zeros((n_tok, pad_hi), F32)], axis=1)
    return cos_t, sin_t


def _ab_weights(w_in, w_uq, w_ukv):
    perm = _rope_swap_perm()
    pad_hi = LANES - MLA_DN - MLA_DR
    w_h = w_in[:, :5 * HA_W]
    cq = w_in[:, 5 * HA_W:5 * HA_W + MLA_Q_RANK]
    ckv = w_in[:, 5 * HA_W + MLA_Q_RANK:5 * HA_W + MLA_Q_RANK + MLA_KV_RANK]
    kr = w_in[:, 5 * HA_W + MLA_Q_RANK + MLA_KV_RANK:]
    d = w_in.shape[0]
    z_lo = jnp.zeros((d, MLA_DN), F32)
    z_hi = jnp.zeros((d, pad_hi), F32)
    w_m = jnp.concatenate([cq, ckv, z_lo, kr, z_hi, z_lo, kr[:, perm], z_hi], axis=1)
    uq = w_uq.reshape(MLA_Q_RANK, MLA_HEADS, MLA_DN + MLA_DR)
    q_nope, q_rope = uq[..., :MLA_DN], uq[..., MLA_DN:]
    zq_hi = jnp.zeros((MLA_Q_RANK, MLA_HEADS, pad_hi), F32)
    wq1 = jnp.concatenate([q_nope, q_rope, zq_hi], axis=-1).reshape(MLA_Q_RANK, -1)
    wq2 = jnp.concatenate([jnp.zeros_like(q_nope), q_rope[..., perm], zq_hi], axis=-1).reshape(MLA_Q_RANK, -1)
    ukv = w_ukv.reshape(MLA_KV_RANK, MLA_HEADS, MLA_DN + MLA_DV)
    k_nope, v = ukv[..., :MLA_DN], ukv[..., MLA_DN:]
    wk = jnp.concatenate([k_nope, jnp.zeros((MLA_KV_RANK, MLA_HEADS, LANES - MLA_DN), F32)], axis=-1)
    wk = wk.reshape(MLA_KV_RANK, -1)
    wv = v.reshape(MLA_KV_RANK, MLA_HEADS * MLA_DV).T
    bf = lambda a: a.astype(BF16)
    return bf(w_h), bf(w_m), bf(wq1), bf(wq2), bf(wk), bf(wv)


def _moe_weights(w_group, w_expert, w_gate, w_up, w_down):
    d = w_group.shape[0]
    wr = jnp.concatenate([w_expert, w_group, jnp.zeros((d, LANES - N_EXPERTS - N_GROUPS), F32)], axis=1)
    wr_hi = wr.astype(BF16)
    wr = jnp.concatenate([wr_hi, (wr - wr_hi.astype(F32)).astype(BF16)], axis=1)
    wg = jnp.transpose(w_gate, (1, 0, 2)).reshape(d, N_EXPERTS * EXP_FF).astype(BF16)
    wu = jnp.transpose(w_up, (1, 0, 2)).reshape(d, N_EXPERTS * EXP_FF).astype(BF16)
    wd = w_down.reshape(N_EXPERTS * EXP_FF, d).astype(BF16)
    return wr, wg, wu, wd


def kernel(x_prompt, x_sample, state_hgrn_fwd, state_hgrn_bwd, cache_mla_ckv, cache_mla_krope, c, c_ctx, mod_w, mod_b, ln_g, ln_b, ab_w_in, ab_w_out, hgrn_lb_logits, hgrn_norm_g, mla_q_norm_g, mla_w_uq, mla_kv_norm_g, mla_w_ukv, cd_w_in, cd_w_out, sc_conv_w, cf_conv_w, cf_conv_b, cf_ln_g, cf_ln_b, moe_w_group, moe_w_expert, moe_w_gate, moe_w_up, moe_w_down):
    dec_b = x_sample.shape[0]
    d = D_MODEL
    cvec = jnp.concatenate([c, c_ctx[None, :], jnp.zeros((SUBLANES - dec_b - 1, d), F32)], axis=0)
    mods = _mod_vectors(cvec, mod_w, mod_b)
    rope_tabs = _rope_tables(x_sample.shape[1])
    xp, xs = x_prompt, x_sample
    new_sf = new_sb = new_ckv = new_kr = None
    for l in range(DEPTH):
        mod_lat = mods[l, :dec_b].reshape(dec_b, 6, d)
        mod_ctx = mods[l, dec_b:dec_b + 1].reshape(1, 6, d)
        row = lambda a: a.reshape(1, -1)
        if l % 2 == 0:
            e = l // 2
            w_h, w_m, wq1, wq2, wk, wv = _ab_weights(ab_w_in[e], mla_w_uq[e], mla_w_ukv[e])
            w_out = ab_w_out[e].astype(BF16)
            qg, kvg, gn = row(mla_q_norm_g[e]), row(mla_kv_norm_g[e]), row(hgrn_norm_g[e])
            ph_p, pm_p = _inproj(xp, mod_ctx, [w_h, w_m])
            of_p, ob_p, sf, sb = _hgrn(ph_p, hgrn_lb_logits, None, None, e)
            q_p, k_p, v_p, ckv_p = _mla_proj(pm_p, None, qg, kvg, wq1, None, wk, wv)
            om_p = _attention(q_p, k_p, v_p, None, None)
            xp = _outproj(xp, mod_ctx, (of_p, ob_p, ph_p, gn), om_p, w_out, row(ln_g[l, 0]), row(ln_b[l, 0]), True)
            ph_s, pm_s = _inproj(xs, mod_lat, [w_h, w_m])
            of_s, ob_s, _, _ = _hgrn(ph_s, hgrn_lb_logits, state_hgrn_fwd[:, e], state_hgrn_bwd[:, e], e)
            q_s, k_s, v_s, _ = _mla_proj(pm_s, rope_tabs, qg, kvg, wq1, wq2, wk, wv)
            kc, vc = _mla_ctx(cache_mla_ckv[:, e], cache_mla_krope[:, e], wk, wv)
            om_s = _attention(q_s, k_s, v_s, kc, vc)
            xs = _outproj(xs, mod_lat, (of_s, ob_s, ph_s, gn), om_s, w_out, row(ln_g[l, 0]), row(ln_b[l, 0]), True)
            kr0 = MLA_Q_RANK + MLA_KV_RANK + MLA_DN
            new_sf, new_sb, new_ckv = sf, sb, ckv_p
            new_kr = pm_p[:, :, kr0:kr0 + MLA_DR]
        else:
            jx = l // 2
            w1 = cd_w_in[jx].astype(BF16)
            w_out = cd_w_out[jx].astype(BF16)
            cd = (sc_conv_w[jx], cf_conv_w[jx], row(cf_conv_b[jx]), row(cf_ln_g[jx]), row(cf_ln_b[jx]))
            (p1_p,) = _inproj(xp, mod_ctx, [w1])
            ysc_p, ycf_p = _conv_mixers(p1_p, *cd)
            xp = _outproj(xp, mod_ctx, ysc_p, ycf_p, w_out, row(ln_g[l, 0]), row(ln_b[l, 0]), False)
            (p1_s,) = _inproj(xs, mod_lat, [w1])
            ysc_s, ycf_s = _conv_mixers(p1_s, *cd)
            xs = _outproj(xs, mod_lat, ysc_s, ycf_s, w_out, row(ln_g[l, 0]), row(ln_b[l, 0]), False)
        wr, wg, wu, wd = _moe_weights(moe_w_group[l], moe_w_expert[l], moe_w_gate[l], moe_w_up[l], moe_w_down[l])
        xp_t = xp.reshape(-1, MOE_TILE, d)
        xp = _moe(xp_t, mod_ctx, wr, wg, wu, wd, row(ln_g[l, 1]), row(ln_b[l, 1])).reshape(xp.shape)
        xs = _moe(xs, mod_lat, wr, wg, wu, wd, row(ln_g[l, 1]), row(ln_b[l, 1]))
    return (xp, xs, new_sf[:, None], new_sb[:, None], new_ckv[:, None], new_kr[:, None])
```

```python
import functools

import numpy as np
import jax
import jax.numpy as jnp
from jax import lax
from jax.experimental import pallas as pl
from jax.experimental.pallas import tpu as pltpu

F32 = jnp.float32
BF16 = jnp.bfloat16
HIGHEST = lax.Precision.HIGHEST

D_MODEL = 1024
DEPTH = 2
GRID_W = 64
N_EVEN = (DEPTH + 1) // 2
HA_HEADS = 4
HA_DK = 128
HA_DV = 128
HA_W = HA_HEADS * HA_DK
CHUNK = 32
MLA_HEADS = 8
MLA_DN = 64
MLA_DR = 32
MLA_DV = 64
MLA_Q_RANK = 384
MLA_KV_RANK = 256
ROPE_BASE = 10000.0
SC_W = 512
SC_K = 3
CF_W = 512
CF_K = 31
N_GROUPS = 4
EXP_PER_GROUP = 8
N_EXPERTS = N_GROUPS * EXP_PER_GROUP
EXP_FF = 128
ALPHA = (2.0 * DEPTH) ** 0.25
LOG2_E = 1.4426950408889634
LN_EPS = 1e-5
RMS_EPS = 1e-6

LANES = 128
SUBLANES = 8
VMEM_LIMIT = 56 * 1024 * 1024

HEAD_PAD = LANES
MLA_SLAB = MLA_Q_RANK + MLA_KV_RANK + 2 * LANES
TOK_TILE = 256
GROUP_ROWS = 128
N_LEVELS = 5
SAFE_LOG_DECAY = 60.0
MOE_TILE = 512
MOE_BLOCK = 128
HALO = 16


def _cparams(sem):
    return pltpu.CompilerParams(dimension_semantics=sem, vmem_limit_bytes=VMEM_LIMIT)


def _silu(x):
    return x / (1.0 + jnp.exp(-x))


def _sigmoid(x):
    return 1.0 / (1.0 + jnp.exp(-x))


def _layer_norm(r, g, b):
    mu = jnp.mean(r, axis=-1, keepdims=True)
    d = r - mu
    var = jnp.mean(d * d, axis=-1, keepdims=True)
    return d * lax.rsqrt(var + LN_EPS) * g + b


def _dot(a, b):
    return jnp.dot(a, b, preferred_element_type=F32)


def _dot_nt(a, b):
    return lax.dot_general(a, b, (((1,), (1,)), ((), ())), preferred_element_type=F32)


def _dot_tn(a, b):
    return lax.dot_general(a, b, (((0,), (0,)), ((), ())), preferred_element_type=F32)


def _mod_kernel(c_ref, w_ref, b_ref, o_ref):
    s = _silu(c_ref[...])
    o_ref[0] = jnp.dot(s, w_ref[0], precision=HIGHEST, preferred_element_type=F32) + b_ref[0]


def _mod_vectors(cvec, mod_w, mod_b):
    n_out = mod_w.shape[-1]
    tn = 1536
    return pl.pallas_call(
        _mod_kernel,
        out_shape=jax.ShapeDtypeStruct((DEPTH, SUBLANES, n_out), F32),
        grid=(DEPTH, n_out // tn),
        in_specs=[
            pl.BlockSpec((SUBLANES, D_MODEL), lambda l, j: (0, 0)),
            pl.BlockSpec((1, D_MODEL, tn), lambda l, j: (l, 0, j)),
            pl.BlockSpec((1, 1, tn), lambda l, j: (l, 0, j)),
        ],
        out_specs=pl.BlockSpec((1, SUBLANES, tn), lambda l, j: (l, 0, j)),
        compiler_params=_cparams(("arbitrary", "arbitrary")),
        name="mod_vectors",
    )(cvec, mod_w, mod_b.reshape(DEPTH, 1, n_out))


def _inproj_kernel(*refs, n_w):
    x_ref, mod_ref = refs[0], refs[1]
    w_refs = refs[2:2 + n_w]
    o_refs = refs[2 + n_w:]
    m = mod_ref[0]
    h = (x_ref[0] * (1.0 + m[1:2]) + m[0:1]).astype(BF16)
    for w_ref, o_ref in zip(w_refs, o_refs):
        o_ref[0] = _dot(h, w_ref[...])


def _inproj(x, mod, weights):
    b, l, d = x.shape
    tm = TOK_TILE
    per_batch = mod.shape[0] > 1
    mod_map = (lambda i, j: (i, 0, 0)) if per_batch else (lambda i, j: (0, 0, 0))
    in_specs = [pl.BlockSpec((1, tm, d), lambda i, j: (i, j, 0)),
                pl.BlockSpec((1, 6, d), mod_map)]
    in_specs += [pl.BlockSpec(w.shape, lambda i, j: (0, 0)) for w in weights]
    out_shape = [jax.ShapeDtypeStruct((b, l, w.shape[1]), F32) for w in weights]
    out_specs = [pl.BlockSpec((1, tm, w.shape[1]), lambda i, j: (i, j, 0)) for w in weights]
    return pl.pallas_call(
        functools.partial(_inproj_kernel, n_w=len(weights)),
        out_shape=out_shape,
        grid=(b, l // tm),
        in_specs=in_specs,
        out_specs=out_specs,
        compiler_params=_cparams(("parallel", "parallel")),
        name="modulate_inproj",
    )(x, mod, *weights)


def _hgrn_tables():
    n = GROUP_ROWS
    t = np.arange(n)[:, None]
    j = np.arange(n)[None, :]
    same_chunk = (t // CHUNK) == (j // CHUNK)
    e_f, e_b = [], []
    lv_f = np.full((n, n), -1, np.int32)
    for lvl in range(N_LEVELS):
        m = CHUNK >> (lvl + 1)
        blk0 = (t // (2 * m)) * (2 * m)
        r = blk0 + m - 1
        upper = t > r
        ef = np.where(upper, (j > r) & (j <= t), (j > t) & (j <= r))
        r2 = blk0 + m
        lower = t < r2
        eb = np.where(lower, (j >= t) & (j < r2), (j >= r2) & (j < t))
        e_f.append(ef)
        e_b.append(eb)
        s = np.arange(n)[None, :]
        same_blk = (t // (2 * m)) == (s // (2 * m))
        q_side = (t % (2 * m)) >= m
        k_side = (s % (2 * m)) < m
        lv_f[same_blk & q_side & k_side] = lvl
    lv_f[np.arange(n), np.arange(n)] = N_LEVELS
    e_f = np.concatenate(e_f, axis=0).astype(np.float32)
    e_b = np.concatenate(e_b, axis=0).astype(np.float32)
    e = np.stack([e_f, e_b])
    cum = np.stack([same_chunk & (j <= t), same_chunk & (j >= t)]).astype(np.float32)
    lv = np.stack([lv_f, lv_f.T])
    return e, cum, lv


def _hgrn_kernel(qf_ref, vf_ref, ff_ref, qb_ref, vb_ref, fb_ref, lbl_ref, e_ref, cum_ref, lv_ref, s0f_ref, s0b_ref,
                 of_ref, ob_ref, sf_ref, sb_ref,
                 st_scr, qe_scr, kd_scr, sc_scr, v_scr, oi_scr, dec_scr, *, lt, slot, has_state):
    i = pl.program_id(1)
    n_i = pl.num_programs(1)
    n_chunks = lt // CHUNK
    n_groups = lt // GROUP_ROWS
    chunks_per_group = GROUP_ROWS // CHUNK

    @pl.when(i == 0)
    def _():
        for d, s0_ref in enumerate((s0f_ref, s0b_ref)):
            for h in range(HA_HEADS):
                if has_state:
                    st_scr[d, h] = s0_ref[0, h].T
                else:
                    st_scr[d, h] = jnp.zeros((HA_DV, HA_DK), F32)

    lg = lbl_ref[...]
    n_slots = lg.shape[0]
    mx = lg[0]
    for s in range(1, n_slots):
        mx = jnp.maximum(mx, lg[s])
    ex = [jnp.exp(lg[s] - mx) for s in range(n_slots)]
    den = ex[0]
    for s in range(1, n_slots):
        den = den + ex[s]
    num = ex[0]
    for s in range(1, slot + 1):
        num = num + ex[s]
    lb_all = num / den

    dirs = ((qf_ref, vf_ref, ff_ref), (qb_ref, vb_ref, fb_ref))
    head_cols = [slice(h * HA_DK, (h + 1) * HA_DK) for h in range(HA_HEADS)]

    def gates(d, rows, cols):
        q_ref, _, f_ref = dirs[d]
        q = _silu(q_ref[0, rows, cols])
        lb_h = lb_all[d:d + 1, cols]
        f = lb_h + (1.0 - lb_h) * _sigmoid(f_ref[0, rows, cols])
        g = jnp.log(f)
        g_hi = g.astype(BF16)
        g_lo = (g - g_hi.astype(F32)).astype(BF16)
        return q, 1.0 - f, g_hi, g_lo

    tot_min = jnp.zeros((1, HA_DK), F32)
    for d in range(2):

        def group_step(grp, tot_min, d=d):
            r0 = pl.multiple_of(grp * GROUP_ROWS, GROUP_ROWS)
            rows = pl.ds(r0, GROUP_ROWS)
            cum = cum_ref[d]
            lv = lv_ref[d]
            for cols in head_cols:
                q, k, g_hi, g_lo = gates(d, rows, cols)
                bcum = _dot(cum, g_hi) + _dot(cum, g_lo)
                v_scr[d, rows, cols] = dirs[d][1][0, rows, cols].astype(BF16)
                qe = (q * jnp.exp(bcum)).astype(BF16)
                qe_scr[d, rows, cols] = qe
                p = _dot_nt(qe, (k * jnp.exp(-bcum)).astype(BF16))
                sc_scr[d, rows, cols] = jnp.where(lv >= 0, p, 0.0).astype(BF16)
                for cc in range(chunks_per_group):
                    edge = cc * CHUNK + (CHUNK - 1 if d == 0 else 0)
                    tot = bcum[edge:edge + 1]
                    b_c = bcum[cc * CHUNK:(cc + 1) * CHUNK]
                    k_c = k[cc * CHUNK:(cc + 1) * CHUNK]
                    kd = (k_c * jnp.exp(tot - b_c)).astype(BF16)
                    kd_scr[d, pl.ds(r0 + cc * CHUNK, CHUNK), cols] = kd
                    dec_scr[d, grp * chunks_per_group + cc, :, cols] = jnp.exp(tot)
                    tot_min = jnp.minimum(tot_min, tot)
            return tot_min

        tot_min = lax.fori_loop(0, n_groups, group_step, tot_min)

    def one_factor():
        for d in range(2):
            for grp in range(n_groups):
                rows = slice(grp * GROUP_ROWS, (grp + 1) * GROUP_ROWS)
                for cols in head_cols:
                    oi_scr[d, rows, cols] = _dot(sc_scr[d, rows, cols], v_scr[d, rows, cols])

    def per_level():
        for d in range(2):

            def group_step(grp, carry, d=d):
                rows = pl.ds(pl.multiple_of(grp * GROUP_ROWS, GROUP_ROWS), GROUP_ROWS)
                lv = lv_ref[d]
                for cols in head_cols:
                    q, k, g_hi, g_lo = gates(d, rows, cols)
                    x = jnp.exp(_dot(e_ref[d], g_hi) + _dot(e_ref[d], g_lo))
                    sc = jnp.where(lv == N_LEVELS, _dot_nt(q.astype(BF16), k.astype(BF16)), 0.0)
                    for lvl in range(N_LEVELS):
                        xl = x[lvl * GROUP_ROWS:(lvl + 1) * GROUP_ROWS]
                        p = _dot_nt((q * xl).astype(BF16), (k * xl).astype(BF16))
                        sc = jnp.where(lv == lvl, p, sc)
                    oi_scr[d, rows, cols] = _dot(sc.astype(BF16), v_scr[d, rows, cols])
                return carry

            lax.fori_loop(0, n_groups, group_step, 0)

    lax.cond(jnp.min(tot_min) > -SAFE_LOG_DECAY, one_factor, per_level)

    out_refs = (of_ref, ob_ref)

    def chunk_step(c, carry):
        for d in range(2):
            cc = c if d == 0 else n_chunks - 1 - c
            r0 = pl.multiple_of(cc * CHUNK, CHUNK)
            rows = pl.ds(r0, CHUNK)
            for h in range(HA_HEADS):
                cols = slice(h * HA_DK, (h + 1) * HA_DK)
                st = st_scr[d, h]
                o_state = _dot_nt(qe_scr[d, rows, cols], st.astype(BF16))
                out_refs[d][0, rows, cols] = oi_scr[d, rows, cols] + o_state
                upd = _dot_tn(v_scr[d, rows, cols], kd_scr[d, rows, cols])
                st_scr[d, h] = st * dec_scr[d, cc, :, cols] + upd
        return carry

    lax.fori_loop(0, n_chunks, chunk_step, 0)

    @pl.when(i == n_i - 1)
    def _():
        for d, s_ref in enumerate((sf_ref, sb_ref)):
            for h in range(HA_HEADS):
                s_ref[0, h] = st_scr[d, h].T


def _hgrn(p_h, lb_logits, s0_f, s0_b, slot):
    b, l, _ = p_h.shape
    lt = TOK_TILE
    n_t = l // lt
    has_state = s0_f is not None
    if not has_state:
        s0_f = jnp.zeros((1, HA_HEADS, HA_DK, HA_DV), F32)
        s0_b = s0_f
    e_np, cum_np, lv_np = _hgrn_tables()
    e_mat = jnp.asarray(e_np, BF16)
    cum = jnp.asarray(cum_np, BF16)
    lv = jnp.asarray(lv_np, jnp.int32)
    w = HA_W

    def sec(idx, rev):
        if rev:
            return pl.BlockSpec((1, lt, w), lambda bi, i: (bi, n_t - 1 - i, idx))
        return pl.BlockSpec((1, lt, w), lambda bi, i: (bi, i, idx))

    state_map = (lambda bi, i: (bi, 0, 0, 0)) if has_state else (lambda bi, i: (0, 0, 0, 0))
    in_specs = [
        sec(0, False), sec(3, False), sec(1, False),
        sec(0, True), sec(3, True), sec(2, True),
        pl.BlockSpec(lb_logits.shape, lambda bi, i: (0, 0, 0)),
        pl.BlockSpec(e_mat.shape, lambda bi, i: (0, 0, 0)),
        pl.BlockSpec(cum.shape, lambda bi, i: (0, 0, 0)),
        pl.BlockSpec(lv.shape, lambda bi, i: (0, 0, 0)),
        pl.BlockSpec((1, HA_HEADS, HA_DK, HA_DV), state_map),
        pl.BlockSpec((1, HA_HEADS, HA_DK, HA_DV), state_map),
    ]
    out_shape = [
        jax.ShapeDtypeStruct((b, l, w), F32),
        jax.ShapeDtypeStruct((b, l, w), F32),
        jax.ShapeDtypeStruct((b, HA_HEADS, HA_DK, HA_DV), F32),
        jax.ShapeDtypeStruct((b, HA_HEADS, HA_DK, HA_DV), F32),
    ]
    out_specs = [
        pl.BlockSpec((1, lt, w), lambda bi, i: (bi, i, 0)),
        pl.BlockSpec((1, lt, w), lambda bi, i: (bi, n_t - 1 - i, 0)),
        pl.BlockSpec((1, HA_HEADS, HA_DK, HA_DV), lambda bi, i: (bi, 0, 0, 0)),
        pl.BlockSpec((1, HA_HEADS, HA_DK, HA_DV), lambda bi, i: (bi, 0, 0, 0)),
    ]
    scratch = [
        pltpu.VMEM((2, HA_HEADS, HA_DV, HA_DK), F32),
        pltpu.VMEM((2, lt, w), BF16),
        pltpu.VMEM((2, lt, w), BF16),
        pltpu.VMEM((2, lt, w), BF16),
        pltpu.VMEM((2, lt, w), BF16),
        pltpu.VMEM((2, lt, w), F32),
        pltpu.VMEM((2, lt // CHUNK, 1, w), F32),
    ]
    return pl.pallas_call(
        functools.partial(_hgrn_kernel, lt=lt, slot=slot, has_state=has_state),
        out_shape=out_shape,
        grid=(b, n_t),
        in_specs=in_specs,
        out_specs=out_specs,
        scratch_shapes=scratch,
        compiler_params=_cparams(("parallel", "arbitrary")),
        name="hgrn2_scan",
    )(p_h, p_h, p_h, p_h, p_h, p_h, lb_logits, e_mat, cum, lv, s0_f, s0_b)


def _mla_proj_kernel(*refs, rope):
    if rope:
        pm_ref, cos_ref, sin_ref, qg_ref, kvg_ref, wq1_ref, wq2_ref, wk_ref, wv_ref = refs[:9]
        q_ref, k_ref, v_ref, ckv_ref = refs[9:]
    else:
        pm_ref, qg_ref, kvg_ref, wq1_ref, wk_ref, wv_ref = refs[:6]
        q_ref, k_ref, v_ref, ckv_ref = refs[6:]
    scale = (MLA_DN + MLA_DR) ** -0.5 * LOG2_E
    pm = pm_ref[0]
    cq = pm[:, :MLA_Q_RANK]
    cq = cq * lax.rsqrt(jnp.mean(cq * cq, axis=-1, keepdims=True) + RMS_EPS) * qg_ref[...]
    cq = cq.astype(BF16)
    ckv = pm[:, MLA_Q_RANK:MLA_Q_RANK + MLA_KV_RANK]
    ckv = ckv * lax.rsqrt(jnp.mean(ckv * ckv, axis=-1, keepdims=True) + RMS_EPS) * kvg_ref[...]
    ckv_ref[0] = ckv
    ckv = ckv.astype(BF16)
    kr0 = MLA_Q_RANK + MLA_KV_RANK
    kr = pm[:, kr0:kr0 + LANES]
    qa = _dot(cq, wq1_ref[...])
    kn = _dot(ckv, wk_ref[...])
    v_ref[0] = _dot(ckv, wv_ref[...]).astype(BF16)
    if rope:
        cos = cos_ref[...]
        sin = sin_ref[...]
        qb = _dot(cq, wq2_ref[...])
        kr = kr * cos + pm[:, kr0 + LANES:kr0 + 2 * LANES] * sin
    for h in range(MLA_HEADS):
        sl = slice(h * HEAD_PAD, (h + 1) * HEAD_PAD)
        qh = qa[:, sl]
        if rope:
            qh = qh * cos + qb[:, sl] * sin
        q_ref[0, :, sl] = (qh * scale).astype(BF16)
        k_ref[0, :, sl] = (kn[:, sl] + kr).astype(BF16)


def _mla_proj(p_m, rope_tabs, qg, kvg, wq1, wq2, wk, wv):
    b, l, _ = p_m.shape
    tm = TOK_TILE
    rope = rope_tabs is not None
    full = lambda a: pl.BlockSpec(a.shape, lambda i, j: (0,) * a.ndim)
    hw = MLA_HEADS * HEAD_PAD
    inputs = [p_m]
    in_specs = [pl.BlockSpec((1, tm, MLA_SLAB), lambda i, j: (i, j, 0))]
    if rope:
        inputs += list(rope_tabs)
        in_specs += [pl.BlockSpec((tm, LANES), lambda i, j: (j, 0))] * 2
    ws = [qg, kvg, wq1] + ([wq2] if rope else []) + [wk, wv]
    inputs += ws
    in_specs += [full(a) for a in ws]
    out_shape = [jax.ShapeDtypeStruct((b, l, hw), BF16)] * 3 + [jax.ShapeDtypeStruct((b, l, MLA_KV_RANK), F32)]
    out_specs = [pl.BlockSpec((1, tm, hw), lambda i, j: (i, j, 0))] * 3 + [
        pl.BlockSpec((1, tm, MLA_KV_RANK), lambda i, j: (i, j, 0))]
    return pl.pallas_call(
        functools.partial(_mla_proj_kernel, rope=rope),
        out_shape=out_shape,
        grid=(b, l // tm),
        in_specs=in_specs,
        out_specs=out_specs,
        compiler_params=_cparams(("parallel", "parallel")),
        name="mla_proj",
    )(*inputs)


def _mla_ctx_kernel(ckv_ref, kr_ref, place_ref, wk_ref, wv_ref, k_ref, v_ref):
    ckv = ckv_ref[0].astype(BF16)
    kr = _dot(kr_ref[0].astype(BF16), place_ref[...])
    kn = _dot(ckv, wk_ref[...])
    v_ref[0] = _dot(ckv, wv_ref[...]).astype(BF16)
    for h in range(MLA_HEADS):
        sl = slice(h * HEAD_PAD, (h + 1) * HEAD_PAD)
        k_ref[0, :, sl] = (kn[:, sl] + kr).astype(BF16)


def _mla_ctx(ctx_ckv, ctx_kr, wk, wv):
    b, lc, _ = ctx_ckv.shape
    hw = MLA_HEADS * HEAD_PAD
    place =np.zeros((MLA_DR, LANES), np.float32)
    place[np.arange(MLA_DR), MLA_DN + np.arange(MLA_DR)] = 1.0
    place = jnp.asarray(place, BF16)
    full = lambda a: pl.BlockSpec(a.shape, lambda i: (0,) * a.ndim)
    return pl.pallas_call(
        _mla_ctx_kernel,
        out_shape=[jax.ShapeDtypeStruct((b, lc, hw), BF16)] * 2,
        grid=(b,),
        in_specs=[pl.BlockSpec((1, lc, MLA_KV_RANK), lambda i: (i, 0, 0)),
                  pl.BlockSpec((1, lc, MLA_DR), lambda i: (i, 0, 0)),
                  full(place), full(wk), full(wv)],
        out_specs=[pl.BlockSpec((1, lc, hw), lambda i: (i, 0, 0))] * 2,
        compiler_params=_cparams(("parallel",)),
        name="mla_ctx_keys",
    )(ctx_ckv, ctx_kr, place, wk, wv)


def _attn_kernel(*refs, has_ctx):
    if has_ctx:
        q_ref, k_ref, v_ref, kc_ref, vc_ref, o_ref = refs
    else:
        q_ref, k_ref, v_ref, o_ref = refs
    n_pairs = q_ref.shape[-1] // (2 * HEAD_PAD)
    for pair in range(n_pairs):
        out = None
        for hh in range(2):
            h0 = (2 * pair + hh) * HEAD_PAD
            sl = slice(h0, h0 + HEAD_PAD)
            q = q_ref[0, :, sl]
            s = _dot_nt(q, k_ref[0, :, sl])
            m = jnp.max(s, axis=-1, keepdims=True)
            if has_ctx:
                s2 = _dot_nt(q, kc_ref[0, :, sl])
                m = jnp.maximum(m, jnp.max(s2, axis=-1, keepdims=True))
            p = jnp.exp2(s - m)
            den = jnp.sum(p, axis=-1, keepdims=True)
            acc = _dot(p.astype(BF16), v_ref[0, :, sl])
            if has_ctx:
                p2 = jnp.exp2(s2 - m)
                den = den + jnp.sum(p2, axis=-1, keepdims=True)
                acc = acc + _dot(p2.astype(BF16), vc_ref[0, :, sl])
            o = acc / den
            out = o if out is None else out + o
        o_ref[0, :, pair * LANES:(pair + 1) * LANES] = out.astype(BF16)


def _attention(q, k, v, kc, vc):
    b, l, _ = q.shape
    lk = k.shape[1]
    tq = TOK_TILE
    pairs_per_step = MLA_HEADS // 2 if lk <= TOK_TILE else 1
    pw = 2 * HEAD_PAD * pairs_per_step
    has_ctx = kc is not None
    inputs = [q, k, v]
    in_specs = [pl.BlockSpec((1, tq, pw), lambda bi, p, i: (bi, i, p)),
                pl.BlockSpec((1, lk, pw), lambda bi, p, i: (bi, 0, p)),
                pl.BlockSpec((1, lk, pw), lambda bi, p, i: (bi, 0, p))]
    if has_ctx:
        lc = kc.shape[1]
        inputs += [kc, vc]
        in_specs += [pl.BlockSpec((1, lc, pw), lambda bi, p, i: (bi, 0, p))] * 2
    return pl.pallas_call(
        functools.partial(_attn_kernel, has_ctx=has_ctx),
        out_shape=jax.ShapeDtypeStruct((b, l, MLA_HEADS * MLA_DV), BF16),
        grid=(b, MLA_HEADS // 2 // pairs_per_step, l // tq),
        in_specs=in_specs,
        out_specs=pl.BlockSpec((1, tq, LANES * pairs_per_step), lambda bi, p, i: (bi, i, p)),
        compiler_params=_cparams(("parallel", "parallel", "arbitrary")),
        name="mla_attention",
    )(*inputs)


def _outproj_kernel(*refs, hgrn):
    if hgrn:
        x_ref, mod_ref, of_ref, ob_ref, gl_ref, gn_ref, b_ref, w_ref, lg_ref, lb_ref, o_ref = refs
        o = of_ref[0] + ob_ref[0]
        gate = _silu(gl_ref[0])
        gn = gn_ref[...]
        parts = []
        for h in range(HA_HEADS):
            sl = slice(h * HA_DV, (h + 1) * HA_DV)
            oh = o[:, sl]
            oh = oh * lax.rsqrt(jnp.mean(oh * oh, axis=-1, keepdims=True) + RMS_EPS) * gn
            parts.append((oh * gate[:, sl]).astype(BF16))
        a = jnp.concatenate(parts, axis=-1)
    else:
        x_ref, mod_ref, a_ref, b_ref, w_ref, lg_ref, lb_ref, o_ref = refs
        a = a_ref[0]
    half = a.shape[-1]
    y = _dot(a, w_ref[:half, :]) + _dot(b_ref[0], w_ref[half:, :])
    m = mod_ref[0]
    r = ALPHA * x_ref[0] + m[2:3] * y
    o_ref[0] = _layer_norm(r, lg_ref[...], lb_ref[...])


def _outproj(x, mod, a_inputs, b_in, w, ln_g, ln_b, hgrn):
    b, l, d = x.shape
    tm = TOK_TILE
    per_batch = mod.shape[0] > 1
    mod_map = (lambda i, j: (i, 0, 0)) if per_batch else (lambda i, j: (0, 0, 0))
    row = lambda width: pl.BlockSpec((1, tm, width), lambda i, j: (i, j, 0))
    full = lambda a: pl.BlockSpec(a.shape, lambda i, j: (0,) * a.ndim)
    inputs = [x, mod]
    in_specs = [row(d), pl.BlockSpec((1, 6, d), mod_map)]
    if hgrn:
        o_f, o_b, p_h, g_norm = a_inputs
        inputs += [o_f, o_b, p_h, g_norm]
        in_specs += [row(HA_W), row(HA_W),
                     pl.BlockSpec((1, tm, HA_W), lambda i, j: (i, j, 4)), full(g_norm)]
    else:
        inputs += [a_inputs]
        in_specs += [row(a_inputs.shape[-1])]
    inputs += [b_in, w, ln_g, ln_b]
    in_specs += [row(b_in.shape[-1]), full(w), full(ln_g), full(ln_b)]
    return pl.pallas_call(
        functools.partial(_outproj_kernel, hgrn=hgrn),
        out_shape=jax.ShapeDtypeStruct((b, l, d), F32),
        grid=(b, l // tm),
        in_specs=in_specs,
        out_specs=row(d),
        compiler_params=_cparams(("parallel", "parallel")),
        name="outproj_ln",
    )(*inputs)


def _conv_kernel(pm_ref, pp_ref, pn_ref, scw_ref, cfw_ref, cfb_ref, cfg_ref, cfbeta_ref,
                 ysc_ref, ycf_ref, ext_sc, ext_cf, *, lt):
    i = pl.program_id(1)
    n_i = pl.num_programs(1)
    w = SC_W

    def sc_in(p):
        return p[:, w:2 * w] * p[:, 2 * w:3 * w]

    def cf_in(p):
        return p[:, 3 * w:3 * w + CF_W] * _sigmoid(p[:, 3 * w + CF_W:3 * w + 2 * CF_W])

    pm = pm_ref[0]
    pp = pp_ref[0]
    pn = pn_ref[0]
    has_prev = i > 0
    has_next = i < n_i - 1
    n_ext = lt + 2 * HALO
    for ext, conv_in in ((ext_sc, sc_in), (ext_cf, cf_in)):
        ext[0, 0:HALO, :] = jnp.where(has_prev, conv_in(pp), 0.0)
        ext[0, HALO:HALO + lt, :] = conv_in(pm)
        ext[0, HALO + lt:, :] = jnp.where(has_next, conv_in(pn), 0.0)
    for s in range(1, SUBLANES):
        ext_cf[s, 0:n_ext - SUBLANES, :] = ext_cf[0, s:s + n_ext - SUBLANES, :]
    sc_shifts = sorted({(HALO - SC_K // 2 + j) % SUBLANES for j in range(SC_K)} - {0})
    for s in sc_shifts:
        ext_sc[s, 0:n_ext - SUBLANES, :] = ext_sc[0, s:s + n_ext - SUBLANES, :]

    def tap(ext, off, rows):
        s = off % SUBLANES
        return ext[s, off - s:off - s + rows, :]

    rb = 32
    for r in range(lt // rb):
        base = HALO + r * rb
        acc = None
        for j in range(SC_K):
            term = jnp.tile(scw_ref[j], (rb // SUBLANES, 1)) * tap(ext_sc, base - SC_K // 2 + j, rb)
            acc = term if acc is None else acc + term
        ysc_ref[0, r * rb:(r + 1) * rb, :] = (pm[r * rb:(r + 1) * rb, 0:w] * acc).astype(BF16)
        acc = None
        for j in range(CF_K):
            term = jnp.tile(cfw_ref[j], (rb // SUBLANES, 1)) * tap(ext_cf, base - CF_K // 2 + j, rb)
            acc = term if acc is None else acc + term
        u = _layer_norm(acc + cfb_ref[...], cfg_ref[...], cfbeta_ref[...])
        ycf_ref[0, r * rb:(r + 1) * rb, :] = _silu(u).astype(BF16)


def _conv_mixers(p1, sc_w, cf_w, cf_b, cf_g, cf_beta):
    b, l, width = p1.shape
    lt = TOK_TILE
    hb = lt // HALO
    n_h = l // HALO
    sc_w = jnp.broadcast_to(sc_w[:, None, :], (sc_w.shape[0], SUBLANES, sc_w.shape[1]))
    cf_w = jnp.broadcast_to(cf_w[:, None, :], (cf_w.shape[0], SUBLANES, cf_w.shape[1]))
    full = lambda a: pl.BlockSpec(a.shape, lambda bi, i: (0,) * a.ndim)
    return pl.pallas_call(
        functools.partial(_conv_kernel, lt=lt),
        out_shape=[jax.ShapeDtypeStruct((b, l, SC_W), BF16), jax.ShapeDtypeStruct((b, l, CF_W), BF16)],
        grid=(b, l // lt),
        in_specs=[
            pl.BlockSpec((1, lt, width), lambda bi, i: (bi, i, 0)),
            pl.BlockSpec((1, HALO, width), lambda bi, i: (bi, jnp.maximum(i * hb - 1, 0), 0)),
            pl.BlockSpec((1, HALO, width), lambda bi, i: (bi, jnp.minimum((i + 1) * hb, n_h - 1), 0)),
            full(sc_w), full(cf_w), full(cf_b), full(cf_g), full(cf_beta),
        ],
        out_specs=[pl.BlockSpec((1, lt, SC_W), lambda bi, i: (bi, i, 0)),
                   pl.BlockSpec((1, lt, CF_W), lambda bi, i: (bi, i, 0))],
        scratch_shapes=[pltpu.VMEM((SUBLANES, lt + 2 * HALO, SC_W), F32),
                        pltpu.VMEM((SUBLANES, lt + 2 * HALO, CF_W), F32)],
        compiler_params=_cparams(("parallel", "parallel")),
        name="conv_mixers",
    )(p1, p1, p1, sc_w, cf_w, cf_b, cf_g, cf_beta)


def _route_t(lt):
    t = lt.shape[1]
    row = lax.broadcasted_iota(jnp.int32, (SUBLANES, t), 0).astype(F32)
    neg = -jnp.inf
    big = float(LANES)
    gl = jnp.where(row < N_GROUPS, lt[N_EXPERTS:N_EXPERTS + SUBLANES], neg)
    gmax = jnp.max(gl, axis=0, keepdims=True)
    p_g = 1.0 / jnp.sum(jnp.exp(gl - gmax), axis=0, keepdims=True)
    g_sel = jnp.min(jnp.where(gl == gmax, row, big), axis=0, keepdims=True)
    el = lt[0:EXP_PER_GROUP]
    for gi in range(1, N_GROUPS):
        el = jnp.where(g_sel == gi, lt[gi * EXP_PER_GROUP:(gi + 1) * EXP_PER_GROUP], el)
    v1 = jnp.max(el, axis=0, keepdims=True)
    i1 = jnp.min(jnp.where(el == v1, row, big), axis=0, keepdims=True)
    el2 = jnp.where(row == i1, neg, el)
    v2 = jnp.max(el2, axis=0, keepdims=True)
    i2 = jnp.min(jnp.where(el2 == v2, row, big), axis=0, keepdims=True)
    e2 = jnp.exp(v2 - v1)
    w1 = p_g / (1.0 + e2)
    w2 = p_g * e2 / (1.0 + e2)
    comb = jnp.where(row == i1, w1, 0.0) + jnp.where(row == i2, w2, 0.0)
    onehot = jnp.where(row == g_sel, 1.0, 0.0)
    return onehot, comb


def _moe_kernel(x_ref, mod_ref, wr_ref, tri_ref, exp_ref, wg_ref, wu_ref, wd_ref, lg_ref, lb_ref, o_ref,
                hb_scr, hs_scr, os_scr, pt_scr, cs_scr):
    m = mod_ref[0]
    tm, cap = pt_scr.shape
    d = x_ref.shape[-1]
    half = d // 2
    cw = EXP_PER_GROUP * EXP_FF

    h = x_ref[0] * (1.0 + m[4:5]) + m[3:4]
    h_hi = h.astype(BF16)
    hb_scr[...] = h_hi
    h_lo = (h - h_hi.astype(F32)).astype(BF16)
    l2 = _dot(h_hi, wr_ref[...])
    logits = l2[:, :LANES] + l2[:, LANES:] + _dot(h_lo, wr_ref[:, :LANES])
    onehot, comb = _route_t(logits.T)
    rank = _dot(onehot.astype(BF16), tri_ref[...])
    cnt = jnp.sum(onehot, axis=1, keepdims=True)
    padded = jnp.floor((cnt + (MOE_BLOCK - 1)) * (1.0 / MOE_BLOCK)) * MOE_BLOCK
    start = jnp.zeros((1, 1), F32)
    dest = jnp.zeros((1, tm), F32)
    first, count = [], []
    for gi in range(N_GROUPS):
        dest = dest + onehot[gi:gi + 1] * (start + rank[gi:gi + 1])
        first.append((start[0, 0] * (1.0 / MOE_BLOCK)).astype(jnp.int32))
        count.append((padded[gi, 0] * (1.0 / MOE_BLOCK)).astype(jnp.int32))
        start = start + padded[gi:gi + 1]
    aux = jnp.concatenate([jnp.broadcast_to(dest, (SUBLANES, tm)), comb,
                           jnp.zeros((LANES - 2 * SUBLANES, tm), F32)], axis=0).T
    lane = lax.broadcasted_iota(jnp.int32, (tm, LANES), 1)
    comb_tok = jnp.where((lane >= SUBLANES) & (lane < 2 * SUBLANES), aux, 0.0)
    c_hi = comb_tok.astype(BF16)
    c_mid = (comb_tok - c_hi.astype(F32)).astype(BF16)
    slot_t = lax.broadcasted_iota(jnp.int32, (tm, cap), 1).astype(F32)
    pt_scr[...] = jnp.where(slot_t == aux[:, 0:1], 1.0, 0.0).astype(BF16)
    slot = lax.broadcasted_iota(jnp.int32, (cap, tm), 0).astype(F32)
    perm = jnp.where(slot == dest, 1.0, 0.0).astype(BF16)
    hs_scr[:, :half] = _dot(perm, hb_scr[:, :half]).astype(BF16)
    hs_scr[:, half:] = _dot(perm, hb_scr[:, half:]).astype(BF16)
    cs_scr[...] = _dot(perm, jnp.concatenate([c_hi, c_mid], axis=1)).astype(BF16)
    os_scr[...] = jnp.zeros_like(os_scr)

    for gi in range(N_GROUPS):
        cols = slice(gi * cw, (gi + 1) * cw)

        def block(i, carry, gi=gi, cols=cols):
            r0 = pl.multiple_of((first[gi] + i) * MOE_BLOCK, MOE_BLOCK)
            rows = pl.ds(r0, MOE_BLOCK)
            hs = hs_scr[rows, :]
            ex = exp_ref[...]
            combx = _dot(cs_scr[rows, :LANES], ex) + _dot(cs_scr[rows, LANES:], ex)
            hid = _silu(_dot(hs, wg_ref[:, cols])) * _dot(hs, wu_ref[:, cols])
            os_scr[rows, :] = _dot((hid * combx).astype(BF16), wd_ref[cols, :]).astype(BF16)
            return carry

        lax.fori_loop(0, count[gi], block, 0)

    y = jnp.concatenate([_dot(pt_scr[...], os_scr[:, :half]), _dot(pt_scr[...], os_scr[:, half:])], axis=1)
    r = ALPHA * x_ref[0] + m[5:6] * y
    o_ref[0] = _layer_norm(r, lg_ref[...], lb_ref[...])


def _moe(x, mod, wr, wg, wu, wd, ln_g, ln_b):
    b, l, d = x.shape
    tm = MOE_TILE
    cap = (tm + N_GROUPS * (MOE_BLOCK - 1)) // MOE_BLOCK * MOE_BLOCK
    cw = EXP_PER_GROUP * EXP_FF
    per_batch = mod.shape[0] > 1
    mod_map = (lambda i, t: (i, 0, 0)) if per_batch else (lambda i, t: (0, 0, 0))
    expand = np.zeros((LANES, cw), np.float32)
    for e in range(EXP_PER_GROUP):
        expand[SUBLANES + e, e * EXP_FF:(e + 1) * EXP_FF] = 1.0
    expand = jnp.asarray(expand, BF16)
    tri = jnp.asarray(np.triu(np.ones((tm, tm), np.float32), k=1), BF16)
    full = lambda a: pl.BlockSpec(a.shape, lambda i, t: (0,) * a.ndim, pipeline_mode=pl.Buffered(1))
    return pl.pallas_call(
        _moe_kernel,
        out_shape=jax.ShapeDtypeStruct((b, l, d), F32),
        grid=(b, l // tm),
        in_specs=[
            pl.BlockSpec((1, tm, d), lambda i, t: (i, t, 0)),
            pl.BlockSpec((1, 6, d), mod_map),
            full(wr), full(tri), full(expand), full(wg), full(wu), full(wd), full(ln_g), full(ln_b),
        ],
        out_specs=pl.BlockSpec((1, tm, d), lambda i, t: (i, t, 0)),
        scratch_shapes=[pltpu.VMEM((tm, d), BF16), pltpu.VMEM((cap, d), BF16), pltpu.VMEM((cap, d), BF16),
                        pltpu.VMEM((tm, cap), BF16), pltpu.VMEM((cap, 2 * LANES), BF16)],
        compiler_params=_cparams(("arbitrary", "arbitrary")),
        name="hier_moe_ln",
    )(x, mod, wr, tri, expand, wg, wu, wd, ln_g, ln_b)


def _rope_swap_perm():
    idx = np.arange(MLA_DR)
    return idx ^ (MLA_DR // 4)


def _rope_tables(n_tok):
    rows = n_tok // GRID_W
    pos_r = jnp.repeat(jnp.arange(rows, dtype=F32), GRID_W)
    pos_c = (jnp.arange(rows * GRID_W) % GRID_W).astype(F32)
    n_freq = MLA_DR // 4
    inv = ROPE_BASE ** (-jnp.arange(n_freq, dtype=F32) / n_freq)
    ang = jnp.stack([pos_r[:, None] * inv, pos_c[:, None] * inv], axis=1)
    cos, sin = jnp.cos(ang), jnp.sin(ang)
    cos32 = jnp.stack([cos, cos], axis=2).reshape(n_tok, MLA_DR)
    sin32 = jnp.stack([-sin, sin], axis=2).reshape(n_tok, MLA_DR)
    pad_hi = LANES - MLA_DN - MLA_DR
    cos_t = jnp.concatenate([jnp.ones((n_tok, MLA_DN), F32), cos32, jnp.zeros((n_tok, pad_hi), F32)], axis=1)
    sin_t = jnp.concatenate([jnp.zeros((n_tok, MLA_DN), F32), sin32, jnp.zeros((n_tok, pad_hi), F32)], axis=1)
    return cos_t, sin_t


def _ab_weights(w_in, w_uq, w_ukv):
    perm = _rope_swap_perm()
    pad_hi = LANES - MLA_DN - MLA_DR
    w_h = w_in[:, :5 * HA_W]
    cq = w_in[:, 5 * HA_W:5 * HA_W + MLA_Q_RANK]
    ckv = w_in[:, 5 * HA_W + MLA_Q_RANK:5 * HA_W + MLA_Q_RANK + MLA_KV_RANK]
    kr = w_in[:, 5 * HA_W + MLA_Q_RANK + MLA_KV_RANK:]
    d = w_in.shape[0]
    z_lo = jnp.zeros((d, MLA_DN), F32)
    z_hi = jnp.zeros((d, pad_hi), F32)
    w_m = jnp.concatenate([cq, ckv, z_lo, kr, z_hi, z_lo, kr[:, perm], z_hi], axis=1)
    uq = w_uq.reshape(MLA_Q_RANK, MLA_HEADS, MLA_DN + MLA_DR)
    q_nope, q_rope = uq[..., :MLA_DN], uq[..., MLA_DN:]
    zq_hi = jnp.zeros((MLA_Q_RANK, MLA_HEADS, pad_hi), F32)
    wq1 = jnp.concatenate([q_nope, q_rope, zq_hi], axis=-1).reshape(MLA_Q_RANK, -1)
    wq2 = jnp.concatenate([jnp.zeros_like(q_nope), q_rope[..., perm], zq_hi], axis=-1).reshape(MLA_Q_RANK, -1)
    ukv = w_ukv.reshape(MLA_KV_RANK, MLA_HEADS, MLA_DN + MLA_DV)
    k_nope, v = ukv[..., :MLA_DN], ukv[..., MLA_DN:]
    wk = jnp.concatenate([k_nope, jnp.zeros((MLA_KV_RANK, MLA_HEADS, LANES - MLA_DN), F32)], axis=-1)
    wk = wk.reshape(MLA_KV_RANK, -1)
    v_pairs = v.reshape(MLA_KV_RANK, MLA_HEADS // 2, 2, MLA_DV)
    zv = jnp.zeros_like(v_pairs[:, :, 0])
    wv = jnp.stack([jnp.concatenate([v_pairs[:, :, 0], zv], axis=-1),
                    jnp.concatenate([zv, v_pairs[:, :, 1]], axis=-1)], axis=2).reshape(MLA_KV_RANK, -1)
    bf = lambda a: a.astype(BF16)
    return bf(w_h), bf(w_m), bf(wq1), bf(wq2), bf(wk), bf(wv)


def _moe_weights(w_group, w_expert, w_gate, w_up, w_down):
    d = w_group.shape[0]
    wr = jnp.concatenate([w_expert, w_group, jnp.zeros((d, LANES - N_EXPERTS - N_GROUPS), F32)], axis=1)
    wr_hi = wr.astype(BF16)
    wr = jnp.concatenate([wr_hi, (wr - wr_hi.astype(F32)).astype(BF16)], axis=1)
    wg = jnp.transpose(w_gate, (1, 0, 2)).reshape(d, N_EXPERTS * EXP_FF).astype(BF16)
    wu = jnp.transpose(w_up, (1, 0, 2)).reshape(d, N_EXPERTS * EXP_FF).astype(BF16)
    wd = w_down.reshape(N_EXPERTS * EXP_FF, d).astype(BF16)
    return wr, wg, wu, wd


def kernel(x_prompt, x_sample, state_hgrn_fwd, state_hgrn_bwd, cache_mla_ckv, cache_mla_krope, c, c_ctx, mod_w, mod_b, ln_g, ln_b, ab_w_in, ab_w_out, hgrn_lb_logits, hgrn_norm_g, mla_q_norm_g, mla_w_uq, mla_kv_norm_g, mla_w_ukv, cd_w_in, cd_w_out, sc_conv_w, cf_conv_w, cf_conv_b, cf_ln_g, cf_ln_b, moe_w_group, moe_w_expert, moe_w_gate, moe_w_up, moe_w_down):
    dec_b = x_sample.shape[0]
    d = D_MODEL
    cvec = jnp.concatenate([c, c_ctx[None, :], jnp.zeros((SUBLANES - dec_b - 1, d), F32)], axis=0)
    mods = _mod_vectors(cvec, mod_w, mod_b)
    rope_tabs = _rope_tables(x_sample.shape[1])
    xp, xs = x_prompt, x_sample
    new_sf = new_sb = new_ckv = new_kr = None
    for l in range(DEPTH):
        mod_lat = mods[l, :dec_b].reshape(dec_b, 6, d)
        mod_ctx = mods[l, dec_b:dec_b + 1].reshape(1, 6, d)
        row = lambda a: a.reshape(1, -1)
        if l % 2 == 0:
            e = l // 2
            w_h, w_m, wq1, wq2, wk, wv = _ab_weights(ab_w_in[e], mla_w_uq[e], mla_w_ukv[e])
            w_out = ab_w_out[e].astype(BF16)
            qg, kvg, gn = row(mla_q_norm_g[e]), row(mla_kv_norm_g[e]), row(hgrn_norm_g[e])
            ph_p, pm_p = _inproj(xp, mod_ctx, [w_h, w_m])
            of_p, ob_p, sf, sb = _hgrn(ph_p, hgrn_lb_logits, None, None, e)
            q_p, k_p, v_p, ckv_p = _mla_proj(pm_p, None, qg, kvg, wq1, None, wk, wv)
            om_p = _attention(q_p, k_p, v_p, None, None)
            xp = _outproj(xp, mod_ctx, (of_p, ob_p, ph_p, gn), om_p, w_out, row(ln_g[l, 0]), row(ln_b[l, 0]), True)
            ph_s, pm_s = _inproj(xs, mod_lat, [w_h, w_m])
            of_s, ob_s, _, _ = _hgrn(ph_s, hgrn_lb_logits, state_hgrn_fwd[:, e], state_hgrn_bwd[:, e], e)
            q_s, k_s, v_s, _ = _mla_proj(pm_s, rope_tabs, qg, kvg, wq1, wq2, wk, wv)
            kc, vc = _mla_ctx(cache_mla_ckv[:, e], cache_mla_krope[:, e], wk, wv)
            om_s = _attention(q_s, k_s, v_s, kc, vc)
            xs = _outproj(xs, mod_lat, (of_s, ob_s, ph_s, gn), om_s, w_out, row(ln_g[l, 0]), row(ln_b[l, 0]), True)
            kr0 = MLA_Q_RANK + MLA_KV_RANK + MLA_DN
            new_sf, new_sb, new_ckv = sf, sb, ckv_p
            new_kr = pm_p[:, :, kr0:kr0 + MLA_DR]
        else:
            jx = l // 2
            w1 = cd_w_in[jx].astype(BF16)
            w_out = cd_w_out[jx].astype(BF16)
            cd = (sc_conv_w[jx], cf_conv_w[jx], row(cf_conv_b[jx]), row(cf_ln_g[jx]), row(cf_ln_b[jx]))
            (p1_p,) = _inproj(xp, mod_ctx, [w1])
            ysc_p, ycf_p = _conv_mixers(p1_p, *cd)
            xp = _outproj(xp, mod_ctx, ysc_p, ycf_p, w_out, row(ln_g[l, 0]), row(ln_b[l, 0]), False)
            (p1_s,) = _inproj(xs, mod_lat, [w1])
            ysc_s, ycf_s = _conv_mixers(p1_s, *cd)
            xs = _outproj(xs, mod_lat, ysc_s, ycf_s, w_out, row(ln_g[l, 0]), row(ln_b[l, 0]), False)
        wr, wg, wu, wd = _moe_weights(moe_w_group[l], moe_w_expert[l], moe_w_gate[l], moe_w_up[l], moe_w_down[l])
        xp_t = xp.reshape(-1, MOE_TILE, d)
        xp = _moe(xp_t, mod_ctx, wr, wg, wu, wd, row(ln_g[l, 1]), row(ln_b[l, 1])).reshape(xp.shape)
        xs = _moe(xs, mod_lat, wr, wg, wu, wd, row(ln_g[l, 1]), row(ln_b[l, 1]))
    return (xp, xs, new_sf[:, None], new_sb[:, None], new_ckv[:, None], new_kr[:, None])
```

```python
import functools

import numpy as np
import jax
import jax.numpy as jnp
from jax import lax
from jax.experimental import pallas as pl
from jax.experimental.pallas import tpu as pltpu

F32 = jnp.float32
BF16 = jnp.bfloat16
HIGHEST = lax.Precision.HIGHEST

D_MODEL = 1024
DEPTH = 2
GRID_W = 64
N_EVEN = (DEPTH + 1) // 2
HA_HEADS = 4
HA_DK = 128
HA_DV = 128
HA_W = HA_HEADS * HA_DK
CHUNK = 32
MLA_HEADS = 8
MLA_DN = 64
MLA_DR = 32
MLA_DV = 64
MLA_Q_RANK = 384
MLA_KV_RANK = 256
ROPE_BASE = 10000.0
SC_W = 512
SC_K = 3
CF_W = 512
CF_K = 31
N_GROUPS = 4
EXP_PER_GROUP = 8
N_EXPERTS = N_GROUPS * EXP_PER_GROUP
EXP_FF = 128
ALPHA = (2.0 * DEPTH) ** 0.25
LOG2_E = 1.4426950408889634
LN_EPS = 1e-5
RMS_EPS = 1e-6

LANES = 128
SUBLANES = 8
VMEM_LIMIT = 56 * 1024 * 1024

HEAD_PAD = LANES
MLA_SLAB = MLA_Q_RANK + MLA_KV_RANK + 2 * LANES
TOK_TILE = 256
GROUP_ROWS = 128
N_LEVELS = 5
SAFE_LOG_DECAY = 60.0
MOE_TILE = 512
MOE_BLOCK = 128
HALO = 16


def _cparams(sem):
    return pltpu.CompilerParams(dimension_semantics=sem, vmem_limit_bytes=VMEM_LIMIT)


def _silu(x):
    return x / (1.0 + jnp.exp(-x))


def _sigmoid(x):
    return 1.0 / (1.0 + jnp.exp(-x))


def _layer_norm(r, g, b):
    mu = jnp.mean(r, axis=-1, keepdims=True)
    d = r - mu
    var = jnp.mean(d * d, axis=-1, keepdims=True)
    return d * lax.rsqrt(var + LN_EPS) * g + b


def _dot(a, b):
    return jnp.dot(a, b, preferred_element_type=F32)


def _dot_nt(a, b):
    return lax.dot_general(a, b, (((1,), (1,)), ((), ())), preferred_element_type=F32)


def _dot_tn(a, b):
    return lax.dot_general(a, b, (((0,), (0,)), ((), ())), preferred_element_type=F32)


def _mod_kernel(c_ref, w_ref, b_ref, o_ref):
    s = _silu(c_ref[...])
    o_ref[0] = jnp.dot(s, w_ref[0], precision=HIGHEST, preferred_element_type=F32) + b_ref[0]


def _mod_vectors(cvec, mod_w, mod_b):
    n_out = mod_w.shape[-1]
    tn = 1536
    return pl.pallas_call(
        _mod_kernel,
        out_shape=jax.ShapeDtypeStruct((DEPTH, SUBLANES, n_out), F32),
        grid=(DEPTH, n_out // tn),
        in_specs=[
            pl.BlockSpec((SUBLANES, D_MODEL), lambda l, j: (0, 0)),
            pl.BlockSpec((1, D_MODEL, tn), lambda l, j: (l, 0, j)),
            pl.BlockSpec((1, 1, tn), lambda l, j: (l, 0, j)),
        ],
        out_specs=pl.BlockSpec((1, SUBLANES, tn), lambda l, j: (l, 0, j)),
        compiler_params=_cparams(("arbitrary", "arbitrary")),
        name="mod_vectors",
    )(cvec, mod_w, mod_b.reshape(DEPTH, 1, n_out))


def _inproj_kernel(*refs, n_w):
    x_ref, mod_ref = refs[0], refs[1]
    w_refs = refs[2:2 + n_w]
    o_refs = refs[2 + n_w:]
    m = mod_ref[0]
    h = (x_ref[0] * (1.0 + m[1:2]) + m[0:1]).astype(BF16)
    for w_ref, o_ref in zip(w_refs, o_refs):
        o_ref[0] = _dot(h, w_ref[...])


def _inproj(x, mod, weights):
    b, l, d = x.shape
    tm = TOK_TILE
    per_batch = mod.shape[0] > 1
    mod_map = (lambda i, j: (i, 0, 0)) if per_batch else (lambda i, j: (0, 0, 0))
    in_specs = [pl.BlockSpec((1, tm, d), lambda i, j: (i, j, 0)),
                pl.BlockSpec((1, 6, d), mod_map)]
    in_specs += [pl.BlockSpec(w.shape, lambda i, j: (0, 0)) for w in weights]
    out_shape = [jax.ShapeDtypeStruct((b, l, w.shape[1]), F32) for w in weights]
    out_specs = [pl.BlockSpec((1, tm, w.shape[1]), lambda i, j: (i, j, 0)) for w in weights]
    return pl.pallas_call(
        functools.partial(_inproj_kernel, n_w=len(weights)),
        out_shape=out_shape,
        grid=(b, l // tm),
        in_specs=in_specs,
        out_specs=out_specs,
        compiler_params=_cparams(("parallel", "parallel")),
        name="modulate_inproj",
    )(x, mod, *weights)


def _hgrn_tables():
    n = GROUP_ROWS
    t = np.arange(n)[:, None]
    j = np.arange(n)[None, :]
    same_chunk = (t // CHUNK) == (j // CHUNK)
    e_f, e_b = [], []
    lv_f = np.full((n, n), -1, np.int32)
    for lvl in range(N_LEVELS):
        m = CHUNK >> (lvl + 1)
        blk0 = (t // (2 * m)) * (2 * m)
        r = blk0 + m - 1
        upper = t > r
        ef = np.where(upper, (j > r) & (j <= t), (j > t) & (j <= r))
        r2 = blk0 + m
        lower = t < r2
        eb = np.where(lower, (j >= t) & (j < r2), (j >= r2) & (j < t))
        e_f.append(ef)
        e_b.append(eb)
        s = np.arange(n)[None, :]
        same_blk = (t // (2 * m)) == (s // (2 * m))
        q_side = (t % (2 * m)) >= m
        k_side = (s % (2 * m)) < m
        lv_f[same_blk & q_side & k_side] = lvl
    lv_f[np.arange(n), np.arange(n)] = N_LEVELS
    e_f = np.concatenate(e_f, axis=0).astype(np.float32)
    e_b = np.concatenate(e_b, axis=0).astype(np.float32)
    e = np.stack([e_f, e_b])
    cum = np.stack([same_chunk & (j <= t), same_chunk & (j >= t)]).astype(np.float32)
    lv = np.stack([lv_f, lv_f.T])
    return e, cum, lv


def _hgrn_kernel(qf_ref, vf_ref, ff_ref, qb_ref, vb_ref, fb_ref, lbl_ref, e_ref, cum_ref, lv_ref, s0f_ref, s0b_ref,
                 of_ref, ob_ref, sf_ref, sb_ref,
                 st_scr, qe_scr, kd_scr, sc_scr, v_scr, oi_scr, dec_scr, *, lt, slot, has_state):
    i = pl.program_id(1)
    n_i = pl.num_programs(1)
    n_chunks = lt // CHUNK
    n_groups = lt // GROUP_ROWS
    chunks_per_group = GROUP_ROWS // CHUNK

    @pl.when(i == 0)
    def _():
        for d, s0_ref in enumerate((s0f_ref, s0b_ref)):
            for h in range(HA_HEADS):
                if has_state:
                    st_scr[d, h] = s0_ref[0, h].T
                else:
                    st_scr[d, h] = jnp.zeros((HA_DV, HA_DK), F32)

    lg = lbl_ref[...]
    n_slots = lg.shape[0]
    mx = lg[0]
    for s in range(1, n_slots):
        mx = jnp.maximum(mx, lg[s])
    ex = [jnp.exp(lg[s] - mx) for s in range(n_slots)]
    den = ex[0]
    for s in range(1, n_slots):
        den = den + ex[s]
    num = ex[0]
    for s in range(1, slot + 1):
        num = num + ex[s]
    lb_all = num / den

    dirs = ((qf_ref, vf_ref, ff_ref), (qb_ref, vb_ref, fb_ref))
    head_cols = [slice(h * HA_DK, (h + 1) * HA_DK) for h in range(HA_HEADS)]

    def gates(d, rows, cols):
        q_ref, _, f_ref = dirs[d]
        q = _silu(q_ref[0, rows, cols])
        lb_h = lb_all[d:d + 1, cols]
        f = lb_h + (1.0 - lb_h) * _sigmoid(f_ref[0, rows, cols])
        g = jnp.log(f)
        g_hi = g.astype(BF16)
        g_lo = (g - g_hi.astype(F32)).astype(BF16)
        return q, 1.0 - f, g_hi, g_lo

    tot_min = jnp.zeros((1, HA_DK), F32)
    for d in range(2):

        def group_step(grp, tot_min, d=d):
            r0 = pl.multiple_of(grp * GROUP_ROWS, GROUP_ROWS)
            rows = pl.ds(r0, GROUP_ROWS)
            cum = cum_ref[d]
            lv = lv_ref[d]
            for cols in head_cols:
                q, k, g_hi, g_lo = gates(d, rows, cols)
                bcum = _dot(cum, g_hi) + _dot(cum, g_lo)
                v_scr[d, rows, cols] = dirs[d][1][0, rows, cols].astype(BF16)
                qe = (q * jnp.exp(bcum)).astype(BF16)
                qe_scr[d, rows, cols] = qe
                p = _dot_nt(qe, (k * jnp.exp(-bcum)).astype(BF16))
                sc_scr[d, rows, cols] = jnp.where(lv >= 0, p, 0.0).astype(BF16)
                for cc in range(chunks_per_group):
                    edge = cc * CHUNK + (CHUNK - 1 if d == 0 else 0)
                    tot = bcum[edge:edge + 1]
                    b_c = bcum[cc * CHUNK:(cc + 1) * CHUNK]
                    k_c = k[cc * CHUNK:(cc + 1) * CHUNK]
                    kd = (k_c * jnp.exp(tot - b_c)).astype(BF16)
                    kd_scr[d, pl.ds(r0 + cc * CHUNK, CHUNK), cols] = kd
                    dec_scr[d, grp * chunks_per_group + cc, :, cols] = jnp.exp(tot)
                    tot_min = jnp.minimum(tot_min, tot)
            return tot_min

        tot_min = lax.fori_loop(0, n_groups, group_step, tot_min)

    def one_factor():
        for d in range(2):
            for grp in range(n_groups):
                rows = slice(grp * GROUP_ROWS, (grp + 1) * GROUP_ROWS)
                for cols in head_cols:
                    oi_scr[d, rows, cols] = _dot(sc_scr[d, rows, cols], v_scr[d, rows, cols])

    def per_level():
        for d in range(2):

            def group_step(grp, carry, d=d):
                rows = pl.ds(pl.multiple_of(grp * GROUP_ROWS, GROUP_ROWS), GROUP_ROWS)
                lv = lv_ref[d]
                for cols in head_cols:
                    q, k, g_hi, g_lo = gates(d, rows, cols)
                    x = jnp.exp(_dot(e_ref[d], g_hi) + _dot(e_ref[d], g_lo))
                    sc = jnp.where(lv == N_LEVELS, _dot_nt(q.astype(BF16), k.astype(BF16)), 0.0)
                    for lvl in range(N_LEVELS):
                        xl = x[lvl * GROUP_ROWS:(lvl + 1) * GROUP_ROWS]
                        p = _dot_nt((q * xl).astype(BF16), (k * xl).astype(BF16))
                        sc = jnp.where(lv == lvl, p, sc)
                    oi_scr[d, rows, cols] = _dot(sc.astype(BF16), v_scr[d, rows, cols])
                return carry

            lax.fori_loop(0, n_groups, group_step, 0)

    lax.cond(jnp.min(tot_min) > -SAFE_LOG_DECAY, one_factor, per_level)

    out_refs = (of_ref, ob_ref)

    def chunk_step(c, carry):
        for d in range(2):
            cc = c if d == 0 else n_chunks - 1 - c
            r0 = pl.multiple_of(cc * CHUNK, CHUNK)
            rows = pl.ds(r0, CHUNK)
            for h in range(HA_HEADS):
                cols = slice(h * HA_DK, (h + 1) * HA_DK)
                st = st_scr[d, h]
                o_state = _dot_nt(qe_scr[d, rows, cols], st.astype(BF16))
                out_refs[d][0, rows, cols] = oi_scr[d, rows, cols] + o_state
                upd = _dot_tn(v_scr[d, rows, cols], kd_scr[d, rows, cols])
                st_scr[d, h] = st * dec_scr[d, cc, :, cols] + upd
        return carry

    lax.fori_loop(0, n_chunks, chunk_step, 0)

    @pl.when(i == n_i - 1)
    def _():
        for d, s_ref in enumerate((sf_ref, sb_ref)):
            for h in range(HA_HEADS):
                s_ref[0, h] = st_scr[d, h].T


def _hgrn(p_h, lb_logits, s0_f, s0_b, slot):
    b, l, _ = p_h.shape
    lt = TOK_TILE
    n_t = l // lt
    has_state = s0_f is not None
    if not has_state:
        s0_f = jnp.zeros((1, HA_HEADS, HA_DK, HA_DV), F32)
        s0_b = s0_f
    e_np, cum_np, lv_np = _hgrn_tables()
    e_mat = jnp.asarray(e_np, BF16)
    cum = jnp.asarray(cum_np, BF16)
    lv = jnp.asarray(lv_np, jnp.int32)
    w = HA_W

    def sec(idx, rev):
        if rev:
            return pl.BlockSpec((1, lt, w), lambda bi, i: (bi, n_t - 1 - i, idx))
        return pl.BlockSpec((1, lt, w), lambda bi, i: (bi, i, idx))

    state_map = (lambda bi, i: (bi, 0, 0, 0)) if has_state else (lambda bi, i: (0, 0, 0, 0))
    in_specs = [
        sec(0, False), sec(3, False), sec(1, False),
        sec(0, True), sec(3, True), sec(2, True),
        pl.BlockSpec(lb_logits.shape, lambda bi, i: (0, 0, 0)),
        pl.BlockSpec(e_mat.shape, lambda bi, i: (0, 0, 0)),
        pl.BlockSpec(cum.shape, lambda bi, i: (0, 0, 0)),
        pl.BlockSpec(lv.shape, lambda bi, i: (0, 0, 0)),
        pl.BlockSpec((1, HA_HEADS, HA_DK, HA_DV), state_map),
        pl.BlockSpec((1, HA_HEADS, HA_DK, HA_DV), state_map),
    ]
    out_shape = [
        jax.ShapeDtypeStruct((b, l, w), F32),
        jax.ShapeDtypeStruct((b, l, w), F32),
        jax.ShapeDtypeStruct((b, HA_HEADS, HA_DK, HA_DV), F32),
        jax.ShapeDtypeStruct((b, HA_HEADS, HA_DK, HA_DV), F32),
    ]
    out_specs = [
        pl.BlockSpec((1, lt, w), lambda bi, i: (bi, i, 0)),
        pl.BlockSpec((1, lt, w), lambda bi, i: (bi, n_t - 1 - i, 0)),
        pl.BlockSpec((1, HA_HEADS, HA_DK, HA_DV), lambda bi, i: (bi, 0, 0, 0)),
        pl.BlockSpec((1, HA_HEADS, HA_DK, HA_DV), lambda bi, i: (bi, 0, 0, 0)),
    ]
    scratch = [
        pltpu.VMEM((2, HA_HEADS, HA_DV, HA_DK), F32),
        pltpu.VMEM((2, lt, w), BF16),
        pltpu.VMEM((2, lt, w), BF16),
        pltpu.VMEM((2, lt, w), BF16),
        pltpu.VMEM((2, lt, w), BF16),
        pltpu.VMEM((2, lt, w), F32),
        pltpu.VMEM((2, lt // CHUNK, 1, w), F32),
    ]
    return pl.pallas_call(
        functools.partial(_hgrn_kernel, lt=lt, slot=slot, has_state=has_state),
        out_shape=out_shape,
        grid=(b, n_t),
        in_specs=in_specs,
        out_specs=out_specs,
        scratch_shapes=scratch,
        compiler_params=_cparams(("parallel", "arbitrary")),
        name="hgrn2_scan",
    )(p_h, p_h, p_h, p_h, p_h, p_h, lb_logits, e_mat, cum, lv, s0_f, s0_b)


def _ab_inproj_kernel(*refs, rope):
    if rope:
        (x_ref, mod_ref, wh_ref, wm_ref, cos_ref, sin_ref, qg_ref, kvg_ref, wq1_ref, wq2_ref, wk_ref,
         wv_ref) = refs[:12]
        ph_ref, q_ref, k_ref, v_ref, ckv_ref, kr_ref = refs[12:]
    else:
        x_ref, mod_ref, wh_ref, wm_ref, qg_ref, kvg_ref, wq1_ref, wk_ref, wv_ref = refs[:9]
        ph_ref, q_ref, k_ref, v_ref, ckv_ref, kr_ref = refs[9:]
    scale = (MLA_DN + MLA_DR) ** -0.5 * LOG2_E
    mod = mod_ref[0]
    hmod = (x_ref[0] * (1.0 + mod[1:2]) + mod[0:1]).astype(BF16)
    ph_ref[0] = _dot(hmod, wh_ref[...])
    pm = _dot(hmod, wm_ref[...])
    kr_ref[0] = pm[:, MLA_Q_RANK + MLA_KV_RANK:MLA_Q_RANK + MLA_KV_RANK + LANES]
    cq = pm[:, :MLA_Q_RANK]
    cq = cq * lax.rsqrt(jnp.mean(cq * cq, axis=-1, keepdims=True) + RMS_EPS) * qg_ref[...]
    cq = cq.astype(BF16)
    ckv = pm[:, MLA_Q_RANK:MLA_Q_RANK + MLA_KV_RANK]
    ckv = ckv * lax.rsqrt(jnp.mean(ckv * ckv, axis=-1, keepdims=True) + RMS_EPS) * kvg_ref[...]
    ckv_ref[0] = ckv
    ckv = ckv.astype(BF16)
    kr0 = MLA_Q_RANK + MLA_KV_RANK
    kr = pm[:, kr0:kr0 + LANES]
    qa = _dot(cq, wq1_ref[...])
    kn = _dot(ckv, wk_ref[...])
    v_ref[0] = _dot(ckv, wv_ref[...]).astype(BF16)
    if rope:
        cos = cos_ref[...]
        sin = sin_ref[...]
        qb = _dot(cq, wq2_ref[...])
        kr = kr * cos + pm[:, kr0 + LANES:kr0 + 2 * LANES] * sin
    for h in range(MLA_HEADS):
        sl = slice(h * HEAD_PAD, (h + 1) * HEAD_PAD)
        qh = qa[:, sl]
        if rope:
            qh = qh * cos + qb[:, sl] * sin
        q_ref[0, :, sl] = (qh * scale).astype(BF16)
        k_ref[0, :, sl] = (kn[:, sl] + kr).astype(BF16)


def _ab_inproj(x, mod, w_h, w_m, rope_tabs, qg, kvg, wq1, wq2, wk, wv):
    b, l, d = x.shape
    tm = TOK_TILE
    rope = rope_tabs is not None
    per_batch = mod.shape[0] > 1
    mod_map = (lambda i, j: (i, 0, 0)) if per_batch else (lambda i, j: (0, 0, 0))
    full = lambda a: pl.BlockSpec(a.shape, lambda i, j: (0,) * a.ndim)
    row = lambda width: pl.BlockSpec((1, tm, width), lambda i, j: (i, j, 0))
    hw = MLA_HEADS * HEAD_PAD
    inputs = [x, mod, w_h, w_m]
    in_specs = [row(d), pl.BlockSpec((1, 6, d), mod_map), full(w_h), full(w_m)]
    if rope:
        inputs += list(rope_tabs)
        in_specs += [pl.BlockSpec((tm, LANES), lambda i, j: (j, 0))] * 2
    ws = [qg, kvg, wq1] + ([wq2] if rope else []) + [wk, wv]
    inputs += ws
    in_specs += [full(a) for a in ws]
    widths = [(w_h.shape[1], F32), (hw, BF16), (hw, BF16), (hw, BF16), (MLA_KV_RANK, F32), (LANES, F32)]
    return pl.pallas_call(
        functools.partial(_ab_inproj_kernel, rope=rope),
        out_shape=[jax.ShapeDtypeStruct((b, l, wd), dt) for wd, dt in widths],
        grid=(b, l // tm),
        in_specs=in_specs,
        out_specs=[row(wd) for wd, _ in widths],
        compiler_params=_cparams(("parallel", "parallel")),
        name="ab_inproj",
    )(*inputs)


def _mla_ctx_kernel(ckv_ref, kr_ref, place_ref, wk_ref, wv_ref, k_ref, v_ref):
    ckv = ckv_ref[0].astype(BF16)
    kr = _dot(kr_ref[0].astype(BF16), place_ref[...])
    kn = _dot(ckv, wk_ref[...])
    v_ref[0] = _dot(ckv, wv_ref[...]).astype(BF16)
    for h in range(MLA_HEADS):
        sl = slice(h * HEAD_PAD, (h + 1) * HEAD_PAD)
        k_ref[0, :, sl] = (kn[:, sl] + kr).astype(BF16)


def _mla_ctx(ctx_ckv, ctx_kr, wk, wv):
    b, lc, _ = ctx_ckv.shape
    hw = MLA_HEADS * HEAD_PAD
    place =np.zeros((MLA_DR, LANES), np.float32)
    place[np.arange(MLA_DR), MLA_DN + np.arange(MLA_DR)] = 1.0
    place = jnp.asarray(place, BF16)
    full = lambda a: pl.BlockSpec(a.shape, lambda i: (0,) * a.ndim)
    return pl.pallas_call(
        _mla_ctx_kernel,
        out_shape=[jax.ShapeDtypeStruct((b, lc, hw), BF16)] * 2,
        grid=(b,),
        in_specs=[pl.BlockSpec((1, lc, MLA_KV_RANK), lambda i: (i, 0, 0)),
                  pl.BlockSpec((1, lc, MLA_DR), lambda i: (i, 0, 0)),
                  full(place), full(wk), full(wv)],
        out_specs=[pl.BlockSpec((1, lc, hw), lambda i: (i, 0, 0))] * 2,
        compiler_params=_cparams(("parallel",)),
        name="mla_ctx_keys",
    )(ctx_ckv, ctx_kr, place, wk, wv)


def _attn_kernel(*refs, has_ctx):
    if has_ctx:
        q_ref, k_ref, v_ref, kc_ref, vc_ref, o_ref = refs
    else:
        q_ref, k_ref, v_ref, o_ref = refs
    n_pairs = q_ref.shape[-1] // (2 * HEAD_PAD)
    for pair in range(n_pairs):
        out = None
        for hh in range(2):
            h0 = (2 * pair + hh) * HEAD_PAD
            sl = slice(h0, h0 + HEAD_PAD)
            q = q_ref[0, :, sl]
            s = _dot_nt(q, k_ref[0, :, sl])
            m = jnp.max(s, axis=-1, keepdims=True)
            if has_ctx:
                s2 = _dot_nt(q, kc_ref[0, :, sl])
                m = jnp.maximum(m, jnp.max(s2, axis=-1, keepdims=True))
            p = jnp.exp2(s - m)
            den = jnp.sum(p, axis=-1, keepdims=True)
            acc = _dot(p.astype(BF16), v_ref[0, :, sl])
            if has_ctx:
                p2 = jnp.exp2(s2 - m)
                den = den + jnp.sum(p2, axis=-1, keepdims=True)
                acc = acc + _dot(p2.astype(BF16), vc_ref[0, :, sl])
            o = acc / den
            out = o if out is None else out + o
        o_ref[0, :, pair * LANES:(pair + 1) * LANES] = out.astype(BF16)


def _attention(q, k, v, kc, vc):
    b, l, _ = q.shape
    lk = k.shape[1]
    tq = TOK_TILE
    pairs_per_step = MLA_HEADS // 2 if lk <= TOK_TILE else 1
    pw = 2 * HEAD_PAD * pairs_per_step
    has_ctx = kc is not None
    inputs = [q, k, v]
    in_specs = [pl.BlockSpec((1, tq, pw), lambda bi, p, i: (bi, i, p)),
                pl.BlockSpec((1, lk, pw), lambda bi, p, i: (bi, 0, p)),
                pl.BlockSpec((1, lk, pw), lambda bi, p, i: (bi, 0, p))]
    if has_ctx:
        lc = kc.shape[1]
        inputs += [kc, vc]
        in_specs += [pl.BlockSpec((1, lc, pw), lambda bi, p, i: (bi, 0, p))] * 2
    return pl.pallas_call(
        functools.partial(_attn_kernel, has_ctx=has_ctx),
        out_shape=jax.ShapeDtypeStruct((b, l, MLA_HEADS * MLA_DV), BF16),
        grid=(b, MLA_HEADS // 2 // pairs_per_step, l // tq),
        in_specs=in_specs,
        out_specs=pl.BlockSpec((1, tq, LANES * pairs_per_step), lambda bi, p, i: (bi, i, p)),
        compiler_params=_cparams(("parallel", "parallel", "arbitrary")),
        name="mla_attention",
    )(*inputs)


def _outproj_kernel(*refs, hgrn):
    if hgrn:
        x_ref, mod_ref, of_ref, ob_ref, gl_ref, gn_ref, b_ref, w_ref, lg_ref, lb_ref, o_ref = refs
        o = of_ref[0] + ob_ref[0]
        gate = _silu(gl_ref[0])
        gn = gn_ref[...]
        parts = []
        for h in range(HA_HEADS):
            sl = slice(h * HA_DV, (h + 1) * HA_DV)
            oh = o[:, sl]
            oh = oh * lax.rsqrt(jnp.mean(oh * oh, axis=-1, keepdims=True) + RMS_EPS) * gn
            parts.append((oh * gate[:, sl]).astype(BF16))
        a = jnp.concatenate(parts, axis=-1)
    else:
        x_ref, mod_ref, a_ref, b_ref, w_ref, lg_ref, lb_ref, o_ref = refs
        a = a_ref[0]
    half = a.shape[-1]
    y = _dot(a, w_ref[:half, :]) + _dot(b_ref[0], w_ref[half:, :])
    m = mod_ref[0]
    r = ALPHA * x_ref[0] + m[2:3] * y
    o_ref[0] = _layer_norm(r, lg_ref[...], lb_ref[...])


def _outproj(x, mod, a_inputs, b_in, w, ln_g, ln_b, hgrn):
    b, l, d = x.shape
    tm = TOK_TILE
    per_batch = mod.shape[0] > 1
    mod_map = (lambda i, j: (i, 0, 0)) if per_batch else (lambda i, j: (0, 0, 0))
    row = lambda width: pl.BlockSpec((1, tm, width), lambda i, j: (i, j, 0))
    full = lambda a: pl.BlockSpec(a.shape, lambda i, j: (0,) * a.ndim)
    inputs = [x, mod]
    in_specs = [row(d), pl.BlockSpec((1, 6, d), mod_map)]
    if hgrn:
        o_f, o_b, p_h, g_norm = a_inputs
        inputs += [o_f, o_b, p_h, g_norm]
        in_specs += [row(HA_W), row(HA_W),
                     pl.BlockSpec((1, tm, HA_W), lambda i, j: (i, j, 4)), full(g_norm)]
    else:
        inputs += [a_inputs]
        in_specs += [row(a_inputs.shape[-1])]
    inputs += [b_in, w, ln_g, ln_b]
    in_specs += [row(b_in.shape[-1]), full(w), full(ln_g), full(ln_b)]
    return pl.pallas_call(
        functools.partial(_outproj_kernel, hgrn=hgrn),
        out_shape=jax.ShapeDtypeStruct((b, l, d), F32),
        grid=(b, l // tm),
        in_specs=in_specs,
        out_specs=row(d),
        compiler_params=_cparams(("parallel", "parallel")),
        name="outproj_ln",
    )(*inputs)


def _conv_kernel(pm_ref, pp_ref, pn_ref, scw_ref, cfw_ref, cfb_ref, cfg_ref, cfbeta_ref,
                 ysc_ref, ycf_ref, ext_sc, ext_cf, *, lt):
    i = pl.program_id(1)
    n_i = pl.num_programs(1)
    w = SC_W

    def sc_in(p):
        return p[:, w:2 * w] * p[:, 2 * w:3 * w]

    def cf_in(p):
        return p[:, 3 * w:3 * w + CF_W] * _sigmoid(p[:, 3 * w + CF_W:3 * w + 2 * CF_W])

    pm = pm_ref[0]
    pp = pp_ref[0]
    pn = pn_ref[0]
    has_prev = i > 0
    has_next = i < n_i - 1
    n_ext = lt + 2 * HALO
    for ext, conv_in in ((ext_sc, sc_in), (ext_cf, cf_in)):
        ext[0, 0:HALO, :] = jnp.where(has_prev, conv_in(pp), 0.0)
        ext[0, HALO:HALO + lt, :] = conv_in(pm)
        ext[0, HALO + lt:, :] = jnp.where(has_next, conv_in(pn), 0.0)
    for s in range(1, SUBLANES):
        ext_cf[s, 0:n_ext - SUBLANES, :] = ext_cf[0, s:s + n_ext - SUBLANES, :]
    sc_shifts = sorted({(HALO - SC_K // 2 + j) % SUBLANES for j in range(SC_K)} - {0})
    for s in sc_shifts:
        ext_sc[s, 0:n_ext - SUBLANES, :] = ext_sc[0, s:s + n_ext - SUBLANES, :]

    def tap(ext, off, rows):
        s = off % SUBLANES
        return ext[s, off - s:off - s + rows, :]

    rb = 32
    for r in range(lt // rb):
        base = HALO + r * rb
        acc = None
        for j in range(SC_K):
            term = jnp.tile(scw_ref[j], (rb // SUBLANES, 1)) * tap(ext_sc, base - SC_K // 2 + j, rb)
            acc = term if acc is None else acc + term
        ysc_ref[0, r * rb:(r + 1) * rb, :] = (pm[r * rb:(r + 1) * rb, 0:w] * acc).astype(BF16)
        acc = None
        for j in range(CF_K):
            term = jnp.tile(cfw_ref[j], (rb // SUBLANES, 1)) * tap(ext_cf, base - CF_K // 2 + j, rb)
            acc = term if acc is None else acc + term
        u = _layer_norm(acc + cfb_ref[...], cfg_ref[...], cfbeta_ref[...])
        ycf_ref[0, r * rb:(r + 1) * rb, :] = _silu(u).astype(BF16)


def _conv_mixers(p1, sc_w, cf_w, cf_b, cf_g, cf_beta):
    b, l, width = p1.shape
    lt = TOK_TILE
    hb = lt // HALO
    n_h = l // HALO
    sc_w = jnp.broadcast_to(sc_w[:, None, :], (sc_w.shape[0], SUBLANES, sc_w.shape[1]))
    cf_w = jnp.broadcast_to(cf_w[:, None, :], (cf_w.shape[0], SUBLANES, cf_w.shape[1]))
    full = lambda a: pl.BlockSpec(a.shape, lambda bi, i: (0,) * a.ndim)
    return pl.pallas_call(
        functools.partial(_conv_kernel, lt=lt),
        out_shape=[jax.ShapeDtypeStruct((b, l, SC_W), BF16), jax.ShapeDtypeStruct((b, l, CF_W), BF16)],
        grid=(b, l // lt),
        in_specs=[
            pl.BlockSpec((1, lt, width), lambda bi, i: (bi, i, 0)),
            pl.BlockSpec((1, HALO, width), lambda bi, i: (bi, jnp.maximum(i * hb - 1, 0), 0)),
            pl.BlockSpec((1, HALO, width), lambda bi, i: (bi, jnp.minimum((i + 1) * hb, n_h - 1), 0)),
            full(sc_w), full(cf_w), full(cf_b), full(cf_g), full(cf_beta),
        ],
        out_specs=[pl.BlockSpec((1, lt, SC_W), lambda bi, i: (bi, i, 0)),
                   pl.BlockSpec((1, lt, CF_W), lambda bi, i: (bi, i, 0))],
        scratch_shapes=[pltpu.VMEM((SUBLANES, lt + 2 * HALO, SC_W), F32),
                        pltpu.VMEM((SUBLANES, lt + 2 * HALO, CF_W), F32)],
        compiler_params=_cparams(("parallel", "parallel")),
        name="conv_mixers",
    )(p1, p1, p1, sc_w, cf_w, cf_b, cf_g, cf_beta)


def _route_t(lt):
    t = lt.shape[1]
    row = lax.broadcasted_iota(jnp.int32, (SUBLANES, t), 0).astype(F32)
    neg = -jnp.inf
    big = float(LANES)
    gl = jnp.where(row < N_GROUPS, lt[N_EXPERTS:N_EXPERTS + SUBLANES], neg)
    gmax = jnp.max(gl, axis=0, keepdims=True)
    p_g = 1.0 / jnp.sum(jnp.exp(gl - gmax), axis=0, keepdims=True)
    g_sel = jnp.min(jnp.where(gl == gmax, row, big), axis=0, keepdims=True)
    el = lt[0:EXP_PER_GROUP]
    for gi in range(1, N_GROUPS):
        el = jnp.where(g_sel == gi, lt[gi * EXP_PER_GROUP:(gi + 1) * EXP_PER_GROUP], el)
    v1 = jnp.max(el, axis=0, keepdims=True)
    i1 = jnp.min(jnp.where(el == v1, row, big), axis=0, keepdims=True)
    el2 = jnp.where(row == i1, neg, el)
    v2 = jnp.max(el2, axis=0, keepdims=True)
    i2 = jnp.min(jnp.where(el2 == v2, row, big), axis=0, keepdims=True)
    e2 = jnp.exp(v2 - v1)
    w1 = p_g / (1.0 + e2)
    w2 = p_g * e2 / (1.0 + e2)
    comb = jnp.where(row == i1, w1, 0.0) + jnp.where(row == i2, w2, 0.0)
    onehot = jnp.where(row == g_sel, 1.0, 0.0)
    return onehot, comb


def _moe_kernel(x_ref, mod_ref, wr_ref, tri_ref, wg_ref, wu_ref, wd_ref, lg_ref, lb_ref, o_ref,
                hb_scr, hs_scr, os_scr, pt_scr, cs_scr):
    m = mod_ref[0]
    tm, cap = pt_scr.shape
    d = x_ref.shape[-1]
    half = d // 2
    cw = EXP_PER_GROUP * EXP_FF

    h = x_ref[0] * (1.0 + m[4:5]) + m[3:4]
    h_hi = h.astype(BF16)
    hb_scr[...] = h_hi
    h_lo = (h - h_hi.astype(F32)).astype(BF16)
    l2 = _dot(h_hi, wr_ref[...])
    logits = l2[:, :LANES] + l2[:, LANES:] + _dot(h_lo, wr_ref[:, :LANES])
    onehot, comb = _route_t(logits.T)
    rank = _dot(onehot.astype(BF16), tri_ref[...])
    cnt = jnp.sum(onehot, axis=1, keepdims=True)
    padded = jnp.floor((cnt + (MOE_BLOCK - 1)) * (1.0 / MOE_BLOCK)) * MOE_BLOCK
    start = jnp.zeros((1, 1), F32)
    dest = jnp.zeros((1, tm), F32)
    first, count = [], []
    for gi in range(N_GROUPS):
        dest = dest + onehot[gi:gi + 1] * (start + rank[gi:gi + 1])
        first.append((start[0, 0] * (1.0 / MOE_BLOCK)).astype(jnp.int32))
        count.append((padded[gi, 0] * (1.0 / MOE_BLOCK)).astype(jnp.int32))
        start = start + padded[gi:gi + 1]
    aux = jnp.concatenate([jnp.broadcast_to(dest, (SUBLANES, tm)), comb,
                           jnp.zeros((LANES - 2 * SUBLANES, tm), F32)], axis=0).T
    lane = lax.broadcasted_iota(jnp.int32, (tm, LANES), 1)
    comb_tok = jnp.where((lane >= SUBLANES) & (lane < 2 * SUBLANES), aux, 0.0)
    c_hi = comb_tok.astype(BF16)
    c_mid = (comb_tok - c_hi.astype(F32)).astype(BF16)
    slot_t = lax.broadcasted_iota(jnp.int32, (tm, cap), 1).astype(F32)
    pt_scr[...] = jnp.where(slot_t == aux[:, 0:1], 1.0, 0.0).astype(BF16)
    slot = lax.broadcasted_iota(jnp.int32, (cap, tm), 0).astype(F32)
    perm = jnp.where(slot == dest, 1.0, 0.0).astype(BF16)
    hs_scr[:, :half] = _dot(perm, hb_scr[:, :half]).astype(BF16)
    hs_scr[:, half:] = _dot(perm, hb_scr[:, half:]).astype(BF16)
    cs2 = _dot(perm, jnp.concatenate([c_hi, c_mid], axis=1))
    cs_scr[...] = cs2[:, :LANES] + cs2[:, LANES:]
    os_scr[...] = jnp.zeros_like(os_scr)

    for gi in range(N_GROUPS):
        cols = slice(gi * cw, (gi + 1) * cw)

        def block(i, carry, gi=gi, cols=cols):
            r0 = pl.multiple_of((first[gi] + i) * MOE_BLOCK, MOE_BLOCK)
            rows = pl.ds(r0, MOE_BLOCK)
            hs = hs_scr[rows, :]
            cs = cs_scr[rows, :]
            hid = _silu(_dot(hs, wg_ref[:, cols])) * _dot(hs, wu_ref[:, cols])
            hid = jnp.concatenate(
                [hid[:, e * EXP_FF:(e + 1) * EXP_FF] * cs[:, SUBLANES + e:SUBLANES + e + 1]
                 for e in range(EXP_PER_GROUP)], axis=1)
            os_scr[rows, :] = _dot(hid.astype(BF16), wd_ref[cols, :]).astype(BF16)
            return carry

        lax.fori_loop(0, count[gi], block, 0)

    y = jnp.concatenate([_dot(pt_scr[...], os_scr[:, :half]), _dot(pt_scr[...], os_scr[:, half:])], axis=1)
    r = ALPHA * x_ref[0] + m[5:6] * y
    o_ref[0] = _layer_norm(r, lg_ref[...], lb_ref[...])


def _moe(x, mod, wr, wg, wu, wd, ln_g, ln_b):
    b, l, d = x.shape
    tm = MOE_TILE
    cap = (tm + N_GROUPS * (MOE_BLOCK - 1)) // MOE_BLOCK * MOE_BLOCK
    cw = EXP_PER_GROUP * EXP_FF
    per_batch = mod.shape[0] > 1
    mod_map = (lambda i, t: (i, 0, 0)) if per_batch else (lambda i, t: (0, 0, 0))
    tri = jnp.asarray(np.triu(np.ones((tm, tm), np.float32), k=1), BF16)
    full = lambda a: pl.BlockSpec(a.shape, lambda i, t: (0,) * a.ndim, pipeline_mode=pl.Buffered(1))
    return pl.pallas_call(
        _moe_kernel,
        out_shape=jax.ShapeDtypeStruct((b, l, d), F32),
        grid=(b, l // tm),
        in_specs=[
            pl.BlockSpec((1, tm, d), lambda i, t: (i, t, 0)),
            pl.BlockSpec((1, 6, d), mod_map),
            full(wr), full(tri), full(wg), full(wu), full(wd), full(ln_g), full(ln_b),
        ],
        out_specs=pl.BlockSpec((1, tm, d), lambda i, t: (i, t, 0)),
        scratch_shapes=[pltpu.VMEM((tm, d), BF16), pltpu.VMEM((cap, d), BF16), pltpu.VMEM((cap, d), BF16),
                        pltpu.VMEM((tm, cap), BF16), pltpu.VMEM((cap, LANES), F32)],
        compiler_params=_cparams(("arbitrary", "arbitrary")),
        name="hier_moe_ln",
    )(x, mod, wr, tri, wg, wu, wd, ln_g, ln_b)


def _rope_swap_perm():
    idx = np.arange(MLA_DR)
    return idx ^ (MLA_DR // 4)


def _rope_tables(n_tok):
    rows = n_tok // GRID_W
    pos_r = jnp.repeat(jnp.arange(rows, dtype=F32), GRID_W)
    pos_c = (jnp.arange(rows * GRID_W) % GRID_W).astype(F32)
    n_freq = MLA_DR // 4
    inv = ROPE_BASE ** (-jnp.arange(n_freq, dtype=F32) / n_freq)
    ang = jnp.stack([pos_r[:, None] * inv, pos_c[:, None] * inv], axis=1)
    cos, sin = jnp.cos(ang), jnp.sin(ang)
    cos32 = jnp.stack([cos, cos], axis=2).reshape(n_tok, MLA_DR)
    sin32 = jnp.stack([-sin, sin], axis=2).reshape(n_tok, MLA_DR)
    pad_hi = LANES - MLA_DN - MLA_DR
    cos_t = jnp.concatenate([jnp.ones((n_tok, MLA_DN), F32), cos32, jnp.zeros((n_tok, pad_hi), F32)], axis=1)
    sin_t = jnp.concatenate([jnp.zeros((n_tok, MLA_DN), F32), sin32, jnp.zeros((n_tok, pad_hi), F32)], axis=1)
    return cos_t, sin_t


def _ab_weights(w_in, w_uq, w_ukv):
    perm = _rope_swap_perm()
    pad_hi = LANES - MLA_DN - MLA_DR
    w_h = w_in[:, :5 * HA_W]
    cq = w_in[:, 5 * HA_W:5 * HA_W + MLA_Q_RANK]
    ckv = w_in[:, 5 * HA_W + MLA_Q_RANK:5 * HA_W + MLA_Q_RANK + MLA_KV_RANK]
    kr = w_in[:, 5 * HA_W + MLA_Q_RANK + MLA_KV_RANK:]
    d = w_in.shape[0]
    z_lo = jnp.zeros((d, MLA_DN), F32)
    z_hi = jnp.zeros((d, pad_hi), F32)
    w_m = jnp.concatenate([cq, ckv, z_lo, kr, z_hi, z_lo, kr[:, perm], z_hi], axis=1)
    uq = w_uq.reshape(MLA_Q_RANK, MLA_HEADS, MLA_DN + MLA_DR)
    q_nope, q_rope = uq[..., :MLA_DN], uq[..., MLA_DN:]
    zq_hi = jnp.zeros((MLA_Q_RANK, MLA_HEADS, pad_hi), F32)
    wq1 = jnp.concatenate([q_nope, q_rope, zq_hi], axis=-1).reshape(MLA_Q_RANK, -1)
    wq2 = jnp.concatenate([jnp.zeros_like(q_nope), q_rope[..., perm], zq_hi], axis=-1).reshape(MLA_Q_RANK, -1)
    ukv = w_ukv.reshape(MLA_KV_RANK, MLA_HEADS, MLA_DN + MLA_DV)
    k_nope, v = ukv[..., :MLA_DN], ukv[..., MLA_DN:]
    wk = jnp.concatenate([k_nope, jnp.zeros((MLA_KV_RANK, MLA_HEADS, LANES - MLA_DN), F32)], axis=-1)
    wk = wk.reshape(MLA_KV_RANK, -1)
    v_pairs = v.reshape(MLA_KV_RANK, MLA_HEADS // 2, 2, MLA_DV)
    zv = jnp.zeros_like(v_pairs[:, :, 0])
    wv = jnp.stack([jnp.concatenate([v_pairs[:, :, 0], zv], axis=-1),
                    jnp.concatenate([zv, v_pairs[:, :, 1]], axis=-1)], axis=2).reshape(MLA_KV_RANK, -1)
    bf = lambda a: a.astype(BF16)
    return bf(w_h), bf(w_m), bf(wq1), bf(wq2), bf(wk), bf(wv)


def _moe_weights(w_group, w_expert, w_gate, w_up, w_down):
    d = w_group.shape[0]
    wr = jnp.concatenate([w_expert, w_group, jnp.zeros((d, LANES - N_EXPERTS - N_GROUPS), F32)], axis=1)
    wr_hi = wr.astype(BF16)
    wr = jnp.concatenate([wr_hi, (wr - wr_hi.astype(F32)).astype(BF16)], axis=1)
    wg = jnp.transpose(w_gate, (1, 0, 2)).reshape(d, N_EXPERTS * EXP_FF).astype(BF16)
    wu = jnp.transpose(w_up, (1, 0, 2)).reshape(d, N_EXPERTS * EXP_FF).astype(BF16)
    wd = w_down.reshape(N_EXPERTS * EXP_FF, d).astype(BF16)
    return wr, wg, wu, wd


def kernel(x_prompt, x_sample, state_hgrn_fwd, state_hgrn_bwd, cache_mla_ckv, cache_mla_krope, c, c_ctx, mod_w, mod_b, ln_g, ln_b, ab_w_in, ab_w_out, hgrn_lb_logits, hgrn_norm_g, mla_q_norm_g, mla_w_uq, mla_kv_norm_g, mla_w_ukv, cd_w_in, cd_w_out, sc_conv_w, cf_conv_w, cf_conv_b, cf_ln_g, cf_ln_b, moe_w_group, moe_w_expert, moe_w_gate, moe_w_up, moe_w_down):
    dec_b = x_sample.shape[0]
    d = D_MODEL
    cvec = jnp.concatenate([c, c_ctx[None, :], jnp.zeros((SUBLANES - dec_b - 1, d), F32)], axis=0)
    mods = _mod_vectors(cvec, mod_w, mod_b)
    rope_tabs = _rope_tables(x_sample.shape[1])
    xp, xs = x_prompt, x_sample
    new_sf = new_sb = new_ckv = new_kr = None
    for l in range(DEPTH):
        mod_lat = mods[l, :dec_b].reshape(dec_b, 6, d)
        mod_ctx = mods[l, dec_b:dec_b + 1].reshape(1, 6, d)
        row = lambda a: a.reshape(1, -1)
        if l % 2 == 0:
            e = l // 2
            w_h, w_m, wq1, wq2, wk, wv = _ab_weights(ab_w_in[e], mla_w_uq[e], mla_w_ukv[e])
            w_out = ab_w_out[e].astype(BF16)
            qg, kvg, gn = row(mla_q_norm_g[e]), row(mla_kv_norm_g[e]), row(hgrn_norm_g[e])
            ph_p, q_p, k_p, v_p, ckv_p, kr_p = _ab_inproj(xp, mod_ctx, w_h, w_m, None, qg, kvg, wq1, None, wk, wv)
            of_p, ob_p, sf, sb = _hgrn(ph_p, hgrn_lb_logits, None, None, e)
            om_p = _attention(q_p, k_p, v_p, None, None)
            xp = _outproj(xp, mod_ctx, (of_p, ob_p, ph_p, gn), om_p, w_out, row(ln_g[l, 0]), row(ln_b[l, 0]), True)
            ph_s, q_s, k_s, v_s, _, _ = _ab_inproj(xs, mod_lat, w_h, w_m, rope_tabs, qg, kvg, wq1, wq2, wk, wv)
            of_s, ob_s, _, _ = _hgrn(ph_s, hgrn_lb_logits, state_hgrn_fwd[:, e], state_hgrn_bwd[:, e], e)
            kc, vc = _mla_ctx(cache_mla_ckv[:, e], cache_mla_krope[:, e], wk, wv)
            om_s = _attention(q_s, k_s, v_s, kc, vc)
            xs = _outproj(xs, mod_lat, (of_s, ob_s, ph_s, gn), om_s, w_out, row(ln_g[l, 0]), row(ln_b[l, 0]), True)
            new_sf, new_sb, new_ckv = sf, sb, ckv_p
            new_kr = kr_p[:, :, MLA_DN:MLA_DN + MLA_DR]
        else:
            jx = l // 2
            w1 = cd_w_in[jx].astype(BF16)
            w_out = cd_w_out[jx].astype(BF16)
            cd = (sc_conv_w[jx], cf_conv_w[jx], row(cf_conv_b[jx]), row(cf_ln_g[jx]), row(cf_ln_b[jx]))
            (p1_p,) = _inproj(xp, mod_ctx, [w1])
            ysc_p, ycf_p = _conv_mixers(p1_p, *cd)
            xp = _outproj(xp, mod_ctx, ysc_p, ycf_p, w_out, row(ln_g[l, 0]), row(ln_b[l, 0]), False)
            (p1_s,) = _inproj(xs, mod_lat, [w1])
            ysc_s, ycf_s = _conv_mixers(p1_s, *cd)
            xs = _outproj(xs, mod_lat, ysc_s, ycf_s, w_out, row(ln_g[l, 0]), row(ln_b[l, 0]), False)
        wr, wg, wu, wd = _moe_weights(moe_w_group[l], moe_w_expert[l], moe_w_gate[l], moe_w_up[l], moe_w_down[l])
        xp_t = xp.reshape(-1, MOE_TILE, d)
        xp = _moe(xp_t, mod_ctx, wr, wg, wu, wd, row(ln_g[l, 1]), row(ln_b[l, 1])).reshape(xp.shape)
        xs = _moe(xs, mod_lat, wr, wg, wu, wd, row(ln_g[l, 1]), row(ln_b[l, 1]))
    return (xp, xs, new_sf[:, None], new_sb[:, None], new_ckv[:, None], new_kr[:, None])
```

```python
import functools

import numpy as np
import jax
import jax.numpy as jnp
from jax import lax
from jax.experimental import pallas as pl
from jax.experimental.pallas import tpu as pltpu

F32 = jnp.float32
BF16 = jnp.bfloat16
HIGHEST = lax.Precision.HIGHEST

D_MODEL = 1024
DEPTH = 2
GRID_W = 64
N_EVEN = (DEPTH + 1) // 2
HA_HEADS = 4
HA_DK = 128
HA_DV = 128
HA_W = HA_HEADS * HA_DK
CHUNK = 32
MLA_HEADS = 8
MLA_DN = 64
MLA_DR = 32
MLA_DV = 64
MLA_Q_RANK = 384
MLA_KV_RANK = 256
ROPE_BASE = 10000.0
SC_W = 512
SC_K = 3
CF_W = 512
CF_K = 31
N_GROUPS = 4
EXP_PER_GROUP = 8
N_EXPERTS = N_GROUPS * EXP_PER_GROUP
EXP_FF = 128
ALPHA = (2.0 * DEPTH) ** 0.25
LOG2_E = 1.4426950408889634
LN_EPS = 1e-5
RMS_EPS = 1e-6

LANES = 128
SUBLANES = 8
VMEM_LIMIT = 56 * 1024 * 1024

HEAD_PAD = LANES
MLA_SLAB = MLA_Q_RANK + MLA_KV_RANK + 2 * LANES
TOK_TILE = 256
GROUP_ROWS = 128
N_LEVELS = 5
SAFE_LOG_DECAY = 60.0
ATTN_KEY_BLOCK = 512
MOE_TILE = 512
MOE_BLOCK = 128
HALO = 16


def _cparams(sem):
    return pltpu.CompilerParams(dimension_semantics=sem, vmem_limit_bytes=VMEM_LIMIT)


def _silu(x):
    return x / (1.0 + jnp.exp(-x))


def _sigmoid(x):
    return 1.0 / (1.0 + jnp.exp(-x))


def _layer_norm(r, g, b):
    mu = jnp.mean(r, axis=-1, keepdims=True)
    d = r - mu
    var = jnp.mean(d * d, axis=-1, keepdims=True)
    return d * lax.rsqrt(var + LN_EPS) * g + b


def _dot(a, b):
    return jnp.dot(a, b, preferred_element_type=F32)


def _dot_nt(a, b):
    return lax.dot_general(a, b, (((1,), (1,)), ((), ())), preferred_element_type=F32)


def _dot_tn(a, b):
    return lax.dot_general(a, b, (((0,), (0,)), ((), ())), preferred_element_type=F32)


def _mod_kernel(c_ref, w_ref, b_ref, o_ref):
    s = _silu(c_ref[...])
    o_ref[0] = jnp.dot(s, w_ref[0], precision=HIGHEST, preferred_element_type=F32) + b_ref[0]


def _mod_vectors(cvec, mod_w, mod_b):
    n_out = mod_w.shape[-1]
    tn = 1536
    return pl.pallas_call(
        _mod_kernel,
        out_shape=jax.ShapeDtypeStruct((DEPTH, SUBLANES, n_out), F32),
        grid=(DEPTH, n_out // tn),
        in_specs=[
            pl.BlockSpec((SUBLANES, D_MODEL), lambda l, j: (0, 0)),
            pl.BlockSpec((1, D_MODEL, tn), lambda l, j: (l, 0, j)),
            pl.BlockSpec((1, 1, tn), lambda l, j: (l, 0, j)),
        ],
        out_specs=pl.BlockSpec((1, SUBLANES, tn), lambda l, j: (l, 0, j)),
        compiler_params=_cparams(("arbitrary", "arbitrary")),
        name="mod_vectors",
    )(cvec, mod_w, mod_b.reshape(DEPTH, 1, n_out))


def _inproj_kernel(*refs, n_w):
    x_ref, mod_ref = refs[0], refs[1]
    w_refs = refs[2:2 + n_w]
    o_refs = refs[2 + n_w:]
    m = mod_ref[0]
    h = (x_ref[0] * (1.0 + m[1:2]) + m[0:1]).astype(BF16)
    for w_ref, o_ref in zip(w_refs, o_refs):
        o_ref[0] = _dot(h, w_ref[...])


def _inproj(x, mod, weights):
    b, l, d = x.shape
    tm = TOK_TILE
    per_batch = mod.shape[0] > 1
    mod_map = (lambda i, j: (i, 0, 0)) if per_batch else (lambda i, j: (0, 0, 0))
    in_specs = [pl.BlockSpec((1, tm, d), lambda i, j: (i, j, 0)),
                pl.BlockSpec((1, 6, d), mod_map)]
    in_specs += [pl.BlockSpec(w.shape, lambda i, j: (0, 0)) for w in weights]
    out_shape = [jax.ShapeDtypeStruct((b, l, w.shape[1]), F32) for w in weights]
    out_specs = [pl.BlockSpec((1, tm, w.shape[1]), lambda i, j: (i, j, 0)) for w in weights]
    return pl.pallas_call(
        functools.partial(_inproj_kernel, n_w=len(weights)),
        out_shape=out_shape,
        grid=(b, l // tm),
        in_specs=in_specs,
        out_specs=out_specs,
        compiler_params=_cparams(("parallel", "parallel")),
        name="modulate_inproj",
    )(x, mod, *weights)


def _hgrn_tables():
    n = GROUP_ROWS
    t = np.arange(n)[:, None]
    j = np.arange(n)[None, :]
    same_chunk = (t // CHUNK) == (j // CHUNK)
    e_f, e_b = [], []
    lv_f = np.full((n, n), -1, np.int32)
    for lvl in range(N_LEVELS):
        m = CHUNK >> (lvl + 1)
        blk0 = (t // (2 * m)) * (2 * m)
        r = blk0 + m - 1
        upper = t > r
        ef = np.where(upper, (j > r) & (j <= t), (j > t) & (j <= r))
        r2 = blk0 + m
        lower = t < r2
        eb = np.where(lower, (j >= t) & (j < r2), (j >= r2) & (j < t))
        e_f.append(ef)
        e_b.append(eb)
        s = np.arange(n)[None, :]
        same_blk = (t // (2 * m)) == (s // (2 * m))
        q_side = (t % (2 * m)) >= m
        k_side = (s % (2 * m)) < m
        lv_f[same_blk & q_side & k_side] = lvl
    lv_f[np.arange(n), np.arange(n)] = N_LEVELS
    e_f = np.concatenate(e_f, axis=0).astype(np.float32)
    e_b = np.concatenate(e_b, axis=0).astype(np.float32)
    e = np.stack([e_f, e_b])
    cum = np.stack([same_chunk & (j <= t), same_chunk & (j >= t)]).astype(np.float32)
    lv = np.stack([lv_f, lv_f.T])
    return e, cum, lv


def _hgrn_kernel(qf_ref, vf_ref, ff_ref, qb_ref, vb_ref, fb_ref, lbl_ref, e_ref, cum_ref, lv_ref, s0f_ref, s0b_ref,
                 of_ref, ob_ref, sf_ref, sb_ref,
                 st_scr, qe_scr, kd_scr, sc_scr, v_scr, oi_scr, dec_scr, *, lt, slot, has_state):
    i = pl.program_id(1)
    n_i = pl.num_programs(1)
    n_chunks = lt // CHUNK
    n_groups = lt // GROUP_ROWS
    chunks_per_group = GROUP_ROWS // CHUNK

    @pl.when(i == 0)
    def _():
        for d, s0_ref in enumerate((s0f_ref, s0b_ref)):
            for h in range(HA_HEADS):
                if has_state:
                    st_scr[d, h] = s0_ref[0, h].T
                else:
                    st_scr[d, h] = jnp.zeros((HA_DV, HA_DK), F32)

    lg = lbl_ref[...]
    n_slots = lg.shape[0]
    mx = lg[0]
    for s in range(1, n_slots):
        mx = jnp.maximum(mx, lg[s])
    ex = [jnp.exp(lg[s] - mx) for s in range(n_slots)]
    den = ex[0]
    for s in range(1, n_slots):
        den = den + ex[s]
    num = ex[0]
    for s in range(1, slot + 1):
        num = num + ex[s]
    lb_all = num / den

    dirs = ((qf_ref, vf_ref, ff_ref), (qb_ref, vb_ref, fb_ref))
    head_cols = [slice(h * HA_DK, (h + 1) * HA_DK) for h in range(HA_HEADS)]

    def gates(d, rows, cols):
        q_ref, _, f_ref = dirs[d]
        q = _silu(q_ref[0, rows, cols])
        lb_h = lb_all[d:d + 1, cols]
        f = lb_h + (1.0 - lb_h) * _sigmoid(f_ref[0, rows, cols])
        g = jnp.log(f)
        g_hi = g.astype(BF16)
        g_lo = (g - g_hi.astype(F32)).astype(BF16)
        return q, 1.0 - f, g_hi, g_lo

    tot_min = jnp.zeros((1, HA_DK), F32)
    for d in range(2):

        def group_step(grp, tot_min, d=d):
            r0 = pl.multiple_of(grp * GROUP_ROWS, GROUP_ROWS)
            rows = pl.ds(r0, GROUP_ROWS)
            cum = cum_ref[d]
            lv = lv_ref[d]
            for cols in head_cols:
                q, k, g_hi, g_lo = gates(d, rows, cols)
                bcum = _dot(cum, g_hi) + _dot(cum, g_lo)
                v_scr[d, rows, cols] = dirs[d][1][0, rows, cols].astype(BF16)
                qe = (q * jnp.exp(bcum)).astype(BF16)
                qe_scr[d, rows, cols] = qe
                p = _dot_nt(qe, (k * jnp.exp(-bcum)).astype(BF16))
                sc_scr[d, rows, cols] = jnp.where(lv >= 0, p, 0.0).astype(BF16)
                for cc in range(chunks_per_group):
                    edge = cc * CHUNK + (CHUNK - 1 if d == 0 else 0)
                    tot = bcum[edge:edge + 1]
                    b_c = bcum[cc * CHUNK:(cc + 1) * CHUNK]
                    k_c = k[cc * CHUNK:(cc + 1) * CHUNK]
                    kd = (k_c * jnp.exp(tot - b_c)).astype(BF16)
                    kd_scr[d, pl.ds(r0 + cc * CHUNK, CHUNK), cols] = kd
                    dec_scr[d, grp * chunks_per_group + cc, :, cols] = jnp.exp(tot)
                    tot_min = jnp.minimum(tot_min, tot)
            return tot_min

        tot_min = lax.fori_loop(0, n_groups, group_step, tot_min)

    def one_factor():
        for d in range(2):
            for grp in range(n_groups):
                rows = slice(grp * GROUP_ROWS, (grp + 1) * GROUP_ROWS)
                for cols in head_cols:
                    oi_scr[d, rows, cols] = _dot(sc_scr[d, rows, cols], v_scr[d, rows, cols])

    def per_level():
        for d in range(2):

            def group_step(grp, carry, d=d):
                rows = pl.ds(pl.multiple_of(grp * GROUP_ROWS, GROUP_ROWS), GROUP_ROWS)
                lv = lv_ref[d]
                for cols in head_cols:
                    q, k, g_hi, g_lo = gates(d, rows, cols)
                    x = jnp.exp(_dot(e_ref[d], g_hi) + _dot(e_ref[d], g_lo))
                    sc = jnp.where(lv == N_LEVELS, _dot_nt(q.astype(BF16), k.astype(BF16)), 0.0)
                    for lvl in range(N_LEVELS):
                        xl = x[lvl * GROUP_ROWS:(lvl + 1) * GROUP_ROWS]
                        p = _dot_nt((q * xl).astype(BF16), (k * xl).astype(BF16))
                        sc = jnp.where(lv == lvl, p, sc)
                    oi_scr[d, rows, cols] = _dot(sc.astype(BF16), v_scr[d, rows, cols])
                return carry

            lax.fori_loop(0, n_groups, group_step, 0)

    lax.cond(jnp.min(tot_min) > -SAFE_LOG_DECAY, one_factor, per_level)

    out_refs = (of_ref, ob_ref)

    def chunk_step(c, carry):
        for d in range(2):
            cc = c if d == 0 else n_chunks - 1 - c
            r0 = pl.multiple_of(cc * CHUNK, CHUNK)
            rows = pl.ds(r0, CHUNK)
            for h in range(HA_HEADS):
                cols = slice(h * HA_DK, (h + 1) * HA_DK)
                st = st_scr[d, h]
                o_state = _dot_nt(qe_scr[d, rows, cols], st.astype(BF16))
                out_refs[d][0, rows, cols] = oi_scr[d, rows, cols] + o_state
                upd = _dot_tn(v_scr[d, rows, cols], kd_scr[d, rows, cols])
                st_scr[d, h] = st * dec_scr[d, cc, :, cols] + upd
        return carry

    lax.fori_loop(0, n_chunks, chunk_step, 0)

    @pl.when(i == n_i - 1)
    def _():
        for d, s_ref in enumerate((sf_ref, sb_ref)):
            for h in range(HA_HEADS):
                s_ref[0, h] = st_scr[d, h].T


def _hgrn(p_h, lb_logits, s0_f, s0_b, slot):
    b, l, _ = p_h.shape
    lt = TOK_TILE
    n_t = l // lt
    has_state = s0_f is not None
    if not has_state:
        s0_f = jnp.zeros((1, HA_HEADS, HA_DK, HA_DV), F32)
        s0_b = s0_f
    e_np, cum_np, lv_np = _hgrn_tables()
    e_mat = jnp.asarray(e_np, BF16)
    cum = jnp.asarray(cum_np, BF16)
    lv = jnp.asarray(lv_np, jnp.int32)
    w = HA_W

    def sec(idx, rev):
        if rev:
            return pl.BlockSpec((1, lt, w), lambda bi, i: (bi, n_t - 1 - i, idx))
        return pl.BlockSpec((1, lt, w), lambda bi, i: (bi, i, idx))

    state_map = (lambda bi, i: (bi, 0, 0, 0)) if has_state else (lambda bi, i: (0, 0, 0, 0))
    in_specs = [
        sec(0, False), sec(3, False), sec(1, False),
        sec(0, True), sec(3, True), sec(2, True),
        pl.BlockSpec(lb_logits.shape, lambda bi, i: (0, 0, 0)),
        pl.BlockSpec(e_mat.shape, lambda bi, i: (0, 0, 0)),
        pl.BlockSpec(cum.shape, lambda bi, i: (0, 0, 0)),
        pl.BlockSpec(lv.shape, lambda bi, i: (0, 0, 0)),
        pl.BlockSpec((1, HA_HEADS, HA_DK, HA_DV), state_map),
        pl.BlockSpec((1, HA_HEADS, HA_DK, HA_DV), state_map),
    ]
    out_shape = [
        jax.ShapeDtypeStruct((b, l, w), F32),
        jax.ShapeDtypeStruct((b, l, w), F32),
        jax.ShapeDtypeStruct((b, HA_HEADS, HA_DK, HA_DV), F32),
        jax.ShapeDtypeStruct((b, HA_HEADS, HA_DK, HA_DV), F32),
    ]
    out_specs = [
        pl.BlockSpec((1, lt, w), lambda bi, i: (bi, i, 0)),
        pl.BlockSpec((1, lt, w), lambda bi, i: (bi, n_t - 1 - i, 0)),
        pl.BlockSpec((1, HA_HEADS, HA_DK, HA_DV), lambda bi, i: (bi, 0, 0, 0)),
        pl.BlockSpec((1, HA_HEADS, HA_DK, HA_DV), lambda bi, i: (bi, 0, 0, 0)),
    ]
    scratch = [
        pltpu.VMEM((2, HA_HEADS, HA_DV, HA_DK), F32),
        pltpu.VMEM((2, lt, w), BF16),
        pltpu.VMEM((2, lt, w), BF16),
        pltpu.VMEM((2, lt, w), BF16),
        pltpu.VMEM((2, lt, w), BF16),
        pltpu.VMEM((2, lt, w), F32),
        pltpu.VMEM((2, lt // CHUNK, 1, w), F32),
    ]
    return pl.pallas_call(
        functools.partial(_hgrn_kernel, lt=lt, slot=slot, has_state=has_state),
        out_shape=out_shape,
        grid=(b, n_t),
        in_specs=in_specs,
        out_specs=out_specs,
        scratch_shapes=scratch,
        compiler_params=_cparams(("parallel", "arbitrary")),
        name="hgrn2_scan",
    )(p_h, p_h, p_h, p_h, p_h, p_h, lb_logits, e_mat, cum, lv, s0_f, s0_b)


def _ab_inproj_kernel(*refs, rope, key_major):
    if rope:
        (x_ref, mod_ref, wh_ref, wm_ref, cos_ref, sin_ref, qg_ref, kvg_ref, wq1_ref, wq2_ref, wk_ref,
         wv_ref) = refs[:12]
        ph_ref, q_ref, k_ref, v_ref, ckv_ref, kr_ref = refs[12:]
    else:
        x_ref, mod_ref, wh_ref, wm_ref, qg_ref, kvg_ref, wq1_ref, wk_ref, wv_ref = refs[:9]
        ph_ref, q_ref, k_ref, v_ref, ckv_ref, kr_ref = refs[9:]
    scale = (MLA_DN + MLA_DR) ** -0.5 * LOG2_E
    mod = mod_ref[0]
    hmod = (x_ref[0] * (1.0 + mod[1:2]) + mod[0:1]).astype(BF16)
    ph_ref[0] = _dot(hmod, wh_ref[...])
    pm = _dot(hmod, wm_ref[...])
    kr_ref[0] = pm[:, MLA_Q_RANK + MLA_KV_RANK:MLA_Q_RANK + MLA_KV_RANK + LANES]
    cq = pm[:, :MLA_Q_RANK]
    cq = cq * lax.rsqrt(jnp.mean(cq * cq, axis=-1, keepdims=True) + RMS_EPS) * qg_ref[...]
    cq = cq.astype(BF16)
    ckv = pm[:, MLA_Q_RANK:MLA_Q_RANK + MLA_KV_RANK]
    ckv = ckv * lax.rsqrt(jnp.mean(ckv * ckv, axis=-1, keepdims=True) + RMS_EPS) * kvg_ref[...]
    ckv_ref[0] = ckv
    ckv = ckv.astype(BF16)
    kr0 = MLA_Q_RANK + MLA_KV_RANK
    kr = pm[:, kr0:kr0 + LANES]
    qa = _dot(cq, wq1_ref[...])
    kn = _dot(ckv, wk_ref[...])
    if key_major:
        v_ref[0] = _dot_nt(wv_ref[...], ckv).astype(BF16)
    else:
        v_ref[0] = _dot(ckv, wv_ref[...]).astype(BF16)
    if rope:
        cos = cos_ref[...]
        sin = sin_ref[...]
        qb = _dot(cq, wq2_ref[...])
        kr = kr * cos + pm[:, kr0 + LANES:kr0 + 2 * LANES] * sin
    for h in range(MLA_HEADS):
        sl = slice(h * HEAD_PAD, (h + 1) * HEAD_PAD)
        qh = qa[:, sl]
        if rope:
            qh = qh * cos + qb[:, sl] * sin
        q_ref[0, :, sl] = (qh * scale).astype(BF16)
        k_ref[0, :, sl] = (kn[:, sl] + kr).astype(BF16)


def _ab_inproj(x, mod, w_h, w_m, rope_tabs, qg, kvg, wq1, wq2, wk, wv):
    b, l, d = x.shape
    tm = TOK_TILE
    rope = rope_tabs is not None
    per_batch = mod.shape[0] > 1
    mod_map = (lambda i, j: (i, 0, 0)) if per_batch else (lambda i, j: (0, 0, 0))
    full = lambda a: pl.BlockSpec(a.shape, lambda i, j: (0,) * a.ndim)
    row = lambda width: pl.BlockSpec((1, tm, width), lambda i, j: (i, j, 0))
    hw = MLA_HEADS * HEAD_PAD
    inputs = [x, mod, w_h, w_m]
    in_specs = [row(d), pl.BlockSpec((1, 6, d), mod_map), full(w_h), full(w_m)]
    if rope:
        inputs += list(rope_tabs)
        in_specs += [pl.BlockSpec((tm, LANES), lambda i, j: (j, 0))] * 2
    ws = [qg, kvg, wq1] + ([wq2] if rope else []) + [wk, wv]
    inputs += ws
    in_specs += [full(a) for a in ws]
    key_major = wv.shape[1] == MLA_KV_RANK
    vw = MLA_HEADS * MLA_DV
    shapes = [(b, l, w_h.shape[1]), (b, l, hw), (b, l, hw), (b, vw, l) if key_major else (b, l, hw),
              (b, l, MLA_KV_RANK), (b, l, LANES)]
    dtypes = [F32, BF16, BF16, BF16, F32, F32]
    out_specs = [row(s[2]) for s in shapes]
    if key_major:
        out_specs[3] = pl.BlockSpec((1, vw, tm), lambda i, j: (i, 0, j))
    return pl.pallas_call(
        functools.partial(_ab_inproj_kernel, rope=rope, key_major=key_major),
        out_shape=[jax.ShapeDtypeStruct(s, dt) for s, dt in zip(shapes, dtypes)],
        grid=(b, l // tm),
        in_specs=in_specs,
        out_specs=out_specs,
        compiler_params=_cparams(("parallel", "parallel")),
        name="ab_inproj",
    )(*inputs)


def _mla_ctx_kernel(ckv_ref, kr_ref, place_ref, wk_ref, wvt_ref, k_ref, vt_ref):
    ckv = ckv_ref[0].astype(BF16)
    kr = _dot(kr_ref[0].astype(BF16), place_ref[...])
    kn = _dot(ckv, wk_ref[...])
    vt_ref[0] = _dot_nt(wvt_ref[...], ckv).astype(BF16)
    for h in range(MLA_HEADS):
        sl = slice(h * HEAD_PAD, (h + 1) * HEAD_PAD)
        k_ref[0, :, sl] = (kn[:, sl] + kr).astype(BF16)


def _mla_ctx(ctx_ckv, ctx_kr, wk, wvt):
    b, lc, _ = ctx_ckv.shape
    hw = MLA_HEADS * HEAD_PAD
    vw = MLA_HEADS * MLA_DV
    place = np.zeros((MLA_DR, LANES), np.float32)
    place[np.arange(MLA_DR), MLA_DN + np.arange(MLA_DR)] = 1.0
    place = jnp.asarray(place, BF16)
    full = lambda a: pl.BlockSpec(a.shape, lambda i: (0,) * a.ndim)
    return pl.pallas_call(
        _mla_ctx_kernel,
        out_shape=[jax.ShapeDtypeStruct((b, lc, hw), BF16), jax.ShapeDtypeStruct((b, vw, lc), BF16)],
        grid=(b,),
        in_specs=[pl.BlockSpec((1, lc, MLA_KV_RANK), lambda i: (i, 0, 0)),
                  pl.BlockSpec((1, lc, MLA_DR), lambda i: (i, 0, 0)),
                  full(place), full(wk), full(wvt)],
        out_specs=[pl.BlockSpec((1, lc, hw), lambda i: (i, 0, 0)), pl.BlockSpec((1, vw, lc), lambda i: (i, 0, 0))],
        compiler_params=_cparams(("parallel",)),
        name="mla_ctx_keys",
    )(ctx_ckv, ctx_kr, place, wk, wvt)


def _attn_long_kernel(q_ref, k_ref, vt_ref, kc_ref, vtc_ref, o_ref, s_a, s_b, m_a, m_b):
    i = pl.program_id(2)
    tq = q_ref.shape[1]
    lk = k_ref.shape[1]
    tk = ATTN_KEY_BLOCK
    n_blk = lk // tk
    slabs = [slice(hh * HEAD_PAD, (hh + 1) * HEAD_PAD) for hh in range(2)]
    vrows = [slice(hh * MLA_DV, (hh + 1) * MLA_DV) for hh in range(2)]
    q_t = [q_ref[0, :, sl].astype(F32).T.astype(BF16) for sl in slabs]
    crow = slice(lk, lk + kc_ref.shape[1])

    @pl.when(i == 0)
    def _():
        s_b[...] = jnp.zeros(s_b.shape, F32)
        m_b[...] = jnp.zeros(m_b.shape, F32)

    def run(s_cur, m_cur, s_prv, m_prv):
        def scores(hh, k_blk, rows, m_run):
            s = _dot(k_blk, q_t[hh])
            s_cur[hh, rows, :] = s
            return jnp.maximum(m_run, jnp.max(s.reshape(s.shape[0] // SUBLANES, SUBLANES, tq), axis=0))

        def weigh(hh, vt_blk, rows, m, acc):
            p = jnp.exp2(s_prv[hh, rows, :] - m).astype(BF16)
            lhs = jnp.concatenate([vt_blk, jnp.ones((2 * SUBLANES, vt_blk.shape[1]), BF16)], axis=0)
            return acc + _dot(lhs, p)

        ms = [jnp.max(m_prv[hh], axis=0, keepdims=True) for hh in range(2)]

        def step(j, carry):
            m_runs, accs = carry
            rows = pl.ds(pl.multiple_of(j * tk, tk), tk)
            m_runs = tuple(scores(hh, k_ref[0, rows, slabs[hh]], rows, m_runs[hh]) for hh in range(2))
            accs = tuple(weigh(hh, vt_ref[0, vrows[hh], rows], rows, ms[hh], accs[hh]) for hh in range(2))
            return m_runs, accs

        carry = ((jnp.full((SUBLANES, tq), -jnp.inf, F32),) * 2,
                 (jnp.zeros((MLA_DV + 2 * SUBLANES, tq), F32),) * 2)
        m_runs, accs = lax.fori_loop(0, n_blk, step, carry, unroll=2)
        m_runs = tuple(scores(hh, kc_ref[0, :, slabs[hh]], crow, m_runs[hh]) for hh in range(2))
        accs = tuple(weigh(hh, vtc_ref[0, vrows[hh], :], crow, ms[hh], accs[hh]) for hh in range(2))
        for hh in range(2):
            m_cur[hh] = m_runs[hh]
        outs = [acc[:MLA_DV] / acc[MLA_DV:MLA_DV + 1] for acc in accs]
        o_ref[0] = jnp.concatenate(outs, axis=0).T.astype(BF16)

    @pl.when(i % 2 == 0)
    def _():
        run(s_a, m_a, s_b, m_b)

    @pl.when(i % 2 == 1)
    def _():
        run(s_b, m_b, s_a, m_a)


def _attention_long(q, k, vt, kc, vtc):
    b, l, _ = q.shape
    lk = k.shape[1]
    lc = kc.shape[1]
    tq = TOK_TILE
    pw = 2 * HEAD_PAD
    vpw = 2 * MLA_DV
    assert lk % (2 * ATTN_KEY_BLOCK) == 0
    n_q = l // tq
    return pl.pallas_call(
        _attn_long_kernel,
        out_shape=jax.ShapeDtypeStruct((b, l, MLA_HEADS * MLA_DV), BF16),
        grid=(b, MLA_HEADS // 2, n_q + 1),
        in_specs=[pl.BlockSpec((1, tq, pw), lambda bi, p, i: (bi, jnp.minimum(i, n_q - 1), p)),
                  pl.BlockSpec((1, lk, pw), lambda bi, p, i: (bi, 0, p)),
                  pl.BlockSpec((1, vpw, lk), lambda bi, p, i: (bi, p, 0)),
                  pl.BlockSpec((1, lc, pw), lambda bi, p, i: (bi, 0, p)),
                  pl.BlockSpec((1, vpw, lc), lambda bi, p, i: (bi, p, 0))],
        out_specs=pl.BlockSpec((1, tq, LANES), lambda bi, p, i: (bi, jnp.maximum(i - 1, 0), p)),
        scratch_shapes=[pltpu.VMEM((2, lk + lc, tq), F32), pltpu.VMEM((2, lk + lc, tq), F32),
                        pltpu.VMEM((2, SUBLANES, tq), F32), pltpu.VMEM((2, SUBLANES, tq), F32)],
        compiler_params=_cparams(("parallel", "parallel", "arbitrary")),
        name="mla_attention_long",
    )(q, k, vt, kc, vtc)


def _attn_kernel(*refs, has_ctx):
    if has_ctx:
        q_ref, k_ref, v_ref, kc_ref, vc_ref, o_ref = refs
    else:
        q_ref, k_ref, v_ref, o_ref = refs
    n_pairs = q_ref.shape[-1] // (2 * HEAD_PAD)
    for pair in range(n_pairs):
        out = None
        for hh in range(2):
            h0 = (2 * pair + hh) * HEAD_PAD
            sl = slice(h0, h0 + HEAD_PAD)
            q = q_ref[0, :, sl]
            s = _dot_nt(q, k_ref[0, :, sl])
            m = jnp.max(s, axis=-1, keepdims=True)
            if has_ctx:
                s2 = _dot_nt(q, kc_ref[0, :, sl])
                m = jnp.maximum(m, jnp.max(s2, axis=-1, keepdims=True))
            p = jnp.exp2(s - m)
            den = jnp.sum(p, axis=-1, keepdims=True)
            acc = _dot(p.astype(BF16), v_ref[0, :, sl])
            if has_ctx:
                p2 = jnp.exp2(s2 - m)
                den = den + jnp.sum(p2, axis=-1, keepdims=True)
                acc = acc + _dot(p2.astype(BF16), vc_ref[0, :, sl])
            o = acc / den
            out = o if out is None else out + o
        o_ref[0, :, pair * LANES:(pair + 1) * LANES] = out.astype(BF16)


def _attention(q, k, v, kc, vc):
    b, l, _ = q.shape
    lk = k.shape[1]
    tq = TOK_TILE
    pairs_per_step = MLA_HEADS // 2 if lk <= TOK_TILE else 1
    pw = 2 * HEAD_PAD * pairs_per_step
    has_ctx = kc is not None
    inputs = [q, k, v]
    in_specs = [pl.BlockSpec((1, tq, pw), lambda bi, p, i: (bi, i, p)),
                pl.BlockSpec((1, lk, pw), lambda bi, p, i: (bi, 0, p)),
                pl.BlockSpec((1, lk, pw), lambda bi, p, i: (bi, 0, p))]
    if has_ctx:
        lc = kc.shape[1]
        inputs += [kc, vc]
        in_specs += [pl.BlockSpec((1, lc, pw), lambda bi, p, i: (bi, 0, p))] * 2
    return pl.pallas_call(
        functools.partial(_attn_kernel, has_ctx=has_ctx),
        out_shape=jax.ShapeDtypeStruct((b, l, MLA_HEADS * MLA_DV), BF16),
        grid=(b, MLA_HEADS // 2 // pairs_per_step, l // tq),
        in_specs=in_specs,
        out_specs=pl.BlockSpec((1, tq, LANES * pairs_per_step), lambda bi, p, i: (bi, i, p)),
        compiler_params=_cparams(("parallel", "parallel", "arbitrary")),
        name="mla_attention",
    )(*inputs)


def _outproj_kernel(*refs, hgrn):
    if hgrn:
        x_ref, mod_ref, of_ref, ob_ref, gl_ref, gn_ref, b_ref, w_ref, lg_ref, lb_ref, o_ref = refs
        o = of_ref[0] + ob_ref[0]
        gate = _silu(gl_ref[0])
        gn = gn_ref[...]
        parts = []
        for h in range(HA_HEADS):
            sl = slice(h * HA_DV, (h + 1) * HA_DV)
            oh = o[:, sl]
            oh = oh * lax.rsqrt(jnp.mean(oh * oh, axis=-1, keepdims=True) + RMS_EPS) * gn
            parts.append((oh * gate[:, sl]).astype(BF16))
        a = jnp.concatenate(parts, axis=-1)
    else:
        x_ref, mod_ref, a_ref, b_ref, w_ref, lg_ref, lb_ref, o_ref = refs
        a = a_ref[0]
    half = a.shape[-1]
    y = _dot(a, w_ref[:half, :]) + _dot(b_ref[0], w_ref[half:, :])
    m = mod_ref[0]
    r = ALPHA * x_ref[0] + m[2:3] * y
    o_ref[0] = _layer_norm(r, lg_ref[...], lb_ref[...])


def _outproj(x, mod, a_inputs, b_in, w, ln_g, ln_b, hgrn):
    b, l, d = x.shape
    tm = TOK_TILE
    per_batch = mod.shape[0] > 1
    mod_map = (lambda i, j: (i, 0, 0)) if per_batch else (lambda i, j: (0, 0, 0))
    row = lambda width: pl.BlockSpec((1, tm, width), lambda i, j: (i, j, 0))
    full = lambda a: pl.BlockSpec(a.shape, lambda i, j: (0,) * a.ndim)
    inputs = [x, mod]
    in_specs = [row(d), pl.BlockSpec((1, 6, d), mod_map)]
    if hgrn:
        o_f, o_b, p_h, g_norm = a_inputs
        inputs += [o_f, o_b, p_h, g_norm]
        in_specs += [row(HA_W), row(HA_W),
                     pl.BlockSpec((1, tm, HA_W), lambda i, j: (i, j, 4)), full(g_norm)]
    else:
        inputs += [a_inputs]
        in_specs += [row(a_inputs.shape[-1])]
    inputs += [b_in, w, ln_g, ln_b]
    in_specs += [row(b_in.shape[-1]), full(w), full(ln_g), full(ln_b)]
    return pl.pallas_call(
        functools.partial(_outproj_kernel, hgrn=hgrn),
        out_shape=jax.ShapeDtypeStruct((b, l, d), F32),
        grid=(b, l // tm),
        in_specs=in_specs,
        out_specs=row(d),
        compiler_params=_cparams(("parallel", "parallel")),
        name="outproj_ln",
    )(*inputs)


def _conv_kernel(pm_ref, pp_ref, pn_ref, scw_ref, cfw_ref, cfb_ref, cfg_ref, cfbeta_ref,
                 ysc_ref, ycf_ref, ext_sc, ext_cf, *, lt):
    i = pl.program_id(1)
    n_i = pl.num_programs(1)
    w = SC_W

    def sc_in(p):
        return p[:, w:2 * w] * p[:, 2 * w:3 * w]

    def cf_in(p):
        return p[:, 3 * w:3 * w + CF_W] * _sigmoid(p[:, 3 * w + CF_W:3 * w + 2 * CF_W])

    pm = pm_ref[0]
    pp = pp_ref[0]
    pn = pn_ref[0]
    has_prev = i > 0
    has_next = i < n_i - 1
    n_ext = lt + 2 * HALO
    for ext, conv_in in ((ext_sc, sc_in), (ext_cf, cf_in)):
        ext[0, 0:HALO, :] = jnp.where(has_prev, conv_in(pp), 0.0)
        ext[0, HALO:HALO + lt, :] = conv_in(pm)
        ext[0, HALO + lt:, :] = jnp.where(has_next, conv_in(pn), 0.0)
    for s in range(1, SUBLANES):
        ext_cf[s, 0:n_ext - SUBLANES, :] = ext_cf[0, s:s + n_ext - SUBLANES, :]
    sc_shifts = sorted({(HALO - SC_K // 2 + j) % SUBLANES for j in range(SC_K)} - {0})
    for s in sc_shifts:
        ext_sc[s, 0:n_ext - SUBLANES, :] = ext_sc[0, s:s + n_ext - SUBLANES, :]

    def tap(ext, off, rows):
        s = off % SUBLANES
        return ext[s, off - s:off - s + rows, :]

    rb = 32
    for r in range(lt // rb):
        base = HALO + r * rb
        acc = None
        for j in range(SC_K):
            term = jnp.tile(scw_ref[j], (rb // SUBLANES, 1)) * tap(ext_sc, base - SC_K // 2 + j, rb)
            acc = term if acc is None else acc + term
        ysc_ref[0, r * rb:(r + 1) * rb, :] = (pm[r * rb:(r + 1) * rb, 0:w] * acc).astype(BF16)
        acc = None
        for j in range(CF_K):
            term = jnp.tile(cfw_ref[j], (rb // SUBLANES, 1)) * tap(ext_cf, base - CF_K // 2 + j, rb)
            acc = term if acc is None else acc + term
        u = _layer_norm(acc + cfb_ref[...], cfg_ref[...], cfbeta_ref[...])
        ycf_ref[0, r * rb:(r + 1) * rb, :] = _silu(u).astype(BF16)


def _conv_mixers(p1, sc_w, cf_w, cf_b, cf_g, cf_beta):
    b, l, width = p1.shape
    lt = TOK_TILE
    hb = lt // HALO
    n_h = l // HALO
    sc_w = jnp.broadcast_to(sc_w[:, None, :], (sc_w.shape[0], SUBLANES, sc_w.shape[1]))
    cf_w = jnp.broadcast_to(cf_w[:, None, :], (cf_w.shape[0], SUBLANES, cf_w.shape[1]))
    full = lambda a: pl.BlockSpec(a.shape, lambda bi, i: (0,) * a.ndim)
    return pl.pallas_call(
        functools.partial(_conv_kernel, lt=lt),
        out_shape=[jax.ShapeDtypeStruct((b, l, SC_W), BF16), jax.ShapeDtypeStruct((b, l, CF_W), BF16)],
        grid=(b, l // lt),
        in_specs=[
            pl.BlockSpec((1, lt, width), lambda bi, i: (bi, i, 0)),
            pl.BlockSpec((1, HALO, width), lambda bi, i: (bi, jnp.maximum(i * hb - 1, 0), 0)),
            pl.BlockSpec((1, HALO, width), lambda bi, i: (bi, jnp.minimum((i + 1) * hb, n_h - 1), 0)),
            full(sc_w), full(cf_w), full(cf_b), full(cf_g), full(cf_beta),
        ],
        out_specs=[pl.BlockSpec((1, lt, SC_W), lambda bi, i: (bi, i, 0)),
                   pl.BlockSpec((1, lt, CF_W), lambda bi, i: (bi, i, 0))],
        scratch_shapes=[pltpu.VMEM((SUBLANES, lt + 2 * HALO, SC_W), F32),
                        pltpu.VMEM((SUBLANES, lt + 2 * HALO, CF_W), F32)],
        compiler_params=_cparams(("parallel", "parallel")),
        name="conv_mixers",
    )(p1, p1, p1, sc_w, cf_w, cf_b, cf_g, cf_beta)


def _route_t(lt):
    t = lt.shape[1]
    row = lax.broadcasted_iota(jnp.int32, (SUBLANES, t), 0).astype(F32)
    neg = -jnp.inf
    big = float(LANES)
    gl = jnp.where(row < N_GROUPS, lt[N_EXPERTS:N_EXPERTS + SUBLANES], neg)
    gmax = jnp.max(gl, axis=0, keepdims=True)
    p_g = 1.0 / jnp.sum(jnp.exp(gl - gmax), axis=0, keepdims=True)
    g_sel = jnp.min(jnp.where(gl == gmax, row, big), axis=0, keepdims=True)
    el = lt[0:EXP_PER_GROUP]
    for gi in range(1, N_GROUPS):
        el = jnp.where(g_sel == gi, lt[gi * EXP_PER_GROUP:(gi + 1) * EXP_PER_GROUP], el)
    v1 = jnp.max(el, axis=0, keepdims=True)
    i1 = jnp.min(jnp.where(el == v1, row, big), axis=0, keepdims=True)
    el2 = jnp.where(row == i1, neg, el)
    v2 = jnp.max(el2, axis=0, keepdims=True)
    i2 = jnp.min(jnp.where(el2 == v2, row, big), axis=0, keepdims=True)
    e2 = jnp.exp(v2 - v1)
    w1 = p_g / (1.0 + e2)
    w2 = p_g * e2 / (1.0 + e2)
    comb = jnp.where(row == i1, w1, 0.0) + jnp.where(row == i2, w2, 0.0)
    onehot = jnp.where(row == g_sel, 1.0, 0.0)
    return onehot, comb


def _moe_kernel(x_ref, mod_ref, wr_ref, tri_ref, wg_ref, wu_ref, wd_ref, lg_ref, lb_ref, o_ref,
                hb_scr, hs_scr, os_scr, pt_scr, cs_scr):
    m = mod_ref[0]
    tm, cap = pt_scr.shape
    d = x_ref.shape[-1]
    half = d // 2
    cw = EXP_PER_GROUP * EXP_FF

    h = x_ref[0] * (1.0 + m[4:5]) + m[3:4]
    h_hi = h.astype(BF16)
    hb_scr[...] = h_hi
    h_lo = (h - h_hi.astype(F32)).astype(BF16)
    l2 = _dot(h_hi, wr_ref[...])
    logits = l2[:, :LANES] + l2[:, LANES:] + _dot(h_lo, wr_ref[:, :LANES])
    onehot, comb = _route_t(logits.T)
    rank = _dot(onehot.astype(BF16), tri_ref[...])
    cnt = jnp.sum(onehot, axis=1, keepdims=True)
    padded = jnp.floor((cnt + (MOE_BLOCK - 1)) * (1.0 / MOE_BLOCK)) * MOE_BLOCK
    start = jnp.zeros((1, 1), F32)
    dest = jnp.zeros((1, tm), F32)
    first, count = [], []
    for gi in range(N_GROUPS):
        dest = dest + onehot[gi:gi + 1] * (start + rank[gi:gi + 1])
        first.append((start[0, 0] * (1.0 / MOE_BLOCK)).astype(jnp.int32))
        count.append((padded[gi, 0] * (1.0 / MOE_BLOCK)).astype(jnp.int32))
        start = start + padded[gi:gi + 1]
    aux = jnp.concatenate([jnp.broadcast_to(dest, (SUBLANES, tm)), comb,
                           jnp.zeros((LANES - 2 * SUBLANES, tm), F32)], axis=0).T
    lane = lax.broadcasted_iota(jnp.int32, (tm, LANES), 1)
    comb_tok = jnp.where((lane >= SUBLANES) & (lane < 2 * SUBLANES), aux, 0.0)
    c_hi = comb_tok.astype(BF16)
    c_mid = (comb_tok - c_hi.astype(F32)).astype(BF16)
    slot_t = lax.broadcasted_iota(jnp.int32, (tm, cap), 1).astype(F32)
    pt_scr[...] = jnp.where(slot_t == aux[:, 0:1], 1.0, 0.0).astype(BF16)
    slot = lax.broadcasted_iota(jnp.int32, (cap, tm), 0).astype(F32)
    perm = jnp.where(slot == dest, 1.0, 0.0).astype(BF16)
    hs_scr[:, :half] = _dot(perm, hb_scr[:, :half]).astype(BF16)
    hs_scr[:, half:] = _dot(perm, hb_scr[:, half:]).astype(BF16)
    cs2 = _dot(perm, jnp.concatenate([c_hi, c_mid], axis=1))
    cs_scr[...] = cs2[:, :LANES] + cs2[:, LANES:]
    os_scr[...] = jnp.zeros_like(os_scr)

    for gi in range(N_GROUPS):
        cols = slice(gi * cw, (gi + 1) * cw)

        def block(i, carry, gi=gi, cols=cols):
            r0 = pl.multiple_of((first[gi] + i) * MOE_BLOCK, MOE_BLOCK)
            rows = pl.ds(r0, MOE_BLOCK)
            hs = hs_scr[rows, :]
            cs = cs_scr[rows, :]
            hid = _silu(_dot(hs, wg_ref[:, cols])) * _dot(hs, wu_ref[:, cols])
            hid = jnp.concatenate(
                [hid[:, e * EXP_FF:(e + 1) * EXP_FF] * cs[:, SUBLANES + e:SUBLANES + e + 1]
                 for e in range(EXP_PER_GROUP)], axis=1)
            os_scr[rows, :] = _dot(hid.astype(BF16), wd_ref[cols, :]).astype(BF16)
            return carry

        lax.fori_loop(0, count[gi], block, 0)

    y = jnp.concatenate([_dot(pt_scr[...], os_scr[:, :half]), _dot(pt_scr[...], os_scr[:, half:])], axis=1)
    r = ALPHA * x_ref[0] + m[5:6] * y
    o_ref[0] = _layer_norm(r, lg_ref[...], lb_ref[...])


def _moe(x, mod, wr, wg, wu, wd, ln_g, ln_b):
    b, l, d = x.shape
    tm = MOE_TILE
    cap = (tm + N_GROUPS * (MOE_BLOCK - 1)) // MOE_BLOCK * MOE_BLOCK
    cw = EXP_PER_GROUP * EXP_FF
    per_batch = mod.shape[0] > 1
    mod_map = (lambda i, t: (i, 0, 0)) if per_batch else (lambda i, t: (0, 0, 0))
    tri = jnp.asarray(np.triu(np.ones((tm, tm), np.float32), k=1), BF16)
    full = lambda a: pl.BlockSpec(a.shape, lambda i, t: (0,) * a.ndim, pipeline_mode=pl.Buffered(1))
    return pl.pallas_call(
        _moe_kernel,
        out_shape=jax.ShapeDtypeStruct((b, l, d), F32),
        grid=(b, l // tm),
        in_specs=[
            pl.BlockSpec((1, tm, d), lambda i, t: (i, t, 0)),
            pl.BlockSpec((1, 6, d), mod_map),
            full(wr), full(tri), full(wg), full(wu), full(wd), full(ln_g), full(ln_b),
        ],
        out_specs=pl.BlockSpec((1, tm, d), lambda i, t: (i, t, 0)),
        scratch_shapes=[pltpu.VMEM((tm, d), BF16), pltpu.VMEM((cap, d), BF16), pltpu.VMEM((cap, d), BF16),
                        pltpu.VMEM((tm, cap), BF16), pltpu.VMEM((cap, LANES), F32)],
        compiler_params=_cparams(("arbitrary", "arbitrary")),
        name="hier_moe_ln",
    )(x, mod, wr, tri, wg, wu, wd, ln_g, ln_b)


def _rope_swap_perm():
    idx = np.arange(MLA_DR)
    return idx ^ (MLA_DR // 4)


def _rope_tables(n_tok):
    rows = n_tok // GRID_W
    pos_r = jnp.repeat(jnp.arange(rows, dtype=F32), GRID_W)
    pos_c = (jnp.arange(rows * GRID_W) % GRID_W).astype(F32)
    n_freq = MLA_DR // 4
    inv = ROPE_BASE ** (-jnp.arange(n_freq, dtype=F32) / n_freq)
    ang = jnp.stack([pos_r[:, None] * inv, pos_c[:, None] * inv], axis=1)
    cos, sin = jnp.cos(ang), jnp.sin(ang)
    cos32 = jnp.stack([cos, cos], axis=2).reshape(n_tok, MLA_DR)
    sin32 = jnp.stack([-sin, sin], axis=2).reshape(n_tok, MLA_DR)
    pad_hi = LANES - MLA_DN - MLA_DR
    cos_t = jnp.concatenate([jnp.ones((n_tok, MLA_DN), F32), cos32, jnp.zeros((n_tok, pad_hi), F32)], axis=1)
    sin_t = jnp.concatenate([jnp.zeros((n_tok, MLA_DN), F32), sin32, jnp.zeros((n_tok, pad_hi), F32)], axis=1)
    return cos_t, sin_t


def _ab_weights(w_in, w_uq, w_ukv):
    perm = _rope_swap_perm()
    pad_hi = LANES - MLA_DN - MLA_DR
    w_h = w_in[:, :5 * HA_W]
    cq = w_in[:, 5 * HA_W:5 * HA_W + MLA_Q_RANK]
    ckv = w_in[:, 5 * HA_W + MLA_Q_RANK:5 * HA_W + MLA_Q_RANK + MLA_KV_RANK]
    kr = w_in[:, 5 * HA_W + MLA_Q_RANK + MLA_KV_RANK:]
    d = w_in.shape[0]
    z_lo = jnp.zeros((d, MLA_DN), F32)
    z_hi = jnp.zeros((d, pad_hi), F32)
    w_m = jnp.concatenate([cq, ckv, z_lo, kr, z_hi, z_lo, kr[:, perm], z_hi], axis=1)
    uq = w_uq.reshape(MLA_Q_RANK, MLA_HEADS, MLA_DN + MLA_DR)
    q_nope, q_rope = uq[..., :MLA_DN], uq[..., MLA_DN:]
    zq_hi = jnp.zeros((MLA_Q_RANK, MLA_HEADS, pad_hi), F32)
    wq1 = jnp.concatenate([q_nope, q_rope, zq_hi], axis=-1).reshape(MLA_Q_RANK, -1)
    wq2 = jnp.concatenate([jnp.zeros_like(q_nope), q_rope[..., perm], zq_hi], axis=-1).reshape(MLA_Q_RANK, -1)
    ukv = w_ukv.reshape(MLA_KV_RANK, MLA_HEADS, MLA_DN + MLA_DV)
    k_nope, v = ukv[..., :MLA_DN], ukv[..., MLA_DN:]
    wk = jnp.concatenate([k_nope, jnp.zeros((MLA_KV_RANK, MLA_HEADS, LANES - MLA_DN), F32)], axis=-1)
    wk = wk.reshape(MLA_KV_RANK, -1)
    v_pairs = v.reshape(MLA_KV_RANK, MLA_HEADS // 2, 2, MLA_DV)
    zv = jnp.zeros_like(v_pairs[:, :, 0])
    wv = jnp.stack([jnp.concatenate([v_pairs[:, :, 0], zv], axis=-1),
                    jnp.concatenate([zv, v_pairs[:, :, 1]], axis=-1)], axis=2).reshape(MLA_KV_RANK, -1)
    wvt = v.reshape(MLA_KV_RANK, MLA_HEADS * MLA_DV).T
    bf = lambda a: a.astype(BF16)
    return bf(w_h), bf(w_m), bf(wq1), bf(wq2), bf(wk), bf(wv), bf(wvt)


def _moe_weights(w_group, w_expert, w_gate, w_up, w_down):
    d = w_group.shape[0]
    wr = jnp.concatenate([w_expert, w_group, jnp.zeros((d, LANES - N_EXPERTS - N_GROUPS), F32)], axis=1)
    wr_hi = wr.astype(BF16)
    wr = jnp.concatenate([wr_hi, (wr - wr_hi.astype(F32)).astype(BF16)], axis=1)
    wg = jnp.transpose(w_gate, (1, 0, 2)).reshape(d, N_EXPERTS * EXP_FF).astype(BF16)
    wu = jnp.transpose(w_up, (1, 0, 2)).reshape(d, N_EXPERTS * EXP_FF).astype(BF16)
    wd = w_down.reshape(N_EXPERTS * EXP_FF, d).astype(BF16)
    return wr, wg, wu, wd


def kernel(x_prompt, x_sample, state_hgrn_fwd, state_hgrn_bwd, cache_mla_ckv, cache_mla_krope, c, c_ctx, mod_w, mod_b, ln_g, ln_b, ab_w_in, ab_w_out, hgrn_lb_logits, hgrn_norm_g, mla_q_norm_g, mla_w_uq, mla_kv_norm_g, mla_w_ukv, cd_w_in, cd_w_out, sc_conv_w, cf_conv_w, cf_conv_b, cf_ln_g, cf_ln_b, moe_w_group, moe_w_expert, moe_w_gate, moe_w_up, moe_w_down):
    dec_b = x_sample.shape[0]
    d = D_MODEL
    cvec = jnp.concatenate([c, c_ctx[None, :], jnp.zeros((SUBLANES - dec_b - 1, d), F32)], axis=0)
    mods = _mod_vectors(cvec, mod_w, mod_b)
    rope_tabs = _rope_tables(x_sample.shape[1])
    xp, xs = x_prompt, x_sample
    new_sf = new_sb = new_ckv = new_kr = None
    for l in range(DEPTH):
        mod_lat = mods[l, :dec_b].reshape(dec_b, 6, d)
        mod_ctx = mods[l, dec_b:dec_b + 1].reshape(1, 6, d)
        row = lambda a: a.reshape(1, -1)
        if l % 2 == 0:
            e = l // 2
            w_h, w_m, wq1, wq2, wk, wv, wvt = _ab_weights(ab_w_in[e], mla_w_uq[e], mla_w_ukv[e])
            w_out = ab_w_out[e].astype(BF16)
            qg, kvg, gn = row(mla_q_norm_g[e]), row(mla_kv_norm_g[e]), row(hgrn_norm_g[e])
            ph_p, q_p, k_p, v_p, ckv_p, kr_p = _ab_inproj(xp, mod_ctx, w_h, w_m, None, qg, kvg, wq1, None, wk, wv)
            of_p, ob_p, sf, sb = _hgrn(ph_p, hgrn_lb_logits, None, None, e)
            om_p = _attention(q_p, k_p, v_p, None, None)
            xp = _outproj(xp, mod_ctx, (of_p, ob_p, ph_p, gn), om_p, w_out, row(ln_g[l, 0]), row(ln_b[l, 0]), True)
            ph_s, q_s, k_s, vt_s, _, _ = _ab_inproj(xs, mod_lat, w_h, w_m, rope_tabs, qg, kvg, wq1, wq2, wk, wvt)
            of_s, ob_s, _, _ = _hgrn(ph_s, hgrn_lb_logits, state_hgrn_fwd[:, e], state_hgrn_bwd[:, e], e)
            kc, vtc = _mla_ctx(cache_mla_ckv[:, e], cache_mla_krope[:, e], wk, wvt)
            om_s = _attention_long(q_s, k_s, vt_s, kc, vtc)
            xs = _outproj(xs, mod_lat, (of_s, ob_s, ph_s, gn), om_s, w_out, row(ln_g[l, 0]), row(ln_b[l, 0]), True)
            new_sf, new_sb, new_ckv = sf, sb, ckv_p
            new_kr = kr_p[:, :, MLA_DN:MLA_DN + MLA_DR]
        else:
            jx = l // 2
            w1 = cd_w_in[jx].astype(BF16)
            w_out = cd_w_out[jx].astype(BF16)
            cd = (sc_conv_w[jx], cf_conv_w[jx], row(cf_conv_b[jx]), row(cf_ln_g[jx]), row(cf_ln_b[jx]))
            (p1_p,) = _inproj(xp, mod_ctx, [w1])
            ysc_p, ycf_p = _conv_mixers(p1_p, *cd)
            xp = _outproj(xp, mod_ctx, ysc_p, ycf_p, w_out, row(ln_g[l, 0]), row(ln_b[l, 0]), False)
            (p1_s,) = _inproj(xs, mod_lat, [w1])
            ysc_s, ycf_s = _conv_mixers(p1_s, *cd)
            xs = _outproj(xs, mod_lat, ysc_s, ycf_s, w_out, row(ln_g[l, 0]), row(ln_b[l, 0]), False)
        wr, wg, wu, wd = _moe_weights(moe_w_group[l], moe_w_expert[l], moe_w_gate[l], moe_w_up[l], moe_w_down[l])
        xp_t = xp.reshape(-1, MOE_TILE, d)
        xp = _moe(xp_t, mod_ctx, wr, wg, wu, wd, row(ln_g[l, 1]), row(ln_b[l, 1])).reshape(xp.shape)
        xs = _moe(xs, mod_lat, wr, wg, wu, wd, row(ln_g[l, 1]), row(ln_b[l, 1]))
    return (xp, xs, new_sf[:, None], new_sb[:, None], new_ckv[:, None], new_kr[:, None])
```

```python
import functools

import numpy as np
import jax
import jax.numpy as jnp
from jax import lax
from jax.experimental import pallas as pl
from jax.experimental.pallas import tpu as pltpu

F32 = jnp.float32
BF16 = jnp.bfloat16
HIGHEST = lax.Precision.HIGHEST

D_MODEL = 1024
DEPTH = 2
GRID_W = 64
N_EVEN = (DEPTH + 1) // 2
HA_HEADS = 4
HA_DK = 128
HA_DV = 128
HA_W = HA_HEADS * HA_DK
CHUNK = 32
MLA_HEADS = 8
MLA_DN = 64
MLA_DR = 32
MLA_DV = 64
MLA_Q_RANK = 384
MLA_KV_RANK = 256
ROPE_BASE = 10000.0
SC_W = 512
SC_K = 3
CF_W = 512
CF_K = 31
N_GROUPS = 4
EXP_PER_GROUP = 8
N_EXPERTS = N_GROUPS * EXP_PER_GROUP
EXP_FF = 128
ALPHA = (2.0 * DEPTH) ** 0.25
LOG2_E = 1.4426950408889634
LN_EPS = 1e-5
RMS_EPS = 1e-6

LANES = 128
SUBLANES = 8
VMEM_LIMIT = 56 * 1024 * 1024

HEAD_PAD = LANES
MLA_SLAB = MLA_Q_RANK + MLA_KV_RANK + 2 * LANES
TOK_TILE = 256
GROUP_ROWS = 128
N_LEVELS = 5
SAFE_LOG_DECAY = 60.0
ATTN_KEY_BLOCK = 512
ATTN_Q_TILE = 256
MOE_TILE = 512
MOE_BLOCK = 128
HALO = 16


def _cparams(sem):
    return pltpu.CompilerParams(dimension_semantics=sem, vmem_limit_bytes=VMEM_LIMIT)


def _sigmoid(x):
    return 0.5 * jnp.tanh(0.5 * x) + 0.5


def _silu(x):
    half = 0.5 * x
    return half * jnp.tanh(half) + half


def _layer_norm(r, g, b):
    mu = jnp.mean(r, axis=-1, keepdims=True)
    d = r - mu
    var = jnp.mean(d * d, axis=-1, keepdims=True)
    return d * lax.rsqrt(var + LN_EPS) * g + b


def _dot(a, b):
    return jnp.dot(a, b, preferred_element_type=F32)


def _dot_nt(a, b):
    return lax.dot_general(a, b, (((1,), (1,)), ((), ())), preferred_element_type=F32)


def _dot_tn(a, b):
    return lax.dot_general(a, b, (((0,), (0,)), ((), ())), preferred_element_type=F32)


def _mod_kernel(c_ref, w_ref, b_ref, o_ref):
    s = _silu(c_ref[...])
    o_ref[0] = jnp.dot(s, w_ref[0], precision=HIGHEST, preferred_element_type=F32) + b_ref[0]


def _mod_vectors(cvec, mod_w, mod_b):
    n_out = mod_w.shape[-1]
    tn = 1536
    return pl.pallas_call(
        _mod_kernel,
        out_shape=jax.ShapeDtypeStruct((DEPTH, SUBLANES, n_out), F32),
        grid=(DEPTH, n_out // tn),
        in_specs=[
            pl.BlockSpec((SUBLANES, D_MODEL), lambda l, j: (0, 0)),
            pl.BlockSpec((1, D_MODEL, tn), lambda l, j: (l, 0, j)),
            pl.BlockSpec((1, 1, tn), lambda l, j: (l, 0, j)),
        ],
        out_specs=pl.BlockSpec((1, SUBLANES, tn), lambda l, j: (l, 0, j)),
        compiler_params=_cparams(("arbitrary", "arbitrary")),
        name="mod_vectors",
    )(cvec, mod_w, mod_b.reshape(DEPTH, 1, n_out))


def _inproj_kernel(*refs, n_w):
    x_ref, mod_ref = refs[0], refs[1]
    w_refs = refs[2:2 + n_w]
    o_refs = refs[2 + n_w:]
    m = mod_ref[0]
    h = (x_ref[0] * (1.0 + m[1:2]) + m[0:1]).astype(BF16)
    for w_ref, o_ref in zip(w_refs, o_refs):
        o_ref[0] = _dot(h, w_ref[...])


def _inproj(x, mod, weights):
    b, l, d = x.shape
    tm = TOK_TILE
    per_batch = mod.shape[0] > 1
    mod_map = (lambda i, j: (i, 0, 0)) if per_batch else (lambda i, j: (0, 0, 0))
    in_specs = [pl.BlockSpec((1, tm, d), lambda i, j: (i, j, 0)),
                pl.BlockSpec((1, 6, d), mod_map)]
    in_specs += [pl.BlockSpec(w.shape, lambda i, j: (0, 0)) for w in weights]
    out_shape = [jax.ShapeDtypeStruct((b, l, w.shape[1]), F32) for w in weights]
    out_specs = [pl.BlockSpec((1, tm, w.shape[1]), lambda i, j: (i, j, 0)) for w in weights]
    return pl.pallas_call(
        functools.partial(_inproj_kernel, n_w=len(weights)),
        out_shape=out_shape,
        grid=(b, l // tm),
        in_specs=in_specs,
        out_specs=out_specs,
        compiler_params=_cparams(("parallel", "parallel")),
        name="modulate_inproj",
    )(x, mod, *weights)


def _hgrn_tables():
    n = GROUP_ROWS
    t = np.arange(n)[:, None]
    j = np.arange(n)[None, :]
    same_chunk = (t // CHUNK) == (j // CHUNK)
    e_f, e_b = [], []
    lv_f = np.full((n, n), -1, np.int32)
    for lvl in range(N_LEVELS):
        m = CHUNK >> (lvl + 1)
        blk0 = (t // (2 * m)) * (2 * m)
        r = blk0 + m - 1
        upper = t > r
        ef = np.where(upper, (j > r) & (j <= t), (j > t) & (j <= r))
        r2 = blk0 + m
        lower = t < r2
        eb = np.where(lower, (j >= t) & (j < r2), (j >= r2) & (j < t))
        e_f.append(ef)
        e_b.append(eb)
        s = np.arange(n)[None, :]
        same_blk = (t // (2 * m)) == (s // (2 * m))
        q_side = (t % (2 * m)) >= m
        k_side = (s % (2 * m)) < m
        lv_f[same_blk & q_side & k_side] = lvl
    lv_f[np.arange(n), np.arange(n)] = N_LEVELS
    e_f = np.concatenate(e_f, axis=0).astype(np.float32)
    e_b = np.concatenate(e_b, axis=0).astype(np.float32)
    e = np.stack([e_f, e_b])
    cum = np.stack([same_chunk & (j <= t), same_chunk & (j >= t)]).astype(np.float32)
    lv = np.stack([lv_f, lv_f.T])
    return e, cum, lv


def _hgrn_kernel(qf_ref, vf_ref, ff_ref, qb_ref, vb_ref, fb_ref, lbl_ref, e_ref, cum_ref, lv_ref, s0f_ref, s0b_ref,
                 of_ref, ob_ref, sf_ref, sb_ref,
                 st_scr, qe_scr, kd_scr, sc_scr, v_scr, oi_scr, dec_scr, *, lt, slot, has_state):
    i = pl.program_id(1)
    n_i = pl.num_programs(1)
    n_chunks = lt // CHUNK
    n_groups = lt // GROUP_ROWS
    chunks_per_group = GROUP_ROWS // CHUNK

    @pl.when(i == 0)
    def _():
        for d, s0_ref in enumerate((s0f_ref, s0b_ref)):
            for h in range(HA_HEADS):
                if has_state:
                    st_scr[d, h] = s0_ref[0, h].T
                else:
                    st_scr[d, h] = jnp.zeros((HA_DV, HA_DK), F32)

    lg = lbl_ref[...]
    n_slots = lg.shape[0]
    mx = lg[0]
    for s in range(1, n_slots):
        mx = jnp.maximum(mx, lg[s])
    ex = [jnp.exp(lg[s] - mx) for s in range(n_slots)]
    den = ex[0]
    for s in range(1, n_slots):
        den = den + ex[s]
    num = ex[0]
    for s in range(1, slot + 1):
        num = num + ex[s]
    lb_all = num / den

    dirs = ((qf_ref, vf_ref, ff_ref), (qb_ref, vb_ref, fb_ref))
    head_cols = [slice(h * HA_DK, (h + 1) * HA_DK) for h in range(HA_HEADS)]

    def gates(d, rows, cols):
        q_ref, _, f_ref = dirs[d]
        q = _silu(q_ref[0, rows, cols])
        lb_h = lb_all[d:d + 1, cols]
        f = lb_h + (1.0 - lb_h) * _sigmoid(f_ref[0, rows, cols])
        g = jnp.log(f)
        g_hi = g.astype(BF16)
        g_lo = (g - g_hi.astype(F32)).astype(BF16)
        return q, 1.0 - f, g_hi, g_lo

    def group_step(grp, tot_min):
        r0 = pl.multiple_of(grp * GROUP_ROWS, GROUP_ROWS)
        rows = pl.ds(r0, GROUP_ROWS)
        for d in range(2):
            cum = cum_ref[d]
            lv = lv_ref[d]
            for cols in head_cols:
                q, k, g_hi, g_lo = gates(d, rows, cols)
                bcum = _dot(cum, g_hi) + _dot(cum, g_lo)
                v_scr[d, rows, cols] = dirs[d][1][0, rows, cols].astype(BF16)
                qe = (q * jnp.exp(bcum)).astype(BF16)
                qe_scr[d, rows, cols] = qe
                kx = []
                for cc in range(chunks_per_group):
                    edge = cc * CHUNK + (CHUNK - 1 if d == 0 else 0)
                    tot = bcum[edge:edge + 1]
                    b_c = bcum[cc * CHUNK:(cc + 1) * CHUNK]
                    kd = k[cc * CHUNK:(cc + 1) * CHUNK] * jnp.exp(tot - b_c)
                    kd_scr[d, pl.ds(r0 + cc * CHUNK, CHUNK), cols] = kd.astype(BF16)
                    dec_scr[d, grp * chunks_per_group + cc, :, cols] = jnp.exp(tot)
                    kx.append(kd * jnp.exp(-tot))
                    tot_min = jnp.minimum(tot_min, tot)
                p = _dot_nt(qe, jnp.concatenate(kx, axis=0).astype(BF16))
                sc_scr[d, rows, cols] = jnp.where(lv >= 0, p, 0.0).astype(BF16)
        return tot_min

    tot_min = lax.fori_loop(0, n_groups, group_step, jnp.zeros((1, HA_DK), F32))

    def one_factor():
        for d in range(2):
            for grp in range(n_groups):
                rows = slice(grp * GROUP_ROWS, (grp + 1) * GROUP_ROWS)
                for cols in head_cols:
                    oi_scr[d, rows, cols] = _dot(sc_scr[d, rows, cols], v_scr[d, rows, cols])

    def per_level():
        for d in range(2):

            def group_step(grp, carry, d=d):
                rows = pl.ds(pl.multiple_of(grp * GROUP_ROWS, GROUP_ROWS), GROUP_ROWS)
                lv = lv_ref[d]
                for cols in head_cols:
                    q, k, g_hi, g_lo = gates(d, rows, cols)
                    x = jnp.exp(_dot(e_ref[d], g_hi) + _dot(e_ref[d], g_lo))
                    sc = jnp.where(lv == N_LEVELS, _dot_nt(q.astype(BF16), k.astype(BF16)), 0.0)
                    for lvl in range(N_LEVELS):
                        xl = x[lvl * GROUP_ROWS:(lvl + 1) * GROUP_ROWS]
                        p = _dot_nt((q * xl).astype(BF16), (k * xl).astype(BF16))
                        sc = jnp.where(lv == lvl, p, sc)
                    oi_scr[d, rows, cols] = _dot(sc.astype(BF16), v_scr[d, rows, cols])
                return carry

            lax.fori_loop(0, n_groups, group_step, 0)

    lax.cond(jnp.min(tot_min) > -SAFE_LOG_DECAY, one_factor, per_level)

    out_refs = (of_ref, ob_ref)

    def chunk_step(c, carry):
        for d in range(2):
            cc = c if d == 0 else n_chunks - 1 - c
            r0 = pl.multiple_of(cc * CHUNK, CHUNK)
            rows = pl.ds(r0, CHUNK)
            for h in range(HA_HEADS):
                cols = slice(h * HA_DK, (h + 1) * HA_DK)
                st = st_scr[d, h]
                o_state = _dot_nt(qe_scr[d, rows, cols], st.astype(BF16))
                out_refs[d][0, rows, cols] = oi_scr[d, rows, cols] + o_state
                upd = _dot_tn(v_scr[d, rows, cols], kd_scr[d, rows, cols])
                st_scr[d, h] = st * dec_scr[d, cc, :, cols] + upd
        return carry

    lax.fori_loop(0, n_chunks, chunk_step, 0, unroll=True)

    @pl.when(i == n_i - 1)
    def _():
        for d, s_ref in enumerate((sf_ref, sb_ref)):
            for h in range(HA_HEADS):
                s_ref[0, h] = st_scr[d, h].T


def _hgrn(p_h, lb_logits, s0_f, s0_b, slot):
    b, l, _ = p_h.shape
    lt = TOK_TILE
    n_t = l // lt
    has_state = s0_f is not None
    if not has_state:
        s0_f = jnp.zeros((1, HA_HEADS, HA_DK, HA_DV), F32)
        s0_b = s0_f
    e_np, cum_np, lv_np = _hgrn_tables()
    e_mat = jnp.asarray(e_np, BF16)
    cum = jnp.asarray(cum_np, BF16)
    lv = jnp.asarray(lv_np, jnp.int32)
    w = HA_W

    def sec(idx, rev):
        if rev:
            return pl.BlockSpec((1, lt, w), lambda bi, i: (bi, n_t - 1 - i, idx))
        return pl.BlockSpec((1, lt, w), lambda bi, i: (bi, i, idx))

    state_map = (lambda bi, i: (bi, 0, 0, 0)) if has_state else (lambda bi, i: (0, 0, 0, 0))
    in_specs = [
        sec(0, False), sec(3, False), sec(1, False),
        sec(0, True), sec(3, True), sec(2, True),
        pl.BlockSpec(lb_logits.shape, lambda bi, i: (0, 0, 0)),
        pl.BlockSpec(e_mat.shape, lambda bi, i: (0, 0, 0)),
        pl.BlockSpec(cum.shape, lambda bi, i: (0, 0, 0)),
        pl.BlockSpec(lv.shape, lambda bi, i: (0, 0, 0)),
        pl.BlockSpec((1, HA_HEADS, HA_DK, HA_DV), state_map),
        pl.BlockSpec((1, HA_HEADS, HA_DK, HA_DV), state_map),
    ]
    out_shape = [
        jax.ShapeDtypeStruct((b, l, w), F32),
        jax.ShapeDtypeStruct((b, l, w), F32),
        jax.ShapeDtypeStruct((b, HA_HEADS, HA_DK, HA_DV), F32),
        jax.ShapeDtypeStruct((b, HA_HEADS, HA_DK, HA_DV), F32),
    ]
    out_specs = [
        pl.BlockSpec((1, lt, w), lambda bi, i: (bi, i, 0)),
        pl.BlockSpec((1, lt, w), lambda bi, i: (bi, n_t - 1 - i, 0)),
        pl.BlockSpec((1, HA_HEADS, HA_DK, HA_DV), lambda bi, i: (bi, 0, 0, 0)),
        pl.BlockSpec((1, HA_HEADS, HA_DK, HA_DV), lambda bi, i: (bi, 0, 0, 0)),
    ]
    scratch = [
        pltpu.VMEM((2, HA_HEADS, HA_DV, HA_DK), F32),
        pltpu.VMEM((2, lt, w), BF16),
        pltpu.VMEM((2, lt, w), BF16),
        pltpu.VMEM((2, lt, w), BF16),
        pltpu.VMEM((2, lt, w), BF16),
        pltpu.VMEM((2, lt, w), F32),
        pltpu.VMEM((2, lt // CHUNK, 1, w), F32),
    ]
    return pl.pallas_call(
        functools.partial(_hgrn_kernel, lt=lt, slot=slot, has_state=has_state),
        out_shape=out_shape,
        grid=(b, n_t),
        in_specs=in_specs,
        out_specs=out_specs,
        scratch_shapes=scratch,
        compiler_params=_cparams(("parallel", "arbitrary")),
        name="hgrn2_scan",
    )(p_h, p_h, p_h, p_h, p_h, p_h, lb_logits, e_mat, cum, lv, s0_f, s0_b)


def _ab_inproj_kernel(*refs, rope, key_major):
    if rope:
        (x_ref, mod_ref, wh_ref, wm_ref, cos_ref, sin_ref, qg_ref, kvg_ref, wq1_ref, wq2_ref, wk_ref,
         wv_ref) = refs[:12]
        ph_ref, q_ref, k_ref, v_ref, ckv_ref, kr_ref = refs[12:]
    else:
        x_ref, mod_ref, wh_ref, wm_ref, qg_ref, kvg_ref, wq1_ref, wk_ref, wv_ref = refs[:9]
        ph_ref, q_ref, k_ref, v_ref, ckv_ref, kr_ref = refs[9:]
    scale = (MLA_DN + MLA_DR) ** -0.5 * LOG2_E
    mod = mod_ref[0]
    hmod = (x_ref[0] * (1.0 + mod[1:2]) + mod[0:1]).astype(BF16)
    ph_ref[0] = _dot(hmod, wh_ref[...])
    pm = _dot(hmod, wm_ref[...])
    kr_ref[0] = pm[:, MLA_Q_RANK + MLA_KV_RANK:MLA_Q_RANK + MLA_KV_RANK + LANES]
    cq = pm[:, :MLA_Q_RANK]
    cq = cq * lax.rsqrt(jnp.mean(cq * cq, axis=-1, keepdims=True) + RMS_EPS) * qg_ref[...]
    cq = cq.astype(BF16)
    ckv = pm[:, MLA_Q_RANK:MLA_Q_RANK + MLA_KV_RANK]
    ckv = ckv * lax.rsqrt(jnp.mean(ckv * ckv, axis=-1, keepdims=True) + RMS_EPS) * kvg_ref[...]
    ckv_ref[0] = ckv
    ckv = ckv.astype(BF16)
    kr0 = MLA_Q_RANK + MLA_KV_RANK
    kr = pm[:, kr0:kr0 + LANES]
    qa = _dot(cq, wq1_ref[...])
    kn = _dot(ckv, wk_ref[...])
    if key_major:
        v_ref[0] = _dot_nt(wv_ref[...], ckv).astype(BF16)
    else:
        v_ref[0] = _dot(ckv, wv_ref[...]).astype(BF16)
    if rope:
        cos = cos_ref[...]
        sin = sin_ref[...]
        qb = _dot(cq, wq2_ref[...])
        kr = kr * cos + pm[:, kr0 + LANES:kr0 + 2 * LANES] * sin
    for h in range(MLA_HEADS):
        sl = slice(h * HEAD_PAD, (h + 1) * HEAD_PAD)
        qh = qa[:, sl]
        if rope:
            qh = qh * cos + qb[:, sl] * sin
        q_ref[0, :, sl] = (qh * scale).astype(BF16)
        k_ref[0, :, sl] = (kn[:, sl] + kr).astype(BF16)


def _ab_inproj(x, mod, w_h, w_m, rope_tabs, qg, kvg, wq1, wq2, wk, wv):
    b, l, d = x.shape
    tm = TOK_TILE
    rope = rope_tabs is not None
    per_batch = mod.shape[0] > 1
    mod_map = (lambda i, j: (i, 0, 0)) if per_batch else (lambda i, j: (0, 0, 0))
    full = lambda a: pl.BlockSpec(a.shape, lambda i, j: (0,) * a.ndim)
    row = lambda width: pl.BlockSpec((1, tm, width), lambda i, j: (i, j, 0))
    hw = MLA_HEADS * HEAD_PAD
    inputs = [x, mod, w_h, w_m]
    in_specs = [row(d), pl.BlockSpec((1, 6, d), mod_map), full(w_h), full(w_m)]
    if rope:
        inputs += list(rope_tabs)
        in_specs += [pl.BlockSpec((tm, LANES), lambda i, j: (j, 0))] * 2
    ws = [qg, kvg, wq1] + ([wq2] if rope else []) + [wk, wv]
    inputs += ws
    in_specs += [full(a) for a in ws]
    key_major = wv.shape[1] == MLA_KV_RANK
    vw = MLA_HEADS * MLA_DV
    shapes = [(b, l, w_h.shape[1]), (b, l, hw), (b, l, hw), (b, vw, l) if key_major else (b, l, hw),
              (b, l, MLA_KV_RANK), (b, l, LANES)]
    dtypes = [F32, BF16, BF16, BF16, F32, F32]
    out_specs = [row(s[2]) for s in shapes]
    if key_major:
        out_specs[3] = pl.BlockSpec((1, vw, tm), lambda i, j: (i, 0, j))
    return pl.pallas_call(
        functools.partial(_ab_inproj_kernel, rope=rope, key_major=key_major),
        out_shape=[jax.ShapeDtypeStruct(s, dt) for s, dt in zip(shapes, dtypes)],
        grid=(b, l // tm),
        in_specs=in_specs,
        out_specs=out_specs,
        compiler_params=_cparams(("parallel", "parallel")),
        name="ab_inproj",
    )(*inputs)


def _mla_ctx_kernel(ckv_ref, kr_ref, place_ref, wk_ref, wvt_ref, k_ref, vt_ref):
    ckv = ckv_ref[0].astype(BF16)
    kr = _dot(kr_ref[0].astype(BF16), place_ref[...])
    kn = _dot(ckv, wk_ref[...])
    vt_ref[0] = _dot_nt(wvt_ref[...], ckv).astype(BF16)
    for h in range(MLA_HEADS):
        sl = slice(h * HEAD_PAD, (h + 1) * HEAD_PAD)
        k_ref[0, :, sl] = (kn[:, sl] + kr).astype(BF16)


def _mla_ctx(ctx_ckv, ctx_kr, wk, wvt):
    b, lc, _ = ctx_ckv.shape
    hw = MLA_HEADS * HEAD_PAD
    vw = MLA_HEADS * MLA_DV
    place = np.zeros((MLA_DR, LANES), np.float32)
    place[np.arange(MLA_DR), MLA_DN + np.arange(MLA_DR)] = 1.0
    place = jnp.asarray(place, BF16)
    full = lambda a: pl.BlockSpec(a.shape, lambda i: (0,) * a.ndim)
    return pl.pallas_call(
        _mla_ctx_kernel,
        out_shape=[jax.ShapeDtypeStruct((b, lc, hw), BF16), jax.ShapeDtypeStruct((b, vw, lc), BF16)],
        grid=(b,),
        in_specs=[pl.BlockSpec((1, lc, MLA_KV_RANK), lambda i: (i, 0, 0)),
                  pl.BlockSpec((1, lc, MLA_DR), lambda i: (i, 0, 0)),
                  full(place), full(wk), full(wvt)],
        out_specs=[pl.BlockSpec((1, lc, hw), lambda i: (i, 0, 0)), pl.BlockSpec((1, vw, lc), lambda i: (i, 0, 0))],
        compiler_params=_cparams(("parallel",)),
        name="mla_ctx_keys",
    )(ctx_ckv, ctx_kr, place, wk, wvt)


def _attn_long_kernel(q_ref, k_ref, vt_ref, kc_ref, vtc_ref, o_ref, s_a, s_b, m_a, m_b):
    i = pl.program_id(2)
    tq = q_ref.shape[1]
    lk = k_ref.shape[1]
    tk = ATTN_KEY_BLOCK
    n_blk = lk // tk
    slabs = [slice(hh * HEAD_PAD, (hh + 1) * HEAD_PAD) for hh in range(2)]
    vrows = [slice(hh * MLA_DV, (hh + 1) * MLA_DV) for hh in range(2)]
    q_t = [q_ref[0, :, sl].astype(F32).T.astype(BF16) for sl in slabs]
    crow = slice(lk, lk + kc_ref.shape[1])

    @pl.when(i == 0)
    def _():
        s_b[...] = jnp.zeros(s_b.shape, F32)
        m_b[...] = jnp.zeros(m_b.shape, F32)

    def run(s_cur, m_cur, s_prv, m_prv):
        def scores(hh, k_blk, rows, m_run):
            s = _dot(k_blk, q_t[hh])
            s_cur[hh, rows, :] = s
            return jnp.maximum(m_run, jnp.max(s.reshape(s.shape[0] // SUBLANES, SUBLANES, tq), axis=0))

        def weigh(hh, vt_blk, rows, m, acc):
            p = jnp.exp2(s_prv[hh, rows, :] - m).astype(BF16)
            lhs = jnp.concatenate([vt_blk, jnp.ones((2 * SUBLANES, vt_blk.shape[1]), BF16)], axis=0)
            return acc + _dot(lhs, p)

        ms = [jnp.max(m_prv[hh], axis=0, keepdims=True) for hh in range(2)]

        def step(j, carry):
            m_runs, accs = carry
            rows = pl.ds(pl.multiple_of(j * tk, tk), tk)
            m_runs = tuple(scores(hh, k_ref[0, rows, slabs[hh]], rows, m_runs[hh]) for hh in range(2))
            accs = tuple(weigh(hh, vt_ref[0, vrows[hh], rows], rows, ms[hh], accs[hh]) for hh in range(2))
            return m_runs, accs

        carry = ((jnp.full((SUBLANES, tq), -jnp.inf, F32),) * 2,
                 (jnp.zeros((MLA_DV + 2 * SUBLANES, tq), F32),) * 2)
        m_runs, accs = lax.fori_loop(0, n_blk, step, carry, unroll=2)
        m_runs = tuple(scores(hh, kc_ref[0, :, slabs[hh]], crow, m_runs[hh]) for hh in range(2))
        accs = tuple(weigh(hh, vtc_ref[0, vrows[hh], :], crow, ms[hh], accs[hh]) for hh in range(2))
        for hh in range(2):
            m_cur[hh] = m_runs[hh]
        outs = [acc[:MLA_DV] / acc[MLA_DV:MLA_DV + 1] for acc in accs]
        o_ref[0] = jnp.concatenate(outs, axis=0).T.astype(BF16)

    @pl.when(i % 2 == 0)
    def _():
        run(s_a, m_a, s_b, m_b)

    @pl.when(i % 2 == 1)
    def _():
        run(s_b, m_b, s_a, m_a)


def _attention_long(q, k, vt, kc, vtc):
    b, l, _ = q.shape
    lk = k.shape[1]
    lc = kc.shape[1]
    tq = ATTN_Q_TILE
    pw = 2 * HEAD_PAD
    vpw = 2 * MLA_DV
    assert lk % (2 * ATTN_KEY_BLOCK) == 0 and l % tq == 0
    n_q = l // tq
    return pl.pallas_call(
        _attn_long_kernel,
        out_shape=jax.ShapeDtypeStruct((b, l, MLA_HEADS * MLA_DV), BF16),
        grid=(b, MLA_HEADS // 2, n_q + 1),
        in_specs=[pl.BlockSpec((1, tq, pw), lambda bi, p, i: (bi, jnp.minimum(i, n_q - 1), p)),
                  pl.BlockSpec((1, lk, pw), lambda bi, p, i: (bi, 0, p)),
                  pl.BlockSpec((1, vpw, lk), lambda bi, p, i: (bi, p, 0)),
                  pl.BlockSpec((1, lc, pw), lambda bi, p, i: (bi, 0, p)),
                  pl.BlockSpec((1, vpw, lc), lambda bi, p, i: (bi, p, 0))],
        out_specs=pl.BlockSpec((1, tq, LANES), lambda bi, p, i: (bi, jnp.maximum(i - 1, 0), p)),
        scratch_shapes=[pltpu.VMEM((2, lk + lc, tq), F32), pltpu.VMEM((2, lk + lc, tq), F32),
                        pltpu.VMEM((2, SUBLANES, tq), F32), pltpu.VMEM((2, SUBLANES, tq), F32)],
        compiler_params=_cparams(("parallel", "parallel", "arbitrary")),
        name="mla_attention_long",
    )(q, k, vt, kc, vtc)


def _attn_kernel(*refs, has_ctx):
    if has_ctx:
        q_ref, k_ref, v_ref, kc_ref, vc_ref, o_ref = refs
    else:
        q_ref, k_ref, v_ref, o_ref = refs
    n_pairs = q_ref.shape[-1] // (2 * HEAD_PAD)
    for pair in range(n_pairs):
        out = None
        for hh in range(2):
            h0 = (2 * pair + hh) * HEAD_PAD
            sl = slice(h0, h0 + HEAD_PAD)
            q = q_ref[0, :, sl]
            s = _dot_nt(q, k_ref[0, :, sl])
            m = jnp.max(s, axis=-1, keepdims=True)
            if has_ctx:
                s2 = _dot_nt(q, kc_ref[0, :, sl])
                m = jnp.maximum(m, jnp.max(s2, axis=-1, keepdims=True))
            p = jnp.exp2(s - m)
            den = jnp.sum(p, axis=-1, keepdims=True)
            acc = _dot(p.astype(BF16), v_ref[0, :, sl])
            if has_ctx:
                p2 = jnp.exp2(s2 - m)
                den = den + jnp.sum(p2, axis=-1, keepdims=True)
                acc = acc + _dot(p2.astype(BF16), vc_ref[0, :, sl])
            o = acc / den
            out = o if out is None else out + o
        o_ref[0, :, pair * LANES:(pair + 1) * LANES] = out.astype(BF16)


def _attention(q, k, v, kc, vc):
    b, l, _ = q.shape
    lk = k.shape[1]
    tq = TOK_TILE
    pairs_per_step = MLA_HEADS // 2 if lk <= TOK_TILE else 1
    pw = 2 * HEAD_PAD * pairs_per_step
    has_ctx = kc is not None
    inputs = [q, k, v]
    in_specs = [pl.BlockSpec((1, tq, pw), lambda bi, p, i: (bi, i, p)),
                pl.BlockSpec((1, lk, pw), lambda bi, p, i: (bi, 0, p)),
                pl.BlockSpec((1, lk, pw), lambda bi, p, i: (bi, 0, p))]
    if has_ctx:
        lc = kc.shape[1]
        inputs += [kc, vc]
        in_specs += [pl.BlockSpec((1, lc, pw), lambda bi, p, i: (bi, 0, p))] * 2
    return pl.pallas_call(
        functools.partial(_attn_kernel, has_ctx=has_ctx),
        out_shape=jax.ShapeDtypeStruct((b, l, MLA_HEADS * MLA_DV), BF16),
        grid=(b, MLA_HEADS // 2 // pairs_per_step, l // tq),
        in_specs=in_specs,
        out_specs=pl.BlockSpec((1, tq, LANES * pairs_per_step), lambda bi, p, i: (bi, i, p)),
        compiler_params=_cparams(("parallel", "parallel", "arbitrary")),
        name="mla_attention",
    )(*inputs)


def _outproj_kernel(*refs, hgrn):
    if hgrn:
        x_ref, mod_ref, of_ref, ob_ref, gl_ref, gn_ref, b_ref, w_ref, lg_ref, lb_ref, o_ref = refs
        o = of_ref[0] + ob_ref[0]
        gate = _silu(gl_ref[0])
        gn = gn_ref[...]
        parts = []
        for h in range(HA_HEADS):
            sl = slice(h * HA_DV, (h + 1) * HA_DV)
            oh = o[:, sl]
            oh = oh * lax.rsqrt(jnp.mean(oh * oh, axis=-1, keepdims=True) + RMS_EPS) * gn
            parts.append((oh * gate[:, sl]).astype(BF16))
        a = jnp.concatenate(parts, axis=-1)
    else:
        x_ref, mod_ref, a_ref, b_ref, w_ref, lg_ref, lb_ref, o_ref = refs
        a = a_ref[0]
    half = a.shape[-1]
    y = _dot(a, w_ref[:half, :]) + _dot(b_ref[0], w_ref[half:, :])
    m = mod_ref[0]
    r = ALPHA * x_ref[0] + m[2:3] * y
    o_ref[0] = _layer_norm(r, lg_ref[...], lb_ref[...])


def _outproj(x, mod, a_inputs, b_in, w, ln_g, ln_b, hgrn):
    b, l, d = x.shape
    tm = TOK_TILE
    per_batch = mod.shape[0] > 1
    mod_map = (lambda i, j: (i, 0, 0)) if per_batch else (lambda i, j: (0, 0, 0))
    row = lambda width: pl.BlockSpec((1, tm, width), lambda i, j: (i, j, 0))
    full = lambda a: pl.BlockSpec(a.shape, lambda i, j: (0,) * a.ndim)
    inputs = [x, mod]
    in_specs = [row(d), pl.BlockSpec((1, 6, d), mod_map)]
    if hgrn:
        o_f, o_b, p_h, g_norm = a_inputs
        inputs += [o_f, o_b, p_h, g_norm]
        in_specs += [row(HA_W), row(HA_W),
                     pl.BlockSpec((1, tm, HA_W), lambda i, j: (i, j, 4)), full(g_norm)]
    else:
        inputs += [a_inputs]
        in_specs += [row(a_inputs.shape[-1])]
    inputs += [b_in, w, ln_g, ln_b]
    in_specs += [row(b_in.shape[-1]), full(w), full(ln_g), full(ln_b)]
    return pl.pallas_call(
        functools.partial(_outproj_kernel, hgrn=hgrn),
        out_shape=jax.ShapeDtypeStruct((b, l, d), F32),
        grid=(b, l // tm),
        in_specs=in_specs,
        out_specs=row(d),
        compiler_params=_cparams(("parallel", "parallel")),
        name="outproj_ln",
    )(*inputs)


def _conv_kernel(pm_ref, pp_ref, pn_ref, scw_ref, cfw_ref, cfb_ref, cfg_ref, cfbeta_ref,
                 ysc_ref, ycf_ref, ext_sc, ext_cf, *, lt):
    i = pl.program_id(1)
    n_i = pl.num_programs(1)
    w = SC_W

    def sc_in(p):
        return p[:, w:2 * w] * p[:, 2 * w:3 * w]

    def cf_in(p):
        return p[:, 3 * w:3 * w + CF_W] * _sigmoid(p[:, 3 * w + CF_W:3 * w + 2 * CF_W])

    pm = pm_ref[0]
    pp = pp_ref[0]
    pn = pn_ref[0]
    has_prev = i > 0
    has_next = i < n_i - 1
    n_ext = lt + 2 * HALO
    for ext, conv_in in ((ext_sc, sc_in), (ext_cf, cf_in)):
        ext[0, 0:HALO, :] = jnp.where(has_prev, conv_in(pp), 0.0)
        ext[0, HALO:HALO + lt, :] = conv_in(pm)
        ext[0, HALO + lt:, :] = jnp.where(has_next, conv_in(pn), 0.0)
    for s in range(1, SUBLANES):
        ext_cf[s, 0:n_ext - SUBLANES, :] = ext_cf[0, s:s + n_ext - SUBLANES, :]
    sc_shifts = sorted({(HALO - SC_K // 2 + j) % SUBLANES for j in range(SC_K)} - {0})
    for s in sc_shifts:
        ext_sc[s, 0:n_ext - SUBLANES, :] = ext_sc[0, s:s + n_ext - SUBLANES, :]

    def tap(ext, off, rows):
        s = off % SUBLANES
        return ext[s, off - s:off - s + rows, :]

    rb = 32
    for r in range(lt // rb):
        base = HALO + r * rb
        acc = None
        for j in range(SC_K):
            term = jnp.tile(scw_ref[j], (rb // SUBLANES, 1)) * tap(ext_sc, base - SC_K // 2 + j, rb)
            acc = term if acc is None else acc + term
        ysc_ref[0, r * rb:(r + 1) * rb, :] = (pm[r * rb:(r + 1) * rb, 0:w] * acc).astype(BF16)
        acc = None
        for j in range(CF_K):
            term = jnp.tile(cfw_ref[j], (rb // SUBLANES, 1)) * tap(ext_cf, base - CF_K // 2 + j, rb)
            acc = term if acc is None else acc + term
        u = _layer_norm(acc + cfb_ref[...], cfg_ref[...], cfbeta_ref[...])
        ycf_ref[0, r * rb:(r + 1) * rb, :] = _silu(u).astype(BF16)


def _conv_mixers(p1, sc_w, cf_w, cf_b, cf_g, cf_beta):
    b, l, width = p1.shape
    lt = TOK_TILE
    hb = lt // HALO
    n_h = l // HALO
    sc_w = jnp.broadcast_to(sc_w[:, None, :], (sc_w.shape[0], SUBLANES, sc_w.shape[1]))
    cf_w = jnp.broadcast_to(cf_w[:, None, :], (cf_w.shape[0], SUBLANES, cf_w.shape[1]))
    full = lambda a: pl.BlockSpec(a.shape, lambda bi, i: (0,) * a.ndim)
    return pl.pallas_call(
        functools.partial(_conv_kernel, lt=lt),
        out_shape=[jax.ShapeDtypeStruct((b, l, SC_W), BF16), jax.ShapeDtypeStruct((b, l, CF_W), BF16)],
        grid=(b, l // lt),
        in_specs=[
            pl.BlockSpec((1, lt, width), lambda bi, i: (bi, i, 0)),
            pl.BlockSpec((1, HALO, width), lambda bi, i: (bi, jnp.maximum(i * hb - 1, 0), 0)),
            pl.BlockSpec((1, HALO, width), lambda bi, i: (bi, jnp.minimum((i + 1) * hb, n_h - 1), 0)),
            full(sc_w), full(cf_w), full(cf_b), full(cf_g), full(cf_beta),
        ],
        out_specs=[pl.BlockSpec((1, lt, SC_W), lambda bi, i: (bi, i, 0)),
                   pl.BlockSpec((1, lt, CF_W), lambda bi, i: (bi, i, 0))],
        scratch_shapes=[pltpu.VMEM((SUBLANES, lt + 2 * HALO, SC_W), F32),
                        pltpu.VMEM((SUBLANES, lt + 2 * HALO, CF_W), F32)],
        compiler_params=_cparams(("parallel", "parallel")),
        name="conv_mixers",
    )(p1, p1, p1, sc_w, cf_w, cf_b, cf_g, cf_beta)


def _route_t(lt):
    t = lt.shape[1]
    row = lax.broadcasted_iota(jnp.int32, (SUBLANES, t), 0).astype(F32)
    neg = -jnp.inf
    big = float(LANES)
    gl = jnp.where(row < N_GROUPS, lt[N_EXPERTS:N_EXPERTS + SUBLANES], neg)
    gmax = jnp.max(gl, axis=0, keepdims=True)
    p_g = 1.0 / jnp.sum(jnp.exp(gl - gmax), axis=0, keepdims=True)
    g_sel = jnp.min(jnp.where(gl == gmax, row, big), axis=0, keepdims=True)
    el = lt[0:EXP_PER_GROUP]
    for gi in range(1, N_GROUPS):
        el = jnp.where(g_sel == gi, lt[gi * EXP_PER_GROUP:(gi + 1) * EXP_PER_GROUP], el)
    v1 = jnp.max(el, axis=0, keepdims=True)
    i1 = jnp.min(jnp.where(el == v1, row, big), axis=0, keepdims=True)
    el2 = jnp.where(row == i1, neg, el)
    v2 = jnp.max(el2, axis=0, keepdims=True)
    i2 = jnp.min(jnp.where(el2 == v2, row, big), axis=0, keepdims=True)
    e2 = jnp.exp(v2 - v1)
    w1 = p_g / (1.0 + e2)
    w2 = p_g * e2 / (1.0 + e2)
    comb = jnp.where(row == i1, w1, 0.0) + jnp.where(row == i2, w2, 0.0)
    onehot = jnp.where(row == g_sel, 1.0, 0.0)
    return onehot, comb


def _moe_kernel(x_ref, mod_ref, wr_ref, tri_ref, wg_ref, wu_ref, wd_ref, lg_ref, lb_ref, o_ref,
                hb_scr, hs_scr, os_scr, pt_scr, cs_scr):
    m = mod_ref[0]
    tm, cap = pt_scr.shape
    d = x_ref.shape[-1]
    half = d // 2
    cw = EXP_PER_GROUP * EXP_FF

    h = x_ref[0] * (1.0 + m[4:5]) + m[3:4]
    h_hi = h.astype(BF16)
    hb_scr[...] = h_hi
    h_lo = (h - h_hi.astype(F32)).astype(BF16)
    l2 = _dot(h_hi, wr_ref[...])
    logits = l2[:, :LANES] + l2[:, LANES:] + _dot(h_lo, wr_ref[:, :LANES])
    onehot, comb = _route_t(logits.T)
    rank = _dot(onehot.astype(BF16), tri_ref[...])
    cnt = jnp.sum(onehot, axis=1, keepdims=True)
    padded = jnp.floor((cnt + (MOE_BLOCK - 1)) * (1.0 / MOE_BLOCK)) * MOE_BLOCK
    start = jnp.zeros((1, 1), F32)
    dest = jnp.zeros((1, tm), F32)
    first, count = [], []
    for gi in range(N_GROUPS):
        dest = dest + onehot[gi:gi + 1] * (start + rank[gi:gi + 1])
        first.append((start[0, 0] * (1.0 / MOE_BLOCK)).astype(jnp.int32))
        count.append((padded[gi, 0] * (1.0 / MOE_BLOCK)).astype(jnp.int32))
        start = start + padded[gi:gi + 1]
    aux = jnp.concatenate([jnp.broadcast_to(dest, (SUBLANES, tm)), comb,
                           jnp.zeros((LANES - 2 * SUBLANES, tm), F32)], axis=0).T
    lane = lax.broadcasted_iota(jnp.int32, (tm, LANES), 1)
    comb_tok = jnp.where((lane >= SUBLANES) & (lane < 2 * SUBLANES), aux, 0.0)
    c_hi = comb_tok.astype(BF16)
    c_mid = (comb_tok - c_hi.astype(F32)).astype(BF16)
    slot_t = lax.broadcasted_iota(jnp.int32, (tm, cap), 1).astype(F32)
    pt_scr[...] = jnp.where(slot_t == aux[:, 0:1], 1.0, 0.0).astype(BF16)
    slot = lax.broadcasted_iota(jnp.int32, (cap, tm), 0).astype(F32)
    perm = jnp.where(slot == dest, 1.0, 0.0).astype(BF16)
    hs_scr[:, :half] = _dot(perm, hb_scr[:, :half]).astype(BF16)
    hs_scr[:, half:] = _dot(perm, hb_scr[:, half:]).astype(BF16)
    cs2 = _dot(perm, jnp.concatenate([c_hi, c_mid], axis=1))
    cs_scr[...] = cs2[:, :LANES] + cs2[:, LANES:]
    os_scr[...] = jnp.zeros_like(os_scr)

    for gi in range(N_GROUPS):
        cols = slice(gi * cw, (gi + 1) * cw)

        def block(i, carry, gi=gi, cols=cols):
            r0 = pl.multiple_of((first[gi] + i) * MOE_BLOCK, MOE_BLOCK)
            rows = pl.ds(r0, MOE_BLOCK)
            hs = hs_scr[rows, :]
            cs = cs_scr[rows, :]
            hid = _silu(_dot(hs, wg_ref[:, cols])) * _dot(hs, wu_ref[:, cols])
            hid = jnp.concatenate(
                [hid[:, e * EXP_FF:(e + 1) * EXP_FF] * cs[:, SUBLANES + e:SUBLANES + e + 1]
                 for e in range(EXP_PER_GROUP)], axis=1)
            os_scr[rows, :] = _dot(hid.astype(BF16), wd_ref[cols, :]).astype(BF16)
            return carry

        lax.fori_loop(0, count[gi], block, 0)

    y = jnp.concatenate([_dot(pt_scr[...], os_scr[:, :half]), _dot(pt_scr[...], os_scr[:, half:])], axis=1)
    r = ALPHA * x_ref[0] + m[5:6] * y
    o_ref[0] = _layer_norm(r, lg_ref[...], lb_ref[...])


def _moe(x, mod, wr, wg, wu, wd, ln_g, ln_b):
    b, l, d = x.shape
    tm = MOE_TILE
    cap = (tm + N_GROUPS * (MOE_BLOCK - 1)) // MOE_BLOCK * MOE_BLOCK
    cw = EXP_PER_GROUP * EXP_FF
    per_batch = mod.shape[0] > 1
    mod_map = (lambda i, t: (i, 0, 0)) if per_batch else (lambda i, t: (0, 0, 0))
    tri = jnp.asarray(np.triu(np.ones((tm, tm), np.float32), k=1), BF16)
    full = lambda a: pl.BlockSpec(a.shape, lambda i, t: (0,) * a.ndim, pipeline_mode=pl.Buffered(1))
    return pl.pallas_call(
        _moe_kernel,
        out_shape=jax.ShapeDtypeStruct((b, l, d), F32),
        grid=(b, l // tm),
        in_specs=[
            pl.BlockSpec((1, tm, d), lambda i, t: (i, t, 0)),
            pl.BlockSpec((1, 6, d), mod_map),
            full(wr), full(tri), full(wg), full(wu), full(wd), full(ln_g), full(ln_b),
        ],
        out_specs=pl.BlockSpec((1, tm, d), lambda i, t: (i, t, 0)),
        scratch_shapes=[pltpu.VMEM((tm, d), BF16), pltpu.VMEM((cap, d), BF16), pltpu.VMEM((cap, d), BF16),
                        pltpu.VMEM((tm, cap), BF16), pltpu.VMEM((cap, LANES), F32)],
        compiler_params=_cparams(("arbitrary", "arbitrary")),
        name="hier_moe_ln",
    )(x, mod, wr, tri, wg, wu, wd, ln_g, ln_b)


def _rope_swap_perm():
    idx = np.arange(MLA_DR)
    return idx ^ (MLA_DR // 4)


def _rope_tables(n_tok):
    rows = n_tok // GRID_W
    pos_r = jnp.repeat(jnp.arange(rows, dtype=F32), GRID_W)
    pos_c = (jnp.arange(rows * GRID_W) % GRID_W).astype(F32)
    n_freq = MLA_DR // 4
    inv = ROPE_BASE ** (-jnp.arange(n_freq, dtype=F32) / n_freq)
    ang = jnp.stack([pos_r[:, None] * inv, pos_c[:, None] * inv], axis=1)
    cos, sin = jnp.cos(ang), jnp.sin(ang)
    cos32 = jnp.stack([cos, cos], axis=2).reshape(n_tok, MLA_DR)
    sin32 = jnp.stack([-sin, sin], axis=2).reshape(n_tok, MLA_DR)
    pad_hi = LANES - MLA_DN - MLA_DR
    cos_t = jnp.concatenate([jnp.ones((n_tok, MLA_DN), F32), cos32, jnp.zeros((n_tok, pad_hi), F32)], axis=1)
    sin_t = jnp.concatenate([jnp.zeros((n_tok, MLA_DN), F32), sin32, jnp.zeros((n_tok, pad_hi), F32)], axis=1)
    return cos_t, sin_t


def _ab_weights(w_in, w_uq, w_ukv):
    perm = _rope_swap_perm()
    pad_hi = LANES - MLA_DN - MLA_DR
    w_h = w_in[:, :5 * HA_W]
    cq = w_in[:, 5 * HA_W:5 * HA_W + MLA_Q_RANK]
    ckv = w_in[:, 5 * HA_W + MLA_Q_RANK:5 * HA_W + MLA_Q_RANK + MLA_KV_RANK]
    kr = w_in[:, 5 * HA_W + MLA_Q_RANK + MLA_KV_RANK:]
    d = w_in.shape[0]
    z_lo = jnp.zeros((d, MLA_DN), F32)
    z_hi = jnp.zeros((d, pad_hi), F32)
    w_m = jnp.concatenate([cq, ckv, z_lo, kr, z_hi, z_lo, kr[:, perm], z_hi], axis=1)
    uq = w_uq.reshape(MLA_Q_RANK, MLA_HEADS, MLA_DN + MLA_DR)
    q_nope, q_rope = uq[..., :MLA_DN], uq[..., MLA_DN:]
    zq_hi = jnp.zeros((MLA_Q_RANK, MLA_HEADS, pad_hi), F32)
    wq1 = jnp.concatenate([q_nope, q_rope, zq_hi], axis=-1).reshape(MLA_Q_RANK, -1)
    wq2 = jnp.concatenate([jnp.zeros_like(q_nope), q_rope[..., perm], zq_hi], axis=-1).reshape(MLA_Q_RANK, -1)
    ukv = w_ukv.reshape(MLA_KV_RANK, MLA_HEADS, MLA_DN + MLA_DV)
    k_nope, v = ukv[..., :MLA_DN], ukv[..., MLA_DN:]
    wk = jnp.concatenate([k_nope, jnp.zeros((MLA_KV_RANK, MLA_HEADS, LANES - MLA_DN), F32)], axis=-1)
    wk = wk.reshape(MLA_KV_RANK, -1)
    v_pairs = v.reshape(MLA_KV_RANK, MLA_HEADS // 2, 2, MLA_DV)
    zv = jnp.zeros_like(v_pairs[:, :, 0])
    wv = jnp.stack([jnp.concatenate([v_pairs[:, :, 0], zv], axis=-1),
                    jnp.concatenate([zv, v_pairs[:, :, 1]], axis=-1)], axis=2).reshape(MLA_KV_RANK, -1)
    wvt = v.reshape(MLA_KV_RANK, MLA_HEADS * MLA_DV).T
    bf = lambda a: a.astype(BF16)
    return bf(w_h), bf(w_m), bf(wq1), bf(wq2), bf(wk), bf(wv), bf(wvt)


def _moe_weights(w_group, w_expert, w_gate, w_up, w_down):
    d = w_group.shape[0]
    wr = jnp.concatenate([w_expert, w_group, jnp.zeros((d, LANES - N_EXPERTS - N_GROUPS), F32)], axis=1)
    wr_hi = wr.astype(BF16)
    wr = jnp.concatenate([wr_hi, (wr - wr_hi.astype(F32)).astype(BF16)], axis=1)
    wg = jnp.transpose(w_gate, (1, 0, 2)).reshape(d, N_EXPERTS * EXP_FF).astype(BF16)
    wu = jnp.transpose(w_up, (1, 0, 2)).reshape(d, N_EXPERTS * EXP_FF).astype(BF16)
    wd = w_down.reshape(N_EXPERTS * EXP_FF, d).astype(BF16)
    return wr, wg, wu, wd


def kernel(x_prompt, x_sample, state_hgrn_fwd, state_hgrn_bwd, cache_mla_ckv, cache_mla_krope, c, c_ctx, mod_w, mod_b, ln_g, ln_b, ab_w_in, ab_w_out, hgrn_lb_logits, hgrn_norm_g, mla_q_norm_g, mla_w_uq, mla_kv_norm_g, mla_w_ukv, cd_w_in, cd_w_out, sc_conv_w, cf_conv_w, cf_conv_b, cf_ln_g, cf_ln_b, moe_w_group, moe_w_expert, moe_w_gate, moe_w_up, moe_w_down):
    dec_b = x_sample.shape[0]
    d = D_MODEL
    cvec = jnp.concatenate([c, c_ctx[None, :], jnp.zeros((SUBLANES - dec_b - 1, d), F32)], axis=0)
    mods = _mod_vectors(cvec, mod_w, mod_b)
    rope_tabs = _rope_tables(x_sample.shape[1])
    xp, xs = x_prompt, x_sample
    new_sf = new_sb = new_ckv = new_kr = None
    for l in range(DEPTH):
        mod_lat = mods[l, :dec_b].reshape(dec_b, 6, d)
        mod_ctx = mods[l, dec_b:dec_b + 1].reshape(1, 6, d)
        row = lambda a: a.reshape(1, -1)
        if l % 2 == 0:
            e = l // 2
            w_h, w_m, wq1, wq2, wk, wv, wvt = _ab_weights(ab_w_in[e], mla_w_uq[e], mla_w_ukv[e])
            w_out = ab_w_out[e].astype(BF16)
            qg, kvg, gn = row(mla_q_norm_g[e]), row(mla_kv_norm_g[e]), row(hgrn_norm_g[e])
            ph_p, q_p, k_p, v_p, ckv_p, kr_p = _ab_inproj(xp, mod_ctx, w_h, w_m, None, qg, kvg, wq1, None, wk, wv)
            of_p, ob_p, sf, sb = _hgrn(ph_p, hgrn_lb_logits, None, None, e)
            om_p = _attention(q_p, k_p, v_p, None, None)
            xp = _outproj(xp, mod_ctx, (of_p, ob_p, ph_p, gn), om_p, w_out, row(ln_g[l, 0]), row(ln_b[l, 0]), True)
            ph_s, q_s, k_s, vt_s, _, _ = _ab_inproj(xs, mod_lat, w_h, w_m, rope_tabs, qg, kvg, wq1, wq2, wk, wvt)
            of_s, ob_s, _, _ = _hgrn(ph_s, hgrn_lb_logits, state_hgrn_fwd[:, e], state_hgrn_bwd[:, e], e)
            kc, vtc = _mla_ctx(cache_mla_ckv[:, e], cache_mla_krope[:, e], wk, wvt)
            om_s = _attention_long(q_s, k_s, vt_s, kc, vtc)
            xs = _outproj(xs, mod_lat, (of_s, ob_s, ph_s, gn), om_s, w_out, row(ln_g[l, 0]), row(ln_b[l, 0]), True)
            new_sf, new_sb, new_ckv = sf, sb, ckv_p
            new_kr = kr_p[:, :, MLA_DN:MLA_DN + MLA_DR]
        else:
            jx = l // 2
            w1 = cd_w_in[jx].astype(BF16)
            w_out = cd_w_out[jx].astype(BF16)
            cd = (sc_conv_w[jx], cf_conv_w[jx], row(cf_conv_b[jx]), row(cf_ln_g[jx]), row(cf_ln_b[jx]))
            (p1_p,) = _inproj(xp, mod_ctx, [w1])
            ysc_p, ycf_p = _conv_mixers(p1_p, *cd)
            xp = _outproj(xp, mod_ctx, ysc_p, ycf_p, w_out, row(ln_g[l, 0]), row(ln_b[l, 0]), False)
            (p1_s,) = _inproj(xs, mod_lat, [w1])
            ysc_s, ycf_s = _conv_mixers(p1_s, *cd)
            xs = _outproj(xs, mod_lat, ysc_s, ycf_s, w_out, row(ln_g[l, 0]), row(ln_b[l, 0]), False)
        wr, wg, wu, wd = _moe_weights(moe_w_group[l], moe_w_expert[l], moe_w_gate[l], moe_w_up[l], moe_w_down[l])
        xp_t = xp.reshape(-1, MOE_TILE, d)
        xp = _moe(xp_t, mod_ctx, wr, wg, wu, wd, row(ln_g[l, 1]), row(ln_b[l, 1])).reshape(xp.shape)
        xs = _moe(xs, mod_lat, wr, wg, wu, wd, row(ln_g[l, 1]), row(ln_b[l, 1]))
    return (xp, xs, new_sf[:, None], new_sb[:, None], new_ckv[:, None], new_kr[:, None])
```

```python
import functools

import numpy as np
import jax
import jax.numpy as jnp
from jax import lax
from jax.experimental import pallas as pl
from jax.experimental.pallas import tpu as pltpu

F32 = jnp.float32
BF16 = jnp.bfloat16
HIGHEST = lax.Precision.HIGHEST

D_MODEL = 1024
DEPTH = 2
GRID_W = 64
N_EVEN = (DEPTH + 1) // 2
HA_HEADS = 4
HA_DK = 128
HA_DV = 128
HA_W = HA_HEADS * HA_DK
CHUNK = 32
MLA_HEADS = 8
MLA_DN = 64
MLA_DR = 32
MLA_DV = 64
MLA_Q_RANK = 384
MLA_KV_RANK = 256
ROPE_BASE = 10000.0
SC_W = 512
SC_K = 3
CF_W = 512
CF_K = 31
N_GROUPS = 4
EXP_PER_GROUP = 8
N_EXPERTS = N_GROUPS * EXP_PER_GROUP
EXP_FF = 128
ALPHA = (2.0 * DEPTH) ** 0.25
LOG2_E = 1.4426950408889634
LN_EPS = 1e-5
RMS_EPS = 1e-6

LANES = 128
SUBLANES = 8
VMEM_LIMIT = 56 * 1024 * 1024

HEAD_PAD = LANES
MLA_SLAB = MLA_Q_RANK + MLA_KV_RANK + 2 * LANES
TOK_TILE = 256
GROUP_ROWS = 128
N_LEVELS = 5
SAFE_LOG_DECAY = 60.0
ATTN_KEY_BLOCK = 512
ATTN_Q_TILE = 256
MOE_TILE = 512
MOE_BLOCK = 160
HALO = 16


def _cparams(sem):
    return pltpu.CompilerParams(dimension_semantics=sem, vmem_limit_bytes=VMEM_LIMIT)


def _sigmoid(x):
    return 0.5 * jnp.tanh(0.5 * x) + 0.5


def _silu(x):
    half = 0.5 * x
    return half * jnp.tanh(half) + half


def _layer_norm(r, g, b):
    mu = jnp.mean(r, axis=-1, keepdims=True)
    d = r - mu
    var = jnp.mean(d * d, axis=-1, keepdims=True)
    return d * lax.rsqrt(var + LN_EPS) * g + b


def _dot(a, b):
    return jnp.dot(a, b, preferred_element_type=F32)


def _dot_nt(a, b):
    return lax.dot_general(a, b, (((1,), (1,)), ((), ())), preferred_element_type=F32)


def _dot_tn(a, b):
    return lax.dot_general(a, b, (((0,), (0,)), ((), ())), preferred_element_type=F32)


def _mod_kernel(c_ref, w_ref, b_ref, o_ref):
    s = _silu(c_ref[...])
    o_ref[0] = jnp.dot(s, w_ref[0], precision=HIGHEST, preferred_element_type=F32) + b_ref[0]


def _mod_vectors(cvec, mod_w, mod_b):
    n_out = mod_w.shape[-1]
    tn = 3072
    return pl.pallas_call(
        _mod_kernel,
        out_shape=jax.ShapeDtypeStruct((DEPTH, SUBLANES, n_out), F32),
        grid=(DEPTH, n_out // tn),
        in_specs=[
            pl.BlockSpec((SUBLANES, D_MODEL), lambda l, j: (0, 0)),
            pl.BlockSpec((1, D_MODEL, tn), lambda l, j: (l, 0, j)),
            pl.BlockSpec((1, 1, tn), lambda l, j: (l, 0, j)),
        ],
        out_specs=pl.BlockSpec((1, SUBLANES, tn), lambda l, j: (l, 0, j)),
        compiler_params=_cparams(("arbitrary", "arbitrary")),
        name="mod_vectors",
    )(cvec, mod_w, mod_b.reshape(DEPTH, 1, n_out))


def _inproj_kernel(*refs, n_w):
    x_ref, mod_ref = refs[0], refs[1]
    w_refs = refs[2:2 + n_w]
    o_refs = refs[2 + n_w:]
    m = mod_ref[0]
    h = (x_ref[0] * (1.0 + m[1:2]) + m[0:1]).astype(BF16)
    for w_ref, o_ref in zip(w_refs, o_refs):
        o_ref[0] = _dot(h, w_ref[...])


def _inproj(x, mod, weights):
    b, l, d = x.shape
    tm = TOK_TILE
    per_batch = mod.shape[0] > 1
    mod_map = (lambda i, j: (i, 0, 0)) if per_batch else (lambda i, j: (0, 0, 0))
    in_specs = [pl.BlockSpec((1, tm, d), lambda i, j: (i, j, 0)),
                pl.BlockSpec((1, 6, d), mod_map)]
    in_specs += [pl.BlockSpec(w.shape, lambda i, j: (0, 0)) for w in weights]
    out_shape = [jax.ShapeDtypeStruct((b, l, w.shape[1]), F32) for w in weights]
    out_specs = [pl.BlockSpec((1, tm, w.shape[1]), lambda i, j: (i, j, 0)) for w in weights]
    return pl.pallas_call(
        functools.partial(_inproj_kernel, n_w=len(weights)),
        out_shape=out_shape,
        grid=(b, l // tm),
        in_specs=in_specs,
        out_specs=out_specs,
        compiler_params=_cparams(("parallel", "parallel")),
        name="modulate_inproj",
    )(x, mod, *weights)


def _hgrn_tables():
    n = GROUP_ROWS
    t = np.arange(n)[:, None]
    j = np.arange(n)[None, :]
    same_chunk = (t // CHUNK) == (j // CHUNK)
    e_f, e_b = [], []
    lv_f = np.full((n, n), -1, np.int32)
    for lvl in range(N_LEVELS):
        m = CHUNK >> (lvl + 1)
        blk0 = (t // (2 * m)) * (2 * m)
        r = blk0 + m - 1
        upper = t > r
        ef = np.where(upper, (j > r) & (j <= t), (j > t) & (j <= r))
        r2 = blk0 + m
        lower = t < r2
        eb = np.where(lower, (j >= t) & (j < r2), (j >= r2) & (j < t))
        e_f.append(ef)
        e_b.append(eb)
        s = np.arange(n)[None, :]
        same_blk = (t // (2 * m)) == (s // (2 * m))
        q_side = (t % (2 * m)) >= m
        k_side = (s % (2 * m)) < m
        lv_f[same_blk & q_side & k_side] = lvl
    lv_f[np.arange(n), np.arange(n)] = N_LEVELS
    e_f = np.concatenate(e_f, axis=0).astype(np.float32)
    e_b = np.concatenate(e_b, axis=0).astype(np.float32)
    e = np.stack([e_f, e_b])
    cum = np.stack([same_chunk & (j <= t), same_chunk & (j >= t)]).astype(np.float32)
    lv = np.stack([lv_f, lv_f.T])
    return e, cum, lv


def _hgrn_kernel(qf_ref, vf_ref, ff_ref, qb_ref, vb_ref, fb_ref, lbl_ref, e_ref, cum_ref, lv_ref, s0f_ref, s0b_ref,
                 of_ref, ob_ref, sf_ref, sb_ref,
                 st_scr, qe_scr, kd_scr, sc_scr, v_scr, oi_scr, dec_scr, *, lt, slot, has_state):
    i = pl.program_id(1)
    n_i = pl.num_programs(1)
    n_chunks = lt // CHUNK
    n_groups = lt // GROUP_ROWS
    chunks_per_group = GROUP_ROWS // CHUNK

    @pl.when(i == 0)
    def _():
        for d, s0_ref in enumerate((s0f_ref, s0b_ref)):
            for h in range(HA_HEADS):
                if has_state:
                    st_scr[d, h] = s0_ref[0, h].T
                else:
                    st_scr[d, h] = jnp.zeros((HA_DV, HA_DK), F32)

    lg = lbl_ref[...]
    n_slots = lg.shape[0]
    mx = lg[0]
    for s in range(1, n_slots):
        mx = jnp.maximum(mx, lg[s])
    ex = [jnp.exp(lg[s] - mx) for s in range(n_slots)]
    den = ex[0]
    for s in range(1, n_slots):
        den = den + ex[s]
    num = ex[0]
    for s in range(1, slot + 1):
        num = num + ex[s]
    lb_all = num / den

    dirs = ((qf_ref, vf_ref, ff_ref), (qb_ref, vb_ref, fb_ref))
    head_cols = [slice(h * HA_DK, (h + 1) * HA_DK) for h in range(HA_HEADS)]

    def gates(d, rows, cols):
        q_ref, _, f_ref = dirs[d]
        q = _silu(q_ref[0, rows, cols])
        lb_h = lb_all[d:d + 1, cols]
        f = lb_h + (1.0 - lb_h) * _sigmoid(f_ref[0, rows, cols])
        g = jnp.log(f)
        g_hi = g.astype(BF16)
        g_lo = (g - g_hi.astype(F32)).astype(BF16)
        return q, 1.0 - f, g_hi, g_lo

    def group_step(grp, tot_min):
        r0 = pl.multiple_of(grp * GROUP_ROWS, GROUP_ROWS)
        rows = pl.ds(r0, GROUP_ROWS)
        for d in range(2):
            cum = cum_ref[d]
            lv = lv_ref[d]
            for cols in head_cols:
                q, k, g_hi, g_lo = gates(d, rows, cols)
                bcum = _dot(cum, g_hi) + _dot(cum, g_lo)
                v_scr[d, rows, cols] = dirs[d][1][0, rows, cols].astype(BF16)
                qe = (q * jnp.exp(bcum)).astype(BF16)
                qe_scr[d, rows, cols] = qe
                kx = []
                for cc in range(chunks_per_group):
                    edge = cc * CHUNK + (CHUNK - 1 if d == 0 else 0)
                    tot = bcum[edge:edge + 1]
                    b_c = bcum[cc * CHUNK:(cc + 1) * CHUNK]
                    kd = k[cc * CHUNK:(cc + 1) * CHUNK] * jnp.exp(tot - b_c)
                    kd_scr[d, pl.ds(r0 + cc * CHUNK, CHUNK), cols] = kd.astype(BF16)
                    dec_scr[d, grp * chunks_per_group + cc, :, cols] = jnp.exp(tot)
                    kx.append(kd * jnp.exp(-tot))
                    tot_min = jnp.minimum(tot_min, tot)
                p = _dot_nt(qe, jnp.concatenate(kx, axis=0).astype(BF16))
                sc_scr[d, rows, cols] = jnp.where(lv >= 0, p, 0.0).astype(BF16)
        return tot_min

    tot_min = lax.fori_loop(0, n_groups, group_step, jnp.zeros((1, HA_DK), F32), unroll=True)

    def one_factor():
        for d in range(2):
            for grp in range(n_groups):
                rows = slice(grp * GROUP_ROWS, (grp + 1) * GROUP_ROWS)
                for cols in head_cols:
                    oi_scr[d, rows, cols] = _dot(sc_scr[d, rows, cols], v_scr[d, rows, cols])

    def per_level():
        for d in range(2):

            def group_step(grp, carry, d=d):
                rows = pl.ds(pl.multiple_of(grp * GROUP_ROWS, GROUP_ROWS), GROUP_ROWS)
                lv = lv_ref[d]
                for cols in head_cols:
                    q, k, g_hi, g_lo = gates(d, rows, cols)
                    x = jnp.exp(_dot(e_ref[d], g_hi) + _dot(e_ref[d], g_lo))
                    sc = jnp.where(lv == N_LEVELS, _dot_nt(q.astype(BF16), k.astype(BF16)), 0.0)
                    for lvl in range(N_LEVELS):
                        xl = x[lvl * GROUP_ROWS:(lvl + 1) * GROUP_ROWS]
                        p = _dot_nt((q * xl).astype(BF16), (k * xl).astype(BF16))
                        sc = jnp.where(lv == lvl, p, sc)
                    oi_scr[d, rows, cols] = _dot(sc.astype(BF16), v_scr[d, rows, cols])
                return carry

            lax.fori_loop(0, n_groups, group_step, 0)

    lax.cond(jnp.min(tot_min) > -SAFE_LOG_DECAY, one_factor, per_level)

    out_refs = (of_ref, ob_ref)

    def chunk_step(c, carry):
        for d in range(2):
            cc = c if d == 0 else n_chunks - 1 - c
            r0 = pl.multiple_of(cc * CHUNK, CHUNK)
            rows = pl.ds(r0, CHUNK)
            for h in range(HA_HEADS):
                cols = slice(h * HA_DK, (h + 1) * HA_DK)
                st = st_scr[d, h]
                o_state = _dot_nt(qe_scr[d, rows, cols], st.astype(BF16))
                out_refs[d][0, rows, cols] = oi_scr[d, rows, cols] + o_state
                upd = _dot_tn(v_scr[d, rows, cols], kd_scr[d, rows, cols])
                st_scr[d, h] = st * dec_scr[d, cc, :, cols] + upd
        return carry

    lax.fori_loop(0, n_chunks, chunk_step, 0, unroll=True)

    @pl.when(i == n_i - 1)
    def _():
        for d, s_ref in enumerate((sf_ref, sb_ref)):
            for h in range(HA_HEADS):
                s_ref[0, h] = st_scr[d, h].T


def _hgrn(p_h, lb_logits, s0_f, s0_b, slot):
    b, l, _ = p_h.shape
    lt = TOK_TILE
    n_t = l // lt
    has_state = s0_f is not None
    if not has_state:
        s0_f = jnp.zeros((1, HA_HEADS, HA_DK, HA_DV), F32)
        s0_b = s0_f
    e_np, cum_np, lv_np = _hgrn_tables()
    e_mat = jnp.asarray(e_np, BF16)
    cum = jnp.asarray(cum_np, BF16)
    lv = jnp.asarray(lv_np, jnp.int32)
    w = HA_W

    def sec(idx, rev):
        if rev:
            return pl.BlockSpec((1, lt, w), lambda bi, i: (bi, n_t - 1 - i, idx))
        return pl.BlockSpec((1, lt, w), lambda bi, i: (bi, i, idx))

    state_map = (lambda bi, i: (bi, 0, 0, 0)) if has_state else (lambda bi, i: (0, 0, 0, 0))
    in_specs = [
        sec(0, False), sec(3, False), sec(1, False),
        sec(0, True), sec(3, True), sec(2, True),
        pl.BlockSpec(lb_logits.shape, lambda bi, i: (0, 0, 0)),
        pl.BlockSpec(e_mat.shape, lambda bi, i: (0, 0, 0)),
        pl.BlockSpec(cum.shape, lambda bi, i: (0, 0, 0)),
        pl.BlockSpec(lv.shape, lambda bi, i: (0, 0, 0)),
        pl.BlockSpec((1, HA_HEADS, HA_DK, HA_DV), state_map),
        pl.BlockSpec((1, HA_HEADS, HA_DK, HA_DV), state_map),
    ]
    out_shape = [
        jax.ShapeDtypeStruct((b, l, w), F32),
        jax.ShapeDtypeStruct((b, l, w), F32),
        jax.ShapeDtypeStruct((b, HA_HEADS, HA_DK, HA_DV), F32),
        jax.ShapeDtypeStruct((b, HA_HEADS, HA_DK, HA_DV), F32),
    ]
    out_specs = [
        pl.BlockSpec((1, lt, w), lambda bi, i: (bi, i, 0)),
        pl.BlockSpec((1, lt, w), lambda bi, i: (bi, n_t - 1 - i, 0)),
        pl.BlockSpec((1, HA_HEADS, HA_DK, HA_DV), lambda bi, i: (bi, 0, 0, 0)),
        pl.BlockSpec((1, HA_HEADS, HA_DK, HA_DV), lambda bi, i: (bi, 0, 0, 0)),
    ]
    scratch = [
        pltpu.VMEM((2, HA_HEADS, HA_DV, HA_DK), F32),
        pltpu.VMEM((2, lt, w), BF16),
        pltpu.VMEM((2, lt, w), BF16),
        pltpu.VMEM((2, lt, w), BF16),
        pltpu.VMEM((2, lt, w), BF16),
        pltpu.VMEM((2, lt, w), F32),
        pltpu.VMEM((2, lt // CHUNK, 1, w), F32),
    ]
    return pl.pallas_call(
        functools.partial(_hgrn_kernel, lt=lt, slot=slot, has_state=has_state),
        out_shape=out_shape,
        grid=(b, n_t),
        in_specs=in_specs,
        out_specs=out_specs,
        scratch_shapes=scratch,
        compiler_params=_cparams(("parallel", "arbitrary")),
        name="hgrn2_scan",
    )(p_h, p_h, p_h, p_h, p_h, p_h, lb_logits, e_mat, cum, lv, s0_f, s0_b)


def _ab_inproj_kernel(*refs, rope, key_major):
    if rope:
        (x_ref, mod_ref, wh_ref, wm_ref, cos_ref, sin_ref, qg_ref, kvg_ref, wq1_ref, wq2_ref, wk_ref,
         wv_ref) = refs[:12]
        ph_ref, q_ref, k_ref, v_ref, ckv_ref, kr_ref = refs[12:]
    else:
        x_ref, mod_ref, wh_ref, wm_ref, qg_ref, kvg_ref, wq1_ref, wk_ref, wv_ref = refs[:9]
        ph_ref, q_ref, k_ref, v_ref, ckv_ref, kr_ref = refs[9:]
    scale = (MLA_DN + MLA_DR) ** -0.5 * LOG2_E
    mod = mod_ref[0]
    hmod = (x_ref[0] * (1.0 + mod[1:2]) + mod[0:1]).astype(BF16)
    ph_ref[0] = _dot(hmod, wh_ref[...])
    pm = _dot(hmod, wm_ref[...])
    kr_ref[0] = pm[:, MLA_Q_RANK + MLA_KV_RANK:MLA_Q_RANK + MLA_KV_RANK + LANES]
    cq = pm[:, :MLA_Q_RANK]
    cq = cq * lax.rsqrt(jnp.mean(cq * cq, axis=-1, keepdims=True) + RMS_EPS) * qg_ref[...]
    cq = cq.astype(BF16)
    ckv = pm[:, MLA_Q_RANK:MLA_Q_RANK + MLA_KV_RANK]
    ckv = ckv * lax.rsqrt(jnp.mean(ckv * ckv, axis=-1, keepdims=True) + RMS_EPS) * kvg_ref[...]
    ckv_ref[0] = ckv
    ckv = ckv.astype(BF16)
    kr0 = MLA_Q_RANK + MLA_KV_RANK
    kr = pm[:, kr0:kr0 + LANES]
    qa = _dot(cq, wq1_ref[...])
    kn = _dot(ckv, wk_ref[...])
    if key_major:
        v_ref[0] = _dot_nt(wv_ref[...], ckv).astype(BF16)
    else:
        v_ref[0] = _dot(ckv, wv_ref[...]).astype(BF16)
    if rope:
        cos = cos_ref[...]
        sin = sin_ref[...]
        qb = _dot(cq, wq2_ref[...])
        kr = kr * cos + pm[:, kr0 + LANES:kr0 + 2 * LANES] * sin
    for h in range(MLA_HEADS):
        sl = slice(h * HEAD_PAD, (h + 1) * HEAD_PAD)
        qh = qa[:, sl]
        if rope:
            qh = qh * cos + qb[:, sl] * sin
        q_ref[0, :, sl] = (qh * scale).astype(BF16)
        k_ref[0, :, sl] = (kn[:, sl] + kr).astype(BF16)


def _ab_inproj(x, mod, w_h, w_m, rope_tabs, qg, kvg, wq1, wq2, wk, wv):
    b, l, d = x.shape
    tm = TOK_TILE
    rope = rope_tabs is not None
    per_batch = mod.shape[0] > 1
    mod_map = (lambda i, j: (i, 0, 0)) if per_batch else (lambda i, j: (0, 0, 0))
    full = lambda a: pl.BlockSpec(a.shape, lambda i, j: (0,) * a.ndim)
    row = lambda width: pl.BlockSpec((1, tm, width), lambda i, j: (i, j, 0))
    hw = MLA_HEADS * HEAD_PAD
    inputs = [x, mod, w_h, w_m]
    in_specs = [row(d), pl.BlockSpec((1, 6, d), mod_map), full(w_h), full(w_m)]
    if rope:
        inputs += list(rope_tabs)
        in_specs += [pl.BlockSpec((tm, LANES), lambda i, j: (j, 0))] * 2
    ws = [qg, kvg, wq1] + ([wq2] if rope else []) + [wk, wv]
    inputs += ws
    in_specs += [full(a) for a in ws]
    key_major = wv.shape[1] == MLA_KV_RANK
    vw = MLA_HEADS * MLA_DV
    shapes = [(b, l, w_h.shape[1]), (b, l, hw), (b, l, hw), (b, vw, l) if key_major else (b, l, hw),
              (b, l, MLA_KV_RANK), (b, l, LANES)]
    dtypes = [F32, BF16, BF16, BF16, F32, F32]
    out_specs = [row(s[2]) for s in shapes]
    if key_major:
        out_specs[3] = pl.BlockSpec((1, vw, tm), lambda i, j: (i, 0, j))
    return pl.pallas_call(
        functools.partial(_ab_inproj_kernel, rope=rope, key_major=key_major),
        out_shape=[jax.ShapeDtypeStruct(s, dt) for s, dt in zip(shapes, dtypes)],
        grid=(b, l // tm),
        in_specs=in_specs,
        out_specs=out_specs,
        compiler_params=_cparams(("parallel", "parallel")),
        name="ab_inproj",
    )(*inputs)


def _mla_ctx_kernel(ckv_ref, kr_ref, place_ref, wk_ref, wvt_ref, k_ref, vt_ref):
    ckv = ckv_ref[0].astype(BF16)
    kr = _dot(kr_ref[0].astype(BF16), place_ref[...])
    kn = _dot(ckv, wk_ref[...])
    vt_ref[0] = _dot_nt(wvt_ref[...], ckv).astype(BF16)
    for h in range(MLA_HEADS):
        sl = slice(h * HEAD_PAD, (h + 1) * HEAD_PAD)
        k_ref[0, :, sl] = (kn[:, sl] + kr).astype(BF16)


def _mla_ctx(ctx_ckv, ctx_kr, wk, wvt):
    b, lc, _ = ctx_ckv.shape
    hw = MLA_HEADS * HEAD_PAD
    vw = MLA_HEADS * MLA_DV
    place = np.zeros((MLA_DR, LANES), np.float32)
    place[np.arange(MLA_DR), MLA_DN + np.arange(MLA_DR)] = 1.0
    place = jnp.asarray(place, BF16)
    full = lambda a: pl.BlockSpec(a.shape, lambda i: (0,) * a.ndim)
    return pl.pallas_call(
        _mla_ctx_kernel,
        out_shape=[jax.ShapeDtypeStruct((b, lc, hw), BF16), jax.ShapeDtypeStruct((b, vw, lc), BF16)],
        grid=(b,),
        in_specs=[pl.BlockSpec((1, lc, MLA_KV_RANK), lambda i: (i, 0, 0)),
                  pl.BlockSpec((1, lc, MLA_DR), lambda i: (i, 0, 0)),
                  full(place), full(wk), full(wvt)],
        out_specs=[pl.BlockSpec((1, lc, hw), lambda i: (i, 0, 0)), pl.BlockSpec((1, vw, lc), lambda i: (i, 0, 0))],
        compiler_params=_cparams(("parallel",)),
        name="mla_ctx_keys",
    )(ctx_ckv, ctx_kr, place, wk, wvt)


def _attn_long_kernel(q_ref, k_ref, vt_ref, kc_ref, vtc_ref, o_ref, s_a, s_b, m_a, m_b):
    i = pl.program_id(2)
    tq = q_ref.shape[1]
    lk = k_ref.shape[1]
    tk = ATTN_KEY_BLOCK
    n_blk = lk // tk
    slabs = [slice(hh * HEAD_PAD, (hh + 1) * HEAD_PAD) for hh in range(2)]
    vrows = [slice(hh * MLA_DV, (hh + 1) * MLA_DV) for hh in range(2)]
    q_t = [q_ref[0, :, sl].astype(F32).T.astype(BF16) for sl in slabs]
    crow = slice(lk, lk + kc_ref.shape[1])

    @pl.when(i == 0)
    def _():
        s_b[...] = jnp.zeros(s_b.shape, F32)
        m_b[...] = jnp.zeros(m_b.shape, F32)

    def run(s_cur, m_cur, s_prv, m_prv):
        def scores(hh, k_blk, rows, m_run):
            s = _dot(k_blk, q_t[hh])
            s_cur[hh, rows, :] = s
            return jnp.maximum(m_run, jnp.max(s.reshape(s.shape[0] // SUBLANES, SUBLANES, tq), axis=0))

        def weigh(hh, vt_blk, rows, m, acc):
            p = jnp.exp2(s_prv[hh, rows, :] - m).astype(BF16)
            lhs = jnp.concatenate([vt_blk, jnp.ones((2 * SUBLANES, vt_blk.shape[1]), BF16)], axis=0)
            return acc + _dot(lhs, p)

        ms = [jnp.max(m_prv[hh], axis=0, keepdims=True) for hh in range(2)]

        def step(j, carry):
            m_runs, accs = carry
            rows = pl.ds(pl.multiple_of(j * tk, tk), tk)
            m_runs = tuple(scores(hh, k_ref[0, rows, slabs[hh]], rows, m_runs[hh]) for hh in range(2))
            accs = tuple(weigh(hh, vt_ref[0, vrows[hh], rows], rows, ms[hh], accs[hh]) for hh in range(2))
            return m_runs, accs

        carry = ((jnp.full((SUBLANES, tq), -jnp.inf, F32),) * 2,
                 (jnp.zeros((MLA_DV + 2 * SUBLANES, tq), F32),) * 2)
        m_runs, accs = lax.fori_loop(0, n_blk, step, carry, unroll=2)
        m_runs = tuple(scores(hh, kc_ref[0, :, slabs[hh]], crow, m_runs[hh]) for hh in range(2))
        accs = tuple(weigh(hh, vtc_ref[0, vrows[hh], :], crow, ms[hh], accs[hh]) for hh in range(2))
        for hh in range(2):
            m_cur[hh] = m_runs[hh]
        outs = [acc[:MLA_DV] / acc[MLA_DV:MLA_DV + 1] for acc in accs]
        o_ref[0] = jnp.concatenate(outs, axis=0).T.astype(BF16)

    @pl.when(i % 2 == 0)
    def _():
        run(s_a, m_a, s_b, m_b)

    @pl.when(i % 2 == 1)
    def _():
        run(s_b, m_b, s_a, m_a)


def _attention_long(q, k, vt, kc, vtc):
    b, l, _ = q.shape
    lk = k.shape[1]
    lc = kc.shape[1]
    tq = ATTN_Q_TILE
    pw = 2 * HEAD_PAD
    vpw = 2 * MLA_DV
    assert lk % (2 * ATTN_KEY_BLOCK) == 0 and l % tq == 0
    n_q = l // tq
    return pl.pallas_call(
        _attn_long_kernel,
        out_shape=jax.ShapeDtypeStruct((b, l, MLA_HEADS * MLA_DV), BF16),
        grid=(b, MLA_HEADS // 2, n_q + 1),
        in_specs=[pl.BlockSpec((1, tq, pw), lambda bi, p, i: (bi, jnp.minimum(i, n_q - 1), p)),
                  pl.BlockSpec((1, lk, pw), lambda bi, p, i: (bi, 0, p)),
                  pl.BlockSpec((1, vpw, lk), lambda bi, p, i: (bi, p, 0)),
                  pl.BlockSpec((1, lc, pw), lambda bi, p, i: (bi, 0, p)),
                  pl.BlockSpec((1, vpw, lc), lambda bi, p, i: (bi, p, 0))],
        out_specs=pl.BlockSpec((1, tq, LANES), lambda bi, p, i: (bi, jnp.maximum(i - 1, 0), p)),
        scratch_shapes=[pltpu.VMEM((2, lk + lc, tq), F32), pltpu.VMEM((2, lk + lc, tq), F32),
                        pltpu.VMEM((2, SUBLANES, tq), F32), pltpu.VMEM((2, SUBLANES, tq), F32)],
        compiler_params=_cparams(("parallel", "parallel", "arbitrary")),
        name="mla_attention_long",
    )(q, k, vt, kc, vtc)


def _attn_kernel(*refs, has_ctx):
    if has_ctx:
        q_ref, k_ref, v_ref, kc_ref, vc_ref, o_ref = refs
    else:
        q_ref, k_ref, v_ref, o_ref = refs
    n_pairs = q_ref.shape[-1] // (2 * HEAD_PAD)
    for pair in range(n_pairs):
        out = None
        for hh in range(2):
            h0 = (2 * pair + hh) * HEAD_PAD
            sl = slice(h0, h0 + HEAD_PAD)
            q = q_ref[0, :, sl]
            s = _dot_nt(q, k_ref[0, :, sl])
            m = jnp.max(s, axis=-1, keepdims=True)
            if has_ctx:
                s2 = _dot_nt(q, kc_ref[0, :, sl])
                m = jnp.maximum(m, jnp.max(s2, axis=-1, keepdims=True))
            p = jnp.exp2(s - m)
            den = jnp.sum(p, axis=-1, keepdims=True)
            acc = _dot(p.astype(BF16), v_ref[0, :, sl])
            if has_ctx:
                p2 = jnp.exp2(s2 - m)
                den = den + jnp.sum(p2, axis=-1, keepdims=True)
                acc = acc + _dot(p2.astype(BF16), vc_ref[0, :, sl])
            o = acc / den
            out = o if out is None else out + o
        o_ref[0, :, pair * LANES:(pair + 1) * LANES] = out.astype(BF16)


def _attention(q, k, v, kc, vc):
    b, l, _ = q.shape
    lk = k.shape[1]
    tq = TOK_TILE
    pairs_per_step = MLA_HEADS // 2 if lk <= TOK_TILE else 1
    pw = 2 * HEAD_PAD * pairs_per_step
    has_ctx = kc is not None
    inputs = [q, k, v]
    in_specs = [pl.BlockSpec((1, tq, pw), lambda bi, p, i: (bi, i, p)),
                pl.BlockSpec((1, lk, pw), lambda bi, p, i: (bi, 0, p)),
                pl.BlockSpec((1, lk, pw), lambda bi, p, i: (bi, 0, p))]
    if has_ctx:
        lc = kc.shape[1]
        inputs += [kc, vc]
        in_specs += [pl.BlockSpec((1, lc, pw), lambda bi, p, i: (bi, 0, p))] * 2
    return pl.pallas_call(
        functools.partial(_attn_kernel, has_ctx=has_ctx),
        out_shape=jax.ShapeDtypeStruct((b, l, MLA_HEADS * MLA_DV), BF16),
        grid=(b, MLA_HEADS // 2 // pairs_per_step, l // tq),
        in_specs=in_specs,
        out_specs=pl.BlockSpec((1, tq, LANES * pairs_per_step), lambda bi, p, i: (bi, i, p)),
        compiler_params=_cparams(("parallel", "parallel", "arbitrary")),
        name="mla_attention",
    )(*inputs)


def _outproj_kernel(*refs, hgrn):
    if hgrn:
        x_ref, mod_ref, of_ref, ob_ref, gl_ref, gn_ref, b_ref, w_ref, lg_ref, lb_ref, o_ref = refs
        o = of_ref[0] + ob_ref[0]
        gate = _silu(gl_ref[0])
        gn = gn_ref[...]
        parts = []
        for h in range(HA_HEADS):
            sl = slice(h * HA_DV, (h + 1) * HA_DV)
            oh = o[:, sl]
            oh = oh * lax.rsqrt(jnp.mean(oh * oh, axis=-1, keepdims=True) + RMS_EPS) * gn
            parts.append((oh * gate[:, sl]).astype(BF16))
        a = jnp.concatenate(parts, axis=-1)
    else:
        x_ref, mod_ref, a_ref, b_ref, w_ref, lg_ref, lb_ref, o_ref = refs
        a = a_ref[0]
    half = a.shape[-1]
    y = _dot(a, w_ref[:half, :]) + _dot(b_ref[0], w_ref[half:, :])
    m = mod_ref[0]
    r = ALPHA * x_ref[0] + m[2:3] * y
    o_ref[0] = _layer_norm(r, lg_ref[...], lb_ref[...])


def _outproj(x, mod, a_inputs, b_in, w, ln_g, ln_b, hgrn):
    b, l, d = x.shape
    tm = TOK_TILE
    per_batch = mod.shape[0] > 1
    mod_map = (lambda i, j: (i, 0, 0)) if per_batch else (lambda i, j: (0, 0, 0))
    row = lambda width: pl.BlockSpec((1, tm, width), lambda i, j: (i, j, 0))
    full = lambda a: pl.BlockSpec(a.shape, lambda i, j: (0,) * a.ndim)
    inputs = [x, mod]
    in_specs = [row(d), pl.BlockSpec((1, 6, d), mod_map)]
    if hgrn:
        o_f, o_b, p_h, g_norm = a_inputs
        inputs += [o_f, o_b, p_h, g_norm]
        in_specs += [row(HA_W), row(HA_W),
                     pl.BlockSpec((1, tm, HA_W), lambda i, j: (i, j, 4)), full(g_norm)]
    else:
        inputs += [a_inputs]
        in_specs += [row(a_inputs.shape[-1])]
    inputs += [b_in, w, ln_g, ln_b]
    in_specs += [row(b_in.shape[-1]), full(w), full(ln_g), full(ln_b)]
    return pl.pallas_call(
        functools.partial(_outproj_kernel, hgrn=hgrn),
        out_shape=jax.ShapeDtypeStruct((b, l, d), F32),
        grid=(b, l // tm),
        in_specs=in_specs,
        out_specs=row(d),
        compiler_params=_cparams(("parallel", "parallel")),
        name="outproj_ln",
    )(*inputs)


def _conv_kernel(pm_ref, pp_ref, pn_ref, scw_ref, cfw_ref, cfb_ref, cfg_ref, cfbeta_ref,
                 ysc_ref, ycf_ref, ext_sc, ext_cf, *, lt):
    i = pl.program_id(1)
    n_i = pl.num_programs(1)
    w = SC_W

    def sc_in(p):
        return p[:, w:2 * w] * p[:, 2 * w:3 * w]

    def cf_in(p):
        return p[:, 3 * w:3 * w + CF_W] * _sigmoid(p[:, 3 * w + CF_W:3 * w + 2 * CF_W])

    pm = pm_ref[0]
    pp = pp_ref[0]
    pn = pn_ref[0]
    has_prev = i > 0
    has_next = i < n_i - 1
    n_ext = lt + 2 * HALO
    for ext, conv_in in ((ext_sc, sc_in), (ext_cf, cf_in)):
        ext[0, 0:HALO, :] = jnp.where(has_prev, conv_in(pp), 0.0)
        ext[0, HALO:HALO + lt, :] = conv_in(pm)
        ext[0, HALO + lt:, :] = jnp.where(has_next, conv_in(pn), 0.0)
    for s in range(1, SUBLANES):
        ext_cf[s, 0:n_ext - SUBLANES, :] = ext_cf[0, s:s + n_ext - SUBLANES, :]
    sc_shifts = sorted({(HALO - SC_K // 2 + j) % SUBLANES for j in range(SC_K)} - {0})
    for s in sc_shifts:
        ext_sc[s, 0:n_ext - SUBLANES, :] = ext_sc[0, s:s + n_ext - SUBLANES, :]

    def tap(ext, off, rows):
        s = off % SUBLANES
        return ext[s, off - s:off - s + rows, :]

    rb = 32
    for r in range(lt // rb):
        base = HALO + r * rb
        acc = None
        for j in range(SC_K):
            term = jnp.tile(scw_ref[j], (rb // SUBLANES, 1)) * tap(ext_sc, base - SC_K // 2 + j, rb)
            acc = term if acc is None else acc + term
        ysc_ref[0, r * rb:(r + 1) * rb, :] = (pm[r * rb:(r + 1) * rb, 0:w] * acc).astype(BF16)
        acc = None
        for j in range(CF_K):
            term = jnp.tile(cfw_ref[j], (rb // SUBLANES, 1)) * tap(ext_cf, base - CF_K // 2 + j, rb)
            acc = term if acc is None else acc + term
        u = _layer_norm(acc + cfb_ref[...], cfg_ref[...], cfbeta_ref[...])
        ycf_ref[0, r * rb:(r + 1) * rb, :] = _silu(u).astype(BF16)


def _conv_mixers(p1, sc_w, cf_w, cf_b, cf_g, cf_beta):
    b, l, width = p1.shape
    lt = TOK_TILE
    hb = lt // HALO
    n_h = l // HALO
    sc_w = jnp.broadcast_to(sc_w[:, None, :], (sc_w.shape[0], SUBLANES, sc_w.shape[1]))
    cf_w = jnp.broadcast_to(cf_w[:, None, :], (cf_w.shape[0], SUBLANES, cf_w.shape[1]))
    full = lambda a: pl.BlockSpec(a.shape, lambda bi, i: (0,) * a.ndim)
    return pl.pallas_call(
        functools.partial(_conv_kernel, lt=lt),
        out_shape=[jax.ShapeDtypeStruct((b, l, SC_W), BF16), jax.ShapeDtypeStruct((b, l, CF_W), BF16)],
        grid=(b, l // lt),
        in_specs=[
            pl.BlockSpec((1, lt, width), lambda bi, i: (bi, i, 0)),
            pl.BlockSpec((1, HALO, width), lambda bi, i: (bi, jnp.maximum(i * hb - 1, 0), 0)),
            pl.BlockSpec((1, HALO, width), lambda bi, i: (bi, jnp.minimum((i + 1) * hb, n_h - 1), 0)),
            full(sc_w), full(cf_w), full(cf_b), full(cf_g), full(cf_beta),
        ],
        out_specs=[pl.BlockSpec((1, lt, SC_W), lambda bi, i: (bi, i, 0)),
                   pl.BlockSpec((1, lt, CF_W), lambda bi, i: (bi, i, 0))],
        scratch_shapes=[pltpu.VMEM((SUBLANES, lt + 2 * HALO, SC_W), F32),
                        pltpu.VMEM((SUBLANES, lt + 2 * HALO, CF_W), F32)],
        compiler_params=_cparams(("parallel", "parallel")),
        name="conv_mixers",
    )(p1, p1, p1, sc_w, cf_w, cf_b, cf_g, cf_beta)


def _route_t(lt):
    t = lt.shape[1]
    row = lax.broadcasted_iota(jnp.int32, (SUBLANES, t), 0).astype(F32)
    neg = -jnp.inf
    big = float(LANES)
    gl = jnp.where(row < N_GROUPS, lt[N_EXPERTS:N_EXPERTS + SUBLANES], neg)
    gmax = jnp.max(gl, axis=0, keepdims=True)
    p_g = 1.0 / jnp.sum(jnp.exp(gl - gmax), axis=0, keepdims=True)
    g_sel = jnp.min(jnp.where(gl == gmax, row, big), axis=0, keepdims=True)
    el = lt[0:EXP_PER_GROUP]
    for gi in range(1, N_GROUPS):
        el = jnp.where(g_sel == gi, lt[gi * EXP_PER_GROUP:(gi + 1) * EXP_PER_GROUP], el)
    v1 = jnp.max(el, axis=0, keepdims=True)
    i1 = jnp.min(jnp.where(el == v1, row, big), axis=0, keepdims=True)
    el2 = jnp.where(row == i1, neg, el)
    v2 = jnp.max(el2, axis=0, keepdims=True)
    i2 = jnp.min(jnp.where(el2 == v2, row, big), axis=0, keepdims=True)
    e2 = jnp.exp(v2 - v1)
    w1 = p_g / (1.0 + e2)
    w2 = p_g * e2 / (1.0 + e2)
    comb = jnp.where(row == i1, w1, 0.0) + jnp.where(row == i2, w2, 0.0)
    onehot = jnp.where(row == g_sel, 1.0, 0.0)
    return onehot, comb


def _moe_kernel(x_ref, mod_ref, wr_ref, tri_ref, wg_ref, wu_ref, wd_ref, lg_ref, lb_ref, o_ref,
                hb_scr, os_scr, pt_scr, cc_scr):
    m = mod_ref[0]
    tm, cap = pt_scr.shape
    d = x_ref.shape[-1]
    half = d // 2
    cw = EXP_PER_GROUP * EXP_FF

    h = x_ref[0] * (1.0 + m[4:5]) + m[3:4]
    h_hi = h.astype(BF16)
    hb_scr[...] = h_hi
    h_lo = (h - h_hi.astype(F32)).astype(BF16)
    l2 = _dot(h_hi, wr_ref[...])
    logits = l2[:, :LANES] + l2[:, LANES:] + _dot(h_lo, wr_ref[:, :LANES])
    onehot, comb = _route_t(logits.T)
    rank = _dot(onehot.astype(BF16), tri_ref[...])
    cnt = jnp.sum(onehot, axis=1, keepdims=True)
    n_blocks = jnp.floor((cnt + (MOE_BLOCK - 0.5)) * (1.0 / MOE_BLOCK))
    padded = n_blocks * MOE_BLOCK
    start = jnp.zeros((1, 1), F32)
    first_block = jnp.zeros((1, 1), F32)
    dest = jnp.zeros((1, tm), F32)
    first, count = [], []
    for gi in range(N_GROUPS):
        dest = dest + onehot[gi:gi + 1] * (start + rank[gi:gi + 1])
        first.append(first_block[0, 0].astype(jnp.int32))
        count.append(n_blocks[gi, 0].astype(jnp.int32))
        start = start + padded[gi:gi + 1]
        first_block = first_block + n_blocks[gi:gi + 1]
    aux = jnp.concatenate([jnp.broadcast_to(dest, (SUBLANES, tm)), comb,
                           jnp.zeros((LANES - 2 * SUBLANES, tm), F32)], axis=0).T
    lane = lax.broadcasted_iota(jnp.int32, (tm, LANES), 1)
    comb_tok = jnp.where((lane >= SUBLANES) & (lane < 2 * SUBLANES), aux, 0.0)
    c_hi = comb_tok.astype(BF16)
    c_mid = (comb_tok - c_hi.astype(F32)).astype(BF16)
    slot_t = lax.broadcasted_iota(jnp.int32, (tm, cap), 1).astype(F32)
    pt_scr[...] = jnp.where(slot_t == aux[:, 0:1], 1.0, 0.0).astype(BF16)
    cc_scr[...] = jnp.concatenate([c_hi, c_mid], axis=1)
    os_scr[...] = jnp.zeros_like(os_scr)
    block_slot = lax.broadcasted_iota(jnp.int32, (MOE_BLOCK, tm), 0).astype(F32)

    for gi in range(N_GROUPS):
        cols = slice(gi * cw, (gi + 1) * cw)

        def block(i, carry, gi=gi, cols=cols):
            blk = first[gi] + i
            r0 = pl.multiple_of(blk * MOE_BLOCK, 2 * SUBLANES)
            rows = pl.ds(r0, MOE_BLOCK)
            perm = jnp.where(block_slot + (blk * MOE_BLOCK).astype(F32) == dest, 1.0, 0.0).astype(BF16)
            hs = _dot(perm, hb_scr[...]).astype(BF16)
            cs2 = _dot(perm, cc_scr[...])
            cs = cs2[:, :LANES] + cs2[:, LANES:]
            hid = _silu(_dot(hs, wg_ref[:, cols])) * _dot(hs, wu_ref[:, cols])
            hid = jnp.concatenate(
                [hid[:, e * EXP_FF:(e + 1) * EXP_FF] * cs[:, SUBLANES + e:SUBLANES + e + 1]
                 for e in range(EXP_PER_GROUP)], axis=1)
            os_scr[rows, :] = _dot(hid.astype(BF16), wd_ref[cols, :]).astype(BF16)
            return carry

        lax.fori_loop(0, count[gi], block, 0)

    y = jnp.concatenate([_dot(pt_scr[...], os_scr[:, :half]), _dot(pt_scr[...], os_scr[:, half:])], axis=1)
    r = ALPHA * x_ref[0] + m[5:6] * y
    o_ref[0] = _layer_norm(r, lg_ref[...], lb_ref[...])


def _moe(x, mod, wr, wg, wu, wd, ln_g, ln_b):
    b, l, d = x.shape
    tm = MOE_TILE
    cap = (tm + N_GROUPS * (MOE_BLOCK - 1)) // MOE_BLOCK * MOE_BLOCK
    cap = -(-cap // LANES) * LANES
    cw = EXP_PER_GROUP * EXP_FF
    per_batch = mod.shape[0] > 1
    mod_map = (lambda i, t: (i, 0, 0)) if per_batch else (lambda i, t: (0, 0, 0))
    tri = jnp.asarray(np.triu(np.ones((tm, tm), np.float32), k=1), BF16)
    full = lambda a: pl.BlockSpec(a.shape, lambda i, t: (0,) * a.ndim, pipeline_mode=pl.Buffered(1))
    return pl.pallas_call(
        _moe_kernel,
        out_shape=jax.ShapeDtypeStruct((b, l, d), F32),
        grid=(b, l // tm),
        in_specs=[
            pl.BlockSpec((1, tm, d), lambda i, t: (i, t, 0)),
            pl.BlockSpec((1, 6, d), mod_map),
            full(wr), full(tri), full(wg), full(wu), full(wd), full(ln_g), full(ln_b),
        ],
        out_specs=pl.BlockSpec((1, tm, d), lambda i, t: (i, t, 0)),
        scratch_shapes=[pltpu.VMEM((tm, d), BF16), pltpu.VMEM((cap, d), BF16),
                        pltpu.VMEM((tm, cap), BF16), pltpu.VMEM((tm, 2 * LANES), BF16)],
        compiler_params=_cparams(("arbitrary", "arbitrary")),
        name="hier_moe_ln",
    )(x, mod, wr, tri, wg, wu, wd, ln_g, ln_b)


def _rope_swap_perm():
    idx = np.arange(MLA_DR)
    return idx ^ (MLA_DR // 4)


def _rope_tables(n_tok):
    rows = n_tok // GRID_W
    pos_r = jnp.repeat(jnp.arange(rows, dtype=F32), GRID_W)
    pos_c = (jnp.arange(rows * GRID_W) % GRID_W).astype(F32)
    n_freq = MLA_DR // 4
    inv = ROPE_BASE ** (-jnp.arange(n_freq, dtype=F32) / n_freq)
    ang = jnp.stack([pos_r[:, None] * inv, pos_c[:, None] * inv], axis=1)
    cos, sin = jnp.cos(ang), jnp.sin(ang)
    cos32 = jnp.stack([cos, cos], axis=2).reshape(n_tok, MLA_DR)
    sin32 = jnp.stack([-sin, sin], axis=2).reshape(n_tok, MLA_DR)
    pad_hi = LANES - MLA_DN - MLA_DR
    cos_t = jnp.concatenate([jnp.ones((n_tok, MLA_DN), F32), cos32, jnp.zeros((n_tok, pad_hi), F32)], axis=1)
    sin_t = jnp.concatenate([jnp.zeros((n_tok, MLA_DN), F32), sin32, jnp.zeros((n_tok, pad_hi), F32)], axis=1)
    return cos_t, sin_t


def _ab_weights(w_in, w_uq, w_ukv):
    perm = _rope_swap_perm()
    pad_hi = LANES - MLA_DN - MLA_DR
    w_h = w_in[:, :5 * HA_W]
    cq = w_in[:, 5 * HA_W:5 * HA_W + MLA_Q_RANK]
    ckv = w_in[:, 5 * HA_W + MLA_Q_RANK:5 * HA_W + MLA_Q_RANK + MLA_KV_RANK]
    kr = w_in[:, 5 * HA_W + MLA_Q_RANK + MLA_KV_RANK:]
    d = w_in.shape[0]
    z_lo = jnp.zeros((d, MLA_DN), F32)
    z_hi = jnp.zeros((d, pad_hi), F32)
    w_m = jnp.concatenate([cq, ckv, z_lo, kr, z_hi, z_lo, kr[:, perm], z_hi], axis=1)
    uq = w_uq.reshape(MLA_Q_RANK, MLA_HEADS, MLA_DN + MLA_DR)
    q_nope, q_rope = uq[..., :MLA_DN], uq[..., MLA_DN:]
    zq_hi = jnp.zeros((MLA_Q_RANK, MLA_HEADS, pad_hi), F32)
    wq1 = jnp.concatenate([q_nope, q_rope, zq_hi], axis=-1).reshape(MLA_Q_RANK, -1)
    wq2 = jnp.concatenate([jnp.zeros_like(q_nope), q_rope[..., perm], zq_hi], axis=-1).reshape(MLA_Q_RANK, -1)
    ukv = w_ukv.reshape(MLA_KV_RANK, MLA_HEADS, MLA_DN + MLA_DV)
    k_nope, v = ukv[..., :MLA_DN], ukv[..., MLA_DN:]
    wk = jnp.concatenate([k_nope, jnp.zeros((MLA_KV_RANK, MLA_HEADS, LANES - MLA_DN), F32)], axis=-1)
    wk = wk.reshape(MLA_KV_RANK, -1)
    v_pairs = v.reshape(MLA_KV_RANK, MLA_HEADS // 2, 2, MLA_DV)
    zv = jnp.zeros_like(v_pairs[:, :, 0])
    wv = jnp.stack([jnp.concatenate([v_pairs[:, :, 0], zv], axis=-1),
                    jnp.concatenate([zv, v_pairs[:, :, 1]], axis=-1)], axis=2).reshape(MLA_KV_RANK, -1)
    wvt = v.reshape(MLA_KV_RANK, MLA_HEADS * MLA_DV).T
    bf = lambda a: a.astype(BF16)
    return bf(w_h), bf(w_m), bf(wq1), bf(wq2), bf(wk), bf(wv), bf(wvt)


def _moe_weights(w_group, w_expert, w_gate, w_up, w_down):
    d = w_group.shape[0]
    wr = jnp.concatenate([w_expert, w_group, jnp.zeros((d, LANES - N_EXPERTS - N_GROUPS), F32)], axis=1)
    wr_hi = wr.astype(BF16)
    wr = jnp.concatenate([wr_hi, (wr - wr_hi.astype(F32)).astype(BF16)], axis=1)
    wg = jnp.transpose(w_gate, (1, 0, 2)).reshape(d, N_EXPERTS * EXP_FF).astype(BF16)
    wu = jnp.transpose(w_up, (1, 0, 2)).reshape(d, N_EXPERTS * EXP_FF).astype(BF16)
    wd = w_down.reshape(N_EXPERTS * EXP_FF, d).astype(BF16)
    return wr, wg, wu, wd


def kernel(x_prompt, x_sample, state_hgrn_fwd, state_hgrn_bwd, cache_mla_ckv, cache_mla_krope, c, c_ctx, mod_w, mod_b, ln_g, ln_b, ab_w_in, ab_w_out, hgrn_lb_logits, hgrn_norm_g, mla_q_norm_g, mla_w_uq, mla_kv_norm_g, mla_w_ukv, cd_w_in, cd_w_out, sc_conv_w, cf_conv_w, cf_conv_b, cf_ln_g, cf_ln_b, moe_w_group, moe_w_expert, moe_w_gate, moe_w_up, moe_w_down):
    dec_b = x_sample.shape[0]
    d = D_MODEL
    cvec = jnp.concatenate([c, c_ctx[None, :], jnp.zeros((SUBLANES - dec_b - 1, d), F32)], axis=0)
    mods = _mod_vectors(cvec, mod_w, mod_b)
    rope_tabs = _rope_tables(x_sample.shape[1])
    xp, xs = x_prompt, x_sample
    new_sf = new_sb = new_ckv = new_kr = None
    for l in range(DEPTH):
        mod_lat = mods[l, :dec_b].reshape(dec_b, 6, d)
        mod_ctx = mods[l, dec_b:dec_b + 1].reshape(1, 6, d)
        row = lambda a: a.reshape(1, -1)
        if l % 2 == 0:
            e = l // 2
            w_h, w_m, wq1, wq2, wk, wv, wvt = _ab_weights(ab_w_in[e], mla_w_uq[e], mla_w_ukv[e])
            w_out = ab_w_out[e].astype(BF16)
            qg, kvg, gn = row(mla_q_norm_g[e]), row(mla_kv_norm_g[e]), row(hgrn_norm_g[e])
            ph_p, q_p, k_p, v_p, ckv_p, kr_p = _ab_inproj(xp, mod_ctx, w_h, w_m, None, qg, kvg, wq1, None, wk, wv)
            of_p, ob_p, sf, sb = _hgrn(ph_p, hgrn_lb_logits, None, None, e)
            om_p = _attention(q_p, k_p, v_p, None, None)
            xp = _outproj(xp, mod_ctx, (of_p, ob_p, ph_p, gn), om_p, w_out, row(ln_g[l, 0]), row(ln_b[l, 0]), True)
            ph_s, q_s, k_s, vt_s, _, _ = _ab_inproj(xs, mod_lat, w_h, w_m, rope_tabs, qg, kvg, wq1, wq2, wk, wvt)
            of_s, ob_s, _, _ = _hgrn(ph_s, hgrn_lb_logits, state_hgrn_fwd[:, e], state_hgrn_bwd[:, e], e)
            kc, vtc = _mla_ctx(cache_mla_ckv[:, e], cache_mla_krope[:, e], wk, wvt)
            om_s = _attention_long(q_s, k_s, vt_s, kc, vtc)
            xs = _outproj(xs, mod_lat, (of_s, ob_s, ph_s, gn), om_s, w_out, row(ln_g[l, 0]), row(ln_b[l, 0]), True)
            new_sf, new_sb, new_ckv = sf, sb, ckv_p
            new_kr = kr_p[:, :, MLA_DN:MLA_DN + MLA_DR]
        else:
            jx = l // 2
            w1 = cd_w_in[jx].astype(BF16)
            w_out = cd_w_out[jx].astype(BF16)
            cd = (sc_conv_w[jx], cf_conv_w[jx], row(cf_conv_b[jx]), row(cf_ln_g[jx]), row(cf_ln_b[jx]))
            (p1_p,) = _inproj(xp, mod_ctx, [w1])
            ysc_p, ycf_p = _conv_mixers(p1_p, *cd)
            xp = _outproj(xp, mod_ctx, ysc_p, ycf_p, w_out, row(ln_g[l, 0]), row(ln_b[l, 0]), False)
            (p1_s,) = _inproj(xs, mod_lat, [w1])
            ysc_s, ycf_s = _conv_mixers(p1_s, *cd)
            xs = _outproj(xs, mod_lat, ysc_s, ycf_s, w_out, row(ln_g[l, 0]), row(ln_b[l, 0]), False)
        wr, wg, wu, wd = _moe_weights(moe_w_group[l], moe_w_expert[l], moe_w_gate[l], moe_w_up[l], moe_w_down[l])
        xp_t = xp.reshape(-1, MOE_TILE, d)
        xp = _moe(xp_t, mod_ctx, wr, wg, wu, wd, row(ln_g[l, 1]), row(ln_b[l, 1])).reshape(xp.shape)
        xs = _moe(xs, mod_lat, wr, wg, wu, wd, row(ln_g[l, 1]), row(ln_b[l, 1]))
    return (xp, xs, new_sf[:, None], new_sb[:, None], new_ckv[:, None], new_kr[:, None])
```

```python
import functools

import numpy as np
import jax
import jax.numpy as jnp
from jax import lax
from jax.experimental import pallas as pl
from jax.experimental.pallas import tpu as pltpu

F32 = jnp.float32
BF16 = jnp.bfloat16
HIGHEST = lax.Precision.HIGHEST

D_MODEL = 1024
DEPTH = 2
GRID_W = 64
N_EVEN = (DEPTH + 1) // 2
HA_HEADS = 4
HA_DK = 128
HA_DV = 128
HA_W = HA_HEADS * HA_DK
CHUNK = 32
MLA_HEADS = 8
MLA_DN = 64
MLA_DR = 32
MLA_DV = 64
MLA_Q_RANK = 384
MLA_KV_RANK = 256
ROPE_BASE = 10000.0
SC_W = 512
SC_K = 3
CF_W = 512
CF_K = 31
N_GROUPS = 4
EXP_PER_GROUP = 8
N_EXPERTS = N_GROUPS * EXP_PER_GROUP
EXP_FF = 128
ALPHA = (2.0 * DEPTH) ** 0.25
LOG2_E = 1.4426950408889634
LN_EPS = 1e-5
RMS_EPS = 1e-6

LANES = 128
SUBLANES = 8
VMEM_LIMIT = 56 * 1024 * 1024

HEAD_PAD = LANES
MLA_SLAB = MLA_Q_RANK + MLA_KV_RANK + 2 * LANES
PROJ_TILE = 512
TOK_TILE = 256
GROUP_ROWS = 128
N_LEVELS = 5
SAFE_LOG_DECAY = 60.0
ATTN_KEY_BLOCK = 512
ATTN_Q_TILE = 256
MOE_TILE = 512
MOE_BLOCK = 160
HALO = 16


def _cparams(sem):
    return pltpu.CompilerParams(dimension_semantics=sem, vmem_limit_bytes=VMEM_LIMIT)


def _sigmoid(x):
    return 0.5 * jnp.tanh(0.5 * x) + 0.5


def _silu(x):
    half = 0.5 * x
    return half * jnp.tanh(half) + half


def _layer_norm(r, g, b):
    mu = jnp.mean(r, axis=-1, keepdims=True)
    d = r - mu
    var = jnp.mean(d * d, axis=-1, keepdims=True)
    return d * lax.rsqrt(var + LN_EPS) * g + b


def _dot(a, b):
    return jnp.dot(a, b, preferred_element_type=F32)


def _dot_nt(a, b):
    return lax.dot_general(a, b, (((1,), (1,)), ((), ())), preferred_element_type=F32)


def _dot_tn(a, b):
    return lax.dot_general(a, b, (((0,), (0,)), ((), ())), preferred_element_type=F32)


def _mod_kernel(c_ref, w_ref, b_ref, o_ref):
    s = _silu(c_ref[...])
    o_ref[0] = jnp.dot(s, w_ref[0], precision=HIGHEST, preferred_element_type=F32) + b_ref[0]


def _mod_vectors(cvec, mod_w, mod_b):
    n_out = mod_w.shape[-1]
    tn = 1536
    return pl.pallas_call(
        _mod_kernel,
        out_shape=jax.ShapeDtypeStruct((DEPTH, SUBLANES, n_out), F32),
        grid=(DEPTH, n_out // tn),
        in_specs=[
            pl.BlockSpec((SUBLANES, D_MODEL), lambda l, j: (0, 0)),
            pl.BlockSpec((1, D_MODEL, tn), lambda l, j: (l, 0, j)),
            pl.BlockSpec((1, 1, tn), lambda l, j: (l, 0, j)),
        ],
        out_specs=pl.BlockSpec((1, SUBLANES, tn), lambda l, j: (l, 0, j)),
        compiler_params=_cparams(("arbitrary", "arbitrary")),
        name="mod_vectors",
    )(cvec, mod_w, mod_b.reshape(DEPTH, 1, n_out))


def _inproj_kernel(*refs, n_w):
    x_ref, mod_ref = refs[0], refs[1]
    w_refs = refs[2:2 + n_w]
    o_refs = refs[2 + n_w:]
    m = mod_ref[0]
    h = (x_ref[0] * (1.0 + m[1:2]) + m[0:1]).astype(BF16)
    for w_ref, o_ref in zip(w_refs, o_refs):
        o_ref[0] = _dot(h, w_ref[...])


def _inproj(x, mod, weights):
    b, l, d = x.shape
    tm = PROJ_TILE
    per_batch = mod.shape[0] > 1
    mod_map = (lambda i, j: (i, 0, 0)) if per_batch else (lambda i, j: (0, 0, 0))
    in_specs = [pl.BlockSpec((1, tm, d), lambda i, j: (i, j, 0)),
                pl.BlockSpec((1, 6, d), mod_map)]
    in_specs += [pl.BlockSpec(w.shape, lambda i, j: (0, 0), pipeline_mode=pl.Buffered(1)) for w in weights]
    out_shape = [jax.ShapeDtypeStruct((b, l, w.shape[1]), F32) for w in weights]
    out_specs = [pl.BlockSpec((1, tm, w.shape[1]), lambda i, j: (i, j, 0)) for w in weights]
    return pl.pallas_call(
        functools.partial(_inproj_kernel, n_w=len(weights)),
        out_shape=out_shape,
        grid=(b, l // tm),
        in_specs=in_specs,
        out_specs=out_specs,
        compiler_params=_cparams(("parallel", "parallel")),
        name="modulate_inproj",
    )(x, mod, *weights)


def _hgrn_tables():
    n = GROUP_ROWS
    t = np.arange(n)[:, None]
    j = np.arange(n)[None, :]
    same_chunk = (t // CHUNK) == (j // CHUNK)
    e_f, e_b = [], []
    lv_f = np.full((n, n), -1, np.int32)
    for lvl in range(N_LEVELS):
        m = CHUNK >> (lvl + 1)
        blk0 = (t // (2 * m)) * (2 * m)
        r = blk0 + m - 1
        upper = t > r
        ef = np.where(upper, (j > r) & (j <= t), (j > t) & (j <= r))
        r2 = blk0 + m
        lower = t < r2
        eb = np.where(lower, (j >= t) & (j < r2), (j >= r2) & (j < t))
        e_f.append(ef)
        e_b.append(eb)
        s = np.arange(n)[None, :]
        same_blk = (t // (2 * m)) == (s // (2 * m))
        q_side = (t % (2 * m)) >= m
        k_side = (s % (2 * m)) < m
        lv_f[same_blk & q_side & k_side] = lvl
    lv_f[np.arange(n), np.arange(n)] = N_LEVELS
    e_f = np.concatenate(e_f, axis=0).astype(np.float32)
    e_b = np.concatenate(e_b, axis=0).astype(np.float32)
    e = np.stack([e_f, e_b])
    cum = np.stack([same_chunk & (j <= t), same_chunk & (j >= t)]).astype(np.float32)
    lv = np.stack([lv_f, lv_f.T])
    return e, cum, lv


def _hgrn_kernel(qf_ref, vf_ref, ff_ref, qb_ref, vb_ref, fb_ref, lbl_ref, e_ref, cum_ref, lv_ref, s0f_ref, s0b_ref,
                 of_ref, ob_ref, sf_ref, sb_ref,
                 st_scr, qe_scr, kd_scr, sc_scr, v_scr, oi_scr, dec_scr, *, lt, slot, has_state):
    i = pl.program_id(1)
    n_i = pl.num_programs(1)
    n_chunks = lt // CHUNK
    n_groups = lt // GROUP_ROWS
    chunks_per_group = GROUP_ROWS // CHUNK

    @pl.when(i == 0)
    def _():
        for d, s0_ref in enumerate((s0f_ref, s0b_ref)):
            for h in range(HA_HEADS):
                if has_state:
                    st_scr[d, h] = s0_ref[0, h].T
                else:
                    st_scr[d, h] = jnp.zeros((HA_DV, HA_DK), F32)

    lg = lbl_ref[...]
    n_slots = lg.shape[0]
    mx = lg[0]
    for s in range(1, n_slots):
        mx = jnp.maximum(mx, lg[s])
    ex = [jnp.exp(lg[s] - mx) for s in range(n_slots)]
    den = ex[0]
    for s in range(1, n_slots):
        den = den + ex[s]
    num = ex[0]
    for s in range(1, slot + 1):
        num = num + ex[s]
    lb_all = num / den

    dirs = ((qf_ref, vf_ref, ff_ref), (qb_ref, vb_ref, fb_ref))
    head_cols = [slice(h * HA_DK, (h + 1) * HA_DK) for h in range(HA_HEADS)]

    def gates(d, rows, cols):
        q_ref, _, f_ref = dirs[d]
        q = _silu(q_ref[0, rows, cols])
        lb_h = lb_all[d:d + 1, cols]
        f = lb_h + (1.0 - lb_h) * _sigmoid(f_ref[0, rows, cols])
        g = jnp.log(f)
        g_hi = g.astype(BF16)
        g_lo = (g - g_hi.astype(F32)).astype(BF16)
        return q, 1.0 - f, g_hi, g_lo

    def group_step(grp, tot_min):
        r0 = pl.multiple_of(grp * GROUP_ROWS, GROUP_ROWS)
        rows = pl.ds(r0, GROUP_ROWS)
        for d in range(2):
            cum = cum_ref[d]
            lv = lv_ref[d]
            for cols in head_cols:
                q, k, g_hi, g_lo = gates(d, rows, cols)
                bcum = _dot(cum, g_hi) + _dot(cum, g_lo)
                v_scr[d, rows, cols] = dirs[d][1][0, rows, cols].astype(BF16)
                qe = (q * jnp.exp(bcum)).astype(BF16)
                qe_scr[d, rows, cols] = qe
                kx = []
                for cc in range(chunks_per_group):
                    edge = cc * CHUNK + (CHUNK - 1 if d == 0 else 0)
                    tot = bcum[edge:edge + 1]
                    b_c = bcum[cc * CHUNK:(cc + 1) * CHUNK]
                    kd = k[cc * CHUNK:(cc + 1) * CHUNK] * jnp.exp(tot - b_c)
                    kd_scr[d, pl.ds(r0 + cc * CHUNK, CHUNK), cols] = kd.astype(BF16)
                    dec_scr[d, grp * chunks_per_group + cc, :, cols] = jnp.exp(tot)
                    kx.append(kd * jnp.exp(-tot))
                    tot_min = jnp.minimum(tot_min, tot)
                p = _dot_nt(qe, jnp.concatenate(kx, axis=0).astype(BF16))
                sc_scr[d, rows, cols] = jnp.where(lv >= 0, p, 0.0).astype(BF16)
        return tot_min

    tot_min = lax.fori_loop(0, n_groups, group_step, jnp.zeros((1, HA_DK), F32), unroll=True)

    def one_factor():
        for d in range(2):
            for grp in range(n_groups):
                rows = slice(grp * GROUP_ROWS, (grp + 1) * GROUP_ROWS)
                for cols in head_cols:
                    oi_scr[d, rows, cols] = _dot(sc_scr[d, rows, cols], v_scr[d, rows, cols])

    def per_level():
        for d in range(2):

            def group_step(grp, carry, d=d):
                rows = pl.ds(pl.multiple_of(grp * GROUP_ROWS, GROUP_ROWS), GROUP_ROWS)
                lv = lv_ref[d]
                for cols in head_cols:
                    q, k, g_hi, g_lo = gates(d, rows, cols)
                    x = jnp.exp(_dot(e_ref[d], g_hi) + _dot(e_ref[d], g_lo))
                    sc = jnp.where(lv == N_LEVELS, _dot_nt(q.astype(BF16), k.astype(BF16)), 0.0)
                    for lvl in range(N_LEVELS):
                        xl = x[lvl * GROUP_ROWS:(lvl + 1) * GROUP_ROWS]
                        p = _dot_nt((q * xl).astype(BF16), (k * xl).astype(BF16))
                        sc = jnp.where(lv == lvl, p, sc)
                    oi_scr[d, rows, cols] = _dot(sc.astype(BF16), v_scr[d, rows, cols])
                return carry

            lax.fori_loop(0, n_groups, group_step, 0)

    lax.cond(jnp.min(tot_min) > -SAFE_LOG_DECAY, one_factor, per_level)

    out_refs = (of_ref, ob_ref)

    def chunk_step(c, carry):
        for d in range(2):
            cc = c if d == 0 else n_chunks - 1 - c
            r0 = pl.multiple_of(cc * CHUNK, CHUNK)
            rows = pl.ds(r0, CHUNK)
            for h in range(HA_HEADS):
                cols = slice(h * HA_DK, (h + 1) * HA_DK)
                st = st_scr[d, h]
                o_state = _dot_nt(qe_scr[d, rows, cols], st.astype(BF16))
                out_refs[d][0, rows, cols] = oi_scr[d, rows, cols] + o_state
                upd = _dot_tn(v_scr[d, rows, cols], kd_scr[d, rows, cols])
                st_scr[d, h] = st * dec_scr[d, cc, :, cols] + upd
        return carry

    lax.fori_loop(0, n_chunks, chunk_step, 0, unroll=True)

    @pl.when(i == n_i - 1)
    def _():
        for d, s_ref in enumerate((sf_ref, sb_ref)):
            for h in range(HA_HEADS):
                s_ref[0, h] = st_scr[d, h].T


def _hgrn(p_h, lb_logits, s0_f, s0_b, slot):
    b, l, _ = p_h.shape
    lt = TOK_TILE
    n_t = l // lt
    has_state = s0_f is not None
    if not has_state:
        s0_f = jnp.zeros((1, HA_HEADS, HA_DK, HA_DV), F32)
        s0_b = s0_f
    e_np, cum_np, lv_np = _hgrn_tables()
    e_mat = jnp.asarray(e_np, BF16)
    cum = jnp.asarray(cum_np, BF16)
    lv = jnp.asarray(lv_np, jnp.int32)
    w = HA_W

    def sec(idx, rev):
        if rev:
            return pl.BlockSpec((1, lt, w), lambda bi, i: (bi, n_t - 1 - i, idx))
        return pl.BlockSpec((1, lt, w), lambda bi, i: (bi, i, idx))

    state_map = (lambda bi, i: (bi, 0, 0, 0)) if has_state else (lambda bi, i: (0, 0, 0, 0))
    in_specs = [
        sec(0, False), sec(3, False), sec(1, False),
        sec(0, True), sec(3, True), sec(2, True),
        pl.BlockSpec(lb_logits.shape, lambda bi, i: (0, 0, 0)),
        pl.BlockSpec(e_mat.shape, lambda bi, i: (0, 0, 0)),
        pl.BlockSpec(cum.shape, lambda bi, i: (0, 0, 0)),
        pl.BlockSpec(lv.shape, lambda bi, i: (0, 0, 0)),
        pl.BlockSpec((1, HA_HEADS, HA_DK, HA_DV), state_map),
        pl.BlockSpec((1, HA_HEADS, HA_DK, HA_DV), state_map),
    ]
    out_shape = [
        jax.ShapeDtypeStruct((b, l, w), F32),
        jax.ShapeDtypeStruct((b, l, w), F32),
        jax.ShapeDtypeStruct((b, HA_HEADS, HA_DK, HA_DV), F32),
        jax.ShapeDtypeStruct((b, HA_HEADS, HA_DK, HA_DV), F32),
    ]
    out_specs = [
        pl.BlockSpec((1, lt, w), lambda bi, i: (bi, i, 0)),
        pl.BlockSpec((1, lt, w), lambda bi, i: (bi, n_t - 1 - i, 0)),
        pl.BlockSpec((1, HA_HEADS, HA_DK, HA_DV), lambda bi, i: (bi, 0, 0, 0)),
        pl.BlockSpec((1, HA_HEADS, HA_DK, HA_DV), lambda bi, i: (bi, 0, 0, 0)),
    ]
    scratch = [
        pltpu.VMEM((2, HA_HEADS, HA_DV, HA_DK), F32),
        pltpu.VMEM((2, lt, w), BF16),
        pltpu.VMEM((2, lt, w), BF16),
        pltpu.VMEM((2, lt, w), BF16),
        pltpu.VMEM((2, lt, w), BF16),
        pltpu.VMEM((2, lt, w), F32),
        pltpu.VMEM((2, lt // CHUNK, 1, w), F32),
    ]
    return pl.pallas_call(
        functools.partial(_hgrn_kernel, lt=lt, slot=slot, has_state=has_state),
        out_shape=out_shape,
        grid=(b, n_t),
        in_specs=in_specs,
        out_specs=out_specs,
        scratch_shapes=scratch,
        compiler_params=_cparams(("parallel", "arbitrary")),
        name="hgrn2_scan",
    )(p_h, p_h, p_h, p_h, p_h, p_h, lb_logits, e_mat, cum, lv, s0_f, s0_b)


def _ab_inproj_kernel(*refs, rope, key_major):
    if rope:
        (x_ref, mod_ref, wh_ref, wm_ref, cos_ref, sin_ref, qg_ref, kvg_ref, wq1_ref, wq2_ref, wk_ref,
         wv_ref) = refs[:12]
        ph_ref, q_ref, k_ref, v_ref, ckv_ref, kr_ref = refs[12:]
    else:
        x_ref, mod_ref, wh_ref, wm_ref, qg_ref, kvg_ref, wq1_ref, wk_ref, wv_ref = refs[:9]
        ph_ref, q_ref, k_ref, v_ref, ckv_ref, kr_ref = refs[9:]
    scale = (MLA_DN + MLA_DR) ** -0.5 * LOG2_E
    mod = mod_ref[0]
    hmod = (x_ref[0] * (1.0 + mod[1:2]) + mod[0:1]).astype(BF16)
    ph_ref[0] = _dot(hmod, wh_ref[...])
    pm = _dot(hmod, wm_ref[...])
    kr_ref[0] = pm[:, MLA_Q_RANK + MLA_KV_RANK:MLA_Q_RANK + MLA_KV_RANK + LANES]
    cq = pm[:, :MLA_Q_RANK]
    cq = cq * lax.rsqrt(jnp.mean(cq * cq, axis=-1, keepdims=True) + RMS_EPS) * qg_ref[...]
    cq = cq.astype(BF16)
    ckv = pm[:, MLA_Q_RANK:MLA_Q_RANK + MLA_KV_RANK]
    ckv = ckv * lax.rsqrt(jnp.mean(ckv * ckv, axis=-1, keepdims=True) + RMS_EPS) * kvg_ref[...]
    ckv_ref[0] = ckv
    ckv = ckv.astype(BF16)
    kr0 = MLA_Q_RANK + MLA_KV_RANK
    kr = pm[:, kr0:kr0 + LANES]
    qa = _dot(cq, wq1_ref[...])
    kn = _dot(ckv, wk_ref[...])
    if key_major:
        v_ref[0] = _dot_nt(wv_ref[...], ckv).astype(BF16)
    else:
        v_ref[0] = _dot(ckv, wv_ref[...]).astype(BF16)
    if rope:
        cos = cos_ref[...]
        sin = sin_ref[...]
        qb = _dot(cq, wq2_ref[...])
        kr = kr * cos + pm[:, kr0 + LANES:kr0 + 2 * LANES] * sin
    for h in range(MLA_HEADS):
        sl = slice(h * HEAD_PAD, (h + 1) * HEAD_PAD)
        qh = qa[:, sl]
        if rope:
            qh = qh * cos + qb[:, sl] * sin
        q_ref[0, :, sl] = (qh * scale).astype(BF16)
        k_ref[0, :, sl] = (kn[:, sl] + kr).astype(BF16)


def _ab_inproj(x, mod, w_h, w_m, rope_tabs, qg, kvg, wq1, wq2, wk, wv):
    b, l, d = x.shape
    tm = PROJ_TILE
    rope = rope_tabs is not None
    per_batch = mod.shape[0] > 1
    mod_map = (lambda i, j: (i, 0, 0)) if per_batch else (lambda i, j: (0, 0, 0))
    full = lambda a: pl.BlockSpec(a.shape, lambda i, j: (0,) * a.ndim, pipeline_mode=pl.Buffered(1))
    row = lambda width: pl.BlockSpec((1, tm, width), lambda i, j: (i, j, 0))
    hw = MLA_HEADS * HEAD_PAD
    inputs = [x, mod, w_h, w_m]
    in_specs = [row(d), pl.BlockSpec((1, 6, d), mod_map), full(w_h), full(w_m)]
    if rope:
        inputs += list(rope_tabs)
        in_specs += [pl.BlockSpec((tm, LANES), lambda i, j: (j, 0))] * 2
    ws = [qg, kvg, wq1] + ([wq2] if rope else []) + [wk, wv]
    inputs += ws
    in_specs += [full(a) for a in ws]
    key_major = wv.shape[1] == MLA_KV_RANK
    vw = MLA_HEADS * MLA_DV
    shapes = [(b, l, w_h.shape[1]), (b, l, hw), (b, l, hw), (b, vw, l) if key_major else (b, l, hw),
              (b, l, MLA_KV_RANK), (b, l, LANES)]
    dtypes = [F32, BF16, BF16, BF16, F32, F32]
    out_specs = [row(s[2]) for s in shapes]
    if key_major:
        out_specs[3] = pl.BlockSpec((1, vw, tm), lambda i, j: (i, 0, j))
    return pl.pallas_call(
        functools.partial(_ab_inproj_kernel, rope=rope, key_major=key_major),
        out_shape=[jax.ShapeDtypeStruct(s, dt) for s, dt in zip(shapes, dtypes)],
        grid=(b, l // tm),
        in_specs=in_specs,
        out_specs=out_specs,
        compiler_params=_cparams(("parallel", "parallel")),
        name="ab_inproj",
    )(*inputs)


def _mla_ctx_kernel(ckv_ref, kr_ref, place_ref, wk_ref, wvt_ref, k_ref, vt_ref):
    ckv = ckv_ref[0].astype(BF16)
    kr = _dot(kr_ref[0].astype(BF16), place_ref[...])
    kn = _dot(ckv, wk_ref[...])
    vt_ref[0] = _dot_nt(wvt_ref[...], ckv).astype(BF16)
    for h in range(MLA_HEADS):
        sl = slice(h * HEAD_PAD, (h + 1) * HEAD_PAD)
        k_ref[0, :, sl] = (kn[:, sl] + kr).astype(BF16)


def _mla_ctx(ctx_ckv, ctx_kr, wk, wvt):
    b, lc, _ = ctx_ckv.shape
    hw = MLA_HEADS * HEAD_PAD
    vw = MLA_HEADS * MLA_DV
    place = np.zeros((MLA_DR, LANES), np.float32)
    place[np.arange(MLA_DR), MLA_DN + np.arange(MLA_DR)] = 1.0
    place = jnp.asarray(place, BF16)
    full = lambda a: pl.BlockSpec(a.shape, lambda i: (0,) * a.ndim)
    return pl.pallas_call(
        _mla_ctx_kernel,
        out_shape=[jax.ShapeDtypeStruct((b, lc, hw), BF16), jax.ShapeDtypeStruct((b, vw, lc), BF16)],
        grid=(b,),
        in_specs=[pl.BlockSpec((1, lc, MLA_KV_RANK), lambda i: (i, 0, 0)),
                  pl.BlockSpec((1, lc, MLA_DR), lambda i: (i, 0, 0)),
                  full(place), full(wk), full(wvt)],
        out_specs=[pl.BlockSpec((1, lc, hw), lambda i: (i, 0, 0)), pl.BlockSpec((1, vw, lc), lambda i: (i, 0, 0))],
        compiler_params=_cparams(("parallel",)),
        name="mla_ctx_keys",
    )(ctx_ckv, ctx_kr, place, wk, wvt)


def _attn_long_kernel(q_ref, k_ref, vt_ref, kc_ref, vtc_ref, o_ref, s_a, s_b, m_a, m_b):
    i = pl.program_id(2)
    tq = q_ref.shape[1]
    lk = k_ref.shape[1]
    tk = ATTN_KEY_BLOCK
    n_blk = lk // tk
    slabs = [slice(hh * HEAD_PAD, (hh + 1) * HEAD_PAD) for hh in range(2)]
    vrows = [slice(hh * MLA_DV, (hh + 1) * MLA_DV) for hh in range(2)]
    q_t = [q_ref[0, :, sl].astype(F32).T.astype(BF16) for sl in slabs]
    crow = slice(lk, lk + kc_ref.shape[1])

    @pl.when(i == 0)
    def _():
        s_b[...] = jnp.zeros(s_b.shape, F32)
        m_b[...] = jnp.zeros(m_b.shape, F32)

    def run(s_cur, m_cur, s_prv, m_prv):
        def scores(hh, k_blk, rows, m_run):
            s = _dot(k_blk, q_t[hh])
            s_cur[hh, rows, :] = s
            return jnp.maximum(m_run, jnp.max(s.reshape(s.shape[0] // SUBLANES, SUBLANES, tq), axis=0))

        def weigh(hh, vt_blk, rows, m, acc):
            p = jnp.exp2(s_prv[hh, rows, :] - m).astype(BF16)
            lhs = jnp.concatenate([vt_blk, jnp.ones((2 * SUBLANES, vt_blk.shape[1]), BF16)], axis=0)
            return acc + _dot(lhs, p)

        ms = [jnp.max(m_prv[hh], axis=0, keepdims=True) for hh in range(2)]

        def step(j, carry):
            m_runs, accs = carry
            rows = pl.ds(pl.multiple_of(j * tk, tk), tk)
            m_runs = tuple(scores(hh, k_ref[0, rows, slabs[hh]], rows, m_runs[hh]) for hh in range(2))
            accs = tuple(weigh(hh, vt_ref[0, vrows[hh], rows], rows, ms[hh], accs[hh]) for hh in range(2))
            return m_runs, accs

        carry = ((jnp.full((SUBLANES, tq), -jnp.inf, F32),) * 2,
                 (jnp.zeros((MLA_DV + 2 * SUBLANES, tq), F32),) * 2)
        m_runs, accs = lax.fori_loop(0, n_blk, step, carry, unroll=2)
        m_runs = tuple(scores(hh, kc_ref[0, :, slabs[hh]], crow, m_runs[hh]) for hh in range(2))
        accs = tuple(weigh(hh, vtc_ref[0, vrows[hh], :], crow, ms[hh], accs[hh]) for hh in range(2))
        for hh in range(2):
            m_cur[hh] = m_runs[hh]
        outs = [acc[:MLA_DV] / acc[MLA_DV:MLA_DV + 1] for acc in accs]
        o_ref[0] = jnp.concatenate(outs, axis=0).T.astype(BF16)

    @pl.when(i % 2 == 0)
    def _():
        run(s_a, m_a, s_b, m_b)

    @pl.when(i % 2 == 1)
    def _():
        run(s_b, m_b, s_a, m_a)


def _attention_long(q, k, vt, kc, vtc):
    b, l, _ = q.shape
    lk = k.shape[1]
    lc = kc.shape[1]
    tq = ATTN_Q_TILE
    pw = 2 * HEAD_PAD
    vpw = 2 * MLA_DV
    assert lk % (2 * ATTN_KEY_BLOCK) == 0 and l % tq == 0
    n_q = l // tq
    return pl.pallas_call(
        _attn_long_kernel,
        out_shape=jax.ShapeDtypeStruct((b, l, MLA_HEADS * MLA_DV), BF16),
        grid=(b, MLA_HEADS // 2, n_q + 1),
        in_specs=[pl.BlockSpec((1, tq, pw), lambda bi, p, i: (bi, jnp.minimum(i, n_q - 1), p)),
                  pl.BlockSpec((1, lk, pw), lambda bi, p, i: (bi, 0, p)),
                  pl.BlockSpec((1, vpw, lk), lambda bi, p, i: (bi, p, 0)),
                  pl.BlockSpec((1, lc, pw), lambda bi, p, i: (bi, 0, p)),
                  pl.BlockSpec((1, vpw, lc), lambda bi, p, i: (bi, p, 0))],
        out_specs=pl.BlockSpec((1, tq, LANES), lambda bi, p, i: (bi, jnp.maximum(i - 1, 0), p)),
        scratch_shapes=[pltpu.VMEM((2, lk + lc, tq), F32), pltpu.VMEM((2, lk + lc, tq), F32),
                        pltpu.VMEM((2, SUBLANES, tq), F32), pltpu.VMEM((2, SUBLANES, tq), F32)],
        compiler_params=_cparams(("parallel", "parallel", "arbitrary")),
        name="mla_attention_long",
    )(q, k, vt, kc, vtc)


def _attn_kernel(*refs, has_ctx):
    if has_ctx:
        q_ref, k_ref, v_ref, kc_ref, vc_ref, o_ref = refs
    else:
        q_ref, k_ref, v_ref, o_ref = refs
    n_pairs = q_ref.shape[-1] // (2 * HEAD_PAD)
    for pair in range(n_pairs):
        out = None
        for hh in range(2):
            h0 = (2 * pair + hh) * HEAD_PAD
            sl = slice(h0, h0 + HEAD_PAD)
            q = q_ref[0, :, sl]
            s = _dot_nt(q, k_ref[0, :, sl])
            m = jnp.max(s, axis=-1, keepdims=True)
            if has_ctx:
                s2 = _dot_nt(q, kc_ref[0, :, sl])
                m = jnp.maximum(m, jnp.max(s2, axis=-1, keepdims=True))
            p = jnp.exp2(s - m)
            den = jnp.sum(p, axis=-1, keepdims=True)
            acc = _dot(p.astype(BF16), v_ref[0, :, sl])
            if has_ctx:
                p2 = jnp.exp2(s2 - m)
                den = den + jnp.sum(p2, axis=-1, keepdims=True)
                acc = acc + _dot(p2.astype(BF16), vc_ref[0, :, sl])
            o = acc / den
            out = o if out is None else out + o
        o_ref[0, :, pair * LANES:(pair + 1) * LANES] = out.astype(BF16)


def _attention(q, k, v, kc, vc):
    b, l, _ = q.shape
    lk = k.shape[1]
    tq = TOK_TILE
    pairs_per_step = MLA_HEADS // 2 if lk <= TOK_TILE else 1
    pw = 2 * HEAD_PAD * pairs_per_step
    has_ctx = kc is not None
    inputs = [q, k, v]
    in_specs = [pl.BlockSpec((1, tq, pw), lambda bi, p, i: (bi, i, p)),
                pl.BlockSpec((1, lk, pw), lambda bi, p, i: (bi, 0, p)),
                pl.BlockSpec((1, lk, pw), lambda bi, p, i: (bi, 0, p))]
    if has_ctx:
        lc = kc.shape[1]
        inputs += [kc, vc]
        in_specs += [pl.BlockSpec((1, lc, pw), lambda bi, p, i: (bi, 0, p))] * 2
    return pl.pallas_call(
        functools.partial(_attn_kernel, has_ctx=has_ctx),
        out_shape=jax.ShapeDtypeStruct((b, l, MLA_HEADS * MLA_DV), BF16),
        grid=(b, MLA_HEADS // 2 // pairs_per_step, l // tq),
        in_specs=in_specs,
        out_specs=pl.BlockSpec((1, tq, LANES * pairs_per_step), lambda bi, p, i: (bi, i, p)),
        compiler_params=_cparams(("parallel", "parallel", "arbitrary")),
        name="mla_attention",
    )(*inputs)


def _outproj_kernel(*refs, hgrn):
    if hgrn:
        x_ref, mod_ref, of_ref, ob_ref, gl_ref, gn_ref, b_ref, w_ref, lg_ref, lb_ref, o_ref = refs
        o = of_ref[0] + ob_ref[0]
        gate = _silu(gl_ref[0])
        gn = gn_ref[...]
        parts = []
        for h in range(HA_HEADS):
            sl = slice(h * HA_DV, (h + 1) * HA_DV)
            oh = o[:, sl]
            oh = oh * lax.rsqrt(jnp.mean(oh * oh, axis=-1, keepdims=True) + RMS_EPS) * gn
            parts.append((oh * gate[:, sl]).astype(BF16))
        a = jnp.concatenate(parts, axis=-1)
    else:
        x_ref, mod_ref, a_ref, b_ref, w_ref, lg_ref, lb_ref, o_ref = refs
        a = a_ref[0]
    half = a.shape[-1]
    y = _dot(a, w_ref[:half, :]) + _dot(b_ref[0], w_ref[half:, :])
    m = mod_ref[0]
    r = ALPHA * x_ref[0] + m[2:3] * y
    o_ref[0] = _layer_norm(r, lg_ref[...], lb_ref[...])


def _outproj(x, mod, a_inputs, b_in, w, ln_g, ln_b, hgrn):
    b, l, d = x.shape
    tm = PROJ_TILE
    per_batch = mod.shape[0] > 1
    mod_map = (lambda i, j: (i, 0, 0)) if per_batch else (lambda i, j: (0, 0, 0))
    row = lambda width: pl.BlockSpec((1, tm, width), lambda i, j: (i, j, 0))
    full = lambda a: pl.BlockSpec(a.shape, lambda i, j: (0,) * a.ndim, pipeline_mode=pl.Buffered(1))
    inputs = [x, mod]
    in_specs = [row(d), pl.BlockSpec((1, 6, d), mod_map)]
    if hgrn:
        o_f, o_b, p_h, g_norm = a_inputs
        inputs += [o_f, o_b, p_h, g_norm]
        in_specs += [row(HA_W), row(HA_W),
                     pl.BlockSpec((1, tm, HA_W), lambda i, j: (i, j, 4)), full(g_norm)]
    else:
        inputs += [a_inputs]
        in_specs += [row(a_inputs.shape[-1])]
    inputs += [b_in, w, ln_g, ln_b]
    in_specs += [row(b_in.shape[-1]), full(w), full(ln_g), full(ln_b)]
    return pl.pallas_call(
        functools.partial(_outproj_kernel, hgrn=hgrn),
        out_shape=jax.ShapeDtypeStruct((b, l, d), F32),
        grid=(b, l // tm),
        in_specs=in_specs,
        out_specs=row(d),
        compiler_params=_cparams(("parallel", "parallel")),
        name="outproj_ln",
    )(*inputs)


def _conv_kernel(pm_ref, pp_ref, pn_ref, scw_ref, cfw_ref, cfb_ref, cfg_ref, cfbeta_ref,
                 ysc_ref, ycf_ref, ext_sc, ext_cf, *, lt):
    i = pl.program_id(1)
    n_i = pl.num_programs(1)
    w = SC_W

    def sc_in(p):
        return p[:, w:2 * w] * p[:, 2 * w:3 * w]

    def cf_in(p):
        return p[:, 3 * w:3 * w + CF_W] * _sigmoid(p[:, 3 * w + CF_W:3 * w + 2 * CF_W])

    pm = pm_ref[0]
    pp = pp_ref[0]
    pn = pn_ref[0]
    has_prev = i > 0
    has_next = i < n_i - 1
    n_ext = lt + 2 * HALO
    for ext, conv_in in ((ext_sc, sc_in), (ext_cf, cf_in)):
        ext[0, 0:HALO, :] = jnp.where(has_prev, conv_in(pp), 0.0)
        ext[0, HALO:HALO + lt, :] = conv_in(pm)
        ext[0, HALO + lt:, :] = jnp.where(has_next, conv_in(pn), 0.0)
    for s in range(1, SUBLANES):
        ext_cf[s, 0:n_ext - SUBLANES, :] = ext_cf[0, s:s + n_ext - SUBLANES, :]
    sc_shifts = sorted({(HALO - SC_K // 2 + j) % SUBLANES for j in range(SC_K)} - {0})
    for s in sc_shifts:
        ext_sc[s, 0:n_ext - SUBLANES, :] = ext_sc[0, s:s + n_ext - SUBLANES, :]

    def tap(ext, off, rows):
        s = off % SUBLANES
        return ext[s, off - s:off - s + rows, :]

    rb = 32
    for r in range(lt // rb):
        base = HALO + r * rb
        acc = None
        for j in range(SC_K):
            term = jnp.tile(scw_ref[j], (rb // SUBLANES, 1)) * tap(ext_sc, base - SC_K // 2 + j, rb)
            acc = term if acc is None else acc + term
        ysc_ref[0, r * rb:(r + 1) * rb, :] = (pm[r * rb:(r + 1) * rb, 0:w] * acc).astype(BF16)
        acc = None
        for j in range(CF_K):
            term = jnp.tile(cfw_ref[j], (rb // SUBLANES, 1)) * tap(ext_cf, base - CF_K // 2 + j, rb)
            acc = term if acc is None else acc + term
        u = _layer_norm(acc + cfb_ref[...], cfg_ref[...], cfbeta_ref[...])
        ycf_ref[0, r * rb:(r + 1) * rb, :] = _silu(u).astype(BF16)


def _conv_mixers(p1, sc_w, cf_w, cf_b, cf_g, cf_beta):
    b, l, width = p1.shape
    lt = TOK_TILE
    hb = lt // HALO
    n_h = l // HALO
    sc_w = jnp.broadcast_to(sc_w[:, None, :], (sc_w.shape[0], SUBLANES, sc_w.shape[1]))
    cf_w = jnp.broadcast_to(cf_w[:, None, :], (cf_w.shape[0], SUBLANES, cf_w.shape[1]))
    full = lambda a: pl.BlockSpec(a.shape, lambda bi, i: (0,) * a.ndim)
    return pl.pallas_call(
        functools.partial(_conv_kernel, lt=lt),
        out_shape=[jax.ShapeDtypeStruct((b, l, SC_W), BF16), jax.ShapeDtypeStruct((b, l, CF_W), BF16)],
        grid=(b, l // lt),
        in_specs=[
            pl.BlockSpec((1, lt, width), lambda bi, i: (bi, i, 0)),
            pl.BlockSpec((1, HALO, width), lambda bi, i: (bi, jnp.maximum(i * hb - 1, 0), 0)),
            pl.BlockSpec((1, HALO, width), lambda bi, i: (bi, jnp.minimum((i + 1) * hb, n_h - 1), 0)),
            full(sc_w), full(cf_w), full(cf_b), full(cf_g), full(cf_beta),
        ],
        out_specs=[pl.BlockSpec((1, lt, SC_W), lambda bi, i: (bi, i, 0)),
                   pl.BlockSpec((1, lt, CF_W), lambda bi, i: (bi, i, 0))],
        scratch_shapes=[pltpu.VMEM((SUBLANES, lt + 2 * HALO, SC_W), F32),
                        pltpu.VMEM((SUBLANES, lt + 2 * HALO, CF_W), F32)],
        compiler_params=_cparams(("parallel", "parallel")),
        name="conv_mixers",
    )(p1, p1, p1, sc_w, cf_w, cf_b, cf_g, cf_beta)


def _route_t(lt):
    t = lt.shape[1]
    row = lax.broadcasted_iota(jnp.int32, (SUBLANES, t), 0).astype(F32)
    neg = -jnp.inf
    big = float(LANES)
    gl = jnp.where(row < N_GROUPS, lt[N_EXPERTS:N_EXPERTS + SUBLANES], neg)
    gmax = jnp.max(gl, axis=0, keepdims=True)
    p_g = 1.0 / jnp.sum(jnp.exp(gl - gmax), axis=0, keepdims=True)
    g_sel = jnp.min(jnp.where(gl == gmax, row, big), axis=0, keepdims=True)
    el = lt[0:EXP_PER_GROUP]
    for gi in range(1, N_GROUPS):
        el = jnp.where(g_sel == gi, lt[gi * EXP_PER_GROUP:(gi + 1) * EXP_PER_GROUP], el)
    v1 = jnp.max(el, axis=0, keepdims=True)
    i1 = jnp.min(jnp.where(el == v1, row, big), axis=0, keepdims=True)
    el2 = jnp.where(row == i1, neg, el)
    v2 = jnp.max(el2, axis=0, keepdims=True)
    i2 = jnp.min(jnp.where(el2 == v2, row, big), axis=0, keepdims=True)
    e2 = jnp.exp(v2 - v1)
    w1 = p_g / (1.0 + e2)
    w2 = p_g * e2 / (1.0 + e2)
    comb = jnp.where(row == i1, w1, 0.0) + jnp.where(row == i2, w2, 0.0)
    onehot = jnp.where(row == g_sel, 1.0, 0.0)
    return onehot, comb


def _moe_kernel(x_ref, mod_ref, wr_ref, tri_ref, wg_ref, wu_ref, wd_ref, lg_ref, lb_ref, o_ref,
                hb_scr, os_scr, pt_scr, cc_scr):
    m = mod_ref[0]
    tm, cap = pt_scr.shape
    d = x_ref.shape[-1]
    half = d // 2
    cw = EXP_PER_GROUP * EXP_FF

    h = x_ref[0] * (1.0 + m[4:5]) + m[3:4]
    h_hi = h.astype(BF16)
    hb_scr[...] = h_hi
    h_lo = (h - h_hi.astype(F32)).astype(BF16)
    l2 = _dot(h_hi, wr_ref[...])
    logits = l2[:, :LANES] + l2[:, LANES:] + _dot(h_lo, wr_ref[:, :LANES])
    onehot, comb = _route_t(logits.T)
    rank = _dot(onehot.astype(BF16), tri_ref[...])
    cnt = jnp.sum(onehot, axis=1, keepdims=True)
    n_blocks = jnp.floor((cnt + (MOE_BLOCK - 0.5)) * (1.0 / MOE_BLOCK))
    padded = n_blocks * MOE_BLOCK
    start = jnp.zeros((1, 1), F32)
    first_block = jnp.zeros((1, 1), F32)
    dest = jnp.zeros((1, tm), F32)
    first, count = [], []
    for gi in range(N_GROUPS):
        dest = dest + onehot[gi:gi + 1] * (start + rank[gi:gi + 1])
        first.append(first_block[0, 0].astype(jnp.int32))
        count.append(n_blocks[gi, 0].astype(jnp.int32))
        start = start + padded[gi:gi + 1]
        first_block = first_block + n_blocks[gi:gi + 1]
    aux = jnp.concatenate([jnp.broadcast_to(dest, (SUBLANES, tm)), comb,
                           jnp.zeros((LANES - 2 * SUBLANES, tm), F32)], axis=0).T
    lane = lax.broadcasted_iota(jnp.int32, (tm, LANES), 1)
    comb_tok = jnp.where((lane >= SUBLANES) & (lane < 2 * SUBLANES), aux, 0.0)
    c_hi = comb_tok.astype(BF16)
    c_mid = (comb_tok - c_hi.astype(F32)).astype(BF16)
    slot_t = lax.broadcasted_iota(jnp.int32, (tm, cap), 1).astype(F32)
    pt_scr[...] = jnp.where(slot_t == aux[:, 0:1], 1.0, 0.0).astype(BF16)
    cc_scr[...] = jnp.concatenate([c_hi, c_mid], axis=1)
    os_scr[...] = jnp.zeros_like(os_scr)
    block_slot = lax.broadcasted_iota(jnp.int32, (MOE_BLOCK, tm), 0).astype(F32)

    for gi in range(N_GROUPS):
        cols = slice(gi * cw, (gi + 1) * cw)

        def block(i, carry, gi=gi, cols=cols):
            blk = first[gi] + i
            r0 = pl.multiple_of(blk * MOE_BLOCK, 2 * SUBLANES)
            rows = pl.ds(r0, MOE_BLOCK)
            perm = jnp.where(block_slot + (blk * MOE_BLOCK).astype(F32) == dest, 1.0, 0.0).astype(BF16)
            hs = _dot(perm, hb_scr[...]).astype(BF16)
            cs2 = _dot(perm, cc_scr[...])
            cs = cs2[:, :LANES] + cs2[:, LANES:]
            hid = _silu(_dot(hs, wg_ref[:, cols])) * _dot(hs, wu_ref[:, cols])
            hid = jnp.concatenate(
                [hid[:, e * EXP_FF:(e + 1) * EXP_FF] * cs[:, SUBLANES + e:SUBLANES + e + 1]
                 for e in range(EXP_PER_GROUP)], axis=1)
            os_scr[rows, :] = _dot(hid.astype(BF16), wd_ref[cols, :]).astype(BF16)
            return carry

        lax.fori_loop(0, count[gi], block, 0)

    y = jnp.concatenate([_dot(pt_scr[...], os_scr[:, :half]), _dot(pt_scr[...], os_scr[:, half:])], axis=1)
    r = ALPHA * x_ref[0] + m[5:6] * y
    o_ref[0] = _layer_norm(r, lg_ref[...], lb_ref[...])


def _moe(x, mod, wr, wg, wu, wd, ln_g, ln_b):
    b, l, d = x.shape
    tm = MOE_TILE
    cap = (tm + N_GROUPS * (MOE_BLOCK - 1)) // MOE_BLOCK * MOE_BLOCK
    cap = -(-cap // LANES) * LANES
    cw = EXP_PER_GROUP * EXP_FF
    per_batch = mod.shape[0] > 1
    mod_map = (lambda i, t: (i, 0, 0)) if per_batch else (lambda i, t: (0, 0, 0))
    tri = jnp.asarray(np.triu(np.ones((tm, tm), np.float32), k=1), BF16)
    full = lambda a: pl.BlockSpec(a.shape, lambda i, t: (0,) * a.ndim, pipeline_mode=pl.Buffered(1))
    return pl.pallas_call(
        _moe_kernel,
        out_shape=jax.ShapeDtypeStruct((b, l, d), F32),
        grid=(b, l // tm),
        in_specs=[
            pl.BlockSpec((1, tm, d), lambda i, t: (i, t, 0)),
            pl.BlockSpec((1, 6, d), mod_map),
            full(wr), full(tri), full(wg), full(wu), full(wd), full(ln_g), full(ln_b),
        ],
        out_specs=pl.BlockSpec((1, tm, d), lambda i, t: (i, t, 0)),
        scratch_shapes=[pltpu.VMEM((tm, d), BF16), pltpu.VMEM((cap, d), BF16),
                        pltpu.VMEM((tm, cap), BF16), pltpu.VMEM((tm, 2 * LANES), BF16)],
        compiler_params=_cparams(("arbitrary", "arbitrary")),
        name="hier_moe_ln",
    )(x, mod, wr, tri, wg, wu, wd, ln_g, ln_b)


def _rope_swap_perm():
    idx = np.arange(MLA_DR)
    return idx ^ (MLA_DR // 4)


def _rope_tables(n_tok):
    rows = n_tok // GRID_W
    pos_r = jnp.repeat(jnp.arange(rows, dtype=F32), GRID_W)
    pos_c = (jnp.arange(rows * GRID_W) % GRID_W).astype(F32)
    n_freq = MLA_DR // 4
    inv = ROPE_BASE ** (-jnp.arange(n_freq, dtype=F32) / n_freq)
    ang = jnp.stack([pos_r[:, None] * inv, pos_c[:, None] * inv], axis=1)
    cos, sin = jnp.cos(ang), jnp.sin(ang)
    cos32 = jnp.stack([cos, cos], axis=2).reshape(n_tok, MLA_DR)
    sin32 = jnp.stack([-sin, sin], axis=2).reshape(n_tok, MLA_DR)
    pad_hi = LANES - MLA_DN - MLA_DR
    cos_t = jnp.concatenate([jnp.ones((n_tok, MLA_DN), F32), cos32, jnp.zeros((n_tok, pad_hi), F32)], axis=1)
    sin_t = jnp.concatenate([jnp.zeros((n_tok, MLA_DN), F32), sin32, jnp.zeros((n_tok, pad_hi), F32)], axis=1)
    return cos_t, sin_t


def _ab_weights(w_in, w_uq, w_ukv):
    perm = _rope_swap_perm()
    pad_hi = LANES - MLA_DN - MLA_DR
    w_h = w_in[:, :5 * HA_W]
    cq = w_in[:, 5 * HA_W:5 * HA_W + MLA_Q_RANK]
    ckv = w_in[:, 5 * HA_W + MLA_Q_RANK:5 * HA_W + MLA_Q_RANK + MLA_KV_RANK]
    kr = w_in[:, 5 * HA_W + MLA_Q_RANK + MLA_KV_RANK:]
    d = w_in.shape[0]
    z_lo = jnp.zeros((d, MLA_DN), F32)
    z_hi = jnp.zeros((d, pad_hi), F32)
    w_m = jnp.concatenate([cq, ckv, z_lo, kr, z_hi, z_lo, kr[:, perm], z_hi], axis=1)
    uq = w_uq.reshape(MLA_Q_RANK, MLA_HEADS, MLA_DN + MLA_DR)
    q_nope, q_rope = uq[..., :MLA_DN], uq[..., MLA_DN:]
    zq_hi = jnp.zeros((MLA_Q_RANK, MLA_HEADS, pad_hi), F32)
    wq1 = jnp.concatenate([q_nope, q_rope, zq_hi], axis=-1).reshape(MLA_Q_RANK, -1)
    wq2 = jnp.concatenate([jnp.zeros_like(q_nope), q_rope[..., perm], zq_hi], axis=-1).reshape(MLA_Q_RANK, -1)
    ukv = w_ukv.reshape(MLA_KV_RANK, MLA_HEADS, MLA_DN + MLA_DV)
    k_nope, v = ukv[..., :MLA_DN], ukv[..., MLA_DN:]
    wk = jnp.concatenate([k_nope, jnp.zeros((MLA_KV_RANK, MLA_HEADS, LANES - MLA_DN), F32)], axis=-1)
    wk = wk.reshape(MLA_KV_RANK, -1)
    v_pairs = v.reshape(MLA_KV_RANK, MLA_HEADS // 2, 2, MLA_DV)
    zv = jnp.zeros_like(v_pairs[:, :, 0])
    wv = jnp.stack([jnp.concatenate([v_pairs[:, :, 0], zv], axis=-1),
                    jnp.concatenate([zv, v_pairs[:, :, 1]], axis=-1)], axis=2).reshape(MLA_KV_RANK, -1)
    wvt = v.reshape(MLA_KV_RANK, MLA_HEADS * MLA_DV).T
    bf = lambda a: a.astype(BF16)
    return bf(w_h), bf(w_m), bf(wq1), bf(wq2), bf(wk), bf(wv), bf(wvt)


def _moe_weights(w_group, w_expert, w_gate, w_up, w_down):
    d = w_group.shape[0]
    wr = jnp.concatenate([w_expert, w_group, jnp.zeros((d, LANES - N_EXPERTS - N_GROUPS), F32)], axis=1)
    wr_hi = wr.astype(BF16)
    wr = jnp.concatenate([wr_hi, (wr - wr_hi.astype(F32)).astype(BF16)], axis=1)
    wg = jnp.transpose(w_gate, (1, 0, 2)).reshape(d, N_EXPERTS * EXP_FF).astype(BF16)
    wu = jnp.transpose(w_up, (1, 0, 2)).reshape(d, N_EXPERTS * EXP_FF).astype(BF16)
    wd = w_down.reshape(N_EXPERTS * EXP_FF, d).astype(BF16)
    return wr, wg, wu, wd


def kernel(x_prompt, x_sample, state_hgrn_fwd, state_hgrn_bwd, cache_mla_ckv, cache_mla_krope, c, c_ctx, mod_w, mod_b, ln_g, ln_b, ab_w_in, ab_w_out, hgrn_lb_logits, hgrn_norm_g, mla_q_norm_g, mla_w_uq, mla_kv_norm_g, mla_w_ukv, cd_w_in, cd_w_out, sc_conv_w, cf_conv_w, cf_conv_b, cf_ln_g, cf_ln_b, moe_w_group, moe_w_expert, moe_w_gate, moe_w_up, moe_w_down):
    dec_b = x_sample.shape[0]
    d = D_MODEL
    cvec = jnp.concatenate([c, c_ctx[None, :], jnp.zeros((SUBLANES - dec_b - 1, d), F32)], axis=0)
    mods = _mod_vectors(cvec, mod_w, mod_b)
    rope_tabs = _rope_tables(x_sample.shape[1])
    xp, xs = x_prompt, x_sample
    new_sf = new_sb = new_ckv = new_kr = None
    for l in range(DEPTH):
        mod_lat = mods[l, :dec_b].reshape(dec_b, 6, d)
        mod_ctx = mods[l, dec_b:dec_b + 1].reshape(1, 6, d)
        row = lambda a: a.reshape(1, -1)
        fold = lambda a: a.reshape(-1, PROJ_TILE, a.shape[-1])
        unfold = lambda a: a.reshape(x_prompt.shape[0], -1, a.shape[-1])
        if l % 2 == 0:
            e = l // 2
            w_h, w_m, wq1, wq2, wk, wv, wvt = _ab_weights(ab_w_in[e], mla_w_uq[e], mla_w_ukv[e])
            w_out = ab_w_out[e].astype(BF16)
            qg, kvg, gn = row(mla_q_norm_g[e]), row(mla_kv_norm_g[e]), row(hgrn_norm_g[e])
            ph_p, q_p, k_p, v_p, ckv_p, kr_p = map(unfold, _ab_inproj(
                fold(xp), mod_ctx, w_h, w_m, None, qg, kvg, wq1, None, wk, wv))
            of_p, ob_p, sf, sb = _hgrn(ph_p, hgrn_lb_logits, None, None, e)
            om_p = _attention(q_p, k_p, v_p, None, None)
            xp = unfold(_outproj(fold(xp), mod_ctx, (fold(of_p), fold(ob_p), fold(ph_p), gn), fold(om_p), w_out,
                                 row(ln_g[l, 0]), row(ln_b[l, 0]), True))
            ph_s, q_s, k_s, vt_s, _, _ = _ab_inproj(xs, mod_lat, w_h, w_m, rope_tabs, qg, kvg, wq1, wq2, wk, wvt)
            of_s, ob_s, _, _ = _hgrn(ph_s, hgrn_lb_logits, state_hgrn_fwd[:, e], state_hgrn_bwd[:, e], e)
            kc, vtc = _mla_ctx(cache_mla_ckv[:, e], cache_mla_krope[:, e], wk, wvt)
            om_s = _attention_long(q_s, k_s, vt_s, kc, vtc)
            xs = _outproj(xs, mod_lat, (of_s, ob_s, ph_s, gn), om_s, w_out, row(ln_g[l, 0]), row(ln_b[l, 0]), True)
            new_sf, new_sb, new_ckv = sf, sb, ckv_p
            new_kr = kr_p[:, :, MLA_DN:MLA_DN + MLA_DR]
        else:
            jx = l // 2
            w1 = cd_w_in[jx].astype(BF16)
            w_out = cd_w_out[jx].astype(BF16)
            cd = (sc_conv_w[jx], cf_conv_w[jx], row(cf_conv_b[jx]), row(cf_ln_g[jx]), row(cf_ln_b[jx]))
            (p1_p,) = _inproj(fold(xp), mod_ctx, [w1])
            ysc_p, ycf_p = _conv_mixers(unfold(p1_p), *cd)
            xp = unfold(_outproj(fold(xp), mod_ctx, fold(ysc_p), fold(ycf_p), w_out,
                                 row(ln_g[l, 0]), row(ln_b[l, 0]), False))
            (p1_s,) = _inproj(xs, mod_lat, [w1])
            ysc_s, ycf_s = _conv_mixers(p1_s, *cd)
            xs = _outproj(xs, mod_lat, ysc_s, ycf_s, w_out, row(ln_g[l, 0]), row(ln_b[l, 0]), False)
        wr, wg, wu, wd = _moe_weights(moe_w_group[l], moe_w_expert[l], moe_w_gate[l], moe_w_up[l], moe_w_down[l])
        xp_t = xp.reshape(-1, MOE_TILE, d)
        xp = _moe(xp_t, mod_ctx, wr, wg, wu, wd, row(ln_g[l, 1]), row(ln_b[l, 1])).reshape(xp.shape)
        xs = _moe(xs, mod_lat, wr, wg, wu, wd, row(ln_g[l, 1]), row(ln_b[l, 1]))
    return (xp, xs, new_sf[:, None], new_sb[:, None], new_ckv[:, None], new_kr[:, None])
```

```python
import functools

import numpy as np
import jax
import jax.numpy as jnp
from jax import lax
from jax.experimental import pallas as pl
from jax.experimental.pallas import tpu as pltpu

F32 = jnp.float32
BF16 = jnp.bfloat16
HIGHEST = lax.Precision.HIGHEST

D_MODEL = 1024
DEPTH = 2
GRID_W = 64
N_EVEN = (DEPTH + 1) // 2
HA_HEADS = 4
HA_DK = 128
HA_DV = 128
HA_W = HA_HEADS * HA_DK
CHUNK = 32
MLA_HEADS = 8
MLA_DN = 64
MLA_DR = 32
MLA_DV = 64
MLA_Q_RANK = 384
MLA_KV_RANK = 256
ROPE_BASE = 10000.0
SC_W = 512
SC_K = 3
CF_W = 512
CF_K = 31
N_GROUPS = 4
EXP_PER_GROUP = 8
N_EXPERTS = N_GROUPS * EXP_PER_GROUP
EXP_FF = 128
ALPHA = (2.0 * DEPTH) ** 0.25
LOG2_E = 1.4426950408889634
LN_EPS = 1e-5
RMS_EPS = 1e-6

LANES = 128
SUBLANES = 8
VMEM_LIMIT = 56 * 1024 * 1024

HEAD_PAD = LANES
MLA_SLAB = MLA_Q_RANK + MLA_KV_RANK + 2 * LANES
PROJ_TILE = 512
TOK_TILE = 256
GROUP_ROWS = 128
N_LEVELS = 5
SAFE_LOG_DECAY = 60.0
ATTN_KEY_BLOCK = 512
ATTN_Q_TILE = 256
MOE_TILE = 512
MOE_BLOCK = 160
HALO = 16


def _cparams(sem):
    return pltpu.CompilerParams(dimension_semantics=sem, vmem_limit_bytes=VMEM_LIMIT)


def _sigmoid(x):
    return 0.5 * jnp.tanh(0.5 * x) + 0.5


def _silu(x):
    half = 0.5 * x
    return half * jnp.tanh(half) + half


def _layer_norm(r, g, b):
    mu = jnp.mean(r, axis=-1, keepdims=True)
    d = r - mu
    var = jnp.mean(d * d, axis=-1, keepdims=True)
    return d * lax.rsqrt(var + LN_EPS) * g + b


def _dot(a, b):
    return jnp.dot(a, b, preferred_element_type=F32)


def _dot_nt(a, b):
    return lax.dot_general(a, b, (((1,), (1,)), ((), ())), preferred_element_type=F32)


def _dot_tn(a, b):
    return lax.dot_general(a, b, (((0,), (0,)), ((), ())), preferred_element_type=F32)


def _mod_kernel(c_ref, w_ref, b_ref, o_ref):
    s = _silu(c_ref[...])
    o_ref[0] = jnp.dot(s, w_ref[0], precision=HIGHEST, preferred_element_type=F32) + b_ref[0]


def _mod_vectors(cvec, mod_w, mod_b):
    n_out = mod_w.shape[-1]
    tn = 1536
    return pl.pallas_call(
        _mod_kernel,
        out_shape=jax.ShapeDtypeStruct((DEPTH, SUBLANES, n_out), F32),
        grid=(DEPTH, n_out // tn),
        in_specs=[
            pl.BlockSpec((SUBLANES, D_MODEL), lambda l, j: (0, 0)),
            pl.BlockSpec((1, D_MODEL, tn), lambda l, j: (l, 0, j)),
            pl.BlockSpec((1, 1, tn), lambda l, j: (l, 0, j)),
        ],
        out_specs=pl.BlockSpec((1, SUBLANES, tn), lambda l, j: (l, 0, j)),
        compiler_params=_cparams(("arbitrary", "arbitrary")),
        name="mod_vectors",
    )(cvec, mod_w, mod_b.reshape(DEPTH, 1, n_out))


def _inproj_kernel(*refs, n_w):
    x_ref, mod_ref = refs[0], refs[1]
    w_refs = refs[2:2 + n_w]
    o_refs = refs[2 + n_w:]
    m = mod_ref[0]
    h = (x_ref[0] * (1.0 + m[1:2]) + m[0:1]).astype(BF16)
    for w_ref, o_ref in zip(w_refs, o_refs):
        o_ref[0] = _dot(h, w_ref[...])


def _inproj(x, mod, weights):
    b, l, d = x.shape
    tm = PROJ_TILE
    per_batch = mod.shape[0] > 1
    mod_map = (lambda i, j: (i, 0, 0)) if per_batch else (lambda i, j: (0, 0, 0))
    in_specs = [pl.BlockSpec((1, tm, d), lambda i, j: (i, j, 0)),
                pl.BlockSpec((1, 6, d), mod_map)]
    in_specs += [pl.BlockSpec(w.shape, lambda i, j: (0, 0), pipeline_mode=pl.Buffered(1)) for w in weights]
    out_shape = [jax.ShapeDtypeStruct((b, l, w.shape[1]), F32) for w in weights]
    out_specs = [pl.BlockSpec((1, tm, w.shape[1]), lambda i, j: (i, j, 0)) for w in weights]
    return pl.pallas_call(
        functools.partial(_inproj_kernel, n_w=len(weights)),
        out_shape=out_shape,
        grid=(b, l // tm),
        in_specs=in_specs,
        out_specs=out_specs,
        compiler_params=_cparams(("parallel", "parallel")),
        name="modulate_inproj",
    )(x, mod, *weights)


def _hgrn_tables():
    n = GROUP_ROWS
    t = np.arange(n)[:, None]
    j = np.arange(n)[None, :]
    same_chunk = (t // CHUNK) == (j // CHUNK)
    e_f, e_b = [], []
    lv_f = np.full((n, n), -1, np.int32)
    for lvl in range(N_LEVELS):
        m = CHUNK >> (lvl + 1)
        blk0 = (t // (2 * m)) * (2 * m)
        r = blk0 + m - 1
        upper = t > r
        ef = np.where(upper, (j > r) & (j <= t), (j > t) & (j <= r))
        r2 = blk0 + m
        lower = t < r2
        eb = np.where(lower, (j >= t) & (j < r2), (j >= r2) & (j < t))
        e_f.append(ef)
        e_b.append(eb)
        s = np.arange(n)[None, :]
        same_blk = (t // (2 * m)) == (s // (2 * m))
        q_side = (t % (2 * m)) >= m
        k_side = (s % (2 * m)) < m
        lv_f[same_blk & q_side & k_side] = lvl
    lv_f[np.arange(n), np.arange(n)] = N_LEVELS
    e_f = np.concatenate(e_f, axis=0).astype(np.float32)
    e_b = np.concatenate(e_b, axis=0).astype(np.float32)
    e = np.stack([e_f, e_b])
    cum = np.stack([same_chunk & (j <= t), same_chunk & (j >= t)]).astype(np.float32)
    lv = np.stack([lv_f, lv_f.T])
    return e, cum, lv


def _hgrn_kernel(qf_ref, vf_ref, ff_ref, qb_ref, vb_ref, fb_ref, lbl_ref, e_ref, cum_ref, lv_ref, s0f_ref, s0b_ref,
                 of_ref, ob_ref, sf_ref, sb_ref,
                 st_scr, qe_scr, kd_scr, sc_scr, v_scr, oi_scr, dec_scr, *, lt, slot, has_state):
    i = pl.program_id(1)
    n_i = pl.num_programs(1)
    n_chunks = lt // CHUNK
    n_groups = lt // GROUP_ROWS
    chunks_per_group = GROUP_ROWS // CHUNK

    @pl.when(i == 0)
    def _():
        for d, s0_ref in enumerate((s0f_ref, s0b_ref)):
            for h in range(HA_HEADS):
                if has_state:
                    st_scr[d, h] = s0_ref[0, h].T
                else:
                    st_scr[d, h] = jnp.zeros((HA_DV, HA_DK), F32)

    lg = lbl_ref[...]
    n_slots = lg.shape[0]
    mx = lg[0]
    for s in range(1, n_slots):
        mx = jnp.maximum(mx, lg[s])
    ex = [jnp.exp(lg[s] - mx) for s in range(n_slots)]
    den = ex[0]
    for s in range(1, n_slots):
        den = den + ex[s]
    num = ex[0]
    for s in range(1, slot + 1):
        num = num + ex[s]
    lb_all = num / den

    dirs = ((qf_ref, vf_ref, ff_ref), (qb_ref, vb_ref, fb_ref))
    head_cols = [slice(h * HA_DK, (h + 1) * HA_DK) for h in range(HA_HEADS)]

    def gates(d, rows, cols):
        q_ref, _, f_ref = dirs[d]
        q = _silu(q_ref[0, rows, cols])
        lb_h = lb_all[d:d + 1, cols]
        f = lb_h + (1.0 - lb_h) * _sigmoid(f_ref[0, rows, cols])
        g = jnp.log(f)
        g_hi = g.astype(BF16)
        g_lo = (g - g_hi.astype(F32)).astype(BF16)
        return q, 1.0 - f, g_hi, g_lo

    def group_step(grp, tot_min):
        r0 = pl.multiple_of(grp * GROUP_ROWS, GROUP_ROWS)
        rows = pl.ds(r0, GROUP_ROWS)
        for d in range(2):
            cum = cum_ref[d]
            lv = lv_ref[d]
            for cols in head_cols:
                q, k, g_hi, g_lo = gates(d, rows, cols)
                bcum = _dot(cum, g_hi) + _dot(cum, g_lo)
                v_scr[d, rows, cols] = dirs[d][1][0, rows, cols].astype(BF16)
                qe = (q * jnp.exp(bcum)).astype(BF16)
                qe_scr[d, rows, cols] = qe
                kx = []
                for cc in range(chunks_per_group):
                    edge = cc * CHUNK + (CHUNK - 1 if d == 0 else 0)
                    tot = bcum[edge:edge + 1]
                    b_c = bcum[cc * CHUNK:(cc + 1) * CHUNK]
                    kd = k[cc * CHUNK:(cc + 1) * CHUNK] * jnp.exp(tot - b_c)
                    kd_scr[d, pl.ds(r0 + cc * CHUNK, CHUNK), cols] = kd.astype(BF16)
                    dec_scr[d, grp * chunks_per_group + cc, :, cols] = jnp.exp(tot)
                    kx.append(kd * jnp.exp(-tot))
                    tot_min = jnp.minimum(tot_min, tot)
                p = _dot_nt(qe, jnp.concatenate(kx, axis=0).astype(BF16))
                sc_scr[d, rows, cols] = jnp.where(lv >= 0, p, 0.0).astype(BF16)
        return tot_min

    tot_min = lax.fori_loop(0, n_groups, group_step, jnp.zeros((1, HA_DK), F32), unroll=True)

    def one_factor():
        for d in range(2):
            for grp in range(n_groups):
                rows = slice(grp * GROUP_ROWS, (grp + 1) * GROUP_ROWS)
                for cols in head_cols:
                    oi_scr[d, rows, cols] = _dot(sc_scr[d, rows, cols], v_scr[d, rows, cols])

    def per_level():
        for d in range(2):

            def group_step(grp, carry, d=d):
                rows = pl.ds(pl.multiple_of(grp * GROUP_ROWS, GROUP_ROWS), GROUP_ROWS)
                lv = lv_ref[d]
                for cols in head_cols:
                    q, k, g_hi, g_lo = gates(d, rows, cols)
                    x = jnp.exp(_dot(e_ref[d], g_hi) + _dot(e_ref[d], g_lo))
                    sc = jnp.where(lv == N_LEVELS, _dot_nt(q.astype(BF16), k.astype(BF16)), 0.0)
                    for lvl in range(N_LEVELS):
                        xl = x[lvl * GROUP_ROWS:(lvl + 1) * GROUP_ROWS]
                        p = _dot_nt((q * xl).astype(BF16), (k * xl).astype(BF16))
                        sc = jnp.where(lv == lvl, p, sc)
                    oi_scr[d, rows, cols] = _dot(sc.astype(BF16), v_scr[d, rows, cols])
                return carry

            lax.fori_loop(0, n_groups, group_step, 0)

    lax.cond(jnp.min(tot_min) > -SAFE_LOG_DECAY, one_factor, per_level)

    out_refs = (of_ref, ob_ref)

    def chunk_step(c, carry):
        for d in range(2):
            cc = c if d == 0 else n_chunks - 1 - c
            r0 = pl.multiple_of(cc * CHUNK, CHUNK)
            rows = pl.ds(r0, CHUNK)
            for h in range(HA_HEADS):
                cols = slice(h * HA_DK, (h + 1) * HA_DK)
                st = st_scr[d, h]
                o_state = _dot_nt(qe_scr[d, rows, cols], st.astype(BF16))
                out_refs[d][0, rows, cols] = oi_scr[d, rows, cols] + o_state
                upd = _dot_tn(v_scr[d, rows, cols], kd_scr[d, rows, cols])
                st_scr[d, h] = st * dec_scr[d, cc, :, cols] + upd
        return carry

    lax.fori_loop(0, n_chunks, chunk_step, 0, unroll=True)

    @pl.when(i == n_i - 1)
    def _():
        for d, s_ref in enumerate((sf_ref, sb_ref)):
            for h in range(HA_HEADS):
                s_ref[0, h] = st_scr[d, h].T


def _hgrn(p_h, lb_logits, s0_f, s0_b, slot):
    b, l, _ = p_h.shape
    lt = TOK_TILE
    n_t = l // lt
    has_state = s0_f is not None
    if not has_state:
        s0_f = jnp.zeros((1, HA_HEADS, HA_DK, HA_DV), F32)
        s0_b = s0_f
    e_np, cum_np, lv_np = _hgrn_tables()
    e_mat = jnp.asarray(e_np, BF16)
    cum = jnp.asarray(cum_np, BF16)
    lv = jnp.asarray(lv_np, jnp.int32)
    w = HA_W

    def sec(idx, rev):
        if rev:
            return pl.BlockSpec((1, lt, w), lambda bi, i: (bi, n_t - 1 - i, idx))
        return pl.BlockSpec((1, lt, w), lambda bi, i: (bi, i, idx))

    state_map = (lambda bi, i: (bi, 0, 0, 0)) if has_state else (lambda bi, i: (0, 0, 0, 0))
    in_specs = [
        sec(0, False), sec(3, False), sec(1, False),
        sec(0, True), sec(3, True), sec(2, True),
        pl.BlockSpec(lb_logits.shape, lambda bi, i: (0, 0, 0)),
        pl.BlockSpec(e_mat.shape, lambda bi, i: (0, 0, 0)),
        pl.BlockSpec(cum.shape, lambda bi, i: (0, 0, 0)),
        pl.BlockSpec(lv.shape, lambda bi, i: (0, 0, 0)),
        pl.BlockSpec((1, HA_HEADS, HA_DK, HA_DV), state_map),
        pl.BlockSpec((1, HA_HEADS, HA_DK, HA_DV), state_map),
    ]
    out_shape = [
        jax.ShapeDtypeStruct((b, l, w), F32),
        jax.ShapeDtypeStruct((b, l, w), F32),
        jax.ShapeDtypeStruct((b, HA_HEADS, HA_DK, HA_DV), F32),
        jax.ShapeDtypeStruct((b, HA_HEADS, HA_DK, HA_DV), F32),
    ]
    out_specs = [
        pl.BlockSpec((1, lt, w), lambda bi, i: (bi, i, 0)),
        pl.BlockSpec((1, lt, w), lambda bi, i: (bi, n_t - 1 - i, 0)),
        pl.BlockSpec((1, HA_HEADS, HA_DK, HA_DV), lambda bi, i: (bi, 0, 0, 0)),
        pl.BlockSpec((1, HA_HEADS, HA_DK, HA_DV), lambda bi, i: (bi, 0, 0, 0)),
    ]
    scratch = [
        pltpu.VMEM((2, HA_HEADS, HA_DV, HA_DK), F32),
        pltpu.VMEM((2, lt, w), BF16),
        pltpu.VMEM((2, lt, w), BF16),
        pltpu.VMEM((2, lt, w), BF16),
        pltpu.VMEM((2, lt, w), BF16),
        pltpu.VMEM((2, lt, w), F32),
        pltpu.VMEM((2, lt // CHUNK, 1, w), F32),
    ]
    return pl.pallas_call(
        functools.partial(_hgrn_kernel, lt=lt, slot=slot, has_state=has_state),
        out_shape=out_shape,
        grid=(b, n_t),
        in_specs=in_specs,
        out_specs=out_specs,
        scratch_shapes=scratch,
        compiler_params=_cparams(("parallel", "arbitrary")),
        name="hgrn2_scan",
    )(p_h, p_h, p_h, p_h, p_h, p_h, lb_logits, e_mat, cum, lv, s0_f, s0_b)


def _ab_inproj_kernel(*refs, rope, key_major):
    if rope:
        (x_ref, mod_ref, wh_ref, wm_ref, cos_ref, sin_ref, qg_ref, kvg_ref, wq1_ref, wq2_ref, wk_ref,
         wv_ref) = refs[:12]
        ph_ref, q_ref, k_ref, v_ref, ckv_ref, kr_ref = refs[12:]
    else:
        x_ref, mod_ref, wh_ref, wm_ref, qg_ref, kvg_ref, wq1_ref, wk_ref, wv_ref = refs[:9]
        ph_ref, q_ref, k_ref, v_ref, ckv_ref, kr_ref = refs[9:]
    scale = (MLA_DN + MLA_DR) ** -0.5 * LOG2_E
    mod = mod_ref[0]
    hmod = (x_ref[0] * (1.0 + mod[1:2]) + mod[0:1]).astype(BF16)
    ph_ref[0] = _dot(hmod, wh_ref[...])
    pm = _dot(hmod, wm_ref[...])
    kr_ref[0] = pm[:, MLA_Q_RANK + MLA_KV_RANK:MLA_Q_RANK + MLA_KV_RANK + LANES]
    cq = pm[:, :MLA_Q_RANK]
    cq = cq * lax.rsqrt(jnp.mean(cq * cq, axis=-1, keepdims=True) + RMS_EPS) * qg_ref[...]
    cq = cq.astype(BF16)
    ckv = pm[:, MLA_Q_RANK:MLA_Q_RANK + MLA_KV_RANK]
    ckv = ckv * lax.rsqrt(jnp.mean(ckv * ckv, axis=-1, keepdims=True) + RMS_EPS) * kvg_ref[...]
    ckv_ref[0] = ckv
    ckv = ckv.astype(BF16)
    kr0 = MLA_Q_RANK + MLA_KV_RANK
    kr = pm[:, kr0:kr0 + LANES]
    qa = _dot(cq, wq1_ref[...])
    kn = _dot(ckv, wk_ref[...])
    if key_major:
        v_ref[0] = _dot_nt(wv_ref[...], ckv).astype(BF16)
    else:
        v_ref[0] = _dot(ckv, wv_ref[...]).astype(BF16)
    if rope:
        cos = cos_ref[...]
        sin = sin_ref[...]
        qb = _dot(cq, wq2_ref[...])
        kr = kr * cos + pm[:, kr0 + LANES:kr0 + 2 * LANES] * sin
    for h in range(MLA_HEADS):
        sl = slice(h * HEAD_PAD, (h + 1) * HEAD_PAD)
        qh = qa[:, sl]
        if rope:
            qh = qh * cos + qb[:, sl] * sin
        q_ref[0, :, sl] = (qh * scale).astype(BF16)
        k_ref[0, :, sl] = (kn[:, sl] + kr).astype(BF16)


def _ab_inproj(x, mod, w_h, w_m, rope_tabs, qg, kvg, wq1, wq2, wk, wv):
    b, l, d = x.shape
    tm = PROJ_TILE
    rope = rope_tabs is not None
    per_batch = mod.shape[0] > 1
    mod_map = (lambda i, j: (i, 0, 0)) if per_batch else (lambda i, j: (0, 0, 0))
    full = lambda a: pl.BlockSpec(a.shape, lambda i, j: (0,) * a.ndim, pipeline_mode=pl.Buffered(1))
    row = lambda width: pl.BlockSpec((1, tm, width), lambda i, j: (i, j, 0))
    hw = MLA_HEADS * HEAD_PAD
    inputs = [x, mod, w_h, w_m]
    in_specs = [row(d), pl.BlockSpec((1, 6, d), mod_map), full(w_h), full(w_m)]
    if rope:
        inputs += list(rope_tabs)
        in_specs += [pl.BlockSpec((tm, LANES), lambda i, j: (j, 0))] * 2
    ws = [qg, kvg, wq1] + ([wq2] if rope else []) + [wk, wv]
    inputs += ws
    in_specs += [full(a) for a in ws]
    key_major = wv.shape[1] == MLA_KV_RANK
    vw = MLA_HEADS * MLA_DV
    shapes = [(b, l, w_h.shape[1]), (b, l, hw), (b, l, hw), (b, vw, l) if key_major else (b, l, hw),
              (b, l, MLA_KV_RANK), (b, l, LANES)]
    dtypes = [F32, BF16, BF16, BF16, F32, F32]
    out_specs = [row(s[2]) for s in shapes]
    if key_major:
        out_specs[3] = pl.BlockSpec((1, vw, tm), lambda i, j: (i, 0, j))
    return pl.pallas_call(
        functools.partial(_ab_inproj_kernel, rope=rope, key_major=key_major),
        out_shape=[jax.ShapeDtypeStruct(s, dt) for s, dt in zip(shapes, dtypes)],
        grid=(b, l // tm),
        in_specs=in_specs,
        out_specs=out_specs,
        compiler_params=_cparams(("parallel", "parallel")),
        name="ab_inproj",
    )(*inputs)


def _mla_ctx_kernel(ckv_ref, kr_ref, place_ref, wk_ref, wvt_ref, k_ref, vt_ref):
    ckv = ckv_ref[0].astype(BF16)
    kr = _dot(kr_ref[0].astype(BF16), place_ref[...])
    kn = _dot(ckv, wk_ref[...])
    vt_ref[0] = _dot_nt(wvt_ref[...], ckv).astype(BF16)
    for h in range(MLA_HEADS):
        sl = slice(h * HEAD_PAD, (h + 1) * HEAD_PAD)
        k_ref[0, :, sl] = (kn[:, sl] + kr).astype(BF16)


def _mla_ctx(ctx_ckv, ctx_kr, wk, wvt):
    b, lc, _ = ctx_ckv.shape
    hw = MLA_HEADS * HEAD_PAD
    vw = MLA_HEADS * MLA_DV
    place = np.zeros((MLA_DR, LANES), np.float32)
    place[np.arange(MLA_DR), MLA_DN + np.arange(MLA_DR)] = 1.0
    place = jnp.asarray(place, BF16)
    full = lambda a: pl.BlockSpec(a.shape, lambda i: (0,) * a.ndim)
    return pl.pallas_call(
        _mla_ctx_kernel,
        out_shape=[jax.ShapeDtypeStruct((b, lc, hw), BF16), jax.ShapeDtypeStruct((b, vw, lc), BF16)],
        grid=(b,),
        in_specs=[pl.BlockSpec((1, lc, MLA_KV_RANK), lambda i: (i, 0, 0)),
                  pl.BlockSpec((1, lc, MLA_DR), lambda i: (i, 0, 0)),
                  full(place), full(wk), full(wvt)],
        out_specs=[pl.BlockSpec((1, lc, hw), lambda i: (i, 0, 0)), pl.BlockSpec((1, vw, lc), lambda i: (i, 0, 0))],
        compiler_params=_cparams(("parallel",)),
        name="mla_ctx_keys",
    )(ctx_ckv, ctx_kr, place, wk, wvt)


def _attn_long_kernel(q_ref, k_ref, vt_ref, kc_ref, vtc_ref, o_ref, s_a, s_b, m_a, m_b):
    i = pl.program_id(2)
    tq = q_ref.shape[1]
    lk = k_ref.shape[1]
    tk = ATTN_KEY_BLOCK
    n_blk = lk // tk
    slabs = [slice(hh * HEAD_PAD, (hh + 1) * HEAD_PAD) for hh in range(2)]
    vrows = [slice(hh * MLA_DV, (hh + 1) * MLA_DV) for hh in range(2)]
    q_t = [q_ref[0, :, sl].astype(F32).T.astype(BF16) for sl in slabs]
    crow = slice(lk, lk + kc_ref.shape[1])

    @pl.when(i == 0)
    def _():
        s_b[...] = jnp.zeros(s_b.shape, F32)
        m_b[...] = jnp.zeros(m_b.shape, F32)

    def run(s_cur, m_cur, s_prv, m_prv):
        def scores(hh, k_blk, rows, m_run):
            s = _dot(k_blk, q_t[hh])
            s_cur[hh, rows, :] = s
            return jnp.maximum(m_run, jnp.max(s.reshape(s.shape[0] // SUBLANES, SUBLANES, tq), axis=0))

        def weigh(hh, vt_blk, rows, m, acc):
            p = jnp.exp2(s_prv[hh, rows, :] - m).astype(BF16)
            lhs = jnp.concatenate([vt_blk, jnp.ones((2 * SUBLANES, vt_blk.shape[1]), BF16)], axis=0)
            return acc + _dot(lhs, p)

        ms = [jnp.max(m_prv[hh], axis=0, keepdims=True) for hh in range(2)]

        def step(j, carry):
            m_runs, accs = carry
            rows = pl.ds(pl.multiple_of(j * tk, tk), tk)
            m_runs = tuple(scores(hh, k_ref[0, rows, slabs[hh]], rows, m_runs[hh]) for hh in range(2))
            accs = tuple(weigh(hh, vt_ref[0, vrows[hh], rows], rows, ms[hh], accs[hh]) for hh in range(2))
            return m_runs, accs

        carry = ((jnp.full((SUBLANES, tq), -jnp.inf, F32),) * 2,
                 (jnp.zeros((MLA_DV + 2 * SUBLANES, tq), F32),) * 2)
        m_runs, accs = lax.fori_loop(0, n_blk, step, carry, unroll=2)
        m_runs = tuple(scores(hh, kc_ref[0, :, slabs[hh]], crow, m_runs[hh]) for hh in range(2))
        accs = tuple(weigh(hh, vtc_ref[0, vrows[hh], :], crow, ms[hh], accs[hh]) for hh in range(2))
        for hh in range(2):
            m_cur[hh] = m_runs[hh]
        outs = [acc[:MLA_DV] / acc[MLA_DV:MLA_DV + 1] for acc in accs]
        o_ref[0] = jnp.concatenate(outs, axis=0).T.astype(BF16)

    @pl.when(i % 2 == 0)
    def _():
        run(s_a, m_a, s_b, m_b)

    @pl.when(i % 2 == 1)
    def _():
        run(s_b, m_b, s_a, m_a)


def _attention_long(q, k, vt, kc, vtc):
    b, l, _ = q.shape
    lk = k.shape[1]
    lc = kc.shape[1]
    tq = ATTN_Q_TILE
    pw = 2 * HEAD_PAD
    vpw = 2 * MLA_DV
    assert lk % (2 * ATTN_KEY_BLOCK) == 0 and l % tq == 0
    n_q = l // tq
    return pl.pallas_call(
        _attn_long_kernel,
        out_shape=jax.ShapeDtypeStruct((b, l, MLA_HEADS * MLA_DV), BF16),
        grid=(b, MLA_HEADS // 2, n_q + 1),
        in_specs=[pl.BlockSpec((1, tq, pw), lambda bi, p, i: (bi, jnp.minimum(i, n_q - 1), p)),
                  pl.BlockSpec((1, lk, pw), lambda bi, p, i: (bi, 0, p)),
                  pl.BlockSpec((1, vpw, lk), lambda bi, p, i: (bi, p, 0)),
                  pl.BlockSpec((1, lc, pw), lambda bi, p, i: (bi, 0, p)),
                  pl.BlockSpec((1, vpw, lc), lambda bi, p, i: (bi, p, 0))],
        out_specs=pl.BlockSpec((1, tq, LANES), lambda bi, p, i: (bi, jnp.maximum(i - 1, 0), p)),
        scratch_shapes=[pltpu.VMEM((2, lk + lc, tq), F32), pltpu.VMEM((2, lk + lc, tq), F32),
                        pltpu.VMEM((2, SUBLANES, tq), F32), pltpu.VMEM((2, SUBLANES, tq), F32)],
        compiler_params=_cparams(("parallel", "parallel", "arbitrary")),
        name="mla_attention_long",
    )(q, k, vt, kc, vtc)


def _attn_kernel(*refs, has_ctx):
    if has_ctx:
        q_ref, k_ref, v_ref, kc_ref, vc_ref, o_ref = refs
    else:
        q_ref, k_ref, v_ref, o_ref = refs
    n_pairs = q_ref.shape[-1] // (2 * HEAD_PAD)
    for pair in range(n_pairs):
        out = None
        for hh in range(2):
            h0 = (2 * pair + hh) * HEAD_PAD
            sl = slice(h0, h0 + HEAD_PAD)
            q = q_ref[0, :, sl]
            s = _dot_nt(q, k_ref[0, :, sl])
            m = jnp.max(s, axis=-1, keepdims=True)
            if has_ctx:
                s2 = _dot_nt(q, kc_ref[0, :, sl])
                m = jnp.maximum(m, jnp.max(s2, axis=-1, keepdims=True))
            p = jnp.exp2(s - m)
            den = jnp.sum(p, axis=-1, keepdims=True)
            acc = _dot(p.astype(BF16), v_ref[0, :, sl])
            if has_ctx:
                p2 = jnp.exp2(s2 - m)
                den = den + jnp.sum(p2, axis=-1, keepdims=True)
                acc = acc + _dot(p2.astype(BF16), vc_ref[0, :, sl])
            o = acc / den
            out = o if out is None else out + o
        o_ref[0, :, pair * LANES:(pair + 1) * LANES] = out.astype(BF16)


def _attention(q, k, v, kc, vc):
    b, l, _ = q.shape
    lk = k.shape[1]
    tq = TOK_TILE
    pairs_per_step = MLA_HEADS // 2 if lk <= TOK_TILE else 1
    pw = 2 * HEAD_PAD * pairs_per_step
    has_ctx = kc is not None
    inputs = [q, k, v]
    in_specs = [pl.BlockSpec((1, tq, pw), lambda bi, p, i: (bi, i, p)),
                pl.BlockSpec((1, lk, pw), lambda bi, p, i: (bi, 0, p)),
                pl.BlockSpec((1, lk, pw), lambda bi, p, i: (bi, 0, p))]
    if has_ctx:
        lc = kc.shape[1]
        inputs += [kc, vc]
        in_specs += [pl.BlockSpec((1, lc, pw), lambda bi, p, i: (bi, 0, p))] * 2
    return pl.pallas_call(
        functools.partial(_attn_kernel, has_ctx=has_ctx),
        out_shape=jax.ShapeDtypeStruct((b, l, MLA_HEADS * MLA_DV), BF16),
        grid=(b, MLA_HEADS // 2 // pairs_per_step, l // tq),
        in_specs=in_specs,
        out_specs=pl.BlockSpec((1, tq, LANES * pairs_per_step), lambda bi, p, i: (bi, i, p)),
        compiler_params=_cparams(("parallel", "parallel", "arbitrary")),
        name="mla_attention",
    )(*inputs)


def _outproj_kernel(*refs, hgrn):
    if hgrn:
        x_ref, mod_ref, of_ref, ob_ref, gl_ref, gn_ref, b_ref, w_ref, lg_ref, lb_ref, o_ref = refs
        o = of_ref[0] + ob_ref[0]
        gate = _silu(gl_ref[0])
        gn = gn_ref[...]
        parts = []
        for h in range(HA_HEADS):
            sl = slice(h * HA_DV, (h + 1) * HA_DV)
            oh = o[:, sl]
            oh = oh * lax.rsqrt(jnp.mean(oh * oh, axis=-1, keepdims=True) + RMS_EPS) * gn
            parts.append((oh * gate[:, sl]).astype(BF16))
        a = jnp.concatenate(parts, axis=-1)
    else:
        x_ref, mod_ref, a_ref, b_ref, w_ref, lg_ref, lb_ref, o_ref = refs
        a = a_ref[0]
    half = a.shape[-1]
    y = _dot(a, w_ref[:half, :]) + _dot(b_ref[0], w_ref[half:, :])
    m = mod_ref[0]
    r = ALPHA * x_ref[0] + m[2:3] * y
    o_ref[0] = _layer_norm(r, lg_ref[...], lb_ref[...])


def _outproj(x, mod, a_inputs, b_in, w, ln_g, ln_b, hgrn):
    b, l, d = x.shape
    tm = PROJ_TILE
    per_batch = mod.shape[0] > 1
    mod_map = (lambda i, j: (i, 0, 0)) if per_batch else (lambda i, j: (0, 0, 0))
    row = lambda width: pl.BlockSpec((1, tm, width), lambda i, j: (i, j, 0))
    full = lambda a: pl.BlockSpec(a.shape, lambda i, j: (0,) * a.ndim, pipeline_mode=pl.Buffered(1))
    inputs = [x, mod]
    in_specs = [row(d), pl.BlockSpec((1, 6, d), mod_map)]
    if hgrn:
        o_f, o_b, p_h, g_norm = a_inputs
        inputs += [o_f, o_b, p_h, g_norm]
        in_specs += [row(HA_W), row(HA_W),
                     pl.BlockSpec((1, tm, HA_W), lambda i, j: (i, j, 4)), full(g_norm)]
    else:
        inputs += [a_inputs]
        in_specs += [row(a_inputs.shape[-1])]
    inputs += [b_in, w, ln_g, ln_b]
    in_specs += [row(b_in.shape[-1]), full(w), full(ln_g), full(ln_b)]
    return pl.pallas_call(
        functools.partial(_outproj_kernel, hgrn=hgrn),
        out_shape=jax.ShapeDtypeStruct((b, l, d), F32),
        grid=(b, l // tm),
        in_specs=in_specs,
        out_specs=row(d),
        compiler_params=_cparams(("parallel", "parallel")),
        name="outproj_ln",
    )(*inputs)


def _conv_kernel(pm_ref, pp_ref, pn_ref, scw_ref, cfw_ref, cfb_ref, cfg_ref, cfbeta_ref,
                 ysc_ref, ycf_ref, ext_sc, ext_cf, *, lt):
    i = pl.program_id(1)
    n_i = pl.num_programs(1)
    w = SC_W

    def sc_in(p):
        return p[:, w:2 * w] * p[:, 2 * w:3 * w]

    def cf_in(p):
        return p[:, 3 * w:3 * w + CF_W] * _sigmoid(p[:, 3 * w + CF_W:3 * w + 2 * CF_W])

    pm = pm_ref[0]
    pp = pp_ref[0]
    pn = pn_ref[0]
    has_prev = i > 0
    has_next = i < n_i - 1
    n_ext = lt + 2 * HALO
    for ext, conv_in in ((ext_sc, sc_in), (ext_cf, cf_in)):
        ext[0, 0:HALO, :] = jnp.where(has_prev, conv_in(pp), 0.0)
        ext[0, HALO:HALO + lt, :] = conv_in(pm)
        ext[0, HALO + lt:, :] = jnp.where(has_next, conv_in(pn), 0.0)
    for s in range(1, SUBLANES):
        ext_cf[s, 0:n_ext - SUBLANES, :] = ext_cf[0, s:s + n_ext - SUBLANES, :]
    sc_shifts = sorted({(HALO - SC_K // 2 + j) % SUBLANES for j in range(SC_K)} - {0})
    for s in sc_shifts:
        ext_sc[s, 0:n_ext - SUBLANES, :] = ext_sc[0, s:s + n_ext - SUBLANES, :]

    def tap(ext, off, rows):
        s = off % SUBLANES
        return ext[s, off - s:off - s + rows, :]

    rb = 32
    for r in range(lt // rb):
        base = HALO + r * rb
        acc = None
        for j in range(SC_K):
            term = jnp.tile(scw_ref[j], (rb // SUBLANES, 1)) * tap(ext_sc, base - SC_K // 2 + j, rb)
            acc = term if acc is None else acc + term
        ysc_ref[0, r * rb:(r + 1) * rb, :] = (pm[r * rb:(r + 1) * rb, 0:w] * acc).astype(BF16)
        acc = None
        for j in range(CF_K):
            term = jnp.tile(cfw_ref[j], (rb // SUBLANES, 1)) * tap(ext_cf, base - CF_K // 2 + j, rb)
            acc = term if acc is None else acc + term
        u = _layer_norm(acc + cfb_ref[...], cfg_ref[...], cfbeta_ref[...])
        ycf_ref[0, r * rb:(r + 1) * rb, :] = _silu(u).astype(BF16)


def _conv_mixers(p1, sc_w, cf_w, cf_b, cf_g, cf_beta):
    b, l, width = p1.shape
    lt = TOK_TILE
    hb = lt // HALO
    n_h = l // HALO
    sc_w = jnp.broadcast_to(sc_w[:, None, :], (sc_w.shape[0], SUBLANES, sc_w.shape[1]))
    cf_w = jnp.broadcast_to(cf_w[:, None, :], (cf_w.shape[0], SUBLANES, cf_w.shape[1]))
    full = lambda a: pl.BlockSpec(a.shape, lambda bi, i: (0,) * a.ndim)
    return pl.pallas_call(
        functools.partial(_conv_kernel, lt=lt),
        out_shape=[jax.ShapeDtypeStruct((b, l, SC_W), BF16), jax.ShapeDtypeStruct((b, l, CF_W), BF16)],
        grid=(b, l // lt),
        in_specs=[
            pl.BlockSpec((1, lt, width), lambda bi, i: (bi, i, 0)),
            pl.BlockSpec((1, HALO, width), lambda bi, i: (bi, jnp.maximum(i * hb - 1, 0), 0)),
            pl.BlockSpec((1, HALO, width), lambda bi, i: (bi, jnp.minimum((i + 1) * hb, n_h - 1), 0)),
            full(sc_w), full(cf_w), full(cf_b), full(cf_g), full(cf_beta),
        ],
        out_specs=[pl.BlockSpec((1, lt, SC_W), lambda bi, i: (bi, i, 0)),
                   pl.BlockSpec((1, lt, CF_W), lambda bi, i: (bi, i, 0))],
        scratch_shapes=[pltpu.VMEM((SUBLANES, lt + 2 * HALO, SC_W), F32),
                        pltpu.VMEM((SUBLANES, lt + 2 * HALO, CF_W), F32)],
        compiler_params=_cparams(("parallel", "parallel")),
        name="conv_mixers",
    )(p1, p1, p1, sc_w, cf_w, cf_b, cf_g, cf_beta)


def _route_t(lt):
    t = lt.shape[1]
    row = lax.broadcasted_iota(jnp.int32, (SUBLANES, t), 0).astype(F32)
    neg = -jnp.inf
    big = float(LANES)
    gl = jnp.where(row < N_GROUPS, lt[N_EXPERTS:N_EXPERTS + SUBLANES], neg)
    gmax = jnp.max(gl, axis=0, keepdims=True)
    p_g = 1.0 / jnp.sum(jnp.exp(gl - gmax), axis=0, keepdims=True)
    g_sel = jnp.min(jnp.where(gl == gmax, row, big), axis=0, keepdims=True)
    el = lt[0:EXP_PER_GROUP]
    for gi in range(1, N_GROUPS):
        el = jnp.where(g_sel == gi, lt[gi * EXP_PER_GROUP:(gi + 1) * EXP_PER_GROUP], el)
    v1 = jnp.max(el, axis=0, keepdims=True)
    i1 = jnp.min(jnp.where(el == v1, row, big), axis=0, keepdims=True)
    el2 = jnp.where(row == i1, neg, el)
    v2 = jnp.max(el2, axis=0, keepdims=True)
    i2 = jnp.min(jnp.where(el2 == v2, row, big), axis=0, keepdims=True)
    e2 = jnp.exp(v2 - v1)
    w1 = p_g / (1.0 + e2)
    w2 = p_g * e2 / (1.0 + e2)
    comb = jnp.where(row == i1, w1, 0.0) + jnp.where(row == i2, w2, 0.0)
    onehot = jnp.where(row == g_sel, 1.0, 0.0)
    return onehot, comb


def _moe_kernel(xc_ref, xl_ref, mod_ref, wr_ref, tri_ref, wg_ref, wu_ref, wd_ref, lg_ref, lb_ref, oc_ref, ol_ref,
                hb_scr, os_scr, pt_scr, cc_scr, *, n_ctx):
    is_ctx = pl.program_id(0) < n_ctx
    m = mod_ref[0]
    tm, cap = pt_scr.shape
    d = xc_ref.shape[-1]
    half = d // 2
    cw = EXP_PER_GROUP * EXP_FF

    h = jnp.where(is_ctx, xc_ref[0], xl_ref[0]) * (1.0 + m[4:5]) + m[3:4]
    h_hi = h.astype(BF16)
    hb_scr[...] = h_hi
    h_lo = (h - h_hi.astype(F32)).astype(BF16)
    l2 = _dot(h_hi, wr_ref[...])
    logits = l2[:, :LANES] + l2[:, LANES:] + _dot(h_lo, wr_ref[:, :LANES])
    onehot, comb = _route_t(logits.T)
    rank = _dot(onehot.astype(BF16), tri_ref[...])
    cnt = jnp.sum(onehot, axis=1, keepdims=True)
    n_blocks = jnp.floor((cnt + (MOE_BLOCK - 0.5)) * (1.0 / MOE_BLOCK))
    padded = n_blocks * MOE_BLOCK
    start = jnp.zeros((1, 1), F32)
    first_block = jnp.zeros((1, 1), F32)
    dest = jnp.zeros((1, tm), F32)
    first, count = [], []
    for gi in range(N_GROUPS):
        dest = dest + onehot[gi:gi + 1] * (start + rank[gi:gi + 1])
        first.append(first_block[0, 0].astype(jnp.int32))
        count.append(n_blocks[gi, 0].astype(jnp.int32))
        start = start + padded[gi:gi + 1]
        first_block = first_block + n_blocks[gi:gi + 1]
    aux = jnp.concatenate([jnp.broadcast_to(dest, (SUBLANES, tm)), comb,
                           jnp.zeros((LANES - 2 * SUBLANES, tm), F32)], axis=0).T
    lane = lax.broadcasted_iota(jnp.int32, (tm, LANES), 1)
    comb_tok = jnp.where((lane >= SUBLANES) & (lane < 2 * SUBLANES), aux, 0.0)
    c_hi = comb_tok.astype(BF16)
    c_mid = (comb_tok - c_hi.astype(F32)).astype(BF16)
    slot_t = lax.broadcasted_iota(jnp.int32, (tm, cap), 1).astype(F32)
    pt_scr[...] = jnp.where(slot_t == aux[:, 0:1], 1.0, 0.0).astype(BF16)
    cc_scr[...] = jnp.concatenate([c_hi, c_mid], axis=1)
    os_scr[...] = jnp.zeros_like(os_scr)
    block_slot = lax.broadcasted_iota(jnp.int32, (MOE_BLOCK, tm), 0).astype(F32)

    for gi in range(N_GROUPS):
        cols = slice(gi * cw, (gi + 1) * cw)

        def block(i, carry, gi=gi, cols=cols):
            blk = first[gi] + i
            r0 = pl.multiple_of(blk * MOE_BLOCK, 2 * SUBLANES)
            rows = pl.ds(r0, MOE_BLOCK)
            perm = jnp.where(block_slot + (blk * MOE_BLOCK).astype(F32) == dest, 1.0, 0.0).astype(BF16)
            hs = _dot(perm, hb_scr[...]).astype(BF16)
            cs2 = _dot(perm, cc_scr[...])
            cs = cs2[:, :LANES] + cs2[:, LANES:]
            hid = _silu(_dot(hs, wg_ref[:, cols])) * _dot(hs, wu_ref[:, cols])
            hid = jnp.concatenate(
                [hid[:, e * EXP_FF:(e + 1) * EXP_FF] * cs[:, SUBLANES + e:SUBLANES + e + 1]
                 for e in range(EXP_PER_GROUP)], axis=1)
            os_scr[rows, :] = _dot(hid.astype(BF16), wd_ref[cols, :]).astype(BF16)
            return carry

        lax.fori_loop(0, count[gi], block, 0)

    y = jnp.concatenate([_dot(pt_scr[...], os_scr[:, :half]), _dot(pt_scr[...], os_scr[:, half:])], axis=1)
    r = ALPHA * jnp.where(is_ctx, xc_ref[0], xl_ref[0]) + m[5:6] * y
    out = _layer_norm(r, lg_ref[...], lb_ref[...])

    @pl.when(is_ctx)
    def _():
        oc_ref[0] = out

    @pl.when(jnp.logical_not(is_ctx))
    def _():
        ol_ref[0] = out


def _moe(x_ctx, x_lat, mods, wr, wg, wu, wd, ln_g, ln_b):
    tm = MOE_TILE
    d = x_lat.shape[-1]
    x_c = x_ctx.reshape(-1, tm, d)
    n_ctx = x_c.shape[0]
    b, l, _ = x_lat.shape
    per_b = l // tm
    cap = (tm + N_GROUPS * (MOE_BLOCK - 1)) // MOE_BLOCK * MOE_BLOCK
    cap = -(-cap // LANES) * LANES
    tri = jnp.asarray(np.triu(np.ones((tm, tm), np.float32), k=1), BF16)
    ctx_map = lambda t: (jnp.minimum(t, n_ctx - 1), 0, 0)
    lat_t = lambda t: jnp.maximum(t - n_ctx, 0)
    lat_map = lambda t: (lat_t(t) // per_b, lat_t(t) % per_b, 0)
    mod_map = lambda t: (jnp.where(t < n_ctx, b, lat_t(t) // per_b), 0, 0)
    full = lambda a: pl.BlockSpec(a.shape, lambda t: (0,) * a.ndim, pipeline_mode=pl.Buffered(1))
    y_c, y_l = pl.pallas_call(
        functools.partial(_moe_kernel, n_ctx=n_ctx),
        out_shape=[jax.ShapeDtypeStruct(x_c.shape, F32), jax.ShapeDtypeStruct(x_lat.shape, F32)],
        grid=(n_ctx + b * per_b,),
        in_specs=[
            pl.BlockSpec((1, tm, d), ctx_map), pl.BlockSpec((1, tm, d), lat_map), pl.BlockSpec((1, 6, d), mod_map),
            full(wr), full(tri), full(wg), full(wu), full(wd), full(ln_g), full(ln_b),
        ],
        out_specs=[pl.BlockSpec((1, tm, d), ctx_map), pl.BlockSpec((1, tm, d), lat_map)],
        scratch_shapes=[pltpu.VMEM((tm, d), BF16), pltpu.VMEM((cap, d), BF16),
                        pltpu.VMEM((tm, cap), BF16), pltpu.VMEM((tm, 2 * LANES), BF16)],
        compiler_params=_cparams(("arbitrary",)),
        name="hier_moe_ln",
    )(x_c, x_lat, mods, wr, tri, wg, wu, wd, ln_g, ln_b)
    return y_c.reshape(x_ctx.shape), y_l


def _rope_swap_perm():
    idx = np.arange(MLA_DR)
    return idx ^ (MLA_DR // 4)


def _rope_tables(n_tok):
    rows = n_tok // GRID_W
    pos_r = jnp.repeat(jnp.arange(rows, dtype=F32), GRID_W)
    pos_c = (jnp.arange(rows * GRID_W) % GRID_W).astype(F32)
    n_freq = MLA_DR // 4
    inv = ROPE_BASE ** (-jnp.arange(n_freq, dtype=F32) / n_freq)
    ang = jnp.stack([pos_r[:, None] * inv, pos_c[:, None] * inv], axis=1)
    cos, sin = jnp.cos(ang), jnp.sin(ang)
    cos32 = jnp.stack([cos, cos], axis=2).reshape(n_tok, MLA_DR)
    sin32 = jnp.stack([-sin, sin], axis=2).reshape(n_tok, MLA_DR)
    pad_hi = LANES - MLA_DN - MLA_DR
    cos_t = jnp.concatenate([jnp.ones((n_tok, MLA_DN), F32), cos32, jnp.zeros((n_tok, pad_hi), F32)], axis=1)
    sin_t = jnp.concatenate([jnp.zeros((n_tok, MLA_DN), F32), sin32, jnp.zeros((n_tok, pad_hi), F32)], axis=1)
    return cos_t, sin_t


def _ab_weights(w_in, w_uq, w_ukv):
    perm = _rope_swap_perm()
    pad_hi = LANES - MLA_DN - MLA_DR
    w_h = w_in[:, :5 * HA_W]
    cq = w_in[:, 5 * HA_W:5 * HA_W + MLA_Q_RANK]
    ckv = w_in[:, 5 * HA_W + MLA_Q_RANK:5 * HA_W + MLA_Q_RANK + MLA_KV_RANK]
    kr = w_in[:, 5 * HA_W + MLA_Q_RANK + MLA_KV_RANK:]
    d = w_in.shape[0]
    z_lo = jnp.zeros((d, MLA_DN), F32)
    z_hi = jnp.zeros((d, pad_hi), F32)
    w_m = jnp.concatenate([cq, ckv, z_lo, kr, z_hi, z_lo, kr[:, perm], z_hi], axis=1)
    uq = w_uq.reshape(MLA_Q_RANK, MLA_HEADS, MLA_DN + MLA_DR)
    q_nope, q_rope = uq[..., :MLA_DN], uq[..., MLA_DN:]
    zq_hi = jnp.zeros((MLA_Q_RANK, MLA_HEADS, pad_hi), F32)
    wq1 = jnp.concatenate([q_nope, q_rope, zq_hi], axis=-1).reshape(MLA_Q_RANK, -1)
    wq2 = jnp.concatenate([jnp.zeros_like(q_nope), q_rope[..., perm], zq_hi], axis=-1).reshape(MLA_Q_RANK, -1)
    ukv = w_ukv.reshape(MLA_KV_RANK, MLA_HEADS, MLA_DN + MLA_DV)
    k_nope, v = ukv[..., :MLA_DN], ukv[..., MLA_DN:]
    wk = jnp.concatenate([k_nope, jnp.zeros((MLA_KV_RANK, MLA_HEADS, LANES - MLA_DN), F32)], axis=-1)
    wk = wk.reshape(MLA_KV_RANK, -1)
    v_pairs = v.reshape(MLA_KV_RANK, MLA_HEADS // 2, 2, MLA_DV)
    zv = jnp.zeros_like(v_pairs[:, :, 0])
    wv = jnp.stack([jnp.concatenate([v_pairs[:, :, 0], zv], axis=-1),
                    jnp.concatenate([zv, v_pairs[:, :, 1]], axis=-1)], axis=2).reshape(MLA_KV_RANK, -1)
    wvt = v.reshape(MLA_KV_RANK, MLA_HEADS * MLA_DV).T
    bf = lambda a: a.astype(BF16)
    return bf(w_h), bf(w_m), bf(wq1), bf(wq2), bf(wk), bf(wv), bf(wvt)


def _moe_weights(w_group, w_expert, w_gate, w_up, w_down):
    d = w_group.shape[0]
    wr = jnp.concatenate([w_expert, w_group, jnp.zeros((d, LANES - N_EXPERTS - N_GROUPS), F32)], axis=1)
    wr_hi = wr.astype(BF16)
    wr = jnp.concatenate([wr_hi, (wr - wr_hi.astype(F32)).astype(BF16)], axis=1)
    wg = jnp.transpose(w_gate, (1, 0, 2)).reshape(d, N_EXPERTS * EXP_FF).astype(BF16)
    wu = jnp.transpose(w_up, (1, 0, 2)).reshape(d, N_EXPERTS * EXP_FF).astype(BF16)
    wd = w_down.reshape(N_EXPERTS * EXP_FF, d).astype(BF16)
    return wr, wg, wu, wd


def kernel(x_prompt, x_sample, state_hgrn_fwd, state_hgrn_bwd, cache_mla_ckv, cache_mla_krope, c, c_ctx, mod_w, mod_b, ln_g, ln_b, ab_w_in, ab_w_out, hgrn_lb_logits, hgrn_norm_g, mla_q_norm_g, mla_w_uq, mla_kv_norm_g, mla_w_ukv, cd_w_in, cd_w_out, sc_conv_w, cf_conv_w, cf_conv_b, cf_ln_g, cf_ln_b, moe_w_group, moe_w_expert, moe_w_gate, moe_w_up, moe_w_down):
    dec_b = x_sample.shape[0]
    d = D_MODEL
    cvec = jnp.concatenate([c, c_ctx[None, :], jnp.zeros((SUBLANES - dec_b - 1, d), F32)], axis=0)
    mods = _mod_vectors(cvec, mod_w, mod_b)
    rope_tabs = _rope_tables(x_sample.shape[1])
    xp, xs = x_prompt, x_sample
    new_sf = new_sb = new_ckv = new_kr = None
    for l in range(DEPTH):
        mod_lat = mods[l, :dec_b].reshape(dec_b, 6, d)
        mod_ctx = mods[l, dec_b:dec_b + 1].reshape(1, 6, d)
        row = lambda a: a.reshape(1, -1)
        fold = lambda a: a.reshape(-1, PROJ_TILE, a.shape[-1])
        unfold = lambda a: a.reshape(x_prompt.shape[0], -1, a.shape[-1])
        if l % 2 == 0:
            e = l // 2
            w_h, w_m, wq1, wq2, wk, wv, wvt = _ab_weights(ab_w_in[e], mla_w_uq[e], mla_w_ukv[e])
            w_out = ab_w_out[e].astype(BF16)
            qg, kvg, gn = row(mla_q_norm_g[e]), row(mla_kv_norm_g[e]), row(hgrn_norm_g[e])
            ph_p, q_p, k_p, v_p, ckv_p, kr_p = map(unfold, _ab_inproj(
                fold(xp), mod_ctx, w_h, w_m, None, qg, kvg, wq1, None, wk, wv))
            of_p, ob_p, sf, sb = _hgrn(ph_p, hgrn_lb_logits, None, None, e)
            om_p = _attention(q_p, k_p, v_p, None, None)
            xp = unfold(_outproj(fold(xp), mod_ctx, (fold(of_p), fold(ob_p), fold(ph_p), gn), fold(om_p), w_out,
                                 row(ln_g[l, 0]), row(ln_b[l, 0]), True))
            ph_s, q_s, k_s, vt_s, _, _ = _ab_inproj(xs, mod_lat, w_h, w_m, rope_tabs, qg, kvg, wq1, wq2, wk, wvt)
            of_s, ob_s, _, _ = _hgrn(ph_s, hgrn_lb_logits, state_hgrn_fwd[:, e], state_hgrn_bwd[:, e], e)
            kc, vtc = _mla_ctx(cache_mla_ckv[:, e], cache_mla_krope[:, e], wk, wvt)
            om_s = _attention_long(q_s, k_s, vt_s, kc, vtc)
            xs = _outproj(xs, mod_lat, (of_s, ob_s, ph_s, gn), om_s, w_out, row(ln_g[l, 0]), row(ln_b[l, 0]), True)
            new_sf, new_sb, new_ckv = sf, sb, ckv_p
            new_kr = kr_p[:, :, MLA_DN:MLA_DN + MLA_DR]
        else:
            jx = l // 2
            w1 = cd_w_in[jx].astype(BF16)
            w_out = cd_w_out[jx].astype(BF16)
            cd = (sc_conv_w[jx], cf_conv_w[jx], row(cf_conv_b[jx]), row(cf_ln_g[jx]), row(cf_ln_b[jx]))
            (p1_p,) = _inproj(fold(xp), mod_ctx, [w1])
            ysc_p, ycf_p = _conv_mixers(unfold(p1_p), *cd)
            xp = unfold(_outproj(fold(xp), mod_ctx, fold(ysc_p), fold(ycf_p), w_out,
                                 row(ln_g[l, 0]), row(ln_b[l, 0]), False))
            (p1_s,) = _inproj(xs, mod_lat, [w1])
            ysc_s, ycf_s = _conv_mixers(p1_s, *cd)
            xs = _outproj(xs, mod_lat, ysc_s, ycf_s, w_out, row(ln_g[l, 0]), row(ln_b[l, 0]), False)
        wr, wg, wu, wd = _moe_weights(moe_w_group[l], moe_w_expert[l], moe_w_gate[l], moe_w_up[l], moe_w_down[l])
        mods_l = mods[l, :dec_b + 1].reshape(dec_b + 1, 6, d)
        xp, xs = _moe(xp, xs, mods_l, wr, wg, wu, wd, row(ln_g[l, 1]), row(ln_b[l, 1]))
    return (xp, xs, new_sf[:, None], new_sb[:, None], new_ckv[:, None], new_kr[:, None])
```

```python
import functools

import numpy as np
import jax
import jax.numpy as jnp
from jax import lax
from jax.experimental import pallas as pl
from jax.experimental.pallas import tpu as pltpu

F32 = jnp.float32
BF16 = jnp.bfloat16
HIGHEST = lax.Precision.HIGHEST

D_MODEL = 1024
DEPTH = 2
GRID_W = 64
N_EVEN = (DEPTH + 1) // 2
HA_HEADS = 4
HA_DK = 128
HA_DV = 128
HA_W = HA_HEADS * HA_DK
CHUNK = 32
MLA_HEADS = 8
MLA_DN = 64
MLA_DR = 32
MLA_DV = 64
MLA_Q_RANK = 384
MLA_KV_RANK = 256
ROPE_BASE = 10000.0
SC_W = 512
SC_K = 3
CF_W = 512
CF_K = 31
N_GROUPS = 4
EXP_PER_GROUP = 8
N_EXPERTS = N_GROUPS * EXP_PER_GROUP
EXP_FF = 128
ALPHA = (2.0 * DEPTH) ** 0.25
LOG2_E = 1.4426950408889634
LN_EPS = 1e-5
RMS_EPS = 1e-6

LANES = 128
SUBLANES = 8
VMEM_LIMIT = 56 * 1024 * 1024

HEAD_PAD = LANES
MLA_SLAB = MLA_Q_RANK + MLA_KV_RANK + 2 * LANES
PROJ_TILE = 512
SEQ_TILE = 512
TOK_TILE = 256
GROUP_ROWS = 128
N_LEVELS = 5
SAFE_LOG_DECAY = 60.0
ATTN_KEY_BLOCK = 512
ATTN_Q_TILE = 256
MOE_TILE = 512
MOE_BLOCK = 144
HALO = 16


def _cparams(sem):
    return pltpu.CompilerParams(dimension_semantics=sem, vmem_limit_bytes=VMEM_LIMIT)


def _sigmoid(x):
    return 0.5 * jnp.tanh(0.5 * x) + 0.5


def _silu(x):
    half = 0.5 * x
    return half * jnp.tanh(half) + half


def _layer_norm(r, g, b):
    mu = jnp.mean(r, axis=-1, keepdims=True)
    d = r - mu
    var = jnp.mean(d * d, axis=-1, keepdims=True)
    return d * lax.rsqrt(var + LN_EPS) * g + b


def _dot(a, b):
    return jnp.dot(a, b, preferred_element_type=F32)


def _dot_nt(a, b):
    return lax.dot_general(a, b, (((1,), (1,)), ((), ())), preferred_element_type=F32)


def _dot_tn(a, b):
    return lax.dot_general(a, b, (((0,), (0,)), ((), ())), preferred_element_type=F32)


def _mod_kernel(c_ref, w_ref, b_ref, o_ref):
    s = _silu(c_ref[...])
    o_ref[0] = jnp.dot(s, w_ref[0], precision=HIGHEST, preferred_element_type=F32) + b_ref[0]


def _mod_vectors(cvec, mod_w, mod_b):
    n_out = mod_w.shape[-1]
    tn = 1536
    return pl.pallas_call(
        _mod_kernel,
        out_shape=jax.ShapeDtypeStruct((DEPTH, SUBLANES, n_out), F32),
        grid=(DEPTH, n_out // tn),
        in_specs=[
            pl.BlockSpec((SUBLANES, D_MODEL), lambda l, j: (0, 0)),
            pl.BlockSpec((1, D_MODEL, tn), lambda l, j: (l, 0, j)),
            pl.BlockSpec((1, 1, tn), lambda l, j: (l, 0, j)),
        ],
        out_specs=pl.BlockSpec((1, SUBLANES, tn), lambda l, j: (l, 0, j)),
        compiler_params=_cparams(("arbitrary", "arbitrary")),
        name="mod_vectors",
    )(cvec, mod_w, mod_b.reshape(DEPTH, 1, n_out))


def _inproj_kernel(*refs, n_w):
    x_ref, mod_ref = refs[0], refs[1]
    w_refs = refs[2:2 + n_w]
    o_refs = refs[2 + n_w:]
    m = mod_ref[0]
    h = (x_ref[0] * (1.0 + m[1:2]) + m[0:1]).astype(BF16)
    for w_ref, o_ref in zip(w_refs, o_refs):
        o_ref[0] = _dot(h, w_ref[...])


def _inproj(x, mod, weights):
    b, l, d = x.shape
    tm = PROJ_TILE
    per_batch = mod.shape[0] > 1
    mod_map = (lambda i, j: (i, 0, 0)) if per_batch else (lambda i, j: (0, 0, 0))
    in_specs = [pl.BlockSpec((1, tm, d), lambda i, j: (i, j, 0)),
                pl.BlockSpec((1, 6, d), mod_map)]
    in_specs += [pl.BlockSpec(w.shape, lambda i, j: (0, 0), pipeline_mode=pl.Buffered(1)) for w in weights]
    out_shape = [jax.ShapeDtypeStruct((b, l, w.shape[1]), F32) for w in weights]
    out_specs = [pl.BlockSpec((1, tm, w.shape[1]), lambda i, j: (i, j, 0)) for w in weights]
    return pl.pallas_call(
        functools.partial(_inproj_kernel, n_w=len(weights)),
        out_shape=out_shape,
        grid=(b, l // tm),
        in_specs=in_specs,
        out_specs=out_specs,
        compiler_params=_cparams(("parallel", "parallel")),
        name="modulate_inproj",
    )(x, mod, *weights)


def _hgrn_tables():
    n = GROUP_ROWS
    t = np.arange(n)[:, None]
    j = np.arange(n)[None, :]
    same_chunk = (t // CHUNK) == (j // CHUNK)
    e_f, e_b = [], []
    lv_f = np.full((n, n), -1, np.int32)
    for lvl in range(N_LEVELS):
        m = CHUNK >> (lvl + 1)
        blk0 = (t // (2 * m)) * (2 * m)
        r = blk0 + m - 1
        upper = t > r
        ef = np.where(upper, (j > r) & (j <= t), (j > t) & (j <= r))
        r2 = blk0 + m
        lower = t < r2
        eb = np.where(lower, (j >= t) & (j < r2), (j >= r2) & (j < t))
        e_f.append(ef)
        e_b.append(eb)
        s = np.arange(n)[None, :]
        same_blk = (t // (2 * m)) == (s // (2 * m))
        q_side = (t % (2 * m)) >= m
        k_side = (s % (2 * m)) < m
        lv_f[same_blk & q_side & k_side] = lvl
    lv_f[np.arange(n), np.arange(n)] = N_LEVELS
    e_f = np.concatenate(e_f, axis=0).astype(np.float32)
    e_b = np.concatenate(e_b, axis=0).astype(np.float32)
    e = np.stack([e_f, e_b])
    cum = np.stack([same_chunk & (j <= t), same_chunk & (j >= t)]).astype(np.float32)
    lv = np.stack([lv_f, lv_f.T])
    return e, cum, lv


def _hgrn_kernel(qf_ref, vf_ref, ff_ref, qb_ref, vb_ref, fb_ref, lbl_ref, e_ref, cum_ref, lv_ref, s0f_ref, s0b_ref,
                 of_ref, ob_ref, sf_ref, sb_ref,
                 st_scr, qe_scr, kd_scr, sc_scr, v_scr, oi_scr, dec_scr, *, lt, slot, has_state):
    i = pl.program_id(1)
    n_i = pl.num_programs(1)
    n_chunks = lt // CHUNK
    n_groups = lt // GROUP_ROWS
    chunks_per_group = GROUP_ROWS // CHUNK

    @pl.when(i == 0)
    def _():
        for d, s0_ref in enumerate((s0f_ref, s0b_ref)):
            for h in range(HA_HEADS):
                if has_state:
                    st_scr[d, h] = s0_ref[0, h].T
                else:
                    st_scr[d, h] = jnp.zeros((HA_DV, HA_DK), F32)

    lg = lbl_ref[...]
    n_slots = lg.shape[0]
    mx = lg[0]
    for s in range(1, n_slots):
        mx = jnp.maximum(mx, lg[s])
    ex = [jnp.exp(lg[s] - mx) for s in range(n_slots)]
    den = ex[0]
    for s in range(1, n_slots):
        den = den + ex[s]
    num = ex[0]
    for s in range(1, slot + 1):
        num = num + ex[s]
    lb_all = num / den

    dirs = ((qf_ref, vf_ref, ff_ref), (qb_ref, vb_ref, fb_ref))
    head_cols = [slice(h * HA_DK, (h + 1) * HA_DK) for h in range(HA_HEADS)]

    def gates(d, rows, cols):
        q_ref, _, f_ref = dirs[d]
        q = _silu(q_ref[0, rows, cols])
        lb_h = lb_all[d:d + 1, cols]
        f = lb_h + (1.0 - lb_h) * _sigmoid(f_ref[0, rows, cols])
        g = jnp.log(f)
        g_hi = g.astype(BF16)
        g_lo = (g - g_hi.astype(F32)).astype(BF16)
        return q, 1.0 - f, g_hi, g_lo

    def group_step(grp, tot_min):
        r0 = pl.multiple_of(grp * GROUP_ROWS, GROUP_ROWS)
        rows = pl.ds(r0, GROUP_ROWS)
        for d in range(2):
            cum = cum_ref[d]
            lv = lv_ref[d]
            for cols in head_cols:
                q, k, g_hi, g_lo = gates(d, rows, cols)
                bcum = _dot(cum, g_hi) + _dot(cum, g_lo)
                v_scr[d, rows, cols] = dirs[d][1][0, rows, cols].astype(BF16)
                qe = (q * jnp.exp(bcum)).astype(BF16)
                qe_scr[d, rows, cols] = qe
                kx = []
                for cc in range(chunks_per_group):
                    edge = cc * CHUNK + (CHUNK - 1 if d == 0 else 0)
                    tot = bcum[edge:edge + 1]
                    b_c = bcum[cc * CHUNK:(cc + 1) * CHUNK]
                    kd = k[cc * CHUNK:(cc + 1) * CHUNK] * jnp.exp(tot - b_c)
                    kd_scr[d, pl.ds(r0 + cc * CHUNK, CHUNK), cols] = kd.astype(BF16)
                    dec_scr[d, grp * chunks_per_group + cc, :, cols] = jnp.exp(tot)
                    kx.append(kd * jnp.exp(-tot))
                    tot_min = jnp.minimum(tot_min, tot)
                p = _dot_nt(qe, jnp.concatenate(kx, axis=0).astype(BF16))
                sc_scr[d, rows, cols] = jnp.where(lv >= 0, p, 0.0).astype(BF16)
        return tot_min

    tot_min = lax.fori_loop(0, n_groups, group_step, jnp.zeros((1, HA_DK), F32), unroll=True)

    def one_factor():
        for d in range(2):
            for grp in range(n_groups):
                rows = slice(grp * GROUP_ROWS, (grp + 1) * GROUP_ROWS)
                for cols in head_cols:
                    oi_scr[d, rows, cols] = _dot(sc_scr[d, rows, cols], v_scr[d, rows, cols])

    def per_level():
        for d in range(2):

            def group_step(grp, carry, d=d):
                rows = pl.ds(pl.multiple_of(grp * GROUP_ROWS, GROUP_ROWS), GROUP_ROWS)
                lv = lv_ref[d]
                for cols in head_cols:
                    q, k, g_hi, g_lo = gates(d, rows, cols)
                    x = jnp.exp(_dot(e_ref[d], g_hi) + _dot(e_ref[d], g_lo))
                    sc = jnp.where(lv == N_LEVELS, _dot_nt(q.astype(BF16), k.astype(BF16)), 0.0)
                    for lvl in range(N_LEVELS):
                        xl = x[lvl * GROUP_ROWS:(lvl + 1) * GROUP_ROWS]
                        p = _dot_nt((q * xl).astype(BF16), (k * xl).astype(BF16))
                        sc = jnp.where(lv == lvl, p, sc)
                    oi_scr[d, rows, cols] = _dot(sc.astype(BF16), v_scr[d, rows, cols])
                return carry

            lax.fori_loop(0, n_groups, group_step, 0)

    lax.cond(jnp.min(tot_min) > -SAFE_LOG_DECAY, one_factor, per_level)

    out_refs = (of_ref, ob_ref)

    def chunk_step(c, carry):
        for d in range(2):
            cc = c if d == 0 else n_chunks - 1 - c
            r0 = pl.multiple_of(cc * CHUNK, CHUNK)
            rows = pl.ds(r0, CHUNK)
            for h in range(HA_HEADS):
                cols = slice(h * HA_DK, (h + 1) * HA_DK)
                st = st_scr[d, h]
                o_state = _dot_nt(qe_scr[d, rows, cols], st.astype(BF16))
                out_refs[d][0, rows, cols] = oi_scr[d, rows, cols] + o_state
                upd = _dot_tn(v_scr[d, rows, cols], kd_scr[d, rows, cols])
                st_scr[d, h] = st * dec_scr[d, cc, :, cols] + upd
        return carry

    lax.fori_loop(0, n_chunks, chunk_step, 0, unroll=True)

    @pl.when(i == n_i - 1)
    def _():
        for d, s_ref in enumerate((sf_ref, sb_ref)):
            for h in range(HA_HEADS):
                s_ref[0, h] = st_scr[d, h].T


def _hgrn(p_h, lb_logits, s0_f, s0_b, slot):
    b, l, _ = p_h.shape
    lt = min(SEQ_TILE, l)
    n_t = l // lt
    has_state = s0_f is not None
    if not has_state:
        s0_f = jnp.zeros((1, HA_HEADS, HA_DK, HA_DV), F32)
        s0_b = s0_f
    e_np, cum_np, lv_np = _hgrn_tables()
    e_mat = jnp.asarray(e_np, BF16)
    cum = jnp.asarray(cum_np, BF16)
    lv = jnp.asarray(lv_np, jnp.int32)
    w = HA_W

    def sec(idx, rev):
        if rev:
            return pl.BlockSpec((1, lt, w), lambda bi, i: (bi, n_t - 1 - i, idx))
        return pl.BlockSpec((1, lt, w), lambda bi, i: (bi, i, idx))

    state_map = (lambda bi, i: (bi, 0, 0, 0)) if has_state else (lambda bi, i: (0, 0, 0, 0))
    in_specs = [
        sec(0, False), sec(3, False), sec(1, False),
        sec(0, True), sec(3, True), sec(2, True),
        pl.BlockSpec(lb_logits.shape, lambda bi, i: (0, 0, 0)),
        pl.BlockSpec(e_mat.shape, lambda bi, i: (0, 0, 0)),
        pl.BlockSpec(cum.shape, lambda bi, i: (0, 0, 0)),
        pl.BlockSpec(lv.shape, lambda bi, i: (0, 0, 0)),
        pl.BlockSpec((1, HA_HEADS, HA_DK, HA_DV), state_map),
        pl.BlockSpec((1, HA_HEADS, HA_DK, HA_DV), state_map),
    ]
    out_shape = [
        jax.ShapeDtypeStruct((b, l, w), F32),
        jax.ShapeDtypeStruct((b, l, w), F32),
        jax.ShapeDtypeStruct((b, HA_HEADS, HA_DK, HA_DV), F32),
        jax.ShapeDtypeStruct((b, HA_HEADS, HA_DK, HA_DV), F32),
    ]
    out_specs = [
        pl.BlockSpec((1, lt, w), lambda bi, i: (bi, i, 0)),
        pl.BlockSpec((1, lt, w), lambda bi, i: (bi, n_t - 1 - i, 0)),
        pl.BlockSpec((1, HA_HEADS, HA_DK, HA_DV), lambda bi, i: (bi, 0, 0, 0)),
        pl.BlockSpec((1, HA_HEADS, HA_DK, HA_DV), lambda bi, i: (bi, 0, 0, 0)),
    ]
    scratch = [
        pltpu.VMEM((2, HA_HEADS, HA_DV, HA_DK), F32),
        pltpu.VMEM((2, lt, w), BF16),
        pltpu.VMEM((2, lt, w), BF16),
        pltpu.VMEM((2, lt, w), BF16),
        pltpu.VMEM((2, lt, w), BF16),
        pltpu.VMEM((2, lt, w), F32),
        pltpu.VMEM((2, lt // CHUNK, 1, w), F32),
    ]
    return pl.pallas_call(
        functools.partial(_hgrn_kernel, lt=lt, slot=slot, has_state=has_state),
        out_shape=out_shape,
        grid=(b, n_t),
        in_specs=in_specs,
        out_specs=out_specs,
        scratch_shapes=scratch,
        compiler_params=_cparams(("parallel", "arbitrary")),
        name="hgrn2_scan",
    )(p_h, p_h, p_h, p_h, p_h, p_h, lb_logits, e_mat, cum, lv, s0_f, s0_b)


def _ab_inproj_kernel(*refs, rope, key_major):
    if rope:
        (x_ref, mod_ref, wh_ref, wm_ref, cos_ref, sin_ref, qg_ref, kvg_ref, wq1_ref, wq2_ref, wk_ref,
         wv_ref) = refs[:12]
        ph_ref, q_ref, k_ref, v_ref, ckv_ref, kr_ref = refs[12:]
    else:
        x_ref, mod_ref, wh_ref, wm_ref, qg_ref, kvg_ref, wq1_ref, wk_ref, wv_ref = refs[:9]
        ph_ref, q_ref, k_ref, v_ref, ckv_ref, kr_ref = refs[9:]
    scale = (MLA_DN + MLA_DR) ** -0.5 * LOG2_E
    mod = mod_ref[0]
    hmod = (x_ref[0] * (1.0 + mod[1:2]) + mod[0:1]).astype(BF16)
    ph_ref[0] = _dot(hmod, wh_ref[...])
    pm = _dot(hmod, wm_ref[...])
    kr_ref[0] = pm[:, MLA_Q_RANK + MLA_KV_RANK:MLA_Q_RANK + MLA_KV_RANK + LANES]
    cq = pm[:, :MLA_Q_RANK]
    cq = cq * lax.rsqrt(jnp.mean(cq * cq, axis=-1, keepdims=True) + RMS_EPS) * qg_ref[...]
    cq = cq.astype(BF16)
    ckv = pm[:, MLA_Q_RANK:MLA_Q_RANK + MLA_KV_RANK]
    ckv = ckv * lax.rsqrt(jnp.mean(ckv * ckv, axis=-1, keepdims=True) + RMS_EPS) * kvg_ref[...]
    ckv_ref[0] = ckv
    ckv = ckv.astype(BF16)
    kr0 = MLA_Q_RANK + MLA_KV_RANK
    kr = pm[:, kr0:kr0 + LANES]
    qa = _dot(cq, wq1_ref[...])
    kn = _dot(ckv, wk_ref[...])
    if key_major:
        v_ref[0] = _dot_nt(wv_ref[...], ckv).astype(BF16)
    else:
        v_ref[0] = _dot(ckv, wv_ref[...]).astype(BF16)
    if rope:
        cos = cos_ref[...]
        sin = sin_ref[...]
        qb = _dot(cq, wq2_ref[...])
        kr = kr * cos + pm[:, kr0 + LANES:kr0 + 2 * LANES] * sin
    for h in range(MLA_HEADS):
        sl = slice(h * HEAD_PAD, (h + 1) * HEAD_PAD)
        qh = qa[:, sl]
        if rope:
            qh = qh * cos + qb[:, sl] * sin
        q_ref[0, :, sl] = (qh * scale).astype(BF16)
        k_ref[0, :, sl] = (kn[:, sl] + kr).astype(BF16)


def _ab_inproj(x, mod, w_h, w_m, rope_tabs, qg, kvg, wq1, wq2, wk, wv):
    b, l, d = x.shape
    tm = PROJ_TILE
    rope = rope_tabs is not None
    per_batch = mod.shape[0] > 1
    mod_map = (lambda i, j: (i, 0, 0)) if per_batch else (lambda i, j: (0, 0, 0))
    full = lambda a: pl.BlockSpec(a.shape, lambda i, j: (0,) * a.ndim, pipeline_mode=pl.Buffered(1))
    row = lambda width: pl.BlockSpec((1, tm, width), lambda i, j: (i, j, 0))
    hw = MLA_HEADS * HEAD_PAD
    inputs = [x, mod, w_h, w_m]
    in_specs = [row(d), pl.BlockSpec((1, 6, d), mod_map), full(w_h), full(w_m)]
    if rope:
        inputs += list(rope_tabs)
        in_specs += [pl.BlockSpec((tm, LANES), lambda i, j: (j, 0))] * 2
    ws = [qg, kvg, wq1] + ([wq2] if rope else []) + [wk, wv]
    inputs += ws
    in_specs += [full(a) for a in ws]
    key_major = wv.shape[1] == MLA_KV_RANK
    vw = MLA_HEADS * MLA_DV
    shapes = [(b, l, w_h.shape[1]), (b, l, hw), (b, l, hw), (b, vw, l) if key_major else (b, l, hw),
              (b, l, MLA_KV_RANK), (b, l, LANES)]
    dtypes = [F32, BF16, BF16, BF16, F32, F32]
    out_specs = [row(s[2]) for s in shapes]
    if key_major:
        out_specs[3] = pl.BlockSpec((1, vw, tm), lambda i, j: (i, 0, j))
    return pl.pallas_call(
        functools.partial(_ab_inproj_kernel, rope=rope, key_major=key_major),
        out_shape=[jax.ShapeDtypeStruct(s, dt) for s, dt in zip(shapes, dtypes)],
        grid=(b, l // tm),
        in_specs=in_specs,
        out_specs=out_specs,
        compiler_params=_cparams(("parallel", "parallel")),
        name="ab_inproj",
    )(*inputs)


def _mla_ctx_kernel(ckv_ref, kr_ref, place_ref, wk_ref, wvt_ref, k_ref, vt_ref):
    ckv = ckv_ref[0].astype(BF16)
    kr = _dot(kr_ref[0].astype(BF16), place_ref[...])
    kn = _dot(ckv, wk_ref[...])
    vt_ref[0] = _dot_nt(wvt_ref[...], ckv).astype(BF16)
    for h in range(MLA_HEADS):
        sl = slice(h * HEAD_PAD, (h + 1) * HEAD_PAD)
        k_ref[0, :, sl] = (kn[:, sl] + kr).astype(BF16)


def _mla_ctx(ctx_ckv, ctx_kr, wk, wvt):
    b, lc, _ = ctx_ckv.shape
    hw = MLA_HEADS * HEAD_PAD
    vw = MLA_HEADS * MLA_DV
    place = np.zeros((MLA_DR, LANES), np.float32)
    place[np.arange(MLA_DR), MLA_DN + np.arange(MLA_DR)] = 1.0
    place = jnp.asarray(place, BF16)
    full = lambda a: pl.BlockSpec(a.shape, lambda i: (0,) * a.ndim)
    return pl.pallas_call(
        _mla_ctx_kernel,
        out_shape=[jax.ShapeDtypeStruct((b, lc, hw), BF16), jax.ShapeDtypeStruct((b, vw, lc), BF16)],
        grid=(b,),
        in_specs=[pl.BlockSpec((1, lc, MLA_KV_RANK), lambda i: (i, 0, 0)),
                  pl.BlockSpec((1, lc, MLA_DR), lambda i: (i, 0, 0)),
                  full(place), full(wk), full(wvt)],
        out_specs=[pl.BlockSpec((1, lc, hw), lambda i: (i, 0, 0)), pl.BlockSpec((1, vw, lc), lambda i: (i, 0, 0))],
        compiler_params=_cparams(("parallel",)),
        name="mla_ctx_keys",
    )(ctx_ckv, ctx_kr, place, wk, wvt)


def _attn_long_kernel(q_ref, k_ref, vt_ref, kc_ref, vtc_ref, o_ref, s_a, s_b, m_a, m_b):
    i = pl.program_id(2)
    tq = q_ref.shape[1]
    lk = k_ref.shape[1]
    tk = ATTN_KEY_BLOCK
    n_blk = lk // tk
    slabs = [slice(hh * HEAD_PAD, (hh + 1) * HEAD_PAD) for hh in range(2)]
    vrows = [slice(hh * MLA_DV, (hh + 1) * MLA_DV) for hh in range(2)]
    q_t = [q_ref[0, :, sl].astype(F32).T.astype(BF16) for sl in slabs]
    crow = slice(lk, lk + kc_ref.shape[1])

    @pl.when(i == 0)
    def _():
        s_b[...] = jnp.zeros(s_b.shape, F32)
        m_b[...] = jnp.zeros(m_b.shape, F32)

    def run(s_cur, m_cur, s_prv, m_prv):
        def scores(hh, k_blk, rows, m_run):
            s = _dot(k_blk, q_t[hh])
            s_cur[hh, rows, :] = s
            return jnp.maximum(m_run, jnp.max(s.reshape(s.shape[0] // SUBLANES, SUBLANES, tq), axis=0))

        def weigh(hh, vt_blk, rows, m, acc):
            p = jnp.exp2(s_prv[hh, rows, :] - m).astype(BF16)
            lhs = jnp.concatenate([vt_blk, jnp.ones((2 * SUBLANES, vt_blk.shape[1]), BF16)], axis=0)
            return acc + _dot(lhs, p)

        ms = [jnp.max(m_prv[hh], axis=0, keepdims=True) for hh in range(2)]

        def step(j, carry):
            m_runs, accs = carry
            rows = pl.ds(pl.multiple_of(j * tk, tk), tk)
            m_runs = tuple(scores(hh, k_ref[0, rows, slabs[hh]], rows, m_runs[hh]) for hh in range(2))
            accs = tuple(weigh(hh, vt_ref[0, vrows[hh], rows], rows, ms[hh], accs[hh]) for hh in range(2))
            return m_runs, accs

        carry = ((jnp.full((SUBLANES, tq), -jnp.inf, F32),) * 2,
                 (jnp.zeros((MLA_DV + 2 * SUBLANES, tq), F32),) * 2)
        m_runs, accs = lax.fori_loop(0, n_blk, step, carry, unroll=2)
        m_runs = tuple(scores(hh, kc_ref[0, :, slabs[hh]], crow, m_runs[hh]) for hh in range(2))
        accs = tuple(weigh(hh, vtc_ref[0, vrows[hh], :], crow, ms[hh], accs[hh]) for hh in range(2))
        for hh in range(2):
            m_cur[hh] = m_runs[hh]
        outs = [acc[:MLA_DV] / acc[MLA_DV:MLA_DV + 1] for acc in accs]
        o_ref[0] = jnp.concatenate(outs, axis=0).T.astype(BF16)

    @pl.when(i % 2 == 0)
    def _():
        run(s_a, m_a, s_b, m_b)

    @pl.when(i % 2 == 1)
    def _():
        run(s_b, m_b, s_a, m_a)


def _attention_long(q, k, vt, kc, vtc):
    b, l, _ = q.shape
    lk = k.shape[1]
    lc = kc.shape[1]
    tq = ATTN_Q_TILE
    pw = 2 * HEAD_PAD
    vpw = 2 * MLA_DV
    assert lk % (2 * ATTN_KEY_BLOCK) == 0 and l % tq == 0
    n_q = l // tq
    return pl.pallas_call(
        _attn_long_kernel,
        out_shape=jax.ShapeDtypeStruct((b, l, MLA_HEADS * MLA_DV), BF16),
        grid=(b, MLA_HEADS // 2, n_q + 1),
        in_specs=[pl.BlockSpec((1, tq, pw), lambda bi, p, i: (bi, jnp.minimum(i, n_q - 1), p)),
                  pl.BlockSpec((1, lk, pw), lambda bi, p, i: (bi, 0, p)),
                  pl.BlockSpec((1, vpw, lk), lambda bi, p, i: (bi, p, 0)),
                  pl.BlockSpec((1, lc, pw), lambda bi, p, i: (bi, 0, p)),
                  pl.BlockSpec((1, vpw, lc), lambda bi, p, i: (bi, p, 0))],
        out_specs=pl.BlockSpec((1, tq, LANES), lambda bi, p, i: (bi, jnp.maximum(i - 1, 0), p)),
        scratch_shapes=[pltpu.VMEM((2, lk + lc, tq), F32), pltpu.VMEM((2, lk + lc, tq), F32),
                        pltpu.VMEM((2, SUBLANES, tq), F32), pltpu.VMEM((2, SUBLANES, tq), F32)],
        compiler_params=_cparams(("parallel", "parallel", "arbitrary")),
        name="mla_attention_long",
    )(q, k, vt, kc, vtc)


def _attn_kernel(*refs, has_ctx):
    if has_ctx:
        q_ref, k_ref, v_ref, kc_ref, vc_ref, o_ref = refs
    else:
        q_ref, k_ref, v_ref, o_ref = refs
    n_pairs = q_ref.shape[-1] // (2 * HEAD_PAD)
    for pair in range(n_pairs):
        out = None
        for hh in range(2):
            h0 = (2 * pair + hh) * HEAD_PAD
            sl = slice(h0, h0 + HEAD_PAD)
            q = q_ref[0, :, sl]
            s = _dot_nt(q, k_ref[0, :, sl])
            m = jnp.max(s, axis=-1, keepdims=True)
            if has_ctx:
                s2 = _dot_nt(q, kc_ref[0, :, sl])
                m = jnp.maximum(m, jnp.max(s2, axis=-1, keepdims=True))
            p = jnp.exp2(s - m)
            den = jnp.sum(p, axis=-1, keepdims=True)
            acc = _dot(p.astype(BF16), v_ref[0, :, sl])
            if has_ctx:
                p2 = jnp.exp2(s2 - m)
                den = den + jnp.sum(p2, axis=-1, keepdims=True)
                acc = acc + _dot(p2.astype(BF16), vc_ref[0, :, sl])
            o = acc / den
            out = o if out is None else out + o
        o_ref[0, :, pair * LANES:(pair + 1) * LANES] = out.astype(BF16)


def _attention(q, k, v, kc, vc):
    b, l, _ = q.shape
    lk = k.shape[1]
    tq = TOK_TILE
    pairs_per_step = MLA_HEADS // 2 if lk <= TOK_TILE else 1
    pw = 2 * HEAD_PAD * pairs_per_step
    has_ctx = kc is not None
    inputs = [q, k, v]
    in_specs = [pl.BlockSpec((1, tq, pw), lambda bi, p, i: (bi, i, p)),
                pl.BlockSpec((1, lk, pw), lambda bi, p, i: (bi, 0, p)),
                pl.BlockSpec((1, lk, pw), lambda bi, p, i: (bi, 0, p))]
    if has_ctx:
        lc = kc.shape[1]
        inputs += [kc, vc]
        in_specs += [pl.BlockSpec((1, lc, pw), lambda bi, p, i: (bi, 0, p))] * 2
    return pl.pallas_call(
        functools.partial(_attn_kernel, has_ctx=has_ctx),
        out_shape=jax.ShapeDtypeStruct((b, l, MLA_HEADS * MLA_DV), BF16),
        grid=(b, MLA_HEADS // 2 // pairs_per_step, l // tq),
        in_specs=in_specs,
        out_specs=pl.BlockSpec((1, tq, LANES * pairs_per_step), lambda bi, p, i: (bi, i, p)),
        compiler_params=_cparams(("parallel", "parallel", "arbitrary")),
        name="mla_attention",
    )(*inputs)


def _outproj_kernel(*refs, hgrn):
    if hgrn:
        x_ref, mod_ref, of_ref, ob_ref, gl_ref, gn_ref, b_ref, w_ref, lg_ref, lb_ref, o_ref = refs
        o = of_ref[0] + ob_ref[0]
        gate = _silu(gl_ref[0])
        gn = gn_ref[...]
        parts = []
        for h in range(HA_HEADS):
            sl = slice(h * HA_DV, (h + 1) * HA_DV)
            oh = o[:, sl]
            oh = oh * lax.rsqrt(jnp.mean(oh * oh, axis=-1, keepdims=True) + RMS_EPS) * gn
            parts.append((oh * gate[:, sl]).astype(BF16))
        a = jnp.concatenate(parts, axis=-1)
    else:
        x_ref, mod_ref, a_ref, b_ref, w_ref, lg_ref, lb_ref, o_ref = refs
        a = a_ref[0]
    half = a.shape[-1]
    y = _dot(a, w_ref[:half, :]) + _dot(b_ref[0], w_ref[half:, :])
    m = mod_ref[0]
    r = ALPHA * x_ref[0] + m[2:3] * y
    o_ref[0] = _layer_norm(r, lg_ref[...], lb_ref[...])


def _outproj(x, mod, a_inputs, b_in, w, ln_g, ln_b, hgrn):
    b, l, d = x.shape
    tm = PROJ_TILE
    per_batch = mod.shape[0] > 1
    mod_map = (lambda i, j: (i, 0, 0)) if per_batch else (lambda i, j: (0, 0, 0))
    row = lambda width: pl.BlockSpec((1, tm, width), lambda i, j: (i, j, 0))
    full = lambda a: pl.BlockSpec(a.shape, lambda i, j: (0,) * a.ndim, pipeline_mode=pl.Buffered(1))
    inputs = [x, mod]
    in_specs = [row(d), pl.BlockSpec((1, 6, d), mod_map)]
    if hgrn:
        o_f, o_b, p_h, g_norm = a_inputs
        inputs += [o_f, o_b, p_h, g_norm]
        in_specs += [row(HA_W), row(HA_W),
                     pl.BlockSpec((1, tm, HA_W), lambda i, j: (i, j, 4)), full(g_norm)]
    else:
        inputs += [a_inputs]
        in_specs += [row(a_inputs.shape[-1])]
    inputs += [b_in, w, ln_g, ln_b]
    in_specs += [row(b_in.shape[-1]), full(w), full(ln_g), full(ln_b)]
    return pl.pallas_call(
        functools.partial(_outproj_kernel, hgrn=hgrn),
        out_shape=jax.ShapeDtypeStruct((b, l, d), F32),
        grid=(b, l // tm),
        in_specs=in_specs,
        out_specs=row(d),
        compiler_params=_cparams(("parallel", "parallel")),
        name="outproj_ln",
    )(*inputs)


def _conv_kernel(pm_ref, pp_ref, pn_ref, scw_ref, cfw_ref, cfb_ref, cfg_ref, cfbeta_ref,
                 ysc_ref, ycf_ref, ext_sc, ext_cf, *, lt):
    i = pl.program_id(1)
    n_i = pl.num_programs(1)
    w = SC_W

    def sc_in(p):
        return p[:, w:2 * w] * p[:, 2 * w:3 * w]

    def cf_in(p):
        return p[:, 3 * w:3 * w + CF_W] * _sigmoid(p[:, 3 * w + CF_W:3 * w + 2 * CF_W])

    pm = pm_ref[0]
    pp = pp_ref[0]
    pn = pn_ref[0]
    has_prev = i > 0
    has_next = i < n_i - 1
    n_ext = lt + 2 * HALO
    for ext, conv_in in ((ext_sc, sc_in), (ext_cf, cf_in)):
        ext[0, 0:HALO, :] = jnp.where(has_prev, conv_in(pp), 0.0)
        ext[0, HALO:HALO + lt, :] = conv_in(pm)
        ext[0, HALO + lt:, :] = jnp.where(has_next, conv_in(pn), 0.0)
    for s in range(1, SUBLANES):
        ext_cf[s, 0:n_ext - SUBLANES, :] = ext_cf[0, s:s + n_ext - SUBLANES, :]
    sc_shifts = sorted({(HALO - SC_K // 2 + j) % SUBLANES for j in range(SC_K)} - {0})
    for s in sc_shifts:
        ext_sc[s, 0:n_ext - SUBLANES, :] = ext_sc[0, s:s + n_ext - SUBLANES, :]

    def tap(ext, off, rows):
        s = off % SUBLANES
        return ext[s, off - s:off - s + rows, :]

    rb = 32
    for r in range(lt // rb):
        base = HALO + r * rb
        acc = None
        for j in range(SC_K):
            term = jnp.tile(scw_ref[j], (rb // SUBLANES, 1)) * tap(ext_sc, base - SC_K // 2 + j, rb)
            acc = term if acc is None else acc + term
        ysc_ref[0, r * rb:(r + 1) * rb, :] = (pm[r * rb:(r + 1) * rb, 0:w] * acc).astype(BF16)
        acc = None
        for j in range(CF_K):
            term = jnp.tile(cfw_ref[j], (rb // SUBLANES, 1)) * tap(ext_cf, base - CF_K // 2 + j, rb)
            acc = term if acc is None else acc + term
        u = _layer_norm(acc + cfb_ref[...], cfg_ref[...], cfbeta_ref[...])
        ycf_ref[0, r * rb:(r + 1) * rb, :] = _silu(u).astype(BF16)


def _conv_mixers(p1, sc_w, cf_w, cf_b, cf_g, cf_beta):
    b, l, width = p1.shape
    lt = min(SEQ_TILE, l)
    hb = lt // HALO
    n_h = l // HALO
    sc_w = jnp.broadcast_to(sc_w[:, None, :], (sc_w.shape[0], SUBLANES, sc_w.shape[1]))
    cf_w = jnp.broadcast_to(cf_w[:, None, :], (cf_w.shape[0], SUBLANES, cf_w.shape[1]))
    full = lambda a: pl.BlockSpec(a.shape, lambda bi, i: (0,) * a.ndim)
    return pl.pallas_call(
        functools.partial(_conv_kernel, lt=lt),
        out_shape=[jax.ShapeDtypeStruct((b, l, SC_W), BF16), jax.ShapeDtypeStruct((b, l, CF_W), BF16)],
        grid=(b, l // lt),
        in_specs=[
            pl.BlockSpec((1, lt, width), lambda bi, i: (bi, i, 0)),
            pl.BlockSpec((1, HALO, width), lambda bi, i: (bi, jnp.maximum(i * hb - 1, 0), 0)),
            pl.BlockSpec((1, HALO, width), lambda bi, i: (bi, jnp.minimum((i + 1) * hb, n_h - 1), 0)),
            full(sc_w), full(cf_w), full(cf_b), full(cf_g), full(cf_beta),
        ],
        out_specs=[pl.BlockSpec((1, lt, SC_W), lambda bi, i: (bi, i, 0)),
                   pl.BlockSpec((1, lt, CF_W), lambda bi, i: (bi, i, 0))],
        scratch_shapes=[pltpu.VMEM((SUBLANES, lt + 2 * HALO, SC_W), F32),
                        pltpu.VMEM((SUBLANES, lt + 2 * HALO, CF_W), F32)],
        compiler_params=_cparams(("parallel", "parallel")),
        name="conv_mixers",
    )(p1, p1, p1, sc_w, cf_w, cf_b, cf_g, cf_beta)


def _route_t(lt):
    t = lt.shape[1]
    row = lax.broadcasted_iota(jnp.int32, (SUBLANES, t), 0).astype(F32)
    neg = -jnp.inf
    big = float(LANES)
    gl = jnp.where(row < N_GROUPS, lt[N_EXPERTS:N_EXPERTS + SUBLANES], neg)
    gmax = jnp.max(gl, axis=0, keepdims=True)
    p_g = 1.0 / jnp.sum(jnp.exp(gl - gmax), axis=0, keepdims=True)
    g_sel = jnp.min(jnp.where(gl == gmax, row, big), axis=0, keepdims=True)
    el = lt[0:EXP_PER_GROUP]
    for gi in range(1, N_GROUPS):
        el = jnp.where(g_sel == gi, lt[gi * EXP_PER_GROUP:(gi + 1) * EXP_PER_GROUP], el)
    v1 = jnp.max(el, axis=0, keepdims=True)
    i1 = jnp.min(jnp.where(el == v1, row, big), axis=0, keepdims=True)
    el2 = jnp.where(row == i1, neg, el)
    v2 = jnp.max(el2, axis=0, keepdims=True)
    i2 = jnp.min(jnp.where(el2 == v2, row, big), axis=0, keepdims=True)
    e2 = jnp.exp(v2 - v1)
    w1 = p_g / (1.0 + e2)
    w2 = p_g * e2 / (1.0 + e2)
    comb = jnp.where(row == i1, w1, 0.0) + jnp.where(row == i2, w2, 0.0)
    onehot = jnp.where(row == g_sel, 1.0, 0.0)
    return onehot, comb


def _moe_kernel(xc_ref, xl_ref, mod_ref, wr_ref, tri_ref, wg_ref, wu_ref, wd_ref, lg_ref, lb_ref, oc_ref, ol_ref,
                hb_scr, os_scr, pt_scr, cc_scr, *, n_ctx):
    is_ctx = pl.program_id(0) < n_ctx
    m = mod_ref[0]
    tm, cap = pt_scr.shape
    d = xc_ref.shape[-1]
    half = d // 2
    cw = EXP_PER_GROUP * EXP_FF

    h = jnp.where(is_ctx, xc_ref[0], xl_ref[0]) * (1.0 + m[4:5]) + m[3:4]
    h_hi = h.astype(BF16)
    hb_scr[...] = h_hi
    h_lo = (h - h_hi.astype(F32)).astype(BF16)
    l2 = _dot(h_hi, wr_ref[...])
    logits = l2[:, :LANES] + l2[:, LANES:] + _dot(h_lo, wr_ref[:, :LANES])
    onehot, comb = _route_t(logits.T)
    rank = _dot(onehot.astype(BF16), tri_ref[...])
    cnt = jnp.sum(onehot, axis=1, keepdims=True)
    n_blocks = jnp.floor((cnt + (MOE_BLOCK - 0.5)) * (1.0 / MOE_BLOCK))
    padded = n_blocks * MOE_BLOCK
    start = jnp.zeros((1, 1), F32)
    first_block = jnp.zeros((1, 1), F32)
    dest = jnp.zeros((1, tm), F32)
    first, count = [], []
    for gi in range(N_GROUPS):
        dest = dest + onehot[gi:gi + 1] * (start + rank[gi:gi + 1])
        first.append(first_block[0, 0].astype(jnp.int32))
        count.append(n_blocks[gi, 0].astype(jnp.int32))
        start = start + padded[gi:gi + 1]
        first_block = first_block + n_blocks[gi:gi + 1]
    aux = jnp.concatenate([jnp.broadcast_to(dest, (SUBLANES, tm)), comb,
                           jnp.zeros((LANES - 2 * SUBLANES, tm), F32)], axis=0).T
    lane = lax.broadcasted_iota(jnp.int32, (tm, LANES), 1)
    comb_tok = jnp.where((lane >= SUBLANES) & (lane < 2 * SUBLANES), aux, 0.0)
    c_hi = comb_tok.astype(BF16)
    c_mid = (comb_tok - c_hi.astype(F32)).astype(BF16)
    slot_t = lax.broadcasted_iota(jnp.int32, (tm, cap), 1).astype(F32)
    pt_scr[...] = jnp.where(slot_t == aux[:, 0:1], 1.0, 0.0).astype(BF16)
    cc_scr[...] = jnp.concatenate([c_hi, c_mid], axis=1)
    os_scr[...] = jnp.zeros_like(os_scr)
    block_slot = lax.broadcasted_iota(jnp.int32, (MOE_BLOCK, tm), 0).astype(F32)

    for gi in range(N_GROUPS):
        cols = slice(gi * cw, (gi + 1) * cw)

        def block(i, carry, gi=gi, cols=cols):
            blk = first[gi] + i
            r0 = pl.multiple_of(blk * MOE_BLOCK, 2 * SUBLANES)
            rows = pl.ds(r0, MOE_BLOCK)
            perm = jnp.where(block_slot + (blk * MOE_BLOCK).astype(F32) == dest, 1.0, 0.0).astype(BF16)
            hs = _dot(perm, hb_scr[...]).astype(BF16)
            cs2 = _dot(perm, cc_scr[...])
            cs = cs2[:, :LANES] + cs2[:, LANES:]
            hid = _silu(_dot(hs, wg_ref[:, cols])) * _dot(hs, wu_ref[:, cols])
            hid = jnp.concatenate(
                [hid[:, e * EXP_FF:(e + 1) * EXP_FF] * cs[:, SUBLANES + e:SUBLANES + e + 1]
                 for e in range(EXP_PER_GROUP)], axis=1)
            os_scr[rows, :] = _dot(hid.astype(BF16), wd_ref[cols, :]).astype(BF16)
            return carry

        lax.fori_loop(0, count[gi], block, 0)

    y = jnp.concatenate([_dot(pt_scr[...], os_scr[:, :half]), _dot(pt_scr[...], os_scr[:, half:])], axis=1)
    r = ALPHA * jnp.where(is_ctx, xc_ref[0], xl_ref[0]) + m[5:6] * y
    out = _layer_norm(r, lg_ref[...], lb_ref[...])

    @pl.when(is_ctx)
    def _():
        oc_ref[0] = out

    @pl.when(jnp.logical_not(is_ctx))
    def _():
        ol_ref[0] = out


def _moe(x_ctx, x_lat, mods, wr, wg, wu, wd, ln_g, ln_b):
    tm = MOE_TILE
    d = x_lat.shape[-1]
    x_c = x_ctx.reshape(-1, tm, d)
    n_ctx = x_c.shape[0]
    b, l, _ = x_lat.shape
    per_b = l // tm
    cap = (tm + N_GROUPS * (MOE_BLOCK - 1)) // MOE_BLOCK * MOE_BLOCK
    cap = -(-cap // LANES) * LANES
    tri = jnp.asarray(np.triu(np.ones((tm, tm), np.float32), k=1), BF16)
    ctx_map = lambda t: (jnp.minimum(t, n_ctx - 1), 0, 0)
    lat_t = lambda t: jnp.maximum(t - n_ctx, 0)
    lat_map = lambda t: (lat_t(t) // per_b, lat_t(t) % per_b, 0)
    mod_map = lambda t: (jnp.where(t < n_ctx, b, lat_t(t) // per_b), 0, 0)
    full = lambda a: pl.BlockSpec(a.shape, lambda t: (0,) * a.ndim, pipeline_mode=pl.Buffered(1))
    y_c, y_l = pl.pallas_call(
        functools.partial(_moe_kernel, n_ctx=n_ctx),
        out_shape=[jax.ShapeDtypeStruct(x_c.shape, F32), jax.ShapeDtypeStruct(x_lat.shape, F32)],
        grid=(n_ctx + b * per_b,),
        in_specs=[
            pl.BlockSpec((1, tm, d), ctx_map), pl.BlockSpec((1, tm, d), lat_map), pl.BlockSpec((1, 6, d), mod_map),
            full(wr), full(tri), full(wg), full(wu), full(wd), full(ln_g), full(ln_b),
        ],
        out_specs=[pl.BlockSpec((1, tm, d), ctx_map), pl.BlockSpec((1, tm, d), lat_map)],
        scratch_shapes=[pltpu.VMEM((tm, d), BF16), pltpu.VMEM((cap, d), BF16),
                        pltpu.VMEM((tm, cap), BF16), pltpu.VMEM((tm, 2 * LANES), BF16)],
        compiler_params=_cparams(("arbitrary",)),
        name="hier_moe_ln",
    )(x_c, x_lat, mods, wr, tri, wg, wu, wd, ln_g, ln_b)
    return y_c.reshape(x_ctx.shape), y_l


def _rope_swap_perm():
    idx = np.arange(MLA_DR)
    return idx ^ (MLA_DR // 4)


def _rope_tables(n_tok):
    rows = n_tok // GRID_W
    pos_r = jnp.repeat(jnp.arange(rows, dtype=F32), GRID_W)
    pos_c = (jnp.arange(rows * GRID_W) % GRID_W).astype(F32)
    n_freq = MLA_DR // 4
    inv = ROPE_BASE ** (-jnp.arange(n_freq, dtype=F32) / n_freq)
    ang = jnp.stack([pos_r[:, None] * inv, pos_c[:, None] * inv], axis=1)
    cos, sin = jnp.cos(ang), jnp.sin(ang)
    cos32 = jnp.stack([cos, cos], axis=2).reshape(n_tok, MLA_DR)
    sin32 = jnp.stack([-sin, sin], axis=2).reshape(n_tok, MLA_DR)
    pad_hi = LANES - MLA_DN - MLA_DR
    cos_t = jnp.concatenate([jnp.ones((n_tok, MLA_DN), F32), cos32, jnp.zeros((n_tok, pad_hi), F32)], axis=1)
    sin_t = jnp.concatenate([jnp.zeros((n_tok, MLA_DN), F32), sin32, jnp.zeros((n_tok, pad_hi), F32)], axis=1)
    return cos_t, sin_t


def _ab_weights(w_in, w_uq, w_ukv):
    perm = _rope_swap_perm()
    pad_hi = LANES - MLA_DN - MLA_DR
    w_h = w_in[:, :5 * HA_W]
    cq = w_in[:, 5 * HA_W:5 * HA_W + MLA_Q_RANK]
    ckv = w_in[:, 5 * HA_W + MLA_Q_RANK:5 * HA_W + MLA_Q_RANK + MLA_KV_RANK]
    kr = w_in[:, 5 * HA_W + MLA_Q_RANK + MLA_KV_RANK:]
    d = w_in.shape[0]
    z_lo = jnp.zeros((d, MLA_DN), F32)
    z_hi = jnp.zeros((d, pad_hi), F32)
    w_m = jnp.concatenate([cq, ckv, z_lo, kr, z_hi, z_lo, kr[:, perm], z_hi], axis=1)
    uq = w_uq.reshape(MLA_Q_RANK, MLA_HEADS, MLA_DN + MLA_DR)
    q_nope, q_rope = uq[..., :MLA_DN], uq[..., MLA_DN:]
    zq_hi = jnp.zeros((MLA_Q_RANK, MLA_HEADS, pad_hi), F32)
    wq1 = jnp.concatenate([q_nope, q_rope, zq_hi], axis=-1).reshape(MLA_Q_RANK, -1)
    wq2 = jnp.concatenate([jnp.zeros_like(q_nope), q_rope[..., perm], zq_hi], axis=-1).reshape(MLA_Q_RANK, -1)
    ukv = w_ukv.reshape(MLA_KV_RANK, MLA_HEADS, MLA_DN + MLA_DV)
    k_nope, v = ukv[..., :MLA_DN], ukv[..., MLA_DN:]
    wk = jnp.concatenate([k_nope, jnp.zeros((MLA_KV_RANK, MLA_HEADS, LANES - MLA_DN), F32)], axis=-1)
    wk = wk.reshape(MLA_KV_RANK, -1)
    v_pairs = v.reshape(MLA_KV_RANK, MLA_HEADS // 2, 2, MLA_DV)
    zv = jnp.zeros_like(v_pairs[:, :, 0])
    wv = jnp.stack([jnp.concatenate([v_pairs[:, :, 0], zv], axis=-1),
                    jnp.concatenate([zv, v_pairs[:, :, 1]], axis=-1)], axis=2).reshape(MLA_KV_RANK, -1)
    wvt = v.reshape(MLA_KV_RANK, MLA_HEADS * MLA_DV).T
    bf = lambda a: a.astype(BF16)
    return bf(w_h), bf(w_m), bf(wq1), bf(wq2), bf(wk), bf(wv), bf(wvt)


def _moe_weights(w_group, w_expert, w_gate, w_up, w_down):
    d = w_group.shape[0]
    wr = jnp.concatenate([w_expert, w_group, jnp.zeros((d, LANES - N_EXPERTS - N_GROUPS), F32)], axis=1)
    wr_hi = wr.astype(BF16)
    wr = jnp.concatenate([wr_hi, (wr - wr_hi.astype(F32)).astype(BF16)], axis=1)
    wg = jnp.transpose(w_gate, (1, 0, 2)).reshape(d, N_EXPERTS * EXP_FF).astype(BF16)
    wu = jnp.transpose(w_up, (1, 0, 2)).reshape(d, N_EXPERTS * EXP_FF).astype(BF16)
    wd = w_down.reshape(N_EXPERTS * EXP_FF, d).astype(BF16)
    return wr, wg, wu, wd


def kernel(x_prompt, x_sample, state_hgrn_fwd, state_hgrn_bwd, cache_mla_ckv, cache_mla_krope, c, c_ctx, mod_w, mod_b, ln_g, ln_b, ab_w_in, ab_w_out, hgrn_lb_logits, hgrn_norm_g, mla_q_norm_g, mla_w_uq, mla_kv_norm_g, mla_w_ukv, cd_w_in, cd_w_out, sc_conv_w, cf_conv_w, cf_conv_b, cf_ln_g, cf_ln_b, moe_w_group, moe_w_expert, moe_w_gate, moe_w_up, moe_w_down):
    dec_b = x_sample.shape[0]
    d = D_MODEL
    cvec = jnp.concatenate([c, c_ctx[None, :], jnp.zeros((SUBLANES - dec_b - 1, d), F32)], axis=0)
    mods = _mod_vectors(cvec, mod_w, mod_b)
    rope_tabs = _rope_tables(x_sample.shape[1])
    xp, xs = x_prompt, x_sample
    new_sf = new_sb = new_ckv = new_kr = None
    for l in range(DEPTH):
        mod_lat = mods[l, :dec_b].reshape(dec_b, 6, d)
        mod_ctx = mods[l, dec_b:dec_b + 1].reshape(1, 6, d)
        row = lambda a: a.reshape(1, -1)
        fold = lambda a: a.reshape(-1, PROJ_TILE, a.shape[-1])
        unfold = lambda a: a.reshape(x_prompt.shape[0], -1, a.shape[-1])
        if l % 2 == 0:
            e = l // 2
            w_h, w_m, wq1, wq2, wk, wv, wvt = _ab_weights(ab_w_in[e], mla_w_uq[e], mla_w_ukv[e])
            w_out = ab_w_out[e].astype(BF16)
            qg, kvg, gn = row(mla_q_norm_g[e]), row(mla_kv_norm_g[e]), row(hgrn_norm_g[e])
            ph_p, q_p, k_p, v_p, ckv_p, kr_p = map(unfold, _ab_inproj(
                fold(xp), mod_ctx, w_h, w_m, None, qg, kvg, wq1, None, wk, wv))
            of_p, ob_p, sf, sb = _hgrn(ph_p, hgrn_lb_logits, None, None, e)
            om_p = _attention(q_p, k_p, v_p, None, None)
            xp = unfold(_outproj(fold(xp), mod_ctx, (fold(of_p), fold(ob_p), fold(ph_p), gn), fold(om_p), w_out,
                                 row(ln_g[l, 0]), row(ln_b[l, 0]), True))
            ph_s, q_s, k_s, vt_s, _, _ = _ab_inproj(xs, mod_lat, w_h, w_m, rope_tabs, qg, kvg, wq1, wq2, wk, wvt)
            of_s, ob_s, _, _ = _hgrn(ph_s, hgrn_lb_logits, state_hgrn_fwd[:, e], state_hgrn_bwd[:, e], e)
            kc, vtc = _mla_ctx(cache_mla_ckv[:, e], cache_mla_krope[:, e], wk, wvt)
            om_s = _attention_long(q_s, k_s, vt_s, kc, vtc)
            xs = _outproj(xs, mod_lat, (of_s, ob_s, ph_s, gn), om_s, w_out, row(ln_g[l, 0]), row(ln_b[l, 0]), True)
            new_sf, new_sb, new_ckv = sf, sb, ckv_p
            new_kr = kr_p[:, :, MLA_DN:MLA_DN + MLA_DR]
        else:
            jx = l // 2
            w1 = cd_w_in[jx].astype(BF16)
            w_out = cd_w_out[jx].astype(BF16)
            cd = (sc_conv_w[jx], cf_conv_w[jx], row(cf_conv_b[jx]), row(cf_ln_g[jx]), row(cf_ln_b[jx]))
            (p1_p,) = _inproj(fold(xp), mod_ctx, [w1])
            ysc_p, ycf_p = _conv_mixers(unfold(p1_p), *cd)
            xp = unfold(_outproj(fold(xp), mod_ctx, fold(ysc_p), fold(ycf_p), w_out,
                                 row(ln_g[l, 0]), row(ln_b[l, 0]), False))
            (p1_s,) = _inproj(xs, mod_lat, [w1])
            ysc_s, ycf_s = _conv_mixers(p1_s, *cd)
            xs = _outproj(xs, mod_lat, ysc_s, ycf_s, w_out, row(ln_g[l, 0]), row(ln_b[l, 0]), False)
        wr, wg, wu, wd = _moe_weights(moe_w_group[l], moe_w_expert[l], moe_w_gate[l], moe_w_up[l], moe_w_down[l])
        mods_l = mods[l, :dec_b + 1].reshape(dec_b + 1, 6, d)
        xp, xs = _moe(xp, xs, mods_l, wr, wg, wu, wd, row(ln_g[l, 1]), row(ln_b[l, 1]))
    return (xp, xs, new_sf[:, None], new_sb[:, None], new_ckv[:, None], new_kr[:, None])
```

```python
import functools

import numpy as np
import jax
import jax.numpy as jnp
from jax import lax
from jax.experimental import pallas as pl
from jax.experimental.pallas import tpu as pltpu

F32 = jnp.float32
BF16 = jnp.bfloat16
HIGHEST = lax.Precision.HIGHEST

D_MODEL = 1024
DEPTH = 2
GRID_W = 64
HA_HEADS = 4
HA_DK = 128
HA_DV = 128
HA_W = HA_HEADS * HA_DK
CHUNK = 32
MLA_HEADS = 8
MLA_DN = 64
MLA_DR = 32
MLA_DV = 64
MLA_Q_RANK = 384
MLA_KV_RANK = 256
ROPE_BASE = 10000.0
SC_W = 512
SC_K = 3
CF_W = 512
CF_K = 31
N_GROUPS = 4
EXP_PER_GROUP = 8
N_EXPERTS = N_GROUPS * EXP_PER_GROUP
EXP_FF = 128
ALPHA = (2.0 * DEPTH) ** 0.25
LOG2_E = 1.4426950408889634
LN_EPS = 1e-5
RMS_EPS = 1e-6

LANES = 128
SUBLANES = 8
VMEM_LIMIT = 56 * 1024 * 1024

HEAD_PAD = LANES
PROJ_TILE = 512
SEQ_TILE = 512
TOK_TILE = 256
GROUP_ROWS = 128
N_LEVELS = 5
SAFE_LOG_DECAY = 60.0
ATTN_KEY_BLOCK = 512
ATTN_Q_TILE = 256
MOE_TILE = 512
MOE_BLOCK = 144
HALO = 16


def _cparams(sem):
    return pltpu.CompilerParams(dimension_semantics=sem, vmem_limit_bytes=VMEM_LIMIT)


def _sigmoid(x):
    return 0.5 * jnp.tanh(0.5 * x) + 0.5


def _silu(x):
    half = 0.5 * x
    return half * jnp.tanh(half) + half


def _layer_norm(r, g, b):
    mu = jnp.mean(r, axis=-1, keepdims=True)
    d = r - mu
    var = jnp.mean(d * d, axis=-1, keepdims=True)
    return d * lax.rsqrt(var + LN_EPS) * g + b


def _dot(a, b):
    return jnp.dot(a, b, preferred_element_type=F32)


def _dot_nt(a, b):
    return lax.dot_general(a, b, (((1,), (1,)), ((), ())), preferred_element_type=F32)


def _dot_tn(a, b):
    return lax.dot_general(a, b, (((0,), (0,)), ((), ())), preferred_element_type=F32)


def _mod_kernel(c_ref, w_ref, b_ref, o_ref):
    s = _silu(c_ref[...])
    o_ref[0] = jnp.dot(s, w_ref[0], precision=HIGHEST, preferred_element_type=F32) + b_ref[0]


def _mod_vectors(cvec, mod_w, mod_b):
    n_out = mod_w.shape[-1]
    tn = 1536
    return pl.pallas_call(
        _mod_kernel,
        out_shape=jax.ShapeDtypeStruct((DEPTH, SUBLANES, n_out), F32),
        grid=(DEPTH, n_out // tn),
        in_specs=[
            pl.BlockSpec((SUBLANES, D_MODEL), lambda l, j: (0, 0)),
            pl.BlockSpec((1, D_MODEL, tn), lambda l, j: (l, 0, j)),
            pl.BlockSpec((1, 1, tn), lambda l, j: (l, 0, j)),
        ],
        out_specs=pl.BlockSpec((1, SUBLANES, tn), lambda l, j: (l, 0, j)),
        compiler_params=_cparams(("arbitrary", "arbitrary")),
        name="mod_vectors",
    )(cvec, mod_w, mod_b.reshape(DEPTH, 1, n_out))


def _inproj_kernel(*refs, n_w):
    x_ref, mod_ref = refs[0], refs[1]
    w_refs = refs[2:2 + n_w]
    o_refs = refs[2 + n_w:]
    m = mod_ref[0]
    h = (x_ref[0] * (1.0 + m[1:2]) + m[0:1]).astype(BF16)
    for w_ref, o_ref in zip(w_refs, o_refs):
        o_ref[0] = _dot(h, w_ref[...])


def _inproj(x, mod, weights):
    b, l, d = x.shape
    tm = PROJ_TILE
    assert l % tm == 0
    per_batch = mod.shape[0] > 1
    mod_map = (lambda i, j: (i, 0, 0)) if per_batch else (lambda i, j: (0, 0, 0))
    in_specs = [pl.BlockSpec((1, tm, d), lambda i, j: (i, j, 0)),
                pl.BlockSpec((1, 6, d), mod_map)]
    in_specs += [pl.BlockSpec(w.shape, lambda i, j: (0, 0), pipeline_mode=pl.Buffered(1)) for w in weights]
    out_shape = [jax.ShapeDtypeStruct((b, l, w.shape[1]), F32) for w in weights]
    out_specs = [pl.BlockSpec((1, tm, w.shape[1]), lambda i, j: (i, j, 0)) for w in weights]
    return pl.pallas_call(
        functools.partial(_inproj_kernel, n_w=len(weights)),
        out_shape=out_shape,
        grid=(b, l // tm),
        in_specs=in_specs,
        out_specs=out_specs,
        compiler_params=_cparams(("parallel", "parallel")),
        name="modulate_inproj",
    )(x, mod, *weights)


def _hgrn_tables():
    n = GROUP_ROWS
    t = np.arange(n)[:, None]
    j = np.arange(n)[None, :]
    same_chunk = (t // CHUNK) == (j // CHUNK)
    e_f, e_b = [], []
    lv_f = np.full((n, n), -1, np.int32)
    for lvl in range(N_LEVELS):
        m = CHUNK >> (lvl + 1)
        blk0 = (t // (2 * m)) * (2 * m)
        r = blk0 + m - 1
        upper = t > r
        ef = np.where(upper, (j > r) & (j <= t), (j > t) & (j <= r))
        r2 = blk0 + m
        lower = t < r2
        eb = np.where(lower, (j >= t) & (j < r2), (j >= r2) & (j < t))
        e_f.append(ef)
        e_b.append(eb)
        s = np.arange(n)[None, :]
        same_blk = (t // (2 * m)) == (s // (2 * m))
        q_side = (t % (2 * m)) >= m
        k_side = (s % (2 * m)) < m
        lv_f[same_blk & q_side & k_side] = lvl
    lv_f[np.arange(n), np.arange(n)] = N_LEVELS
    e_f = np.concatenate(e_f, axis=0).astype(np.float32)
    e_b = np.concatenate(e_b, axis=0).astype(np.float32)
    e = np.stack([e_f, e_b])
    cum = np.stack([same_chunk & (j <= t), same_chunk & (j >= t)]).astype(np.float32)
    lv = np.stack([lv_f, lv_f.T])
    return e, cum, lv


def _hgrn_kernel(qf_ref, vf_ref, ff_ref, qb_ref, vb_ref, fb_ref, lbl_ref, e_ref, cum_ref, lv_ref, s0f_ref, s0b_ref,
                 of_ref, ob_ref, sf_ref, sb_ref,
                 st_scr, qe_scr, kd_scr, sc_scr, v_scr, oi_scr, dec_scr, *, lt, slot, has_state):
    i = pl.program_id(1)
    n_i = pl.num_programs(1)
    n_chunks = lt // CHUNK
    n_groups = lt // GROUP_ROWS
    chunks_per_group = GROUP_ROWS // CHUNK

    @pl.when(i == 0)
    def _():
        for d, s0_ref in enumerate((s0f_ref, s0b_ref)):
            for h in range(HA_HEADS):
                if has_state:
                    st_scr[d, h] = s0_ref[0, h].T
                else:
                    st_scr[d, h] = jnp.zeros((HA_DV, HA_DK), F32)

    lg = lbl_ref[...]
    n_slots = lg.shape[0]
    mx = lg[0]
    for s in range(1, n_slots):
        mx = jnp.maximum(mx, lg[s])
    ex = [jnp.exp(lg[s] - mx) for s in range(n_slots)]
    den = ex[0]
    for s in range(1, n_slots):
        den = den + ex[s]
    num = ex[0]
    for s in range(1, slot + 1):
        num = num + ex[s]
    lb_all = num / den

    dirs = ((qf_ref, vf_ref, ff_ref), (qb_ref, vb_ref, fb_ref))
    head_cols = [slice(h * HA_DK, (h + 1) * HA_DK) for h in range(HA_HEADS)]

    def gates(d, rows, cols):
        q_ref, _, f_ref = dirs[d]
        q = _silu(q_ref[0, rows, cols])
        lb_h = lb_all[d:d + 1, cols]
        f = lb_h + (1.0 - lb_h) * _sigmoid(f_ref[0, rows, cols])
        g = jnp.log(f)
        g_hi = g.astype(BF16)
        g_lo = (g - g_hi.astype(F32)).astype(BF16)
        return q, 1.0 - f, g_hi, g_lo

    def group_step(grp, tot_min):
        r0 = pl.multiple_of(grp * GROUP_ROWS, GROUP_ROWS)
        rows = pl.ds(r0, GROUP_ROWS)
        for d in range(2):
            cum = cum_ref[d]
            lv = lv_ref[d]
            for cols in head_cols:
                q, k, g_hi, g_lo = gates(d, rows, cols)
                bcum = _dot(cum, g_hi) + _dot(cum, g_lo)
                v_scr[d, rows, cols] = dirs[d][1][0, rows, cols].astype(BF16)
                qe = (q * jnp.exp(bcum)).astype(BF16)
                qe_scr[d, rows, cols] = qe
                kx = []
                for cc in range(chunks_per_group):
                    edge = cc * CHUNK + (CHUNK - 1 if d == 0 else 0)
                    tot = bcum[edge:edge + 1]
                    b_c = bcum[cc * CHUNK:(cc + 1) * CHUNK]
                    kd = k[cc * CHUNK:(cc + 1) * CHUNK] * jnp.exp(tot - b_c)
                    kd_scr[d, pl.ds(r0 + cc * CHUNK, CHUNK), cols] = kd.astype(BF16)
                    dec_scr[d, grp * chunks_per_group + cc, :, cols] = jnp.exp(tot)
                    kx.append(kd * jnp.exp(-tot))
                    tot_min = jnp.minimum(tot_min, tot)
                p = _dot_nt(qe, jnp.concatenate(kx, axis=0).astype(BF16))
                sc_scr[d, rows, cols] = jnp.where(lv >= 0, p, 0.0).astype(BF16)
        return tot_min

    tot_min = lax.fori_loop(0, n_groups, group_step, jnp.zeros((1, HA_DK), F32), unroll=True)

    def one_factor():
        for d in range(2):
            for grp in range(n_groups):
                rows = slice(grp * GROUP_ROWS, (grp + 1) * GROUP_ROWS)
                for cols in head_cols:
                    oi_scr[d, rows, cols] = _dot(sc_scr[d, rows, cols], v_scr[d, rows, cols])

    def per_level():
        for d in range(2):

            def group_step(grp, carry, d=d):
                rows = pl.ds(pl.multiple_of(grp * GROUP_ROWS, GROUP_ROWS), GROUP_ROWS)
                lv = lv_ref[d]
                for cols in head_cols:
                    q, k, g_hi, g_lo = gates(d, rows, cols)
                    x = jnp.exp(_dot(e_ref[d], g_hi) + _dot(e_ref[d], g_lo))
                    sc = jnp.where(lv == N_LEVELS, _dot_nt(q.astype(BF16), k.astype(BF16)), 0.0)
                    for lvl in range(N_LEVELS):
                        xl = x[lvl * GROUP_ROWS:(lvl + 1) * GROUP_ROWS]
                        p = _dot_nt((q * xl).astype(BF16), (k * xl).astype(BF16))
                        sc = jnp.where(lv == lvl, p, sc)
                    oi_scr[d, rows, cols] = _dot(sc.astype(BF16), v_scr[d, rows, cols])
                return carry

            lax.fori_loop(0, n_groups, group_step, 0)

    lax.cond(jnp.min(tot_min) > -SAFE_LOG_DECAY, one_factor, per_level)

    out_refs = (of_ref, ob_ref)

    def chunk_step(c, carry):
        for d in range(2):
            cc = c if d == 0 else n_chunks - 1 - c
            r0 = pl.multiple_of(cc * CHUNK, CHUNK)
            rows = pl.ds(r0, CHUNK)
            for h in range(HA_HEADS):
                cols = slice(h * HA_DK, (h + 1) * HA_DK)
                st = st_scr[d, h]
                o_state = _dot_nt(qe_scr[d, rows, cols], st.astype(BF16))
                out_refs[d][0, rows, cols] = oi_scr[d, rows, cols] + o_state
                upd = _dot_tn(v_scr[d, rows, cols], kd_scr[d, rows, cols])
                st_scr[d, h] = st * dec_scr[d, cc, :, cols] + upd
        return carry

    lax.fori_loop(0, n_chunks, chunk_step, 0, unroll=True)

    @pl.when(i == n_i - 1)
    def _():
        for d, s_ref in enumerate((sf_ref, sb_ref)):
            for h in range(HA_HEADS):
                s_ref[0, h] = st_scr[d, h].T


def _hgrn(p_h, lb_logits, s0_f, s0_b, slot):
    b, l, _ = p_h.shape
    lt = min(SEQ_TILE, l)
    assert l % lt == 0 and lt % GROUP_ROWS == 0
    n_t = l // lt
    has_state = s0_f is not None
    if not has_state:
        s0_f = jnp.zeros((1, HA_HEADS, HA_DK, HA_DV), F32)
        s0_b = s0_f
    e_np, cum_np, lv_np = _hgrn_tables()
    e_mat = jnp.asarray(e_np, BF16)
    cum = jnp.asarray(cum_np, BF16)
    lv = jnp.asarray(lv_np, jnp.int32)
    w = HA_W

    def sec(idx, rev):
        if rev:
            return pl.BlockSpec((1, lt, w), lambda bi, i: (bi, n_t - 1 - i, idx))
        return pl.BlockSpec((1, lt, w), lambda bi, i: (bi, i, idx))

    state_map = (lambda bi, i: (bi, 0, 0, 0)) if has_state else (lambda bi, i: (0, 0, 0, 0))
    in_specs = [
        sec(0, False), sec(3, False), sec(1, False),
        sec(0, True), sec(3, True), sec(2, True),
        pl.BlockSpec(lb_logits.shape, lambda bi, i: (0, 0, 0)),
        pl.BlockSpec(e_mat.shape, lambda bi, i: (0, 0, 0)),
        pl.BlockSpec(cum.shape, lambda bi, i: (0, 0, 0)),
        pl.BlockSpec(lv.shape, lambda bi, i: (0, 0, 0)),
        pl.BlockSpec((1, HA_HEADS, HA_DK, HA_DV), state_map),
        pl.BlockSpec((1, HA_HEADS, HA_DK, HA_DV), state_map),
    ]
    out_shape = [
        jax.ShapeDtypeStruct((b, l, w), F32),
        jax.ShapeDtypeStruct((b, l, w), F32),
        jax.ShapeDtypeStruct((b, HA_HEADS, HA_DK, HA_DV), F32),
        jax.ShapeDtypeStruct((b, HA_HEADS, HA_DK, HA_DV), F32),
    ]
    out_specs = [
        pl.BlockSpec((1, lt, w), lambda bi, i: (bi, i, 0)),
        pl.BlockSpec((1, lt, w), lambda bi, i: (bi, n_t - 1 - i, 0)),
        pl.BlockSpec((1, HA_HEADS, HA_DK, HA_DV), lambda bi, i: (bi, 0, 0, 0)),
        pl.BlockSpec((1, HA_HEADS, HA_DK, HA_DV), lambda bi, i: (bi, 0, 0, 0)),
    ]
    scratch = [
        pltpu.VMEM((2, HA_HEADS, HA_DV, HA_DK), F32),
        pltpu.VMEM((2, lt, w), BF16),
        pltpu.VMEM((2, lt, w), BF16),
        pltpu.VMEM((2, lt, w), BF16),
        pltpu.VMEM((2, lt, w), BF16),
        pltpu.VMEM((2, lt, w), F32),
        pltpu.VMEM((2, lt // CHUNK, 1, w), F32),
    ]
    return pl.pallas_call(
        functools.partial(_hgrn_kernel, lt=lt, slot=slot, has_state=has_state),
        out_shape=out_shape,
        grid=(b, n_t),
        in_specs=in_specs,
        out_specs=out_specs,
        scratch_shapes=scratch,
        compiler_params=_cparams(("parallel", "arbitrary")),
        name="hgrn2_scan",
    )(p_h, p_h, p_h, p_h, p_h, p_h, lb_logits, e_mat, cum, lv, s0_f, s0_b)


def _ab_inproj_kernel(*refs, rope, key_major):
    if rope:
        (x_ref, mod_ref, wh_ref, wm_ref, cos_ref, sin_ref, qg_ref, kvg_ref, wq1_ref, wq2_ref, wk_ref,
         wv_ref) = refs[:12]
        ph_ref, q_ref, k_ref, v_ref, ckv_ref, kr_ref = refs[12:]
    else:
        x_ref, mod_ref, wh_ref, wm_ref, qg_ref, kvg_ref, wq1_ref, wk_ref, wv_ref = refs[:9]
        ph_ref, q_ref, k_ref, v_ref, ckv_ref, kr_ref = refs[9:]
    scale = (MLA_DN + MLA_DR) ** -0.5 * LOG2_E
    mod = mod_ref[0]
    hmod = (x_ref[0] * (1.0 + mod[1:2]) + mod[0:1]).astype(BF16)
    ph_ref[0] = _dot(hmod, wh_ref[...])
    pm = _dot(hmod, wm_ref[...])
    kr_ref[0] = pm[:, MLA_Q_RANK + MLA_KV_RANK:MLA_Q_RANK + MLA_KV_RANK + LANES]
    cq = pm[:, :MLA_Q_RANK]
    cq = cq * lax.rsqrt(jnp.mean(cq * cq, axis=-1, keepdims=True) + RMS_EPS) * qg_ref[...]
    cq = cq.astype(BF16)
    ckv = pm[:, MLA_Q_RANK:MLA_Q_RANK + MLA_KV_RANK]
    ckv = ckv * lax.rsqrt(jnp.mean(ckv * ckv, axis=-1, keepdims=True) + RMS_EPS) * kvg_ref[...]
    ckv_ref[0] = ckv
    ckv = ckv.astype(BF16)
    kr0 = MLA_Q_RANK + MLA_KV_RANK
    kr = pm[:, kr0:kr0 + LANES]
    qa = _dot(cq, wq1_ref[...])
    kn = _dot(ckv, wk_ref[...])
    if key_major:
        v_ref[0] = _dot_nt(wv_ref[...], ckv).astype(BF16)
    else:
        v_ref[0] = _dot(ckv, wv_ref[...]).astype(BF16)
    if rope:
        cos = cos_ref[...]
        sin = sin_ref[...]
        qb = _dot(cq, wq2_ref[...])
        kr = kr * cos + pm[:, kr0 + LANES:kr0 + 2 * LANES] * sin
    for h in range(MLA_HEADS):
        sl = slice(h * HEAD_PAD, (h + 1) * HEAD_PAD)
        qh = qa[:, sl]
        if rope:
            qh = qh * cos + qb[:, sl] * sin
        q_ref[0, :, sl] = (qh * scale).astype(BF16)
        k_ref[0, :, sl] = (kn[:, sl] + kr).astype(BF16)


def _ab_inproj(x, mod, w_h, w_m, rope_tabs, qg, kvg, wq1, wq2, wk, wv):
    b, l, d = x.shape
    tm = PROJ_TILE
    assert l % tm == 0
    rope = rope_tabs is not None
    per_batch = mod.shape[0] > 1
    mod_map = (lambda i, j: (i, 0, 0)) if per_batch else (lambda i, j: (0, 0, 0))
    full = lambda a: pl.BlockSpec(a.shape, lambda i, j: (0,) * a.ndim, pipeline_mode=pl.Buffered(1))
    row = lambda width: pl.BlockSpec((1, tm, width), lambda i, j: (i, j, 0))
    hw = MLA_HEADS * HEAD_PAD
    inputs = [x, mod, w_h, w_m]
    in_specs = [row(d), pl.BlockSpec((1, 6, d), mod_map), full(w_h), full(w_m)]
    if rope:
        inputs += list(rope_tabs)
        in_specs += [pl.BlockSpec((tm, LANES), lambda i, j: (j, 0))] * 2
    ws = [qg, kvg, wq1] + ([wq2] if rope else []) + [wk, wv]
    inputs += ws
    in_specs += [full(a) for a in ws]
    key_major = wv.shape[1] == MLA_KV_RANK
    vw = MLA_HEADS * MLA_DV
    shapes = [(b, l, w_h.shape[1]), (b, l, hw), (b, l, hw), (b, vw, l) if key_major else (b, l, hw),
              (b, l, MLA_KV_RANK), (b, l, LANES)]
    dtypes = [F32, BF16, BF16, BF16, F32, F32]
    out_specs = [row(s[2]) for s in shapes]
    if key_major:
        out_specs[3] = pl.BlockSpec((1, vw, tm), lambda i, j: (i, 0, j))
    return pl.pallas_call(
        functools.partial(_ab_inproj_kernel, rope=rope, key_major=key_major),
        out_shape=[jax.ShapeDtypeStruct(s, dt) for s, dt in zip(shapes, dtypes)],
        grid=(b, l // tm),
        in_specs=in_specs,
        out_specs=out_specs,
        compiler_params=_cparams(("parallel", "parallel")),
        name="ab_inproj",
    )(*inputs)


def _mla_ctx_kernel(ckv_ref, kr_ref, place_ref, wk_ref, wvt_ref, k_ref, vt_ref):
    ckv = ckv_ref[0].astype(BF16)
    kr = _dot(kr_ref[0].astype(BF16), place_ref[...])
    kn = _dot(ckv, wk_ref[...])
    vt_ref[0] = _dot_nt(wvt_ref[...], ckv).astype(BF16)
    for h in range(MLA_HEADS):
        sl = slice(h * HEAD_PAD, (h + 1) * HEAD_PAD)
        k_ref[0, :, sl] = (kn[:, sl] + kr).astype(BF16)


def _mla_ctx(ctx_ckv, ctx_kr, wk, wvt):
    b, lc, _ = ctx_ckv.shape
    hw = MLA_HEADS * HEAD_PAD
    vw = MLA_HEADS * MLA_DV
    place = np.zeros((MLA_DR, LANES), np.float32)
    place[np.arange(MLA_DR), MLA_DN + np.arange(MLA_DR)] = 1.0
    place = jnp.asarray(place, BF16)
    full = lambda a: pl.BlockSpec(a.shape, lambda i: (0,) * a.ndim)
    return pl.pallas_call(
        _mla_ctx_kernel,
        out_shape=[jax.ShapeDtypeStruct((b, lc, hw), BF16), jax.ShapeDtypeStruct((b, vw, lc), BF16)],
        grid=(b,),
        in_specs=[pl.BlockSpec((1, lc, MLA_KV_RANK), lambda i: (i, 0, 0)),
                  pl.BlockSpec((1, lc, MLA_DR), lambda i: (i, 0, 0)),
                  full(place), full(wk), full(wvt)],
        out_specs=[pl.BlockSpec((1, lc, hw), lambda i: (i, 0, 0)), pl.BlockSpec((1, vw, lc), lambda i: (i, 0, 0))],
        compiler_params=_cparams(("parallel",)),
        name="mla_ctx_keys",
    )(ctx_ckv, ctx_kr, place, wk, wvt)


def _attn_long_kernel(q_ref, k_ref, vt_ref, kc_ref, vtc_ref, o_ref, s_a, s_b, m_a, m_b):
    i = pl.program_id(2)
    tq = q_ref.shape[1]
    lk = k_ref.shape[1]
    tk = ATTN_KEY_BLOCK
    n_blk = lk // tk
    slabs = [slice(hh * HEAD_PAD, (hh + 1) * HEAD_PAD) for hh in range(2)]
    vrows = [slice(hh * MLA_DV, (hh + 1) * MLA_DV) for hh in range(2)]
    q_t = [q_ref[0, :, sl].astype(F32).T.astype(BF16) for sl in slabs]
    crow = slice(lk, lk + kc_ref.shape[1])

    @pl.when(i == 0)
    def _():
        s_b[...] = jnp.zeros(s_b.shape, F32)
        m_b[...] = jnp.zeros(m_b.shape, F32)

    def run(s_cur, m_cur, s_prv, m_prv):
        def scores(hh, k_blk, rows, m_run):
            s = _dot(k_blk, q_t[hh])
            s_cur[hh, rows, :] = s
            return jnp.maximum(m_run, jnp.max(s.reshape(s.shape[0] // SUBLANES, SUBLANES, tq), axis=0))

        def weigh(hh, vt_blk, rows, m, acc):
            p = jnp.exp2(s_prv[hh, rows, :] - m).astype(BF16)
            lhs = jnp.concatenate([vt_blk, jnp.ones((2 * SUBLANES, vt_blk.shape[1]), BF16)], axis=0)
            return acc + _dot(lhs, p)

        ms = [jnp.max(m_prv[hh], axis=0, keepdims=True) for hh in range(2)]

        def step(j, carry):
            m_runs, accs = carry
            rows = pl.ds(pl.multiple_of(j * tk, tk), tk)
            m_runs = tuple(scores(hh, k_ref[0, rows, slabs[hh]], rows, m_runs[hh]) for hh in range(2))
            accs = tuple(weigh(hh, vt_ref[0, vrows[hh], rows], rows, ms[hh], accs[hh]) for hh in range(2))
            return m_runs, accs

        carry = ((jnp.full((SUBLANES, tq), -jnp.inf, F32),) * 2,
                 (jnp.zeros((MLA_DV + 2 * SUBLANES, tq), F32),) * 2)
        m_runs, accs = lax.fori_loop(0, n_blk, step, carry, unroll=2)
        m_runs = tuple(scores(hh, kc_ref[0, :, slabs[hh]], crow, m_runs[hh]) for hh in range(2))
        accs = tuple(weigh(hh, vtc_ref[0, vrows[hh], :], crow, ms[hh], accs[hh]) for hh in range(2))
        for hh in range(2):
            m_cur[hh] = m_runs[hh]
        outs = [acc[:MLA_DV] / acc[MLA_DV:MLA_DV + 1] for acc in accs]
        o_ref[0] = jnp.concatenate(outs, axis=0).T.astype(BF16)

    @pl.when(i % 2 == 0)
    def _():
        run(s_a, m_a, s_b, m_b)

    @pl.when(i % 2 == 1)
    def _():
        run(s_b, m_b, s_a, m_a)


def _attention_long(q, k, vt, kc, vtc):
    b, l, _ = q.shape
    lk = k.shape[1]
    lc = kc.shape[1]
    tq = ATTN_Q_TILE
    pw = 2 * HEAD_PAD
    vpw = 2 * MLA_DV
    assert lk % (2 * ATTN_KEY_BLOCK) == 0 and l % tq == 0
    n_q = l // tq
    return pl.pallas_call(
        _attn_long_kernel,
        out_shape=jax.ShapeDtypeStruct((b, l, MLA_HEADS * MLA_DV), BF16),
        grid=(b, MLA_HEADS // 2, n_q + 1),
        in_specs=[pl.BlockSpec((1, tq, pw), lambda bi, p, i: (bi, jnp.minimum(i, n_q - 1), p)),
                  pl.BlockSpec((1, lk, pw), lambda bi, p, i: (bi, 0, p)),
                  pl.BlockSpec((1, vpw, lk), lambda bi, p, i: (bi, p, 0)),
                  pl.BlockSpec((1, lc, pw), lambda bi, p, i: (bi, 0, p)),
                  pl.BlockSpec((1, vpw, lc), lambda bi, p, i: (bi, p, 0))],
        out_specs=pl.BlockSpec((1, tq, LANES), lambda bi, p, i: (bi, jnp.maximum(i - 1, 0), p)),
        scratch_shapes=[pltpu.VMEM((2, lk + lc, tq), F32), pltpu.VMEM((2, lk + lc, tq), F32),
                        pltpu.VMEM((2, SUBLANES, tq), F32), pltpu.VMEM((2, SUBLANES, tq), F32)],
        compiler_params=_cparams(("parallel", "parallel", "arbitrary")),
        name="mla_attention_long",
    )(q, k, vt, kc, vtc)


def _attn_kernel(q_ref, k_ref, v_ref, o_ref):
    for pair in range(MLA_HEADS // 2):
        out = None
        for hh in range(2):
            h0 = (2 * pair + hh) * HEAD_PAD
            sl = slice(h0, h0 + HEAD_PAD)
            s = _dot_nt(q_ref[0, :, sl], k_ref[0, :, sl])
            p = jnp.exp2(s - jnp.max(s, axis=-1, keepdims=True))
            den = jnp.sum(p, axis=-1, keepdims=True)
            o = _dot(p.astype(BF16), v_ref[0, :, sl]) / den
            out = o if out is None else out + o
        o_ref[0, :, pair * LANES:(pair + 1) * LANES] = out.astype(BF16)


def _attention(q, k, v):
    b, l, hw = q.shape
    assert k.shape[1] == l and l <= TOK_TILE
    blk = pl.BlockSpec((1, l, hw), lambda bi: (bi, 0, 0))
    return pl.pallas_call(
        _attn_kernel,
        out_shape=jax.ShapeDtypeStruct((b, l, MLA_HEADS * MLA_DV), BF16),
        grid=(b,),
        in_specs=[blk, blk, blk],
        out_specs=pl.BlockSpec((1, l, MLA_HEADS * MLA_DV), lambda bi: (bi, 0, 0)),
        compiler_params=_cparams(("parallel",)),
        name="mla_attention",
    )(q, k, v)


def _outproj_kernel(*refs, hgrn):
    if hgrn:
        x_ref, mod_ref, of_ref, ob_ref, gl_ref, gn_ref, b_ref, w_ref, lg_ref, lb_ref, o_ref = refs
        o = of_ref[0] + ob_ref[0]
        gate = _silu(gl_ref[0])
        gn = gn_ref[...]
        parts = []
        for h in range(HA_HEADS):
            sl = slice(h * HA_DV, (h + 1) * HA_DV)
            oh = o[:, sl]
            oh = oh * lax.rsqrt(jnp.mean(oh * oh, axis=-1, keepdims=True) + RMS_EPS) * gn
            parts.append((oh * gate[:, sl]).astype(BF16))
        a = jnp.concatenate(parts, axis=-1)
    else:
        x_ref, mod_ref, a_ref, b_ref, w_ref, lg_ref, lb_ref, o_ref = refs
        a = a_ref[0]
    half = a.shape[-1]
    y = _dot(a, w_ref[:half, :]) + _dot(b_ref[0], w_ref[half:, :])
    m = mod_ref[0]
    r = ALPHA * x_ref[0] + m[2:3] * y
    o_ref[0] = _layer_norm(r, lg_ref[...], lb_ref[...])


def _outproj(x, mod, a_inputs, b_in, w, ln_g, ln_b, hgrn):
    b, l, d = x.shape
    tm = PROJ_TILE
    assert l % tm == 0
    per_batch = mod.shape[0] > 1
    mod_map = (lambda i, j: (i, 0, 0)) if per_batch else (lambda i, j: (0, 0, 0))
    row = lambda width: pl.BlockSpec((1, tm, width), lambda i, j: (i, j, 0))
    full = lambda a: pl.BlockSpec(a.shape, lambda i, j: (0,) * a.ndim, pipeline_mode=pl.Buffered(1))
    inputs = [x, mod]
    in_specs = [row(d), pl.BlockSpec((1, 6, d), mod_map)]
    if hgrn:
        o_f, o_b, p_h, g_norm = a_inputs
        inputs += [o_f, o_b, p_h, g_norm]
        in_specs += [row(HA_W), row(HA_W),
                     pl.BlockSpec((1, tm, HA_W), lambda i, j: (i, j, 4)), full(g_norm)]
    else:
        inputs += [a_inputs]
        in_specs += [row(a_inputs.shape[-1])]
    inputs += [b_in, w, ln_g, ln_b]
    in_specs += [row(b_in.shape[-1]), full(w), full(ln_g), full(ln_b)]
    return pl.pallas_call(
        functools.partial(_outproj_kernel, hgrn=hgrn),
        out_shape=jax.ShapeDtypeStruct((b, l, d), F32),
        grid=(b, l // tm),
        in_specs=in_specs,
        out_specs=row(d),
        compiler_params=_cparams(("parallel", "parallel")),
        name="outproj_ln",
    )(*inputs)


def _conv_kernel(pm_ref, pp_ref, pn_ref, scw_ref, cfw_ref, cfb_ref, cfg_ref, cfbeta_ref,
                 ysc_ref, ycf_ref, ext_sc, ext_cf, *, lt):
    i = pl.program_id(1)
    n_i = pl.num_programs(1)
    w = SC_W

    def sc_in(p):
        return p[:, w:2 * w] * p[:, 2 * w:3 * w]

    def cf_in(p):
        return p[:, 3 * w:3 * w + CF_W] * _sigmoid(p[:, 3 * w + CF_W:3 * w + 2 * CF_W])

    pm = pm_ref[0]
    pp = pp_ref[0]
    pn = pn_ref[0]
    has_prev = i > 0
    has_next = i < n_i - 1
    n_ext = lt + 2 * HALO
    for ext, conv_in in ((ext_sc, sc_in), (ext_cf, cf_in)):
        ext[0, 0:HALO, :] = jnp.where(has_prev, conv_in(pp), 0.0)
        ext[0, HALO:HALO + lt, :] = conv_in(pm)
        ext[0, HALO + lt:, :] = jnp.where(has_next, conv_in(pn), 0.0)
    for s in range(1, SUBLANES):
        ext_cf[s, 0:n_ext - SUBLANES, :] = ext_cf[0, s:s + n_ext - SUBLANES, :]
    sc_shifts = sorted({(HALO - SC_K // 2 + j) % SUBLANES for j in range(SC_K)} - {0})
    for s in sc_shifts:
        ext_sc[s, 0:n_ext - SUBLANES, :] = ext_sc[0, s:s + n_ext - SUBLANES, :]

    def tap(ext, off, rows):
        s = off % SUBLANES
        return ext[s, off - s:off - s + rows, :]

    rb = 32
    for r in range(lt // rb):
        base = HALO + r * rb
        acc = None
        for j in range(SC_K):
            term = jnp.tile(scw_ref[j], (rb // SUBLANES, 1)) * tap(ext_sc, base - SC_K // 2 + j, rb)
            acc = term if acc is None else acc + term
        ysc_ref[0, r * rb:(r + 1) * rb, :] = (pm[r * rb:(r + 1) * rb, 0:w] * acc).astype(BF16)
        acc = None
        for j in range(CF_K):
            term = jnp.tile(cfw_ref[j], (rb // SUBLANES, 1)) * tap(ext_cf, base - CF_K // 2 + j, rb)
            acc = term if acc is None else acc + term
        u = _layer_norm(acc + cfb_ref[...], cfg_ref[...], cfbeta_ref[...])
        ycf_ref[0, r * rb:(r + 1) * rb, :] = _silu(u).astype(BF16)


def _conv_mixers(p1, sc_w, cf_w, cf_b, cf_g, cf_beta):
    b, l, width = p1.shape
    lt = min(SEQ_TILE, l)
    assert l % lt == 0 and lt % HALO == 0
    hb = lt // HALO
    n_h = l // HALO
    sc_w = jnp.broadcast_to(sc_w[:, None, :], (sc_w.shape[0], SUBLANES, sc_w.shape[1]))
    cf_w = jnp.broadcast_to(cf_w[:, None, :], (cf_w.shape[0], SUBLANES, cf_w.shape[1]))
    full = lambda a: pl.BlockSpec(a.shape, lambda bi, i: (0,) * a.ndim)
    return pl.pallas_call(
        functools.partial(_conv_kernel, lt=lt),
        out_shape=[jax.ShapeDtypeStruct((b, l, SC_W), BF16), jax.ShapeDtypeStruct((b, l, CF_W), BF16)],
        grid=(b, l // lt),
        in_specs=[
            pl.BlockSpec((1, lt, width), lambda bi, i: (bi, i, 0)),
            pl.BlockSpec((1, HALO, width), lambda bi, i: (bi, jnp.maximum(i * hb - 1, 0), 0)),
            pl.BlockSpec((1, HALO, width), lambda bi, i: (bi, jnp.minimum((i + 1) * hb, n_h - 1), 0)),
            full(sc_w), full(cf_w), full(cf_b), full(cf_g), full(cf_beta),
        ],
        out_specs=[pl.BlockSpec((1, lt, SC_W), lambda bi, i: (bi, i, 0)),
                   pl.BlockSpec((1, lt, CF_W), lambda bi, i: (bi, i, 0))],
        scratch_shapes=[pltpu.VMEM((SUBLANES, lt + 2 * HALO, SC_W), F32),
                        pltpu.VMEM((SUBLANES, lt + 2 * HALO, CF_W), F32)],
        compiler_params=_cparams(("parallel", "parallel")),
        name="conv_mixers",
    )(p1, p1, p1, sc_w, cf_w, cf_b, cf_g, cf_beta)


def _route_t(lt):
    t = lt.shape[1]
    row = lax.broadcasted_iota(jnp.int32, (SUBLANES, t), 0).astype(F32)
    neg = -jnp.inf
    big = float(LANES)
    gl = jnp.where(row < N_GROUPS, lt[N_EXPERTS:N_EXPERTS + SUBLANES], neg)
    gmax = jnp.max(gl, axis=0, keepdims=True)
    p_g = 1.0 / jnp.sum(jnp.exp(gl - gmax), axis=0, keepdims=True)
    g_sel = jnp.min(jnp.where(gl == gmax, row, big), axis=0, keepdims=True)
    el = lt[0:EXP_PER_GROUP]
    for gi in range(1, N_GROUPS):
        el = jnp.where(g_sel == gi, lt[gi * EXP_PER_GROUP:(gi + 1) * EXP_PER_GROUP], el)
    v1 = jnp.max(el, axis=0, keepdims=True)
    i1 = jnp.min(jnp.where(el == v1, row, big), axis=0, keepdims=True)
    el2 = jnp.where(row == i1, neg, el)
    v2 = jnp.max(el2, axis=0, keepdims=True)
    i2 = jnp.min(jnp.where(el2 == v2, row, big), axis=0, keepdims=True)
    e2 = jnp.exp(v2 - v1)
    w1 = p_g / (1.0 + e2)
    w2 = p_g * e2 / (1.0 + e2)
    comb = jnp.where(row == i1, w1, 0.0) + jnp.where(row == i2, w2, 0.0)
    onehot = jnp.where(row == g_sel, 1.0, 0.0)
    return onehot, comb


def _moe_kernel(xc_ref, xl_ref, mod_ref, wr_ref, tri_ref, wg_ref, wu_ref, wd_ref, lg_ref, lb_ref, oc_ref, ol_ref,
                hb_scr, os_scr, pt_scr, cc_scr, *, n_ctx):
    is_ctx = pl.program_id(0) < n_ctx
    m = mod_ref[0]
    tm, cap = pt_scr.shape
    d = xc_ref.shape[-1]
    half = d // 2
    cw = EXP_PER_GROUP * EXP_FF

    h = jnp.where(is_ctx, xc_ref[0], xl_ref[0]) * (1.0 + m[4:5]) + m[3:4]
    h_hi = h.astype(BF16)
    hb_scr[...] = h_hi
    h_lo = (h - h_hi.astype(F32)).astype(BF16)
    l2 = _dot(h_hi, wr_ref[...])
    logits = l2[:, :LANES] + l2[:, LANES:] + _dot(h_lo, wr_ref[:, :LANES])
    onehot, comb = _route_t(logits.T)
    rank = _dot(onehot.astype(BF16), tri_ref[...])
    cnt = jnp.sum(onehot, axis=1, keepdims=True)
    n_blocks = jnp.floor((cnt + (MOE_BLOCK - 0.5)) * (1.0 / MOE_BLOCK))
    padded = n_blocks * MOE_BLOCK
    start = jnp.zeros((1, 1), F32)
    first_block = jnp.zeros((1, 1), F32)
    dest = jnp.zeros((1, tm), F32)
    first, count = [], []
    for gi in range(N_GROUPS):
        dest = dest + onehot[gi:gi + 1] * (start + rank[gi:gi + 1])
        first.append(first_block[0, 0].astype(jnp.int32))
        count.append(n_blocks[gi, 0].astype(jnp.int32))
        start = start + padded[gi:gi + 1]
        first_block = first_block + n_blocks[gi:gi + 1]
    aux = jnp.concatenate([jnp.broadcast_to(dest, (SUBLANES, tm)), comb,
                           jnp.zeros((LANES - 2 * SUBLANES, tm), F32)], axis=0).T
    lane = lax.broadcasted_iota(jnp.int32, (tm, LANES), 1)
    comb_tok = jnp.where((lane >= SUBLANES) & (lane < 2 * SUBLANES), aux, 0.0)
    c_hi = comb_tok.astype(BF16)
    c_mid = (comb_tok - c_hi.astype(F32)).astype(BF16)
    slot_t = lax.broadcasted_iota(jnp.int32, (tm, cap), 1).astype(F32)
    pt_scr[...] = jnp.where(slot_t == aux[:, 0:1], 1.0, 0.0).astype(BF16)
    cc_scr[...] = jnp.concatenate([c_hi, c_mid], axis=1)
    os_scr[...] = jnp.zeros_like(os_scr)
    block_slot = lax.broadcasted_iota(jnp.int32, (MOE_BLOCK, tm), 0).astype(F32)

    for gi in range(N_GROUPS):
        cols = slice(gi * cw, (gi + 1) * cw)

        def block(i, carry, gi=gi, cols=cols):
            blk = first[gi] + i
            r0 = pl.multiple_of(blk * MOE_BLOCK, 2 * SUBLANES)
            rows = pl.ds(r0, MOE_BLOCK)
            perm = jnp.where(block_slot + (blk * MOE_BLOCK).astype(F32) == dest, 1.0, 0.0).astype(BF16)
            hs = _dot(perm, hb_scr[...]).astype(BF16)
            cs2 = _dot(perm, cc_scr[...])
            cs = cs2[:, :LANES] + cs2[:, LANES:]
            hid = _silu(_dot(hs, wg_ref[:, cols])) * _dot(hs, wu_ref[:, cols])
            hid = jnp.concatenate(
                [hid[:, e * EXP_FF:(e + 1) * EXP_FF] * cs[:, SUBLANES + e:SUBLANES + e + 1]
                 for e in range(EXP_PER_GROUP)], axis=1)
            os_scr[rows, :] = _dot(hid.astype(BF16), wd_ref[cols, :]).astype(BF16)
            return carry

        lax.fori_loop(0, count[gi], block, 0)

    y = jnp.concatenate([_dot(pt_scr[...], os_scr[:, :half]), _dot(pt_scr[...], os_scr[:, half:])], axis=1)
    r = ALPHA * jnp.where(is_ctx, xc_ref[0], xl_ref[0]) + m[5:6] * y
    out = _layer_norm(r, lg_ref[...], lb_ref[...])

    @pl.when(is_ctx)
    def _():
        oc_ref[0] = out

    @pl.when(jnp.logical_not(is_ctx))
    def _():
        ol_ref[0] = out


def _moe(x_ctx, x_lat, mods, wr, wg, wu, wd, ln_g, ln_b):
    tm = MOE_TILE
    d = x_lat.shape[-1]
    x_c = x_ctx.reshape(-1, tm, d)
    n_ctx = x_c.shape[0]
    b, l, _ = x_lat.shape
    assert l % tm == 0 and mods.shape[0] == b + 1
    per_b = l // tm
    cap = (tm + N_GROUPS * (MOE_BLOCK - 1)) // MOE_BLOCK * MOE_BLOCK
    cap = -(-cap // LANES) * LANES
    tri = jnp.asarray(np.triu(np.ones((tm, tm), np.float32), k=1), BF16)
    ctx_map = lambda t: (jnp.minimum(t, n_ctx - 1), 0, 0)
    lat_t = lambda t: jnp.maximum(t - n_ctx, 0)
    lat_map = lambda t: (lat_t(t) // per_b, lat_t(t) % per_b, 0)
    mod_map = lambda t: (jnp.where(t < n_ctx, b, lat_t(t) // per_b), 0, 0)
    full = lambda a: pl.BlockSpec(a.shape, lambda t: (0,) * a.ndim, pipeline_mode=pl.Buffered(1))
    y_c, y_l = pl.pallas_call(
        functools.partial(_moe_kernel, n_ctx=n_ctx),
        out_shape=[jax.ShapeDtypeStruct(x_c.shape, F32), jax.ShapeDtypeStruct(x_lat.shape, F32)],
        grid=(n_ctx + b * per_b,),
        in_specs=[
            pl.BlockSpec((1, tm, d), ctx_map), pl.BlockSpec((1, tm, d), lat_map), pl.BlockSpec((1, 6, d), mod_map),
            full(wr), full(tri), full(wg), full(wu), full(wd), full(ln_g), full(ln_b),
        ],
        out_specs=[pl.BlockSpec((1, tm, d), ctx_map), pl.BlockSpec((1, tm, d), lat_map)],
        scratch_shapes=[pltpu.VMEM((tm, d), BF16), pltpu.VMEM((cap, d), BF16),
                        pltpu.VMEM((tm, cap), BF16), pltpu.VMEM((tm, 2 * LANES), BF16)],
        compiler_params=_cparams(("arbitrary",)),
        name="hier_moe_ln",
    )(x_c, x_lat, mods, wr, tri, wg, wu, wd, ln_g, ln_b)
    return y_c.reshape(x_ctx.shape), y_l


def _rope_swap_perm():
    idx = np.arange(MLA_DR)
    return idx ^ (MLA_DR // 4)


def _rope_tables(n_tok):
    rows = n_tok // GRID_W
    pos_r = jnp.repeat(jnp.arange(rows, dtype=F32), GRID_W)
    pos_c = (jnp.arange(rows * GRID_W) % GRID_W).astype(F32)
    n_freq = MLA_DR // 4
    inv = ROPE_BASE ** (-jnp.arange(n_freq, dtype=F32) / n_freq)
    ang = jnp.stack([pos_r[:, None] * inv, pos_c[:, None] * inv], axis=1)
    cos, sin = jnp.cos(ang), jnp.sin(ang)
    cos32 = jnp.stack([cos, cos], axis=2).reshape(n_tok, MLA_DR)
    sin32 = jnp.stack([-sin, sin], axis=2).reshape(n_tok, MLA_DR)
    pad_hi = LANES - MLA_DN - MLA_DR
    cos_t = jnp.concatenate([jnp.ones((n_tok, MLA_DN), F32), cos32, jnp.zeros((n_tok, pad_hi), F32)], axis=1)
    sin_t = jnp.concatenate([jnp.zeros((n_tok, MLA_DN), F32), sin32, jnp.zeros((n_tok, pad_hi), F32)], axis=1)
    return cos_t, sin_t


def _ab_weights(w_in, w_uq, w_ukv):
    perm = _rope_swap_perm()
    pad_hi = LANES - MLA_DN - MLA_DR
    w_h = w_in[:, :5 * HA_W]
    cq = w_in[:, 5 * HA_W:5 * HA_W + MLA_Q_RANK]
    ckv = w_in[:, 5 * HA_W + MLA_Q_RANK:5 * HA_W + MLA_Q_RANK + MLA_KV_RANK]
    kr = w_in[:, 5 * HA_W + MLA_Q_RANK + MLA_KV_RANK:]
    d = w_in.shape[0]
    z_lo = jnp.zeros((d, MLA_DN), F32)
    z_hi = jnp.zeros((d, pad_hi), F32)
    w_m = jnp.concatenate([cq, ckv, z_lo, kr, z_hi, z_lo, kr[:, perm], z_hi], axis=1)
    uq = w_uq.reshape(MLA_Q_RANK, MLA_HEADS, MLA_DN + MLA_DR)
    q_nope, q_rope = uq[..., :MLA_DN], uq[..., MLA_DN:]
    zq_hi = jnp.zeros((MLA_Q_RANK, MLA_HEADS, pad_hi), F32)
    wq1 = jnp.concatenate([q_nope, q_rope, zq_hi], axis=-1).reshape(MLA_Q_RANK, -1)
    wq2 = jnp.concatenate([jnp.zeros_like(q_nope), q_rope[..., perm], zq_hi], axis=-1).reshape(MLA_Q_RANK, -1)
    ukv = w_ukv.reshape(MLA_KV_RANK, MLA_HEADS, MLA_DN + MLA_DV)
    k_nope, v = ukv[..., :MLA_DN], ukv[..., MLA_DN:]
    wk = jnp.concatenate([k_nope, jnp.zeros((MLA_KV_RANK, MLA_HEADS, LANES - MLA_DN), F32)], axis=-1)
    wk = wk.reshape(MLA_KV_RANK, -1)
    v_pairs = v.reshape(MLA_KV_RANK, MLA_HEADS // 2, 2, MLA_DV)
    zv = jnp.zeros_like(v_pairs[:, :, 0])
    wv = jnp.stack([jnp.concatenate([v_pairs[:, :, 0], zv], axis=-1),
                    jnp.concatenate([zv, v_pairs[:, :, 1]], axis=-1)], axis=2).reshape(MLA_KV_RANK, -1)
    wvt = v.reshape(MLA_KV_RANK, MLA_HEADS * MLA_DV).T
    bf = lambda a: a.astype(BF16)
    return bf(w_h), bf(w_m), bf(wq1), bf(wq2), bf(wk), bf(wv), bf(wvt)


def _moe_weights(w_group, w_expert, w_gate, w_up, w_down):
    d = w_group.shape[0]
    wr = jnp.concatenate([w_expert, w_group, jnp.zeros((d, LANES - N_EXPERTS - N_GROUPS), F32)], axis=1)
    wr_hi = wr.astype(BF16)
    wr = jnp.concatenate([wr_hi, (wr - wr_hi.astype(F32)).astype(BF16)], axis=1)
    wg = jnp.transpose(w_gate, (1, 0, 2)).reshape(d, N_EXPERTS * EXP_FF).astype(BF16)
    wu = jnp.transpose(w_up, (1, 0, 2)).reshape(d, N_EXPERTS * EXP_FF).astype(BF16)
    wd = w_down.reshape(N_EXPERTS * EXP_FF, d).astype(BF16)
    return wr, wg, wu, wd


def kernel(x_prompt, x_sample, state_hgrn_fwd, state_hgrn_bwd, cache_mla_ckv, cache_mla_krope, c, c_ctx, mod_w, mod_b, ln_g, ln_b, ab_w_in, ab_w_out, hgrn_lb_logits, hgrn_norm_g, mla_q_norm_g, mla_w_uq, mla_kv_norm_g, mla_w_ukv, cd_w_in, cd_w_out, sc_conv_w, cf_conv_w, cf_conv_b, cf_ln_g, cf_ln_b, moe_w_group, moe_w_expert, moe_w_gate, moe_w_up, moe_w_down):
    dec_b = x_sample.shape[0]
    d = D_MODEL
    cvec = jnp.concatenate([c, c_ctx[None, :], jnp.zeros((SUBLANES - dec_b - 1, d), F32)], axis=0)
    mods = _mod_vectors(cvec, mod_w, mod_b)
    rope_tabs = _rope_tables(x_sample.shape[1])
    xp, xs = x_prompt, x_sample
    new_sf = new_sb = new_ckv = new_kr = None
    for l in range(DEPTH):
        mod_lat = mods[l, :dec_b].reshape(dec_b, 6, d)
        mod_ctx = mods[l, dec_b:dec_b + 1].reshape(1, 6, d)
        row = lambda a: a.reshape(1, -1)
        fold = lambda a: a.reshape(-1, PROJ_TILE, a.shape[-1])
        unfold = lambda a: a.reshape(x_prompt.shape[0], -1, a.shape[-1])
        if l % 2 == 0:
            e = l // 2
            w_h, w_m, wq1, wq2, wk, wv, wvt = _ab_weights(ab_w_in[e], mla_w_uq[e], mla_w_ukv[e])
            w_out = ab_w_out[e].astype(BF16)
            qg, kvg, gn = row(mla_q_norm_g[e]), row(mla_kv_norm_g[e]), row(hgrn_norm_g[e])
            ph_p, q_p, k_p, v_p, ckv_p, kr_p = map(unfold, _ab_inproj(
                fold(xp), mod_ctx, w_h, w_m, None, qg, kvg, wq1, None, wk, wv))
            of_p, ob_p, sf, sb = _hgrn(ph_p, hgrn_lb_logits, None, None, e)
            om_p = _attention(q_p, k_p, v_p)
            xp = unfold(_outproj(fold(xp), mod_ctx, (fold(of_p), fold(ob_p), fold(ph_p), gn), fold(om_p), w_out,
                                 row(ln_g[l, 0]), row(ln_b[l, 0]), True))
            ph_s, q_s, k_s, vt_s, _, _ = _ab_inproj(xs, mod_lat, w_h, w_m, rope_tabs, qg, kvg, wq1, wq2, wk, wvt)
            of_s, ob_s, _, _ = _hgrn(ph_s, hgrn_lb_logits, state_hgrn_fwd[:, e], state_hgrn_bwd[:, e], e)
            kc, vtc = _mla_ctx(cache_mla_ckv[:, e], cache_mla_krope[:, e], wk, wvt)
            om_s = _attention_long(q_s, k_s, vt_s, kc, vtc)
            xs = _outproj(xs, mod_lat, (of_s, ob_s, ph_s, gn), om_s, w_out, row(ln_g[l, 0]), row(ln_b[l, 0]), True)
            new_sf, new_sb, new_ckv = sf, sb, ckv_p
            new_kr = kr_p[:, :, MLA_DN:MLA_DN + MLA_DR]
        else:
            jx = l // 2
            w1 = cd_w_in[jx].astype(BF16)
            w_out = cd_w_out[jx].astype(BF16)
            cd = (sc_conv_w[jx], cf_conv_w[jx], row(cf_conv_b[jx]), row(cf_ln_g[jx]), row(cf_ln_b[jx]))
            (p1_p,) = _inproj(fold(xp), mod_ctx, [w1])
            ysc_p, ycf_p = _conv_mixers(unfold(p1_p), *cd)
            xp = unfold(_outproj(fold(xp), mod_ctx, fold(ysc_p), fold(ycf_p), w_out,
                                 row(ln_g[l, 0]), row(ln_b[l, 0]), False))
            (p1_s,) = _inproj(xs, mod_lat, [w1])
            ysc_s, ycf_s = _conv_mixers(p1_s, *cd)
            xs = _outproj(xs, mod_lat, ysc_s, ycf_s, w_out, row(ln_g[l, 0]), row(ln_b[l, 0]), False)
        wr, wg, wu, wd = _moe_weights(moe_w_group[l], moe_w_expert[l], moe_w_gate[l], moe_w_up[l], moe_w_down[l])
        mods_l = mods[l, :dec_b + 1].reshape(dec_b + 1, 6, d)
        xp, xs = _moe(xp, xs, mods_l, wr, wg, wu, wd, row(ln_g[l, 1]), row(ln_b[l, 1]))
    return (xp, xs, new_sf[:, None], new_sb[:, None], new_ckv[:, None], new_kr[:, None])
```

```python
import functools

import numpy as np
import jax
import jax.numpy as jnp
from jax import lax
from jax.experimental import pallas as pl
from jax.experimental.pallas import tpu as pltpu

F32 = jnp.float32
BF16 = jnp.bfloat16
HIGHEST = lax.Precision.HIGHEST

D_MODEL = 1024
DEPTH = 2
GRID_W = 64
HA_HEADS = 4
HA_DK = 128
HA_DV = 128
HA_W = HA_HEADS * HA_DK
CHUNK = 32
MLA_HEADS = 8
MLA_DN = 64
MLA_DR = 32
MLA_DV = 64
MLA_Q_RANK = 384
MLA_KV_RANK = 256
ROPE_BASE = 10000.0
SC_W = 512
SC_K = 3
CF_W = 512
CF_K = 31
N_GROUPS = 4
EXP_PER_GROUP = 8
N_EXPERTS = N_GROUPS * EXP_PER_GROUP
EXP_FF = 128
ALPHA = (2.0 * DEPTH) ** 0.25
LOG2_E = 1.4426950408889634
LN_EPS = 1e-5
RMS_EPS = 1e-6

LANES = 128
SUBLANES = 8
VMEM_LIMIT = 56 * 1024 * 1024

HEAD_PAD = LANES
PROJ_TILE = 512
SEQ_TILE = 512
TOK_TILE = 256
GROUP_ROWS = 128
N_LEVELS = 5
SAFE_LOG_DECAY = 60.0
ATTN_KEY_BLOCK = 512
ATTN_Q_TILE = 256
MOE_TILE = 512
MOE_BLOCK = 144
HALO = 16


def _cparams(sem):
    return pltpu.CompilerParams(dimension_semantics=sem, vmem_limit_bytes=VMEM_LIMIT)


def _sigmoid(x):
    return 0.5 * jnp.tanh(0.5 * x) + 0.5


def _silu(x):
    half = 0.5 * x
    return half * jnp.tanh(half) + half


def _layer_norm(r, g, b):
    mu = jnp.mean(r, axis=-1, keepdims=True)
    d = r - mu
    var = jnp.mean(d * d, axis=-1, keepdims=True)
    return d * lax.rsqrt(var + LN_EPS) * g + b


def _dot(a, b):
    return jnp.dot(a, b, preferred_element_type=F32)


def _dot_nt(a, b):
    return lax.dot_general(a, b, (((1,), (1,)), ((), ())), preferred_element_type=F32)


def _dot_tn(a, b):
    return lax.dot_general(a, b, (((0,), (0,)), ((), ())), preferred_element_type=F32)


def _mod_kernel(c_ref, w_ref, b_ref, o_ref):
    s = _silu(c_ref[...])
    o_ref[0] = jnp.dot(s, w_ref[0], precision=HIGHEST, preferred_element_type=F32) + b_ref[0]


def _mod_vectors(cvec, mod_w, mod_b):
    n_out = mod_w.shape[-1]
    tn = 1536
    return pl.pallas_call(
        _mod_kernel,
        out_shape=jax.ShapeDtypeStruct((DEPTH, SUBLANES, n_out), F32),
        grid=(DEPTH, n_out // tn),
        in_specs=[
            pl.BlockSpec((SUBLANES, D_MODEL), lambda l, j: (0, 0)),
            pl.BlockSpec((1, D_MODEL, tn), lambda l, j: (l, 0, j)),
            pl.BlockSpec((1, 1, tn), lambda l, j: (l, 0, j)),
        ],
        out_specs=pl.BlockSpec((1, SUBLANES, tn), lambda l, j: (l, 0, j)),
        compiler_params=_cparams(("arbitrary", "arbitrary")),
        name="mod_vectors",
    )(cvec, mod_w, mod_b.reshape(DEPTH, 1, n_out))


def _inproj_kernel(*refs, n_w):
    x_ref, mod_ref = refs[0], refs[1]
    w_refs = refs[2:2 + n_w]
    o_refs = refs[2 + n_w:]
    m = mod_ref[0]
    h = (x_ref[0] * (1.0 + m[1:2]) + m[0:1]).astype(BF16)
    for w_ref, o_ref in zip(w_refs, o_refs):
        o_ref[0] = _dot(h, w_ref[...])


def _inproj(x, mod, weights):
    b, l, d = x.shape
    tm = PROJ_TILE
    assert l % tm == 0
    per_batch = mod.shape[0] > 1
    mod_map = (lambda i, j: (i, 0, 0)) if per_batch else (lambda i, j: (0, 0, 0))
    in_specs = [pl.BlockSpec((1, tm, d), lambda i, j: (i, j, 0)),
                pl.BlockSpec((1, 6, d), mod_map)]
    in_specs += [pl.BlockSpec(w.shape, lambda i, j: (0, 0), pipeline_mode=pl.Buffered(1)) for w in weights]
    out_shape = [jax.ShapeDtypeStruct((b, l, w.shape[1]), F32) for w in weights]
    out_specs = [pl.BlockSpec((1, tm, w.shape[1]), lambda i, j: (i, j, 0)) for w in weights]
    return pl.pallas_call(
        functools.partial(_inproj_kernel, n_w=len(weights)),
        out_shape=out_shape,
        grid=(b, l // tm),
        in_specs=in_specs,
        out_specs=out_specs,
        compiler_params=_cparams(("parallel", "parallel")),
        name="modulate_inproj",
    )(x, mod, *weights)


def _hgrn_tables():
    n = GROUP_ROWS
    t = np.arange(n)[:, None]
    j = np.arange(n)[None, :]
    same_chunk = (t // CHUNK) == (j // CHUNK)
    e_f, e_b = [], []
    lv_f = np.full((n, n), -1, np.int32)
    for lvl in range(N_LEVELS):
        m = CHUNK >> (lvl + 1)
        blk0 = (t // (2 * m)) * (2 * m)
        r = blk0 + m - 1
        upper = t > r
        ef = np.where(upper, (j > r) & (j <= t), (j > t) & (j <= r))
        r2 = blk0 + m
        lower = t < r2
        eb = np.where(lower, (j >= t) & (j < r2), (j >= r2) & (j < t))
        e_f.append(ef)
        e_b.append(eb)
        s = np.arange(n)[None, :]
        same_blk = (t // (2 * m)) == (s // (2 * m))
        q_side = (t % (2 * m)) >= m
        k_side = (s % (2 * m)) < m
        lv_f[same_blk & q_side & k_side] = lvl
    lv_f[np.arange(n), np.arange(n)] = N_LEVELS
    e_f = np.concatenate(e_f, axis=0).astype(np.float32)
    e_b = np.concatenate(e_b, axis=0).astype(np.float32)
    e = np.stack([e_f, e_b])
    cum = np.stack([same_chunk & (j <= t), same_chunk & (j >= t)]).astype(np.float32)
    lv = np.stack([lv_f, lv_f.T])
    return e, cum, lv


def _hgrn_kernel(qf_ref, vf_ref, ff_ref, qb_ref, vb_ref, fb_ref, lbl_ref, e_ref, cum_ref, lv_ref, s0f_ref, s0b_ref,
                 of_ref, ob_ref, sf_ref, sb_ref,
                 st_scr, qe_scr, kd_scr, sc_scr, v_scr, oi_scr, dec_scr, *, lt, slot, has_state):
    i = pl.program_id(1)
    n_i = pl.num_programs(1)
    n_chunks = lt // CHUNK
    n_groups = lt // GROUP_ROWS
    chunks_per_group = GROUP_ROWS // CHUNK

    @pl.when(i == 0)
    def _():
        for d, s0_ref in enumerate((s0f_ref, s0b_ref)):
            for h in range(HA_HEADS):
                if has_state:
                    st_scr[d, h] = s0_ref[0, h].T
                else:
                    st_scr[d, h] = jnp.zeros((HA_DV, HA_DK), F32)

    lg = lbl_ref[...]
    n_slots = lg.shape[0]
    mx = lg[0]
    for s in range(1, n_slots):
        mx = jnp.maximum(mx, lg[s])
    ex = [jnp.exp(lg[s] - mx) for s in range(n_slots)]
    den = ex[0]
    for s in range(1, n_slots):
        den = den + ex[s]
    num = ex[0]
    for s in range(1, slot + 1):
        num = num + ex[s]
    lb_all = num / den

    dirs = ((qf_ref, vf_ref, ff_ref), (qb_ref, vb_ref, fb_ref))
    head_cols = [slice(h * HA_DK, (h + 1) * HA_DK) for h in range(HA_HEADS)]

    def gates(d, rows, cols):
        q_ref, _, f_ref = dirs[d]
        q = _silu(q_ref[0, rows, cols])
        lb_h = lb_all[d:d + 1, cols]
        f = lb_h + (1.0 - lb_h) * _sigmoid(f_ref[0, rows, cols])
        g = jnp.log(f)
        g_hi = g.astype(BF16)
        g_lo = (g - g_hi.astype(F32)).astype(BF16)
        return q, 1.0 - f, g_hi, g_lo

    def group_step(grp, tot_min):
        r0 = pl.multiple_of(grp * GROUP_ROWS, GROUP_ROWS)
        rows = pl.ds(r0, GROUP_ROWS)
        for d in range(2):
            cum = cum_ref[d]
            lv = lv_ref[d]
            for cols in head_cols:
                q, k, g_hi, g_lo = gates(d, rows, cols)
                bcum = _dot(cum, g_hi) + _dot(cum, g_lo)
                v_scr[d, rows, cols] = dirs[d][1][0, rows, cols].astype(BF16)
                qe = (q * jnp.exp(bcum)).astype(BF16)
                qe_scr[d, rows, cols] = qe
                kx = []
                for cc in range(chunks_per_group):
                    edge = cc * CHUNK + (CHUNK - 1 if d == 0 else 0)
                    tot = bcum[edge:edge + 1]
                    b_c = bcum[cc * CHUNK:(cc + 1) * CHUNK]
                    kd = k[cc * CHUNK:(cc + 1) * CHUNK] * jnp.exp(tot - b_c)
                    kd_scr[d, pl.ds(r0 + cc * CHUNK, CHUNK), cols] = kd.astype(BF16)
                    dec_scr[d, grp * chunks_per_group + cc, :, cols] = jnp.exp(tot)
                    kx.append(kd * jnp.exp(-tot))
                    tot_min = jnp.minimum(tot_min, tot)
                p = _dot_nt(qe, jnp.concatenate(kx, axis=0).astype(BF16))
                sc_scr[d, rows, cols] = jnp.where(lv >= 0, p, 0.0).astype(BF16)
        return tot_min

    tot_min = lax.fori_loop(0, n_groups, group_step, jnp.zeros((1, HA_DK), F32), unroll=True)

    def one_factor():
        for d in range(2):
            for grp in range(n_groups):
                rows = slice(grp * GROUP_ROWS, (grp + 1) * GROUP_ROWS)
                for cols in head_cols:
                    oi_scr[d, rows, cols] = _dot(sc_scr[d, rows, cols], v_scr[d, rows, cols])

    def per_level():
        for d in range(2):

            def group_step(grp, carry, d=d):
                rows = pl.ds(pl.multiple_of(grp * GROUP_ROWS, GROUP_ROWS), GROUP_ROWS)
                lv = lv_ref[d]
                for cols in head_cols:
                    q, k, g_hi, g_lo = gates(d, rows, cols)
                    x = jnp.exp(_dot(e_ref[d], g_hi) + _dot(e_ref[d], g_lo))
                    sc = jnp.where(lv == N_LEVELS, _dot_nt(q.astype(BF16), k.astype(BF16)), 0.0)
                    for lvl in range(N_LEVELS):
                        xl = x[lvl * GROUP_ROWS:(lvl + 1) * GROUP_ROWS]
                        p = _dot_nt((q * xl).astype(BF16), (k * xl).astype(BF16))
                        sc = jnp.where(lv == lvl, p, sc)
                    oi_scr[d, rows, cols] = _dot(sc.astype(BF16), v_scr[d, rows, cols])
                return carry

            lax.fori_loop(0, n_groups, group_step, 0)

    lax.cond(jnp.min(tot_min) > -SAFE_LOG_DECAY, one_factor, per_level)

    out_refs = (of_ref, ob_ref)

    def chunk_step(c, carry):
        for d in range(2):
            cc = c if d == 0 else n_chunks - 1 - c
            r0 = pl.multiple_of(cc * CHUNK, CHUNK)
            rows = pl.ds(r0, CHUNK)
            for h in range(HA_HEADS):
                cols = slice(h * HA_DK, (h + 1) * HA_DK)
                st = st_scr[d, h]
                o_state = _dot_nt(qe_scr[d, rows, cols], st.astype(BF16))
                out_refs[d][0, rows, cols] = oi_scr[d, rows, cols] + o_state
                upd = _dot_tn(v_scr[d, rows, cols], kd_scr[d, rows, cols])
                st_scr[d, h] = st * dec_scr[d, cc, :, cols] + upd
        return carry

    lax.fori_loop(0, n_chunks, chunk_step, 0, unroll=True)

    @pl.when(i == n_i - 1)
    def _():
        for d, s_ref in enumerate((sf_ref, sb_ref)):
            for h in range(HA_HEADS):
                s_ref[0, h] = st_scr[d, h].T


def _hgrn(p_h, lb_logits, s0_f, s0_b, slot):
    b, l, _ = p_h.shape
    lt = min(SEQ_TILE, l)
    assert l % lt == 0 and lt % GROUP_ROWS == 0
    n_t = l // lt
    has_state = s0_f is not None
    if not has_state:
        s0_f = jnp.zeros((1, HA_HEADS, HA_DK, HA_DV), F32)
        s0_b = s0_f
    e_np, cum_np, lv_np = _hgrn_tables()
    e_mat = jnp.asarray(e_np, BF16)
    cum = jnp.asarray(cum_np, BF16)
    lv = jnp.asarray(lv_np, jnp.int32)
    w = HA_W

    def sec(idx, rev):
        if rev:
            return pl.BlockSpec((1, lt, w), lambda bi, i: (bi, n_t - 1 - i, idx))
        return pl.BlockSpec((1, lt, w), lambda bi, i: (bi, i, idx))

    state_map = (lambda bi, i: (bi, 0, 0, 0)) if has_state else (lambda bi, i: (0, 0, 0, 0))
    in_specs = [
        sec(0, False), sec(3, False), sec(1, False),
        sec(0, True), sec(3, True), sec(2, True),
        pl.BlockSpec(lb_logits.shape, lambda bi, i: (0, 0, 0)),
        pl.BlockSpec(e_mat.shape, lambda bi, i: (0, 0, 0)),
        pl.BlockSpec(cum.shape, lambda bi, i: (0, 0, 0)),
        pl.BlockSpec(lv.shape, lambda bi, i: (0, 0, 0)),
        pl.BlockSpec((1, HA_HEADS, HA_DK, HA_DV), state_map),
        pl.BlockSpec((1, HA_HEADS, HA_DK, HA_DV), state_map),
    ]
    out_shape = [
        jax.ShapeDtypeStruct((b, l, w), F32),
        jax.ShapeDtypeStruct((b, l, w), F32),
        jax.ShapeDtypeStruct((b, HA_HEADS, HA_DK, HA_DV), F32),
        jax.ShapeDtypeStruct((b, HA_HEADS, HA_DK, HA_DV), F32),
    ]
    out_specs = [
        pl.BlockSpec((1, lt, w), lambda bi, i: (bi, i, 0)),
        pl.BlockSpec((1, lt, w), lambda bi, i: (bi, n_t - 1 - i, 0)),
        pl.BlockSpec((1, HA_HEADS, HA_DK, HA_DV), lambda bi, i: (bi, 0, 0, 0)),
        pl.BlockSpec((1, HA_HEADS, HA_DK, HA_DV), lambda bi, i: (bi, 0, 0, 0)),
    ]
    scratch = [
        pltpu.VMEM((2, HA_HEADS, HA_DV, HA_DK), F32),
        pltpu.VMEM((2, lt, w), BF16),
        pltpu.VMEM((2, lt, w), BF16),
        pltpu.VMEM((2, lt, w), BF16),
        pltpu.VMEM((2, lt, w), BF16),
        pltpu.VMEM((2, lt, w), F32),
        pltpu.VMEM((2, lt // CHUNK, 1, w), F32),
    ]
    return pl.pallas_call(
        functools.partial(_hgrn_kernel, lt=lt, slot=slot, has_state=has_state),
        out_shape=out_shape,
        grid=(b, n_t),
        in_specs=in_specs,
        out_specs=out_specs,
        scratch_shapes=scratch,
        compiler_params=_cparams(("parallel", "arbitrary")),
        name="hgrn2_scan",
    )(p_h, p_h, p_h, p_h, p_h, p_h, lb_logits, e_mat, cum, lv, s0_f, s0_b)


def _ab_inproj_kernel(*refs, rope, key_major):
    if rope:
        (x_ref, mod_ref, wh_ref, wm_ref, cos_ref, sin_ref, qg_ref, kvg_ref, wq1_ref, wq2_ref, wk_ref,
         wv_ref) = refs[:12]
        ph_ref, q_ref, k_ref, v_ref, ckv_ref, kr_ref = refs[12:]
    else:
        x_ref, mod_ref, wh_ref, wm_ref, qg_ref, kvg_ref, wq1_ref, wk_ref, wv_ref = refs[:9]
        ph_ref, q_ref, k_ref, v_ref, ckv_ref, kr_ref = refs[9:]
    scale = (MLA_DN + MLA_DR) ** -0.5 * LOG2_E
    mod = mod_ref[0]
    hmod = (x_ref[0] * (1.0 + mod[1:2]) + mod[0:1]).astype(BF16)
    ph_ref[0] = _dot(hmod, wh_ref[...])
    pm = _dot(hmod, wm_ref[...])
    kr_ref[0] = pm[:, MLA_Q_RANK + MLA_KV_RANK:MLA_Q_RANK + MLA_KV_RANK + LANES]
    cq = pm[:, :MLA_Q_RANK]
    cq = cq * lax.rsqrt(jnp.mean(cq * cq, axis=-1, keepdims=True) + RMS_EPS) * qg_ref[...]
    cq = cq.astype(BF16)
    ckv = pm[:, MLA_Q_RANK:MLA_Q_RANK + MLA_KV_RANK]
    ckv = ckv * lax.rsqrt(jnp.mean(ckv * ckv, axis=-1, keepdims=True) + RMS_EPS) * kvg_ref[...]
    ckv_ref[0] = ckv
    ckv = ckv.astype(BF16)
    kr0 = MLA_Q_RANK + MLA_KV_RANK
    kr = pm[:, kr0:kr0 + LANES]
    qa = _dot(cq, wq1_ref[...])
    kn = _dot(ckv, wk_ref[...])
    if key_major:
        v_ref[0] = _dot_nt(wv_ref[...], ckv).astype(BF16)
    else:
        v_ref[0] = _dot(ckv, wv_ref[...]).astype(BF16)
    if rope:
        cos = cos_ref[...]
        sin = sin_ref[...]
        qb = _dot(cq, wq2_ref[...])
        kr = kr * cos + pm[:, kr0 + LANES:kr0 + 2 * LANES] * sin
    for h in range(MLA_HEADS):
        sl = slice(h * HEAD_PAD, (h + 1) * HEAD_PAD)
        qh = qa[:, sl]
        if rope:
            qh = qh * cos + qb[:, sl] * sin
        q_ref[0, :, sl] = (qh * scale).astype(BF16)
        k_ref[0, :, sl] = (kn[:, sl] + kr).astype(BF16)


def _ab_inproj(x, mod, w_h, w_m, rope_tabs, qg, kvg, wq1, wq2, wk, wv):
    b, l, d = x.shape
    tm = PROJ_TILE
    assert l % tm == 0
    rope = rope_tabs is not None
    per_batch = mod.shape[0] > 1
    mod_map = (lambda i, j: (i, 0, 0)) if per_batch else (lambda i, j: (0, 0, 0))
    full = lambda a: pl.BlockSpec(a.shape, lambda i, j: (0,) * a.ndim, pipeline_mode=pl.Buffered(1))
    row = lambda width: pl.BlockSpec((1, tm, width), lambda i, j: (i, j, 0))
    hw = MLA_HEADS * HEAD_PAD
    inputs = [x, mod, w_h, w_m]
    in_specs = [row(d), pl.BlockSpec((1, 6, d), mod_map), full(w_h), full(w_m)]
    if rope:
        inputs += list(rope_tabs)
        in_specs += [pl.BlockSpec((tm, LANES), lambda i, j: (j, 0))] * 2
    ws = [qg, kvg, wq1] + ([wq2] if rope else []) + [wk, wv]
    inputs += ws
    in_specs += [full(a) for a in ws]
    key_major = wv.shape[1] == MLA_KV_RANK
    vw = MLA_HEADS * MLA_DV
    shapes = [(b, l, w_h.shape[1]), (b, l, hw), (b, l, hw), (b, vw, l) if key_major else (b, l, hw),
              (b, l, MLA_KV_RANK), (b, l, LANES)]
    dtypes = [F32, BF16, BF16, BF16, F32, F32]
    out_specs = [row(s[2]) for s in shapes]
    if key_major:
        out_specs[3] = pl.BlockSpec((1, vw, tm), lambda i, j: (i, 0, j))
    return pl.pallas_call(
        functools.partial(_ab_inproj_kernel, rope=rope, key_major=key_major),
        out_shape=[jax.ShapeDtypeStruct(s, dt) for s, dt in zip(shapes, dtypes)],
        grid=(b, l // tm),
        in_specs=in_specs,
        out_specs=out_specs,
        compiler_params=_cparams(("parallel", "parallel")),
        name="ab_inproj",
    )(*inputs)


def _mla_ctx_kernel(ckv_ref, kr_ref, place_ref, wk_ref, wvt_ref, k_ref, vt_ref):
    ckv = ckv_ref[0].astype(BF16)
    kr = _dot(kr_ref[0].astype(BF16), place_ref[...])
    kn = _dot(ckv, wk_ref[...])
    vt_ref[0] = _dot_nt(wvt_ref[...], ckv).astype(BF16)
    for h in range(MLA_HEADS):
        sl = slice(h * HEAD_PAD, (h + 1) * HEAD_PAD)
        k_ref[0, :, sl] = (kn[:, sl] + kr).astype(BF16)


def _mla_ctx(ctx_ckv, ctx_kr, wk, wvt):
    b, lc, _ = ctx_ckv.shape
    hw = MLA_HEADS * HEAD_PAD
    vw = MLA_HEADS * MLA_DV
    place = np.zeros((MLA_DR, LANES), np.float32)
    place[np.arange(MLA_DR), MLA_DN + np.arange(MLA_DR)] = 1.0
    place = jnp.asarray(place, BF16)
    full = lambda a: pl.BlockSpec(a.shape, lambda i: (0,) * a.ndim)
    return pl.pallas_call(
        _mla_ctx_kernel,
        out_shape=[jax.ShapeDtypeStruct((b, lc, hw), BF16), jax.ShapeDtypeStruct((b, vw, lc), BF16)],
        grid=(b,),
        in_specs=[pl.BlockSpec((1, lc, MLA_KV_RANK), lambda i: (i, 0, 0)),
                  pl.BlockSpec((1, lc, MLA_DR), lambda i: (i, 0, 0)),
                  full(place), full(wk), full(wvt)],
        out_specs=[pl.BlockSpec((1, lc, hw), lambda i: (i, 0, 0)), pl.BlockSpec((1, vw, lc), lambda i: (i, 0, 0))],
        compiler_params=_cparams(("parallel",)),
        name="mla_ctx_keys",
    )(ctx_ckv, ctx_kr, place, wk, wvt)


def _attn_long_kernel(q_ref, k_ref, vt_ref, kc_ref, vtc_ref, o_ref, s_a, s_b, m_a, m_b):
    i = pl.program_id(2)
    tq = q_ref.shape[1]
    lk = k_ref.shape[1]
    tk = ATTN_KEY_BLOCK
    n_blk = lk // tk
    slabs = [slice(hh * HEAD_PAD, (hh + 1) * HEAD_PAD) for hh in range(2)]
    vrows = [slice(hh * MLA_DV, (hh + 1) * MLA_DV) for hh in range(2)]
    q_t = [q_ref[0, :, sl].astype(F32).T.astype(BF16) for sl in slabs]
    crow = slice(lk, lk + kc_ref.shape[1])

    def scorer(s_cur):
        def scores(hh, k_blk, rows, m_run):
            s = _dot(k_blk, q_t[hh])
            s_cur[hh, rows, :] = s
            return jnp.maximum(m_run, jnp.max(s.reshape(s.shape[0] // SUBLANES, SUBLANES, tq), axis=0))
        return scores

    m_init = (jnp.full((SUBLANES, tq), -jnp.inf, F32),) * 2

    @pl.when(i == 0)
    def _():
        scores = scorer(s_a)

        def step(j, m_runs):
            rows = pl.ds(pl.multiple_of(j * tk, tk), tk)
            return tuple(scores(hh, k_ref[0, rows, slabs[hh]], rows, m_runs[hh]) for hh in range(2))

        m_runs = lax.fori_loop(0, n_blk, step, m_init, unroll=2)
        for hh in range(2):
            m_a[hh] = scores(hh, kc_ref[0, :, slabs[hh]], crow, m_runs[hh])

    def run(s_cur, m_cur, s_prv, m_prv):
        scores = scorer(s_cur)

        def weigh(hh, vt_blk, rows, m, acc):
            p = jnp.exp2(s_prv[hh, rows, :] - m).astype(BF16)
            lhs = jnp.concatenate([vt_blk, jnp.ones((2 * SUBLANES, vt_blk.shape[1]), BF16)], axis=0)
            return acc + _dot(lhs, p)

        ms = [jnp.max(m_prv[hh], axis=0, keepdims=True) for hh in range(2)]

        def step(j, carry):
            m_runs, accs = carry
            rows = pl.ds(pl.multiple_of(j * tk, tk), tk)
            m_runs = tuple(scores(hh, k_ref[0, rows, slabs[hh]], rows, m_runs[hh]) for hh in range(2))
            accs = tuple(weigh(hh, vt_ref[0, vrows[hh], rows], rows, ms[hh], accs[hh]) for hh in range(2))
            return m_runs, accs

        carry = (m_init, (jnp.zeros((MLA_DV + 2 * SUBLANES, tq), F32),) * 2)
        m_runs, accs = lax.fori_loop(0, n_blk, step, carry, unroll=2)
        m_runs = tuple(scores(hh, kc_ref[0, :, slabs[hh]], crow, m_runs[hh]) for hh in range(2))
        accs = tuple(weigh(hh, vtc_ref[0, vrows[hh], :], crow, ms[hh], accs[hh]) for hh in range(2))
        for hh in range(2):
            m_cur[hh] = m_runs[hh]
        outs = [acc[:MLA_DV] / acc[MLA_DV:MLA_DV + 1] for acc in accs]
        o_ref[0] = jnp.concatenate(outs, axis=0).T.astype(BF16)

    @pl.when((i % 2 == 0) & (i > 0))
    def _():
        run(s_a, m_a, s_b, m_b)

    @pl.when(i % 2 == 1)
    def _():
        run(s_b, m_b, s_a, m_a)


def _attention_long(q, k, vt, kc, vtc):
    b, l, _ = q.shape
    lk = k.shape[1]
    lc = kc.shape[1]
    tq = ATTN_Q_TILE
    pw = 2 * HEAD_PAD
    vpw = 2 * MLA_DV
    assert lk % (2 * ATTN_KEY_BLOCK) == 0 and l % tq == 0
    n_q = l // tq
    return pl.pallas_call(
        _attn_long_kernel,
        out_shape=jax.ShapeDtypeStruct((b, l, MLA_HEADS * MLA_DV), BF16),
        grid=(b, MLA_HEADS // 2, n_q + 1),
        in_specs=[pl.BlockSpec((1, tq, pw), lambda bi, p, i: (bi, jnp.minimum(i, n_q - 1), p)),
                  pl.BlockSpec((1, lk, pw), lambda bi, p, i: (bi, 0, p)),
                  pl.BlockSpec((1, vpw, lk), lambda bi, p, i: (bi, p, 0)),
                  pl.BlockSpec((1, lc, pw), lambda bi, p, i: (bi, 0, p)),
                  pl.BlockSpec((1, vpw, lc), lambda bi, p, i: (bi, p, 0))],
        out_specs=pl.BlockSpec((1, tq, LANES), lambda bi, p, i: (bi, jnp.maximum(i - 1, 0), p)),
        scratch_shapes=[pltpu.VMEM((2, lk + lc, tq), F32), pltpu.VMEM((2, lk + lc, tq), F32),
                        pltpu.VMEM((2, SUBLANES, tq), F32), pltpu.VMEM((2, SUBLANES, tq), F32)],
        compiler_params=_cparams(("parallel", "parallel", "arbitrary")),
        name="mla_attention_long",
    )(q, k, vt, kc, vtc)


def _attn_kernel(q_ref, k_ref, v_ref, o_ref):
    for pair in range(MLA_HEADS // 2):
        out = None
        for hh in range(2):
            h0 = (2 * pair + hh) * HEAD_PAD
            sl = slice(h0, h0 + HEAD_PAD)
            s = _dot_nt(q_ref[0, :, sl], k_ref[0, :, sl])
            p = jnp.exp2(s - jnp.max(s, axis=-1, keepdims=True))
            den = jnp.sum(p, axis=-1, keepdims=True)
            o = _dot(p.astype(BF16), v_ref[0, :, sl]) / den
            out = o if out is None else out + o
        o_ref[0, :, pair * LANES:(pair + 1) * LANES] = out.astype(BF16)


def _attention(q, k, v):
    b, l, hw = q.shape
    assert k.shape[1] == l and l <= TOK_TILE
    blk = pl.BlockSpec((1, l, hw), lambda bi: (bi, 0, 0))
    return pl.pallas_call(
        _attn_kernel,
        out_shape=jax.ShapeDtypeStruct((b, l, MLA_HEADS * MLA_DV), BF16),
        grid=(b,),
        in_specs=[blk, blk, blk],
        out_specs=pl.BlockSpec((1, l, MLA_HEADS * MLA_DV), lambda bi: (bi, 0, 0)),
        compiler_params=_cparams(("parallel",)),
        name="mla_attention",
    )(q, k, v)


def _outproj_kernel(*refs, hgrn):
    if hgrn:
        x_ref, mod_ref, of_ref, ob_ref, gl_ref, gn_ref, b_ref, w_ref, lg_ref, lb_ref, o_ref = refs
        o = of_ref[0] + ob_ref[0]
        gate = _silu(gl_ref[0])
        gn = gn_ref[...]
        parts = []
        for h in range(HA_HEADS):
            sl = slice(h * HA_DV, (h + 1) * HA_DV)
            oh = o[:, sl]
            oh = oh * lax.rsqrt(jnp.mean(oh * oh, axis=-1, keepdims=True) + RMS_EPS) * gn
            parts.append((oh * gate[:, sl]).astype(BF16))
        a = jnp.concatenate(parts, axis=-1)
    else:
        x_ref, mod_ref, a_ref, b_ref, w_ref, lg_ref, lb_ref, o_ref = refs
        a = a_ref[0]
    half = a.shape[-1]
    y = _dot(a, w_ref[:half, :]) + _dot(b_ref[0], w_ref[half:, :])
    m = mod_ref[0]
    r = ALPHA * x_ref[0] + m[2:3] * y
    o_ref[0] = _layer_norm(r, lg_ref[...], lb_ref[...])


def _outproj(x, mod, a_inputs, b_in, w, ln_g, ln_b, hgrn):
    b, l, d = x.shape
    tm = PROJ_TILE
    assert l % tm == 0
    per_batch = mod.shape[0] > 1
    mod_map = (lambda i, j: (i, 0, 0)) if per_batch else (lambda i, j: (0, 0, 0))
    row = lambda width: pl.BlockSpec((1, tm, width), lambda i, j: (i, j, 0))
    full = lambda a: pl.BlockSpec(a.shape, lambda i, j: (0,) * a.ndim, pipeline_mode=pl.Buffered(1))
    inputs = [x, mod]
    in_specs = [row(d), pl.BlockSpec((1, 6, d), mod_map)]
    if hgrn:
        o_f, o_b, p_h, g_norm = a_inputs
        inputs += [o_f, o_b, p_h, g_norm]
        in_specs += [row(HA_W), row(HA_W),
                     pl.BlockSpec((1, tm, HA_W), lambda i, j: (i, j, 4)), full(g_norm)]
    else:
        inputs += [a_inputs]
        in_specs += [row(a_inputs.shape[-1])]
    inputs += [b_in, w, ln_g, ln_b]
    in_specs += [row(b_in.shape[-1]), full(w), full(ln_g), full(ln_b)]
    return pl.pallas_call(
        functools.partial(_outproj_kernel, hgrn=hgrn),
        out_shape=jax.ShapeDtypeStruct((b, l, d), F32),
        grid=(b, l // tm),
        in_specs=in_specs,
        out_specs=row(d),
        compiler_params=_cparams(("parallel", "parallel")),
        name="outproj_ln",
    )(*inputs)


def _conv_kernel(pm_ref, pp_ref, pn_ref, scw_ref, cfw_ref, cfb_ref, cfg_ref, cfbeta_ref,
                 ysc_ref, ycf_ref, ext_sc, ext_cf, *, lt):
    i = pl.program_id(1)
    n_i = pl.num_programs(1)
    w = SC_W

    def sc_in(p):
        return p[:, w:2 * w] * p[:, 2 * w:3 * w]

    def cf_in(p):
        return p[:, 3 * w:3 * w + CF_W] * _sigmoid(p[:, 3 * w + CF_W:3 * w + 2 * CF_W])

    pm = pm_ref[0]
    pp = pp_ref[0]
    pn = pn_ref[0]
    has_prev = i > 0
    has_next = i < n_i - 1
    n_ext = lt + 2 * HALO
    for ext, conv_in in ((ext_sc, sc_in), (ext_cf, cf_in)):
        ext[0, 0:HALO, :] = jnp.where(has_prev, conv_in(pp), 0.0)
        ext[0, HALO:HALO + lt, :] = conv_in(pm)
        ext[0, HALO + lt:, :] = jnp.where(has_next, conv_in(pn), 0.0)
    for s in range(1, SUBLANES):
        ext_cf[s, 0:n_ext - SUBLANES, :] = ext_cf[0, s:s + n_ext - SUBLANES, :]
    sc_shifts = sorted({(HALO - SC_K // 2 + j) % SUBLANES for j in range(SC_K)} - {0})
    for s in sc_shifts:
        ext_sc[s, 0:n_ext - SUBLANES, :] = ext_sc[0, s:s + n_ext - SUBLANES, :]

    def tap(ext, off, rows):
        s = off % SUBLANES
        return ext[s, off - s:off - s + rows, :]

    rb = 32
    for r in range(lt // rb):
        base = HALO + r * rb
        acc = None
        for j in range(SC_K):
            term = jnp.tile(scw_ref[j], (rb // SUBLANES, 1)) * tap(ext_sc, base - SC_K // 2 + j, rb)
            acc = term if acc is None else acc + term
        ysc_ref[0, r * rb:(r + 1) * rb, :] = (pm[r * rb:(r + 1) * rb, 0:w] * acc).astype(BF16)
        acc = None
        for j in range(CF_K):
            term = jnp.tile(cfw_ref[j], (rb // SUBLANES, 1)) * tap(ext_cf, base - CF_K // 2 + j, rb)
            acc = term if acc is None else acc + term
        u = _layer_norm(acc + cfb_ref[...], cfg_ref[...], cfbeta_ref[...])
        ycf_ref[0, r * rb:(r + 1) * rb, :] = _silu(u).astype(BF16)


def _conv_mixers(p1, sc_w, cf_w, cf_b, cf_g, cf_beta):
    b, l, width = p1.shape
    lt = min(SEQ_TILE, l)
    assert l % lt == 0 and lt % HALO == 0
    hb = lt // HALO
    n_h = l // HALO
    sc_w = jnp.broadcast_to(sc_w[:, None, :], (sc_w.shape[0], SUBLANES, sc_w.shape[1]))
    cf_w = jnp.broadcast_to(cf_w[:, None, :], (cf_w.shape[0], SUBLANES, cf_w.shape[1]))
    full = lambda a: pl.BlockSpec(a.shape, lambda bi, i: (0,) * a.ndim)
    return pl.pallas_call(
        functools.partial(_conv_kernel, lt=lt),
        out_shape=[jax.ShapeDtypeStruct((b, l, SC_W), BF16), jax.ShapeDtypeStruct((b, l, CF_W), BF16)],
        grid=(b, l // lt),
        in_specs=[
            pl.BlockSpec((1, lt, width), lambda bi, i: (bi, i, 0)),
            pl.BlockSpec((1, HALO, width), lambda bi, i: (bi, jnp.maximum(i * hb - 1, 0), 0)),
            pl.BlockSpec((1, HALO, width), lambda bi, i: (bi, jnp.minimum((i + 1) * hb, n_h - 1), 0)),
            full(sc_w), full(cf_w), full(cf_b), full(cf_g), full(cf_beta),
        ],
        out_specs=[pl.BlockSpec((1, lt, SC_W), lambda bi, i: (bi, i, 0)),
                   pl.BlockSpec((1, lt, CF_W), lambda bi, i: (bi, i, 0))],
        scratch_shapes=[pltpu.VMEM((SUBLANES, lt + 2 * HALO, SC_W), F32),
                        pltpu.VMEM((SUBLANES, lt + 2 * HALO, CF_W), F32)],
        compiler_params=_cparams(("parallel", "parallel")),
        name="conv_mixers",
    )(p1, p1, p1, sc_w, cf_w, cf_b, cf_g, cf_beta)


def _route_t(lt):
    t = lt.shape[1]
    row = lax.broadcasted_iota(jnp.int32, (SUBLANES, t), 0).astype(F32)
    neg = -jnp.inf
    big = float(LANES)
    gl = jnp.where(row < N_GROUPS, lt[N_EXPERTS:N_EXPERTS + SUBLANES], neg)
    gmax = jnp.max(gl, axis=0, keepdims=True)
    p_g = 1.0 / jnp.sum(jnp.exp(gl - gmax), axis=0, keepdims=True)
    g_sel = jnp.min(jnp.where(gl == gmax, row, big), axis=0, keepdims=True)
    el = lt[0:EXP_PER_GROUP]
    for gi in range(1, N_GROUPS):
        el = jnp.where(g_sel == gi, lt[gi * EXP_PER_GROUP:(gi + 1) * EXP_PER_GROUP], el)
    v1 = jnp.max(el, axis=0, keepdims=True)
    i1 = jnp.min(jnp.where(el == v1, row, big), axis=0, keepdims=True)
    el2 = jnp.where(row == i1, neg, el)
    v2 = jnp.max(el2, axis=0, keepdims=True)
    i2 = jnp.min(jnp.where(el2 == v2, row, big), axis=0, keepdims=True)
    e2 = jnp.exp(v2 - v1)
    w1 = p_g / (1.0 + e2)
    w2 = p_g * e2 / (1.0 + e2)
    comb = jnp.where(row == i1, w1, 0.0) + jnp.where(row == i2, w2, 0.0)
    onehot = jnp.where(row == g_sel, 1.0, 0.0)
    return onehot, comb


def _moe_kernel(xc_ref, xl_ref, mod_ref, wr_ref, tri_ref, wg_ref, wu_ref, wd_ref, lg_ref, lb_ref, oc_ref, ol_ref,
                hb_scr, os_scr, pt_scr, cc_scr, *, n_ctx):
    is_ctx = pl.program_id(0) < n_ctx
    m = mod_ref[0]
    tm, cap = pt_scr.shape
    d = xc_ref.shape[-1]
    half = d // 2
    cw = EXP_PER_GROUP * EXP_FF

    h = jnp.where(is_ctx, xc_ref[0], xl_ref[0]) * (1.0 + m[4:5]) + m[3:4]
    h_hi = h.astype(BF16)
    hb_scr[...] = h_hi
    h_lo = (h - h_hi.astype(F32)).astype(BF16)
    l2 = _dot(h_hi, wr_ref[...])
    logits = l2[:, :LANES] + l2[:, LANES:] + _dot(h_lo, wr_ref[:, :LANES])
    onehot, comb = _route_t(logits.T)
    rank = _dot(onehot.astype(BF16), tri_ref[...])
    cnt = jnp.sum(onehot, axis=1, keepdims=True)
    n_blocks = jnp.floor((cnt + (MOE_BLOCK - 0.5)) * (1.0 / MOE_BLOCK))
    padded = n_blocks * MOE_BLOCK
    start = jnp.zeros((1, 1), F32)
    first_block = jnp.zeros((1, 1), F32)
    dest = jnp.zeros((1, tm), F32)
    first, count = [], []
    for gi in range(N_GROUPS):
        dest = dest + onehot[gi:gi + 1] * (start + rank[gi:gi + 1])
        first.append(first_block[0, 0].astype(jnp.int32))
        count.append(n_blocks[gi, 0].astype(jnp.int32))
        start = start + padded[gi:gi + 1]
        first_block = first_block + n_blocks[gi:gi + 1]
    aux = jnp.concatenate([jnp.broadcast_to(dest, (SUBLANES, tm)), comb,
                           jnp.zeros((LANES - 2 * SUBLANES, tm), F32)], axis=0).T
    lane = lax.broadcasted_iota(jnp.int32, (tm, LANES), 1)
    comb_tok = jnp.where((lane >= SUBLANES) & (lane < 2 * SUBLANES), aux, 0.0)
    c_hi = comb_tok.astype(BF16)
    c_mid = (comb_tok - c_hi.astype(F32)).astype(BF16)
    slot_t = lax.broadcasted_iota(jnp.int32, (tm, cap), 1).astype(F32)
    pt_scr[...] = jnp.where(slot_t == aux[:, 0:1], 1.0, 0.0).astype(BF16)
    cc_scr[...] = jnp.concatenate([c_hi, c_mid], axis=1)
    os_scr[...] = jnp.zeros_like(os_scr)
    block_slot = lax.broadcasted_iota(jnp.int32, (MOE_BLOCK, tm), 0).astype(F32)

    for gi in range(N_GROUPS):
        cols = slice(gi * cw, (gi + 1) * cw)

        def block(i, carry, gi=gi, cols=cols):
            blk = first[gi] + i
            r0 = pl.multiple_of(blk * MOE_BLOCK, 2 * SUBLANES)
            rows = pl.ds(r0, MOE_BLOCK)
            perm = jnp.where(block_slot + (blk * MOE_BLOCK).astype(F32) == dest, 1.0, 0.0).astype(BF16)
            hs = _dot(perm, hb_scr[...]).astype(BF16)
            cs2 = _dot(perm, cc_scr[...])
            cs = cs2[:, :LANES] + cs2[:, LANES:]
            hid = _silu(_dot(hs, wg_ref[:, cols])) * _dot(hs, wu_ref[:, cols])
            hid = jnp.concatenate(
                [hid[:, e * EXP_FF:(e + 1) * EXP_FF] * cs[:, SUBLANES + e:SUBLANES + e + 1]
                 for e in range(EXP_PER_GROUP)], axis=1)
            os_scr[rows, :] = _dot(hid.astype(BF16), wd_ref[cols, :]).astype(BF16)
            return carry

        lax.fori_loop(0, count[gi], block, 0)

    y = jnp.concatenate([_dot(pt_scr[...], os_scr[:, :half]), _dot(pt_scr[...], os_scr[:, half:])], axis=1)
    r = ALPHA * jnp.where(is_ctx, xc_ref[0], xl_ref[0]) + m[5:6] * y
    out = _layer_norm(r, lg_ref[...], lb_ref[...])

    @pl.when(is_ctx)
    def _():
        oc_ref[0] = out

    @pl.when(jnp.logical_not(is_ctx))
    def _():
        ol_ref[0] = out


def _moe(x_ctx, x_lat, mods, wr, wg, wu, wd, ln_g, ln_b):
    tm = MOE_TILE
    d = x_lat.shape[-1]
    x_c = x_ctx.reshape(-1, tm, d)
    n_ctx = x_c.shape[0]
    b, l, _ = x_lat.shape
    assert l % tm == 0 and mods.shape[0] == b + 1
    per_b = l // tm
    cap = (tm + N_GROUPS * (MOE_BLOCK - 1)) // MOE_BLOCK * MOE_BLOCK
    cap = -(-cap // LANES) * LANES
    tri = jnp.asarray(np.triu(np.ones((tm, tm), np.float32), k=1), BF16)
    ctx_map = lambda t: (jnp.minimum(t, n_ctx - 1), 0, 0)
    lat_t = lambda t: jnp.maximum(t - n_ctx, 0)
    lat_map = lambda t: (lat_t(t) // per_b, lat_t(t) % per_b, 0)
    mod_map = lambda t: (jnp.where(t < n_ctx, b, lat_t(t) // per_b), 0, 0)
    full = lambda a: pl.BlockSpec(a.shape, lambda t: (0,) * a.ndim, pipeline_mode=pl.Buffered(1))
    y_c, y_l = pl.pallas_call(
        functools.partial(_moe_kernel, n_ctx=n_ctx),
        out_shape=[jax.ShapeDtypeStruct(x_c.shape, F32), jax.ShapeDtypeStruct(x_lat.shape, F32)],
        grid=(n_ctx + b * per_b,),
        in_specs=[
            pl.BlockSpec((1, tm, d), ctx_map), pl.BlockSpec((1, tm, d), lat_map), pl.BlockSpec((1, 6, d), mod_map),
            full(wr), full(tri), full(wg), full(wu), full(wd), full(ln_g), full(ln_b),
        ],
        out_specs=[pl.BlockSpec((1, tm, d), ctx_map), pl.BlockSpec((1, tm, d), lat_map)],
        scratch_shapes=[pltpu.VMEM((tm, d), BF16), pltpu.VMEM((cap, d), BF16),
                        pltpu.VMEM((tm, cap), BF16), pltpu.VMEM((tm, 2 * LANES), BF16)],
        compiler_params=_cparams(("arbitrary",)),
        name="hier_moe_ln",
    )(x_c, x_lat, mods, wr, tri, wg, wu, wd, ln_g, ln_b)
    return y_c.reshape(x_ctx.shape), y_l


def _rope_swap_perm():
    idx = np.arange(MLA_DR)
    return idx ^ (MLA_DR // 4)


def _rope_tables(n_tok):
    rows = n_tok // GRID_W
    pos_r = jnp.repeat(jnp.arange(rows, dtype=F32), GRID_W)
    pos_c = (jnp.arange(rows * GRID_W) % GRID_W).astype(F32)
    n_freq = MLA_DR // 4
    inv = ROPE_BASE ** (-jnp.arange(n_freq, dtype=F32) / n_freq)
    ang = jnp.stack([pos_r[:, None] * inv, pos_c[:, None] * inv], axis=1)
    cos, sin = jnp.cos(ang), jnp.sin(ang)
    cos32 = jnp.stack([cos, cos], axis=2).reshape(n_tok, MLA_DR)
    sin32 = jnp.stack([-sin, sin], axis=2).reshape(n_tok, MLA_DR)
    pad_hi = LANES - MLA_DN - MLA_DR
    cos_t = jnp.concatenate([jnp.ones((n_tok, MLA_DN), F32), cos32, jnp.zeros((n_tok, pad_hi), F32)], axis=1)
    sin_t = jnp.concatenate([jnp.zeros((n_tok, MLA_DN), F32), sin32, jnp.zeros((n_tok, pad_hi), F32)], axis=1)
    return cos_t, sin_t


def _ab_weights(w_in, w_uq, w_ukv):
    perm = _rope_swap_perm()
    pad_hi = LANES - MLA_DN - MLA_DR
    w_h = w_in[:, :5 * HA_W]
    cq = w_in[:, 5 * HA_W:5 * HA_W + MLA_Q_RANK]
    ckv = w_in[:, 5 * HA_W + MLA_Q_RANK:5 * HA_W + MLA_Q_RANK + MLA_KV_RANK]
    kr = w_in[:, 5 * HA_W + MLA_Q_RANK + MLA_KV_RANK:]
    d = w_in.shape[0]
    z_lo = jnp.zeros((d, MLA_DN), F32)
    z_hi = jnp.zeros((d, pad_hi), F32)
    w_m = jnp.concatenate([cq, ckv, z_lo, kr, z_hi, z_lo, kr[:, perm], z_hi], axis=1)
    uq = w_uq.reshape(MLA_Q_RANK, MLA_HEADS, MLA_DN + MLA_DR)
    q_nope, q_rope = uq[..., :MLA_DN], uq[..., MLA_DN:]
    zq_hi = jnp.zeros((MLA_Q_RANK, MLA_HEADS, pad_hi), F32)
    wq1 = jnp.concatenate([q_nope, q_rope, zq_hi], axis=-1).reshape(MLA_Q_RANK, -1)
    wq2 = jnp.concatenate([jnp.zeros_like(q_nope), q_rope[..., perm], zq_hi], axis=-1).reshape(MLA_Q_RANK, -1)
    ukv = w_ukv.reshape(MLA_KV_RANK, MLA_HEADS, MLA_DN + MLA_DV)
    k_nope, v = ukv[..., :MLA_DN], ukv[..., MLA_DN:]
    wk = jnp.concatenate([k_nope, jnp.zeros((MLA_KV_RANK, MLA_HEADS, LANES - MLA_DN), F32)], axis=-1)
    wk = wk.reshape(MLA_KV_RANK, -1)
    v_pairs = v.reshape(MLA_KV_RANK, MLA_HEADS // 2, 2, MLA_DV)
    zv = jnp.zeros_like(v_pairs[:, :, 0])
    wv = jnp.stack([jnp.concatenate([v_pairs[:, :, 0], zv], axis=-1),
                    jnp.concatenate([zv, v_pairs[:, :, 1]], axis=-1)], axis=2).reshape(MLA_KV_RANK, -1)
    wvt = v.reshape(MLA_KV_RANK, MLA_HEADS * MLA_DV).T
    bf = lambda a: a.astype(BF16)
    return bf(w_h), bf(w_m), bf(wq1), bf(wq2), bf(wk), bf(wv), bf(wvt)


def _moe_weights(w_group, w_expert, w_gate, w_up, w_down):
    d = w_group.shape[0]
    wr = jnp.concatenate([w_expert, w_group, jnp.zeros((d, LANES - N_EXPERTS - N_GROUPS), F32)], axis=1)
    wr_hi = wr.astype(BF16)
    wr = jnp.concatenate([wr_hi, (wr - wr_hi.astype(F32)).astype(BF16)], axis=1)
    wg = jnp.transpose(w_gate, (1, 0, 2)).reshape(d, N_EXPERTS * EXP_FF).astype(BF16)
    wu = jnp.transpose(w_up, (1, 0, 2)).reshape(d, N_EXPERTS * EXP_FF).astype(BF16)
    wd = w_down.reshape(N_EXPERTS * EXP_FF, d).astype(BF16)
    return wr, wg, wu, wd


def kernel(x_prompt, x_sample, state_hgrn_fwd, state_hgrn_bwd, cache_mla_ckv, cache_mla_krope, c, c_ctx, mod_w, mod_b, ln_g, ln_b, ab_w_in, ab_w_out, hgrn_lb_logits, hgrn_norm_g, mla_q_norm_g, mla_w_uq, mla_kv_norm_g, mla_w_ukv, cd_w_in, cd_w_out, sc_conv_w, cf_conv_w, cf_conv_b, cf_ln_g, cf_ln_b, moe_w_group, moe_w_expert, moe_w_gate, moe_w_up, moe_w_down):
    dec_b = x_sample.shape[0]
    d = D_MODEL
    cvec = jnp.concatenate([c, c_ctx[None, :], jnp.zeros((SUBLANES - dec_b - 1, d), F32)], axis=0)
    mods = _mod_vectors(cvec, mod_w, mod_b)
    rope_tabs = _rope_tables(x_sample.shape[1])
    xp, xs = x_prompt, x_sample
    new_sf = new_sb = new_ckv = new_kr = None
    for l in range(DEPTH):
        mod_lat = mods[l, :dec_b].reshape(dec_b, 6, d)
        mod_ctx = mods[l, dec_b:dec_b + 1].reshape(1, 6, d)
        row = lambda a: a.reshape(1, -1)
        fold = lambda a: a.reshape(-1, PROJ_TILE, a.shape[-1])
        unfold = lambda a: a.reshape(x_prompt.shape[0], -1, a.shape[-1])
        if l % 2 == 0:
            e = l // 2
            w_h, w_m, wq1, wq2, wk, wv, wvt = _ab_weights(ab_w_in[e], mla_w_uq[e], mla_w_ukv[e])
            w_out = ab_w_out[e].astype(BF16)
            qg, kvg, gn = row(mla_q_norm_g[e]), row(mla_kv_norm_g[e]), row(hgrn_norm_g[e])
            ph_p, q_p, k_p, v_p, ckv_p, kr_p = map(unfold, _ab_inproj(
                fold(xp), mod_ctx, w_h, w_m, None, qg, kvg, wq1, None, wk, wv))
            of_p, ob_p, sf, sb = _hgrn(ph_p, hgrn_lb_logits, None, None, e)
            om_p = _attention(q_p, k_p, v_p)
            xp = unfold(_outproj(fold(xp), mod_ctx, (fold(of_p), fold(ob_p), fold(ph_p), gn), fold(om_p), w_out,
                                 row(ln_g[l, 0]), row(ln_b[l, 0]), True))
            ph_s, q_s, k_s, vt_s, _, _ = _ab_inproj(xs, mod_lat, w_h, w_m, rope_tabs, qg, kvg, wq1, wq2, wk, wvt)
            of_s, ob_s, _, _ = _hgrn(ph_s, hgrn_lb_logits, state_hgrn_fwd[:, e], state_hgrn_bwd[:, e], e)
            kc, vtc = _mla_ctx(cache_mla_ckv[:, e], cache_mla_krope[:, e], wk, wvt)
            om_s = _attention_long(q_s, k_s, vt_s, kc, vtc)
            xs = _outproj(xs, mod_lat, (of_s, ob_s, ph_s, gn), om_s, w_out, row(ln_g[l, 0]), row(ln_b[l, 0]), True)
            new_sf, new_sb, new_ckv = sf, sb, ckv_p
            new_kr = kr_p[:, :, MLA_DN:MLA_DN + MLA_DR]
        else:
            jx = l // 2
            w1 = cd_w_in[jx].astype(BF16)
            w_out = cd_w_out[jx].astype(BF16)
            cd = (sc_conv_w[jx], cf_conv_w[jx], row(cf_conv_b[jx]), row(cf_ln_g[jx]), row(cf_ln_b[jx]))
            (p1_p,) = _inproj(fold(xp), mod_ctx, [w1])
            ysc_p, ycf_p = _conv_mixers(unfold(p1_p), *cd)
            xp = unfold(_outproj(fold(xp), mod_ctx, fold(ysc_p), fold(ycf_p), w_out,
                                 row(ln_g[l, 0]), row(ln_b[l, 0]), False))
            (p1_s,) = _inproj(xs, mod_lat, [w1])
            ysc_s, ycf_s = _conv_mixers(p1_s, *cd)
            xs = _outproj(xs, mod_lat, ysc_s, ycf_s, w_out, row(ln_g[l, 0]), row(ln_b[l, 0]), False)
        wr, wg, wu, wd = _moe_weights(moe_w_group[l], moe_w_expert[l], moe_w_gate[l], moe_w_up[l], moe_w_down[l])
        mods_l = mods[l, :dec_b + 1].reshape(dec_b + 1, 6, d)
        xp, xs = _moe(xp, xs, mods_l, wr, wg, wu, wd, row(ln_g[l, 1]), row(ln_b[l, 1]))
    return (xp, xs, new_sf[:, None], new_sb[:, None], new_ckv[:, None], new_kr[:, None])
```

```python
import functools

import numpy as np
import jax
import jax.numpy as jnp
from jax import lax
from jax.experimental import pallas as pl
from jax.experimental.pallas import tpu as pltpu

F32 = jnp.float32
BF16 = jnp.bfloat16
HIGHEST = lax.Precision.HIGHEST

D_MODEL = 1024
DEPTH = 2
GRID_W = 64
HA_HEADS = 4
HA_DK = 128
HA_DV = 128
HA_W = HA_HEADS * HA_DK
CHUNK = 32
MLA_HEADS = 8
MLA_DN = 64
MLA_DR = 32
MLA_DV = 64
MLA_Q_RANK = 384
MLA_KV_RANK = 256
ROPE_BASE = 10000.0
SC_W = 512
SC_K = 3
CF_W = 512
CF_K = 31
N_GROUPS = 4
EXP_PER_GROUP = 8
N_EXPERTS = N_GROUPS * EXP_PER_GROUP
EXP_FF = 128
ALPHA = (2.0 * DEPTH) ** 0.25
LOG2_E = 1.4426950408889634
LN_EPS = 1e-5
RMS_EPS = 1e-6

LANES = 128
SUBLANES = 8
VMEM_LIMIT = 56 * 1024 * 1024

HEAD_PAD = LANES
PROJ_TILE = 512
SEQ_TILE = 512
TOK_TILE = 256
GROUP_ROWS = 128
N_LEVELS = 5
SAFE_LOG_DECAY = 60.0
ATTN_KEY_BLOCK = 512
ATTN_Q_TILE = 256
MOE_TILE = 512
MOE_BLOCK = 144
HALO = 16


def _cparams(sem):
    return pltpu.CompilerParams(dimension_semantics=sem, vmem_limit_bytes=VMEM_LIMIT)


def _sigmoid(x):
    return 0.5 * jnp.tanh(0.5 * x) + 0.5


def _silu(x):
    half = 0.5 * x
    return half * jnp.tanh(half) + half


def _layer_norm(r, g, b):
    mu = jnp.mean(r, axis=-1, keepdims=True)
    d = r - mu
    var = jnp.mean(d * d, axis=-1, keepdims=True)
    return d * lax.rsqrt(var + LN_EPS) * g + b


def _dot(a, b):
    return jnp.dot(a, b, preferred_element_type=F32)


def _dot_nt(a, b):
    return lax.dot_general(a, b, (((1,), (1,)), ((), ())), preferred_element_type=F32)


def _dot_tn(a, b):
    return lax.dot_general(a, b, (((0,), (0,)), ((), ())), preferred_element_type=F32)


def _mod_kernel(c_ref, w_ref, b_ref, o_ref):
    s = _silu(c_ref[...])
    o_ref[0] = jnp.dot(s, w_ref[0], precision=HIGHEST, preferred_element_type=F32) + b_ref[0]


def _mod_vectors(cvec, mod_w, mod_b):
    n_out = mod_w.shape[-1]
    tn = 1536
    return pl.pallas_call(
        _mod_kernel,
        out_shape=jax.ShapeDtypeStruct((DEPTH, SUBLANES, n_out), F32),
        grid=(DEPTH, n_out // tn),
        in_specs=[
            pl.BlockSpec((SUBLANES, D_MODEL), lambda l, j: (0, 0)),
            pl.BlockSpec((1, D_MODEL, tn), lambda l, j: (l, 0, j)),
            pl.BlockSpec((1, 1, tn), lambda l, j: (l, 0, j)),
        ],
        out_specs=pl.BlockSpec((1, SUBLANES, tn), lambda l, j: (l, 0, j)),
        compiler_params=_cparams(("arbitrary", "arbitrary")),
        name="mod_vectors",
    )(cvec, mod_w, mod_b.reshape(DEPTH, 1, n_out))


def _inproj_kernel(*refs, n_w):
    x_ref, mod_ref = refs[0], refs[1]
    w_refs = refs[2:2 + n_w]
    o_refs = refs[2 + n_w:]
    m = mod_ref[0]
    h = (x_ref[0] * (1.0 + m[1:2]) + m[0:1]).astype(BF16)
    for w_ref, o_ref in zip(w_refs, o_refs):
        o_ref[0] = _dot(h, w_ref[...])


def _inproj(x, mod, weights):
    b, l, d = x.shape
    tm = PROJ_TILE
    assert l % tm == 0
    per_batch = mod.shape[0] > 1
    mod_map = (lambda i, j: (i, 0, 0)) if per_batch else (lambda i, j: (0, 0, 0))
    in_specs = [pl.BlockSpec((1, tm, d), lambda i, j: (i, j, 0)),
                pl.BlockSpec((1, 6, d), mod_map)]
    in_specs += [pl.BlockSpec(w.shape, lambda i, j: (0, 0), pipeline_mode=pl.Buffered(1)) for w in weights]
    out_shape = [jax.ShapeDtypeStruct((b, l, w.shape[1]), F32) for w in weights]
    out_specs = [pl.BlockSpec((1, tm, w.shape[1]), lambda i, j: (i, j, 0)) for w in weights]
    return pl.pallas_call(
        functools.partial(_inproj_kernel, n_w=len(weights)),
        out_shape=out_shape,
        grid=(b, l // tm),
        in_specs=in_specs,
        out_specs=out_specs,
        compiler_params=_cparams(("parallel", "parallel")),
        name="modulate_inproj",
    )(x, mod, *weights)


def _hgrn_tables():
    n = GROUP_ROWS
    t = np.arange(n)[:, None]
    j = np.arange(n)[None, :]
    same_chunk = (t // CHUNK) == (j // CHUNK)
    e_f, e_b = [], []
    lv_f = np.full((n, n), -1, np.int32)
    for lvl in range(N_LEVELS):
        m = CHUNK >> (lvl + 1)
        blk0 = (t // (2 * m)) * (2 * m)
        r = blk0 + m - 1
        upper = t > r
        ef = np.where(upper, (j > r) & (j <= t), (j > t) & (j <= r))
        r2 = blk0 + m
        lower = t < r2
        eb = np.where(lower, (j >= t) & (j < r2), (j >= r2) & (j < t))
        e_f.append(ef)
        e_b.append(eb)
        s = np.arange(n)[None, :]
        same_blk = (t // (2 * m)) == (s // (2 * m))
        q_side = (t % (2 * m)) >= m
        k_side = (s % (2 * m)) < m
        lv_f[same_blk & q_side & k_side] = lvl
    lv_f[np.arange(n), np.arange(n)] = N_LEVELS
    e_f = np.concatenate(e_f, axis=0).astype(np.float32)
    e_b = np.concatenate(e_b, axis=0).astype(np.float32)
    e = np.stack([e_f, e_b])
    cum = np.stack([same_chunk & (j <= t), same_chunk & (j >= t)]).astype(np.float32)
    lv = np.stack([lv_f, lv_f.T])
    return e, cum, lv


def _hgrn_kernel(qf_ref, vf_ref, ff_ref, qb_ref, vb_ref, fb_ref, lbl_ref, e_ref, cum_ref, lv_ref, s0f_ref, s0b_ref,
                 of_ref, ob_ref, sf_ref, sb_ref,
                 st_scr, qe_scr, kd_scr, sc_scr, v_scr, oi_scr, dec_scr, *, lt, slot, has_state):
    i = pl.program_id(1)
    n_i = pl.num_programs(1)
    n_chunks = lt // CHUNK
    n_groups = lt // GROUP_ROWS
    chunks_per_group = GROUP_ROWS // CHUNK

    @pl.when(i == 0)
    def _():
        for d, s0_ref in enumerate((s0f_ref, s0b_ref)):
            for h in range(HA_HEADS):
                if has_state:
                    st_scr[d, h] = s0_ref[0, h].T
                else:
                    st_scr[d, h] = jnp.zeros((HA_DV, HA_DK), F32)

    lg = lbl_ref[...]
    n_slots = lg.shape[0]
    mx = lg[0]
    for s in range(1, n_slots):
        mx = jnp.maximum(mx, lg[s])
    ex = [jnp.exp(lg[s] - mx) for s in range(n_slots)]
    den = ex[0]
    for s in range(1, n_slots):
        den = den + ex[s]
    num = ex[0]
    for s in range(1, slot + 1):
        num = num + ex[s]
    lb_all = num / den

    dirs = ((qf_ref, vf_ref, ff_ref), (qb_ref, vb_ref, fb_ref))
    head_cols = [slice(h * HA_DK, (h + 1) * HA_DK) for h in range(HA_HEADS)]

    def gates(d, rows, cols):
        q_ref, _, f_ref = dirs[d]
        q = _silu(q_ref[0, rows, cols])
        lb_h = lb_all[d:d + 1, cols]
        f = lb_h + (1.0 - lb_h) * _sigmoid(f_ref[0, rows, cols])
        g = jnp.log(f)
        g_hi = g.astype(BF16)
        g_lo = (g - g_hi.astype(F32)).astype(BF16)
        return q, 1.0 - f, g_hi, g_lo

    def group_step(grp, tot_min):
        r0 = pl.multiple_of(grp * GROUP_ROWS, GROUP_ROWS)
        rows = pl.ds(r0, GROUP_ROWS)
        for d in range(2):
            cum = cum_ref[d]
            lv = lv_ref[d]
            for cols in head_cols:
                q, k, g_hi, g_lo = gates(d, rows, cols)
                bcum = _dot(cum, g_hi) + _dot(cum, g_lo)
                v_scr[d, rows, cols] = dirs[d][1][0, rows, cols].astype(BF16)
                qe = (q * jnp.exp(bcum)).astype(BF16)
                qe_scr[d, rows, cols] = qe
                kx = []
                for cc in range(chunks_per_group):
                    edge = cc * CHUNK + (CHUNK - 1 if d == 0 else 0)
                    tot = bcum[edge:edge + 1]
                    b_c = bcum[cc * CHUNK:(cc + 1) * CHUNK]
                    kd = k[cc * CHUNK:(cc + 1) * CHUNK] * jnp.exp(tot - b_c)
                    kd_scr[d, pl.ds(r0 + cc * CHUNK, CHUNK), cols] = kd.astype(BF16)
                    dec_scr[d, grp * chunks_per_group + cc, :, cols] = jnp.exp(tot)
                    kx.append(kd * jnp.exp(-tot))
                    tot_min = jnp.minimum(tot_min, tot)
                p = _dot_nt(qe, jnp.concatenate(kx, axis=0).astype(BF16))
                sc_scr[d, rows, cols] = jnp.where(lv >= 0, p, 0.0).astype(BF16)
        return tot_min

    tot_min = lax.fori_loop(0, n_groups, group_step, jnp.zeros((1, HA_DK), F32), unroll=True)

    def one_factor():
        for d in range(2):
            for grp in range(n_groups):
                rows = slice(grp * GROUP_ROWS, (grp + 1) * GROUP_ROWS)
                for cols in head_cols:
                    oi_scr[d, rows, cols] = _dot(sc_scr[d, rows, cols], v_scr[d, rows, cols])

    def per_level():
        for d in range(2):

            def group_step(grp, carry, d=d):
                rows = pl.ds(pl.multiple_of(grp * GROUP_ROWS, GROUP_ROWS), GROUP_ROWS)
                lv = lv_ref[d]
                for cols in head_cols:
                    q, k, g_hi, g_lo = gates(d, rows, cols)
                    x = jnp.exp(_dot(e_ref[d], g_hi) + _dot(e_ref[d], g_lo))
                    sc = jnp.where(lv == N_LEVELS, _dot_nt(q.astype(BF16), k.astype(BF16)), 0.0)
                    for lvl in range(N_LEVELS):
                        xl = x[lvl * GROUP_ROWS:(lvl + 1) * GROUP_ROWS]
                        p = _dot_nt((q * xl).astype(BF16), (k * xl).astype(BF16))
                        sc = jnp.where(lv == lvl, p, sc)
                    oi_scr[d, rows, cols] = _dot(sc.astype(BF16), v_scr[d, rows, cols])
                return carry

            lax.fori_loop(0, n_groups, group_step, 0)

    lax.cond(jnp.min(tot_min) > -SAFE_LOG_DECAY, one_factor, per_level)

    out_refs = (of_ref, ob_ref)

    def chunk_step(c, carry):
        for d in range(2):
            cc = c if d == 0 else n_chunks - 1 - c
            r0 = pl.multiple_of(cc * CHUNK, CHUNK)
            rows = pl.ds(r0, CHUNK)
            for h in range(HA_HEADS):
                cols = slice(h * HA_DK, (h + 1) * HA_DK)
                st = st_scr[d, h]
                o_state = _dot_nt(qe_scr[d, rows, cols], st.astype(BF16))
                out_refs[d][0, rows, cols] = oi_scr[d, rows, cols] + o_state
                upd = _dot_tn(v_scr[d, rows, cols], kd_scr[d, rows, cols])
                st_scr[d, h] = st * dec_scr[d, cc, :, cols] + upd
        return carry

    lax.fori_loop(0, n_chunks, chunk_step, 0, unroll=True)

    @pl.when(i == n_i - 1)
    def _():
        for d, s_ref in enumerate((sf_ref, sb_ref)):
            for h in range(HA_HEADS):
                s_ref[0, h] = st_scr[d, h].T


def _hgrn(p_h, lb_logits, s0_f, s0_b, slot):
    b, l, _ = p_h.shape
    lt = min(SEQ_TILE, l)
    assert l % lt == 0 and lt % GROUP_ROWS == 0
    n_t = l // lt
    has_state = s0_f is not None
    if not has_state:
        s0_f = jnp.zeros((1, HA_HEADS, HA_DK, HA_DV), F32)
        s0_b = s0_f
    e_np, cum_np, lv_np = _hgrn_tables()
    e_mat = jnp.asarray(e_np, BF16)
    cum = jnp.asarray(cum_np, BF16)
    lv = jnp.asarray(lv_np, jnp.int32)
    w = HA_W

    def sec(idx, rev):
        if rev:
            return pl.BlockSpec((1, lt, w), lambda bi, i: (bi, n_t - 1 - i, idx))
        return pl.BlockSpec((1, lt, w), lambda bi, i: (bi, i, idx))

    state_map = (lambda bi, i: (bi, 0, 0, 0)) if has_state else (lambda bi, i: (0, 0, 0, 0))
    in_specs = [
        sec(0, False), sec(3, False), sec(1, False),
        sec(0, True), sec(3, True), sec(2, True),
        pl.BlockSpec(lb_logits.shape, lambda bi, i: (0, 0, 0)),
        pl.BlockSpec(e_mat.shape, lambda bi, i: (0, 0, 0)),
        pl.BlockSpec(cum.shape, lambda bi, i: (0, 0, 0)),
        pl.BlockSpec(lv.shape, lambda bi, i: (0, 0, 0)),
        pl.BlockSpec((1, HA_HEADS, HA_DK, HA_DV), state_map),
        pl.BlockSpec((1, HA_HEADS, HA_DK, HA_DV), state_map),
    ]
    out_shape = [
        jax.ShapeDtypeStruct((b, l, w), F32),
        jax.ShapeDtypeStruct((b, l, w), F32),
        jax.ShapeDtypeStruct((b, HA_HEADS, HA_DK, HA_DV), F32),
        jax.ShapeDtypeStruct((b, HA_HEADS, HA_DK, HA_DV), F32),
    ]
    out_specs = [
        pl.BlockSpec((1, lt, w), lambda bi, i: (bi, i, 0)),
        pl.BlockSpec((1, lt, w), lambda bi, i: (bi, n_t - 1 - i, 0)),
        pl.BlockSpec((1, HA_HEADS, HA_DK, HA_DV), lambda bi, i: (bi, 0, 0, 0)),
        pl.BlockSpec((1, HA_HEADS, HA_DK, HA_DV), lambda bi, i: (bi, 0, 0, 0)),
    ]
    scratch = [
        pltpu.VMEM((2, HA_HEADS, HA_DV, HA_DK), F32),
        pltpu.VMEM((2, lt, w), BF16),
        pltpu.VMEM((2, lt, w), BF16),
        pltpu.VMEM((2, lt, w), BF16),
        pltpu.VMEM((2, lt, w), BF16),
        pltpu.VMEM((2, lt, w), F32),
        pltpu.VMEM((2, lt // CHUNK, 1, w), F32),
    ]
    return pl.pallas_call(
        functools.partial(_hgrn_kernel, lt=lt, slot=slot, has_state=has_state),
        out_shape=out_shape,
        grid=(b, n_t),
        in_specs=in_specs,
        out_specs=out_specs,
        scratch_shapes=scratch,
        compiler_params=_cparams(("parallel", "arbitrary")),
        name="hgrn2_scan",
    )(p_h, p_h, p_h, p_h, p_h, p_h, lb_logits, e_mat, cum, lv, s0_f, s0_b)


def _ab_inproj_kernel(*refs, rope, key_major):
    if rope:
        (x_ref, mod_ref, wh_ref, wm_ref, cos_ref, sin_ref, qg_ref, kvg_ref, wq1_ref, wq2_ref, wk_ref,
         wv_ref) = refs[:12]
        ph_ref, q_ref, k_ref, v_ref, ckv_ref, kr_ref = refs[12:]
    else:
        x_ref, mod_ref, wh_ref, wm_ref, qg_ref, kvg_ref, wq1_ref, wk_ref, wv_ref = refs[:9]
        ph_ref, q_ref, k_ref, v_ref, ckv_ref, kr_ref = refs[9:]
    scale = (MLA_DN + MLA_DR) ** -0.5 * LOG2_E
    mod = mod_ref[0]
    hmod = (x_ref[0] * (1.0 + mod[1:2]) + mod[0:1]).astype(BF16)
    ph_ref[0] = _dot(hmod, wh_ref[...])
    pm = _dot(hmod, wm_ref[...])
    kr_ref[0] = pm[:, MLA_Q_RANK + MLA_KV_RANK:MLA_Q_RANK + MLA_KV_RANK + LANES]
    cq = pm[:, :MLA_Q_RANK]
    cq = cq * lax.rsqrt(jnp.mean(cq * cq, axis=-1, keepdims=True) + RMS_EPS) * qg_ref[...]
    cq = cq.astype(BF16)
    ckv = pm[:, MLA_Q_RANK:MLA_Q_RANK + MLA_KV_RANK]
    ckv = ckv * lax.rsqrt(jnp.mean(ckv * ckv, axis=-1, keepdims=True) + RMS_EPS) * kvg_ref[...]
    ckv_ref[0] = ckv
    ckv = ckv.astype(BF16)
    kr0 = MLA_Q_RANK + MLA_KV_RANK
    kr = pm[:, kr0:kr0 + LANES]
    qa = _dot(cq, wq1_ref[...])
    kn = _dot(ckv, wk_ref[...])
    if key_major:
        v_ref[0] = _dot_nt(wv_ref[...], ckv).astype(BF16)
    else:
        v_ref[0] = _dot(ckv, wv_ref[...]).astype(BF16)
    if rope:
        cos = cos_ref[...]
        sin = sin_ref[...]
        qb = _dot(cq, wq2_ref[...])
        kr = kr * cos + pm[:, kr0 + LANES:kr0 + 2 * LANES] * sin
    for h in range(MLA_HEADS):
        sl = slice(h * HEAD_PAD, (h + 1) * HEAD_PAD)
        qh = qa[:, sl]
        if rope:
            qh = qh * cos + qb[:, sl] * sin
        q_ref[0, :, sl] = (qh * scale).astype(BF16)
        k_ref[0, :, sl] = (kn[:, sl] + kr).astype(BF16)


def _ab_inproj(x, mod, w_h, w_m, rope_tabs, qg, kvg, wq1, wq2, wk, wv):
    b, l, d = x.shape
    tm = PROJ_TILE
    assert l % tm == 0
    rope = rope_tabs is not None
    per_batch = mod.shape[0] > 1
    mod_map = (lambda i, j: (i, 0, 0)) if per_batch else (lambda i, j: (0, 0, 0))
    full = lambda a: pl.BlockSpec(a.shape, lambda i, j: (0,) * a.ndim, pipeline_mode=pl.Buffered(1))
    row = lambda width: pl.BlockSpec((1, tm, width), lambda i, j: (i, j, 0))
    hw = MLA_HEADS * HEAD_PAD
    inputs = [x, mod, w_h, w_m]
    in_specs = [row(d), pl.BlockSpec((1, 6, d), mod_map), full(w_h), full(w_m)]
    if rope:
        inputs += list(rope_tabs)
        in_specs += [pl.BlockSpec((tm, LANES), lambda i, j: (j, 0))] * 2
    ws = [qg, kvg, wq1] + ([wq2] if rope else []) + [wk, wv]
    inputs += ws
    in_specs += [full(a) for a in ws]
    key_major = wv.shape[1] == MLA_KV_RANK
    vw = MLA_HEADS * MLA_DV
    shapes = [(b, l, w_h.shape[1]), (b, l, hw), (b, l, hw), (b, vw, l) if key_major else (b, l, hw),
              (b, l, MLA_KV_RANK), (b, l, LANES)]
    dtypes = [F32, BF16, BF16, BF16, F32, F32]
    out_specs = [row(s[2]) for s in shapes]
    if key_major:
        out_specs[3] = pl.BlockSpec((1, vw, tm), lambda i, j: (i, 0, j))
    return pl.pallas_call(
        functools.partial(_ab_inproj_kernel, rope=rope, key_major=key_major),
        out_shape=[jax.ShapeDtypeStruct(s, dt) for s, dt in zip(shapes, dtypes)],
        grid=(b, l // tm),
        in_specs=in_specs,
        out_specs=out_specs,
        compiler_params=_cparams(("parallel", "parallel")),
        name="ab_inproj",
    )(*inputs)


def _mla_ctx_kernel(ckv_ref, kr_ref, place_ref, wk_ref, wvt_ref, k_ref, vt_ref):
    ckv = ckv_ref[0].astype(BF16)
    kr = _dot(kr_ref[0].astype(BF16), place_ref[...])
    kn = _dot(ckv, wk_ref[...])
    vt_ref[0] = _dot_nt(wvt_ref[...], ckv).astype(BF16)
    for h in range(MLA_HEADS):
        sl = slice(h * HEAD_PAD, (h + 1) * HEAD_PAD)
        k_ref[0, :, sl] = (kn[:, sl] + kr).astype(BF16)


def _mla_ctx(ctx_ckv, ctx_kr, wk, wvt):
    b, lc, _ = ctx_ckv.shape
    hw = MLA_HEADS * HEAD_PAD
    vw = MLA_HEADS * MLA_DV
    place = np.zeros((MLA_DR, LANES), np.float32)
    place[np.arange(MLA_DR), MLA_DN + np.arange(MLA_DR)] = 1.0
    place = jnp.asarray(place, BF16)
    full = lambda a: pl.BlockSpec(a.shape, lambda i: (0,) * a.ndim)
    return pl.pallas_call(
        _mla_ctx_kernel,
        out_shape=[jax.ShapeDtypeStruct((b, lc, hw), BF16), jax.ShapeDtypeStruct((b, vw, lc), BF16)],
        grid=(b,),
        in_specs=[pl.BlockSpec((1, lc, MLA_KV_RANK), lambda i: (i, 0, 0)),
                  pl.BlockSpec((1, lc, MLA_DR), lambda i: (i, 0, 0)),
                  full(place), full(wk), full(wvt)],
        out_specs=[pl.BlockSpec((1, lc, hw), lambda i: (i, 0, 0)), pl.BlockSpec((1, vw, lc), lambda i: (i, 0, 0))],
        compiler_params=_cparams(("parallel",)),
        name="mla_ctx_keys",
    )(ctx_ckv, ctx_kr, place, wk, wvt)


def _attn_long_kernel(q_ref, k_ref, vt_ref, kc_ref, vtc_ref, o_ref, s_a, s_b, m_a, m_b, *, n_q):
    i = pl.program_id(2)
    tq = q_ref.shape[1]
    lk = k_ref.shape[1]
    tk = ATTN_KEY_BLOCK
    n_blk = lk // tk
    slabs = [slice(hh * HEAD_PAD, (hh + 1) * HEAD_PAD) for hh in range(2)]
    vrows = [slice(hh * MLA_DV, (hh + 1) * MLA_DV) for hh in range(2)]
    q_t = [q_ref[0, :, sl].astype(F32).T.astype(BF16) for sl in slabs]
    crow = slice(lk, lk + kc_ref.shape[1])

    def scorer(s_cur):
        def scores(hh, k_blk, rows, m_run):
            s = _dot(k_blk, q_t[hh])
            s_cur[hh, rows, :] = s
            return jnp.maximum(m_run, jnp.max(s.reshape(s.shape[0] // SUBLANES, SUBLANES, tq), axis=0))
        return scores

    m_init = (jnp.full((SUBLANES, tq), -jnp.inf, F32),) * 2

    @pl.when(i == 0)
    def _():
        scores = scorer(s_a)

        def step(j, m_runs):
            rows = pl.ds(pl.multiple_of(j * tk, tk), tk)
            return tuple(scores(hh, k_ref[0, rows, slabs[hh]], rows, m_runs[hh]) for hh in range(2))

        m_runs = lax.fori_loop(0, n_blk, step, m_init, unroll=2)
        for hh in range(2):
            m_a[hh] = scores(hh, kc_ref[0, :, slabs[hh]], crow, m_runs[hh])

    def weigher(s_prv):
        def weigh(hh, vt_blk, rows, m, acc):
            p = jnp.exp2(s_prv[hh, rows, :] - m).astype(BF16)
            lhs = jnp.concatenate([vt_blk, jnp.ones((2 * SUBLANES, vt_blk.shape[1]), BF16)], axis=0)
            return acc + _dot(lhs, p)
        return weigh

    acc_init = (jnp.zeros((MLA_DV + 2 * SUBLANES, tq), F32),) * 2

    def finish(accs):
        outs = [acc[:MLA_DV] / acc[MLA_DV:MLA_DV + 1] for acc in accs]
        o_ref[0] = jnp.concatenate(outs, axis=0).T.astype(BF16)

    def run(s_cur, m_cur, s_prv, m_prv):
        scores = scorer(s_cur)
        weigh = weigher(s_prv)
        ms = [jnp.max(m_prv[hh], axis=0, keepdims=True) for hh in range(2)]

        def step(j, carry):
            m_runs, accs = carry
            rows = pl.ds(pl.multiple_of(j * tk, tk), tk)
            m_runs = tuple(scores(hh, k_ref[0, rows, slabs[hh]], rows, m_runs[hh]) for hh in range(2))
            accs = tuple(weigh(hh, vt_ref[0, vrows[hh], rows], rows, ms[hh], accs[hh]) for hh in range(2))
            return m_runs, accs

        m_runs, accs = lax.fori_loop(0, n_blk, step, (m_init, acc_init), unroll=2)
        m_runs = tuple(scores(hh, kc_ref[0, :, slabs[hh]], crow, m_runs[hh]) for hh in range(2))
        accs = tuple(weigh(hh, vtc_ref[0, vrows[hh], :], crow, ms[hh], accs[hh]) for hh in range(2))
        for hh in range(2):
            m_cur[hh] = m_runs[hh]
        finish(accs)

    @pl.when((i % 2 == 0) & (i > 0) & (i < n_q))
    def _():
        run(s_a, m_a, s_b, m_b)

    @pl.when((i % 2 == 1) & (i < n_q))
    def _():
        run(s_b, m_b, s_a, m_a)

    @pl.when(i == n_q)
    def _():
        s_prv, m_prv = (s_b, m_b) if n_q % 2 == 0 else (s_a, m_a)
        weigh = weigher(s_prv)
        ms = [jnp.max(m_prv[hh], axis=0, keepdims=True) for hh in range(2)]

        def step(j, accs):
            rows = pl.ds(pl.multiple_of(j * tk, tk), tk)
            return tuple(weigh(hh, vt_ref[0, vrows[hh], rows], rows, ms[hh], accs[hh]) for hh in range(2))

        accs = lax.fori_loop(0, n_blk, step, acc_init, unroll=2)
        finish(tuple(weigh(hh, vtc_ref[0, vrows[hh], :], crow, ms[hh], accs[hh]) for hh in range(2)))


def _attention_long(q, k, vt, kc, vtc):
    b, l, _ = q.shape
    lk = k.shape[1]
    lc = kc.shape[1]
    tq = ATTN_Q_TILE
    pw = 2 * HEAD_PAD
    vpw = 2 * MLA_DV
    assert lk % (2 * ATTN_KEY_BLOCK) == 0 and l % tq == 0
    n_q = l // tq
    return pl.pallas_call(
        functools.partial(_attn_long_kernel, n_q=n_q),
        out_shape=jax.ShapeDtypeStruct((b, l, MLA_HEADS * MLA_DV), BF16),
        grid=(b, MLA_HEADS // 2, n_q + 1),
        in_specs=[pl.BlockSpec((1, tq, pw), lambda bi, p, i: (bi, jnp.minimum(i, n_q - 1), p)),
                  pl.BlockSpec((1, lk, pw), lambda bi, p, i: (bi, 0, p)),
                  pl.BlockSpec((1, vpw, lk), lambda bi, p, i: (bi, p, 0)),
                  pl.BlockSpec((1, lc, pw), lambda bi, p, i: (bi, 0, p)),
                  pl.BlockSpec((1, vpw, lc), lambda bi, p, i: (bi, p, 0))],
        out_specs=pl.BlockSpec((1, tq, LANES), lambda bi, p, i: (bi, jnp.maximum(i - 1, 0), p)),
        scratch_shapes=[pltpu.VMEM((2, lk + lc, tq), F32), pltpu.VMEM((2, lk + lc, tq), F32),
                        pltpu.VMEM((2, SUBLANES, tq), F32), pltpu.VMEM((2, SUBLANES, tq), F32)],
        compiler_params=_cparams(("parallel", "parallel", "arbitrary")),
        name="mla_attention_long",
    )(q, k, vt, kc, vtc)


def _attn_kernel(q_ref, k_ref, v_ref, o_ref):
    for pair in range(MLA_HEADS // 2):
        out = None
        for hh in range(2):
            h0 = (2 * pair + hh) * HEAD_PAD
            sl = slice(h0, h0 + HEAD_PAD)
            s = _dot_nt(q_ref[0, :, sl], k_ref[0, :, sl])
            p = jnp.exp2(s - jnp.max(s, axis=-1, keepdims=True))
            den = jnp.sum(p, axis=-1, keepdims=True)
            o = _dot(p.astype(BF16), v_ref[0, :, sl]) / den
            out = o if out is None else out + o
        o_ref[0, :, pair * LANES:(pair + 1) * LANES] = out.astype(BF16)


def _attention(q, k, v):
    b, l, hw = q.shape
    assert k.shape[1] == l and l <= TOK_TILE
    blk = pl.BlockSpec((1, l, hw), lambda bi: (bi, 0, 0))
    return pl.pallas_call(
        _attn_kernel,
        out_shape=jax.ShapeDtypeStruct((b, l, MLA_HEADS * MLA_DV), BF16),
        grid=(b,),
        in_specs=[blk, blk, blk],
        out_specs=pl.BlockSpec((1, l, MLA_HEADS * MLA_DV), lambda bi: (bi, 0, 0)),
        compiler_params=_cparams(("parallel",)),
        name="mla_attention",
    )(q, k, v)


def _outproj_kernel(*refs, hgrn):
    if hgrn:
        x_ref, mod_ref, of_ref, ob_ref, gl_ref, gn_ref, b_ref, w_ref, lg_ref, lb_ref, o_ref = refs
        o = of_ref[0] + ob_ref[0]
        gate = _silu(gl_ref[0])
        gn = gn_ref[...]
        parts = []
        for h in range(HA_HEADS):
            sl = slice(h * HA_DV, (h + 1) * HA_DV)
            oh = o[:, sl]
            oh = oh * lax.rsqrt(jnp.mean(oh * oh, axis=-1, keepdims=True) + RMS_EPS) * gn
            parts.append((oh * gate[:, sl]).astype(BF16))
        a = jnp.concatenate(parts, axis=-1)
    else:
        x_ref, mod_ref, a_ref, b_ref, w_ref, lg_ref, lb_ref, o_ref = refs
        a = a_ref[0]
    half = a.shape[-1]
    y = _dot(a, w_ref[:half, :]) + _dot(b_ref[0], w_ref[half:, :])
    m = mod_ref[0]
    r = ALPHA * x_ref[0] + m[2:3] * y
    o_ref[0] = _layer_norm(r, lg_ref[...], lb_ref[...])


def _outproj(x, mod, a_inputs, b_in, w, ln_g, ln_b, hgrn):
    b, l, d = x.shape
    tm = PROJ_TILE
    assert l % tm == 0
    per_batch = mod.shape[0] > 1
    mod_map = (lambda i, j: (i, 0, 0)) if per_batch else (lambda i, j: (0, 0, 0))
    row = lambda width: pl.BlockSpec((1, tm, width), lambda i, j: (i, j, 0))
    full = lambda a: pl.BlockSpec(a.shape, lambda i, j: (0,) * a.ndim, pipeline_mode=pl.Buffered(1))
    inputs = [x, mod]
    in_specs = [row(d), pl.BlockSpec((1, 6, d), mod_map)]
    if hgrn:
        o_f, o_b, p_h, g_norm = a_inputs
        inputs += [o_f, o_b, p_h, g_norm]
        in_specs += [row(HA_W), row(HA_W),
                     pl.BlockSpec((1, tm, HA_W), lambda i, j: (i, j, 4)), full(g_norm)]
    else:
        inputs += [a_inputs]
        in_specs += [row(a_inputs.shape[-1])]
    inputs += [b_in, w, ln_g, ln_b]
    in_specs += [row(b_in.shape[-1]), full(w), full(ln_g), full(ln_b)]
    return pl.pallas_call(
        functools.partial(_outproj_kernel, hgrn=hgrn),
        out_shape=jax.ShapeDtypeStruct((b, l, d), F32),
        grid=(b, l // tm),
        in_specs=in_specs,
        out_specs=row(d),
        compiler_params=_cparams(("parallel", "parallel")),
        name="outproj_ln",
    )(*inputs)


def _conv_kernel(pm_ref, pp_ref, pn_ref, scw_ref, cfw_ref, cfb_ref, cfg_ref, cfbeta_ref,
                 ysc_ref, ycf_ref, ext_sc, ext_cf, *, lt):
    i = pl.program_id(1)
    n_i = pl.num_programs(1)
    w = SC_W

    def sc_in(p):
        return p[:, w:2 * w] * p[:, 2 * w:3 * w]

    def cf_in(p):
        return p[:, 3 * w:3 * w + CF_W] * _sigmoid(p[:, 3 * w + CF_W:3 * w + 2 * CF_W])

    pm = pm_ref[0]
    pp = pp_ref[0]
    pn = pn_ref[0]
    has_prev = i > 0
    has_next = i < n_i - 1
    n_ext = lt + 2 * HALO
    for ext, conv_in in ((ext_sc, sc_in), (ext_cf, cf_in)):
        ext[0, 0:HALO, :] = jnp.where(has_prev, conv_in(pp), 0.0)
        ext[0, HALO:HALO + lt, :] = conv_in(pm)
        ext[0, HALO + lt:, :] = jnp.where(has_next, conv_in(pn), 0.0)
    for s in range(1, SUBLANES):
        ext_cf[s, 0:n_ext - SUBLANES, :] = ext_cf[0, s:s + n_ext - SUBLANES, :]
    sc_shifts = sorted({(HALO - SC_K // 2 + j) % SUBLANES for j in range(SC_K)} - {0})
    for s in sc_shifts:
        ext_sc[s, 0:n_ext - SUBLANES, :] = ext_sc[0, s:s + n_ext - SUBLANES, :]

    def tap(ext, off, rows):
        s = off % SUBLANES
        return ext[s, off - s:off - s + rows, :]

    rb = 32
    for r in range(lt // rb):
        base = HALO + r * rb
        acc = None
        for j in range(SC_K):
            term = jnp.tile(scw_ref[j], (rb // SUBLANES, 1)) * tap(ext_sc, base - SC_K // 2 + j, rb)
            acc = term if acc is None else acc + term
        ysc_ref[0, r * rb:(r + 1) * rb, :] = (pm[r * rb:(r + 1) * rb, 0:w] * acc).astype(BF16)
        acc = None
        for j in range(CF_K):
            term = jnp.tile(cfw_ref[j], (rb // SUBLANES, 1)) * tap(ext_cf, base - CF_K // 2 + j, rb)
            acc = term if acc is None else acc + term
        u = _layer_norm(acc + cfb_ref[...], cfg_ref[...], cfbeta_ref[...])
        ycf_ref[0, r * rb:(r + 1) * rb, :] = _silu(u).astype(BF16)


def _conv_mixers(p1, sc_w, cf_w, cf_b, cf_g, cf_beta):
    b, l, width = p1.shape
    lt = min(SEQ_TILE, l)
    assert l % lt == 0 and lt % HALO == 0
    hb = lt // HALO
    n_h = l // HALO
    sc_w = jnp.broadcast_to(sc_w[:, None, :], (sc_w.shape[0], SUBLANES, sc_w.shape[1]))
    cf_w = jnp.broadcast_to(cf_w[:, None, :], (cf_w.shape[0], SUBLANES, cf_w.shape[1]))
    full = lambda a: pl.BlockSpec(a.shape, lambda bi, i: (0,) * a.ndim)
    return pl.pallas_call(
        functools.partial(_conv_kernel, lt=lt),
        out_shape=[jax.ShapeDtypeStruct((b, l, SC_W), BF16), jax.ShapeDtypeStruct((b, l, CF_W), BF16)],
        grid=(b, l // lt),
        in_specs=[
            pl.BlockSpec((1, lt, width), lambda bi, i: (bi, i, 0)),
            pl.BlockSpec((1, HALO, width), lambda bi, i: (bi, jnp.maximum(i * hb - 1, 0), 0)),
            pl.BlockSpec((1, HALO, width), lambda bi, i: (bi, jnp.minimum((i + 1) * hb, n_h - 1), 0)),
            full(sc_w), full(cf_w), full(cf_b), full(cf_g), full(cf_beta),
        ],
        out_specs=[pl.BlockSpec((1, lt, SC_W), lambda bi, i: (bi, i, 0)),
                   pl.BlockSpec((1, lt, CF_W), lambda bi, i: (bi, i, 0))],
        scratch_shapes=[pltpu.VMEM((SUBLANES, lt + 2 * HALO, SC_W), F32),
                        pltpu.VMEM((SUBLANES, lt + 2 * HALO, CF_W), F32)],
        compiler_params=_cparams(("parallel", "parallel")),
        name="conv_mixers",
    )(p1, p1, p1, sc_w, cf_w, cf_b, cf_g, cf_beta)


def _route_t(lt):
    t = lt.shape[1]
    row = lax.broadcasted_iota(jnp.int32, (SUBLANES, t), 0).astype(F32)
    neg = -jnp.inf
    big = float(LANES)
    gl = jnp.where(row < N_GROUPS, lt[N_EXPERTS:N_EXPERTS + SUBLANES], neg)
    gmax = jnp.max(gl, axis=0, keepdims=True)
    p_g = 1.0 / jnp.sum(jnp.exp(gl - gmax), axis=0, keepdims=True)
    g_sel = jnp.min(jnp.where(gl == gmax, row, big), axis=0, keepdims=True)
    el = lt[0:EXP_PER_GROUP]
    for gi in range(1, N_GROUPS):
        el = jnp.where(g_sel == gi, lt[gi * EXP_PER_GROUP:(gi + 1) * EXP_PER_GROUP], el)
    v1 = jnp.max(el, axis=0, keepdims=True)
    i1 = jnp.min(jnp.where(el == v1, row, big), axis=0, keepdims=True)
    el2 = jnp.where(row == i1, neg, el)
    v2 = jnp.max(el2, axis=0, keepdims=True)
    i2 = jnp.min(jnp.where(el2 == v2, row, big), axis=0, keepdims=True)
    e2 = jnp.exp(v2 - v1)
    w1 = p_g / (1.0 + e2)
    w2 = p_g * e2 / (1.0 + e2)
    comb = jnp.where(row == i1, w1, 0.0) + jnp.where(row == i2, w2, 0.0)
    onehot = jnp.where(row == g_sel, 1.0, 0.0)
    return onehot, comb


def _moe_kernel(xc_ref, xl_ref, mod_ref, wr_ref, tri_ref, wg_ref, wu_ref, wd_ref, lg_ref, lb_ref, oc_ref, ol_ref,
                hb_scr, os_scr, pt_scr, cc_scr, *, n_ctx):
    is_ctx = pl.program_id(0) < n_ctx
    m = mod_ref[0]
    tm, cap = pt_scr.shape
    d = xc_ref.shape[-1]
    half = d // 2
    cw = EXP_PER_GROUP * EXP_FF

    h = jnp.where(is_ctx, xc_ref[0], xl_ref[0]) * (1.0 + m[4:5]) + m[3:4]
    h_hi = h.astype(BF16)
    hb_scr[...] = h_hi
    h_lo = (h - h_hi.astype(F32)).astype(BF16)
    l2 = _dot(h_hi, wr_ref[...])
    logits = l2[:, :LANES] + l2[:, LANES:] + _dot(h_lo, wr_ref[:, :LANES])
    onehot, comb = _route_t(logits.T)
    rank = _dot(onehot.astype(BF16), tri_ref[...])
    cnt = jnp.sum(onehot, axis=1, keepdims=True)
    n_blocks = jnp.floor((cnt + (MOE_BLOCK - 0.5)) * (1.0 / MOE_BLOCK))
    padded = n_blocks * MOE_BLOCK
    start = jnp.zeros((1, 1), F32)
    first_block = jnp.zeros((1, 1), F32)
    dest = jnp.zeros((1, tm), F32)
    first, count = [], []
    for gi in range(N_GROUPS):
        dest = dest + onehot[gi:gi + 1] * (start + rank[gi:gi + 1])
        first.append(first_block[0, 0].astype(jnp.int32))
        count.append(n_blocks[gi, 0].astype(jnp.int32))
        start = start + padded[gi:gi + 1]
        first_block = first_block + n_blocks[gi:gi + 1]
    aux = jnp.concatenate([jnp.broadcast_to(dest, (SUBLANES, tm)), comb,
                           jnp.zeros((LANES - 2 * SUBLANES, tm), F32)], axis=0).T
    lane = lax.broadcasted_iota(jnp.int32, (tm, LANES), 1)
    comb_tok = jnp.where((lane >= SUBLANES) & (lane < 2 * SUBLANES), aux, 0.0)
    c_hi = comb_tok.astype(BF16)
    c_mid = (comb_tok - c_hi.astype(F32)).astype(BF16)
    slot_t = lax.broadcasted_iota(jnp.int32, (tm, cap), 1).astype(F32)
    pt_scr[...] = jnp.where(slot_t == aux[:, 0:1], 1.0, 0.0).astype(BF16)
    cc_scr[...] = jnp.concatenate([c_hi, c_mid], axis=1)
    os_scr[...] = jnp.zeros_like(os_scr)
    block_slot = lax.broadcasted_iota(jnp.int32, (MOE_BLOCK, tm), 0).astype(F32)

    for gi in range(N_GROUPS):
        cols = slice(gi * cw, (gi + 1) * cw)

        def block(i, carry, gi=gi, cols=cols):
            blk = first[gi] + i
            r0 = pl.multiple_of(blk * MOE_BLOCK, 2 * SUBLANES)
            rows = pl.ds(r0, MOE_BLOCK)
            perm = jnp.where(block_slot + (blk * MOE_BLOCK).astype(F32) == dest, 1.0, 0.0).astype(BF16)
            hs = _dot(perm, hb_scr[...]).astype(BF16)
            cs2 = _dot(perm, cc_scr[...])
            cs = cs2[:, :LANES] + cs2[:, LANES:]
            hid = _silu(_dot(hs, wg_ref[:, cols])) * _dot(hs, wu_ref[:, cols])
            hid = jnp.concatenate(
                [hid[:, e * EXP_FF:(e + 1) * EXP_FF] * cs[:, SUBLANES + e:SUBLANES + e + 1]
                 for e in range(EXP_PER_GROUP)], axis=1)
            os_scr[rows, :] = _dot(hid.astype(BF16), wd_ref[cols, :]).astype(BF16)
            return carry

        lax.fori_loop(0, count[gi], block, 0)

    y = jnp.concatenate([_dot(pt_scr[...], os_scr[:, :half]), _dot(pt_scr[...], os_scr[:, half:])], axis=1)
    r = ALPHA * jnp.where(is_ctx, xc_ref[0], xl_ref[0]) + m[5:6] * y
    out = _layer_norm(r, lg_ref[...], lb_ref[...])

    @pl.when(is_ctx)
    def _():
        oc_ref[0] = out

    @pl.when(jnp.logical_not(is_ctx))
    def _():
        ol_ref[0] = out


def _moe(x_ctx, x_lat, mods, wr, wg, wu, wd, ln_g, ln_b):
    tm = MOE_TILE
    d = x_lat.shape[-1]
    x_c = x_ctx.reshape(-1, tm, d)
    n_ctx = x_c.shape[0]
    b, l, _ = x_lat.shape
    assert l % tm == 0 and mods.shape[0] == b + 1
    per_b = l // tm
    cap = (tm + N_GROUPS * (MOE_BLOCK - 1)) // MOE_BLOCK * MOE_BLOCK
    cap = -(-cap // LANES) * LANES
    tri = jnp.asarray(np.triu(np.ones((tm, tm), np.float32), k=1), BF16)
    ctx_map = lambda t: (jnp.minimum(t, n_ctx - 1), 0, 0)
    lat_t = lambda t: jnp.maximum(t - n_ctx, 0)
    lat_map = lambda t: (lat_t(t) // per_b, lat_t(t) % per_b, 0)
    mod_map = lambda t: (jnp.where(t < n_ctx, b, lat_t(t) // per_b), 0, 0)
    full = lambda a: pl.BlockSpec(a.shape, lambda t: (0,) * a.ndim, pipeline_mode=pl.Buffered(1))
    y_c, y_l = pl.pallas_call(
        functools.partial(_moe_kernel, n_ctx=n_ctx),
        out_shape=[jax.ShapeDtypeStruct(x_c.shape, F32), jax.ShapeDtypeStruct(x_lat.shape, F32)],
        grid=(n_ctx + b * per_b,),
        in_specs=[
            pl.BlockSpec((1, tm, d), ctx_map), pl.BlockSpec((1, tm, d), lat_map), pl.BlockSpec((1, 6, d), mod_map),
            full(wr), full(tri), full(wg), full(wu), full(wd), full(ln_g), full(ln_b),
        ],
        out_specs=[pl.BlockSpec((1, tm, d), ctx_map), pl.BlockSpec((1, tm, d), lat_map)],
        scratch_shapes=[pltpu.VMEM((tm, d), BF16), pltpu.VMEM((cap, d), BF16),
                        pltpu.VMEM((tm, cap), BF16), pltpu.VMEM((tm, 2 * LANES), BF16)],
        compiler_params=_cparams(("arbitrary",)),
        name="hier_moe_ln",
    )(x_c, x_lat, mods, wr, tri, wg, wu, wd, ln_g, ln_b)
    return y_c.reshape(x_ctx.shape), y_l


def _rope_swap_perm():
    idx = np.arange(MLA_DR)
    return idx ^ (MLA_DR // 4)


def _rope_tables(n_tok):
    rows = n_tok // GRID_W
    pos_r = jnp.repeat(jnp.arange(rows, dtype=F32), GRID_W)
    pos_c = (jnp.arange(rows * GRID_W) % GRID_W).astype(F32)
    n_freq = MLA_DR // 4
    inv = ROPE_BASE ** (-jnp.arange(n_freq, dtype=F32) / n_freq)
    ang = jnp.stack([pos_r[:, None] * inv, pos_c[:, None] * inv], axis=1)
    cos, sin = jnp.cos(ang), jnp.sin(ang)
    cos32 = jnp.stack([cos, cos], axis=2).reshape(n_tok, MLA_DR)
    sin32 = jnp.stack([-sin, sin], axis=2).reshape(n_tok, MLA_DR)
    pad_hi = LANES - MLA_DN - MLA_DR
    cos_t = jnp.concatenate([jnp.ones((n_tok, MLA_DN), F32), cos32, jnp.zeros((n_tok, pad_hi), F32)], axis=1)
    sin_t = jnp.concatenate([jnp.zeros((n_tok, MLA_DN), F32), sin32, jnp.zeros((n_tok, pad_hi), F32)], axis=1)
    return cos_t, sin_t


def _ab_weights(w_in, w_uq, w_ukv):
    perm = _rope_swap_perm()
    pad_hi = LANES - MLA_DN - MLA_DR
    w_h = w_in[:, :5 * HA_W]
    cq = w_in[:, 5 * HA_W:5 * HA_W + MLA_Q_RANK]
    ckv = w_in[:, 5 * HA_W + MLA_Q_RANK:5 * HA_W + MLA_Q_RANK + MLA_KV_RANK]
    kr = w_in[:, 5 * HA_W + MLA_Q_RANK + MLA_KV_RANK:]
    d = w_in.shape[0]
    z_lo = jnp.zeros((d, MLA_DN), F32)
    z_hi = jnp.zeros((d, pad_hi), F32)
    w_m = jnp.concatenate([cq, ckv, z_lo, kr, z_hi, z_lo, kr[:, perm], z_hi], axis=1)
    uq = w_uq.reshape(MLA_Q_RANK, MLA_HEADS, MLA_DN + MLA_DR)
    q_nope, q_rope = uq[..., :MLA_DN], uq[..., MLA_DN:]
    zq_hi = jnp.zeros((MLA_Q_RANK, MLA_HEADS, pad_hi), F32)
    wq1 = jnp.concatenate([q_nope, q_rope, zq_hi], axis=-1).reshape(MLA_Q_RANK, -1)
    wq2 = jnp.concatenate([jnp.zeros_like(q_nope), q_rope[..., perm], zq_hi], axis=-1).reshape(MLA_Q_RANK, -1)
    ukv = w_ukv.reshape(MLA_KV_RANK, MLA_HEADS, MLA_DN + MLA_DV)
    k_nope, v = ukv[..., :MLA_DN], ukv[..., MLA_DN:]
    wk = jnp.concatenate([k_nope, jnp.zeros((MLA_KV_RANK, MLA_HEADS, LANES - MLA_DN), F32)], axis=-1)
    wk = wk.reshape(MLA_KV_RANK, -1)
    v_pairs = v.reshape(MLA_KV_RANK, MLA_HEADS // 2, 2, MLA_DV)
    zv = jnp.zeros_like(v_pairs[:, :, 0])
    wv = jnp.stack([jnp.concatenate([v_pairs[:, :, 0], zv], axis=-1),
                    jnp.concatenate([zv, v_pairs[:, :, 1]], axis=-1)], axis=2).reshape(MLA_KV_RANK, -1)
    wvt = v.reshape(MLA_KV_RANK, MLA_HEADS * MLA_DV).T
    bf = lambda a: a.astype(BF16)
    return bf(w_h), bf(w_m), bf(wq1), bf(wq2), bf(wk), bf(wv), bf(wvt)


def _moe_weights(w_group, w_expert, w_gate, w_up, w_down):
    d = w_group.shape[0]
    wr = jnp.concatenate([w_expert, w_group, jnp.zeros((d, LANES - N_EXPERTS - N_GROUPS), F32)], axis=1)
    wr_hi = wr.astype(BF16)
    wr = jnp.concatenate([wr_hi, (wr - wr_hi.astype(F32)).astype(BF16)], axis=1)
    wg = jnp.transpose(w_gate, (1, 0, 2)).reshape(d, N_EXPERTS * EXP_FF).astype(BF16)
    wu = jnp.transpose(w_up, (1, 0, 2)).reshape(d, N_EXPERTS * EXP_FF).astype(BF16)
    wd = w_down.reshape(N_EXPERTS * EXP_FF, d).astype(BF16)
    return wr, wg, wu, wd


def kernel(x_prompt, x_sample, state_hgrn_fwd, state_hgrn_bwd, cache_mla_ckv, cache_mla_krope, c, c_ctx, mod_w, mod_b, ln_g, ln_b, ab_w_in, ab_w_out, hgrn_lb_logits, hgrn_norm_g, mla_q_norm_g, mla_w_uq, mla_kv_norm_g, mla_w_ukv, cd_w_in, cd_w_out, sc_conv_w, cf_conv_w, cf_conv_b, cf_ln_g, cf_ln_b, moe_w_group, moe_w_expert, moe_w_gate, moe_w_up, moe_w_down):
    dec_b = x_sample.shape[0]
    d = D_MODEL
    cvec = jnp.concatenate([c, c_ctx[None, :], jnp.zeros((SUBLANES - dec_b - 1, d), F32)], axis=0)
    mods = _mod_vectors(cvec, mod_w, mod_b)
    rope_tabs = _rope_tables(x_sample.shape[1])
    xp, xs = x_prompt, x_sample
    new_sf = new_sb = new_ckv = new_kr = None
    for l in range(DEPTH):
        mod_lat = mods[l, :dec_b].reshape(dec_b, 6, d)
        mod_ctx = mods[l, dec_b:dec_b + 1].reshape(1, 6, d)
        row = lambda a: a.reshape(1, -1)
        fold = lambda a: a.reshape(-1, PROJ_TILE, a.shape[-1])
        unfold = lambda a: a.reshape(x_prompt.shape[0], -1, a.shape[-1])
        if l % 2 == 0:
            e = l // 2
            w_h, w_m, wq1, wq2, wk, wv, wvt = _ab_weights(ab_w_in[e], mla_w_uq[e], mla_w_ukv[e])
            w_out = ab_w_out[e].astype(BF16)
            qg, kvg, gn = row(mla_q_norm_g[e]), row(mla_kv_norm_g[e]), row(hgrn_norm_g[e])
            ph_p, q_p, k_p, v_p, ckv_p, kr_p = map(unfold, _ab_inproj(
                fold(xp), mod_ctx, w_h, w_m, None, qg, kvg, wq1, None, wk, wv))
            of_p, ob_p, sf, sb = _hgrn(ph_p, hgrn_lb_logits, None, None, e)
            om_p = _attention(q_p, k_p, v_p)
            xp = unfold(_outproj(fold(xp), mod_ctx, (fold(of_p), fold(ob_p), fold(ph_p), gn), fold(om_p), w_out,
                                 row(ln_g[l, 0]), row(ln_b[l, 0]), True))
            ph_s, q_s, k_s, vt_s, _, _ = _ab_inproj(xs, mod_lat, w_h, w_m, rope_tabs, qg, kvg, wq1, wq2, wk, wvt)
            of_s, ob_s, _, _ = _hgrn(ph_s, hgrn_lb_logits, state_hgrn_fwd[:, e], state_hgrn_bwd[:, e], e)
            kc, vtc = _mla_ctx(cache_mla_ckv[:, e], cache_mla_krope[:, e], wk, wvt)
            om_s = _attention_long(q_s, k_s, vt_s, kc, vtc)
            xs = _outproj(xs, mod_lat, (of_s, ob_s, ph_s, gn), om_s, w_out, row(ln_g[l, 0]), row(ln_b[l, 0]), True)
            new_sf, new_sb, new_ckv = sf, sb, ckv_p
            new_kr = kr_p[:, :, MLA_DN:MLA_DN + MLA_DR]
        else:
            jx = l // 2
            w1 = cd_w_in[jx].astype(BF16)
            w_out = cd_w_out[jx].astype(BF16)
            cd = (sc_conv_w[jx], cf_conv_w[jx], row(cf_conv_b[jx]), row(cf_ln_g[jx]), row(cf_ln_b[jx]))
            (p1_p,) = _inproj(fold(xp), mod_ctx, [w1])
            ysc_p, ycf_p = _conv_mixers(unfold(p1_p), *cd)
            xp = unfold(_outproj(fold(xp), mod_ctx, fold(ysc_p), fold(ycf_p), w_out,
                                 row(ln_g[l, 0]), row(ln_b[l, 0]), False))
            (p1_s,) = _inproj(xs, mod_lat, [w1])
            ysc_s, ycf_s = _conv_mixers(p1_s, *cd)
            xs = _outproj(xs, mod_lat, ysc_s, ycf_s, w_out, row(ln_g[l, 0]), row(ln_b[l, 0]), False)
        wr, wg, wu, wd = _moe_weights(moe_w_group[l], moe_w_expert[l], moe_w_gate[l], moe_w_up[l], moe_w_down[l])
        mods_l = mods[l, :dec_b + 1].reshape(dec_b + 1, 6, d)
        xp, xs = _moe(xp, xs, mods_l, wr, wg, wu, wd, row(ln_g[l, 1]), row(ln_b[l, 1]))
    return (xp, xs, new_sf[:, None], new_sb[:, None], new_ckv[:, None], new_kr[:, None])
```

```python
import functools

import numpy as np
import jax
import jax.numpy as jnp
from jax import lax
from jax.experimental import pallas as pl
from jax.experimental.pallas import tpu as pltpu

F32 = jnp.float32
BF16 = jnp.bfloat16
HIGHEST = lax.Precision.HIGHEST

D_MODEL = 1024
DEPTH = 2
GRID_W = 64
HA_HEADS = 4
HA_DK = 128
HA_DV = 128
HA_W = HA_HEADS * HA_DK
CHUNK = 32
MLA_HEADS = 8
MLA_DN = 64
MLA_DR = 32
MLA_DV = 64
MLA_Q_RANK = 384
MLA_KV_RANK = 256
ROPE_BASE = 10000.0
SC_W = 512
SC_K = 3
CF_W = 512
CF_K = 31
N_GROUPS = 4
EXP_PER_GROUP = 8
N_EXPERTS = N_GROUPS * EXP_PER_GROUP
EXP_FF = 128
ALPHA = (2.0 * DEPTH) ** 0.25
LOG2_E = 1.4426950408889634
LN_EPS = 1e-5
RMS_EPS = 1e-6

LANES = 128
SUBLANES = 8
VMEM_LIMIT = 56 * 1024 * 1024

HEAD_PAD = LANES
PROJ_TILE = 512
SEQ_TILE = 512
TOK_TILE = 256
GROUP_ROWS = 128
N_LEVELS = 5
SAFE_LOG_DECAY = 60.0
ATTN_KEY_BLOCK = 512
ATTN_Q_TILE = 256
MOE_TILE = 512
MOE_BLOCK = 144
HALO = 16


def _cparams(sem):
    return pltpu.CompilerParams(dimension_semantics=sem, vmem_limit_bytes=VMEM_LIMIT)


def _sigmoid(x):
    return 0.5 * jnp.tanh(0.5 * x) + 0.5


def _silu(x):
    half = 0.5 * x
    return half * jnp.tanh(half) + half


def _layer_norm(r, g, b):
    mu = jnp.mean(r, axis=-1, keepdims=True)
    d = r - mu
    var = jnp.mean(d * d, axis=-1, keepdims=True)
    return d * lax.rsqrt(var + LN_EPS) * g + b


def _dot(a, b):
    return jnp.dot(a, b, preferred_element_type=F32)


def _dot_nt(a, b):
    return lax.dot_general(a, b, (((1,), (1,)), ((), ())), preferred_element_type=F32)


def _dot_tn(a, b):
    return lax.dot_general(a, b, (((0,), (0,)), ((), ())), preferred_element_type=F32)


def _mod_kernel(c_ref, w_ref, b_ref, o_ref):
    s = _silu(c_ref[...])
    o_ref[0] = jnp.dot(s, w_ref[0], precision=HIGHEST, preferred_element_type=F32) + b_ref[0]


def _mod_vectors(cvec, mod_w, mod_b):
    n_out = mod_w.shape[-1]
    tn = 1536
    return pl.pallas_call(
        _mod_kernel,
        out_shape=jax.ShapeDtypeStruct((DEPTH, SUBLANES, n_out), F32),
        grid=(DEPTH, n_out // tn),
        in_specs=[
            pl.BlockSpec((SUBLANES, D_MODEL), lambda l, j: (0, 0)),
            pl.BlockSpec((1, D_MODEL, tn), lambda l, j: (l, 0, j)),
            pl.BlockSpec((1, 1, tn), lambda l, j: (l, 0, j)),
        ],
        out_specs=pl.BlockSpec((1, SUBLANES, tn), lambda l, j: (l, 0, j)),
        compiler_params=_cparams(("arbitrary", "arbitrary")),
        name="mod_vectors",
    )(cvec, mod_w, mod_b.reshape(DEPTH, 1, n_out))


def _inproj_kernel(*refs, n_w):
    x_ref, mod_ref = refs[0], refs[1]
    w_refs = refs[2:2 + n_w]
    o_refs = refs[2 + n_w:]
    m = mod_ref[0]
    h = (x_ref[0] * (1.0 + m[1:2]) + m[0:1]).astype(BF16)
    for w_ref, o_ref in zip(w_refs, o_refs):
        o_ref[0] = _dot(h, w_ref[...])


def _inproj(x, mod, weights):
    b, l, d = x.shape
    tm = PROJ_TILE
    assert l % tm == 0
    per_batch = mod.shape[0] > 1
    mod_map = (lambda i, j: (i, 0, 0)) if per_batch else (lambda i, j: (0, 0, 0))
    in_specs = [pl.BlockSpec((1, tm, d), lambda i, j: (i, j, 0)),
                pl.BlockSpec((1, 6, d), mod_map)]
    in_specs += [pl.BlockSpec(w.shape, lambda i, j: (0, 0), pipeline_mode=pl.Buffered(1)) for w in weights]
    out_shape = [jax.ShapeDtypeStruct((b, l, w.shape[1]), F32) for w in weights]
    out_specs = [pl.BlockSpec((1, tm, w.shape[1]), lambda i, j: (i, j, 0)) for w in weights]
    return pl.pallas_call(
        functools.partial(_inproj_kernel, n_w=len(weights)),
        out_shape=out_shape,
        grid=(b, l // tm),
        in_specs=in_specs,
        out_specs=out_specs,
        compiler_params=_cparams(("parallel", "parallel")),
        name="modulate_inproj",
    )(x, mod, *weights)


def _hgrn_tables():
    n = GROUP_ROWS
    t = np.arange(n)[:, None]
    j = np.arange(n)[None, :]
    same_chunk = (t // CHUNK) == (j // CHUNK)
    e_f, e_b = [], []
    lv_f = np.full((n, n), -1, np.int32)
    for lvl in range(N_LEVELS):
        m = CHUNK >> (lvl + 1)
        blk0 = (t // (2 * m)) * (2 * m)
        r = blk0 + m - 1
        upper = t > r
        ef = np.where(upper, (j > r) & (j <= t), (j > t) & (j <= r))
        r2 = blk0 + m
        lower = t < r2
        eb = np.where(lower, (j >= t) & (j < r2), (j >= r2) & (j < t))
        e_f.append(ef)
        e_b.append(eb)
        s = np.arange(n)[None, :]
        same_blk = (t // (2 * m)) == (s // (2 * m))
        q_side = (t % (2 * m)) >= m
        k_side = (s % (2 * m)) < m
        lv_f[same_blk & q_side & k_side] = lvl
    lv_f[np.arange(n), np.arange(n)] = N_LEVELS
    e_f = np.concatenate(e_f, axis=0).astype(np.float32)
    e_b = np.concatenate(e_b, axis=0).astype(np.float32)
    e = np.stack([e_f, e_b])
    cum = np.stack([same_chunk & (j <= t), same_chunk & (j >= t)]).astype(np.float32)
    lv = np.stack([lv_f, lv_f.T])
    return e, cum, lv


def _hgrn_kernel(qf_ref, vf_ref, ff_ref, qb_ref, vb_ref, fb_ref, lbl_ref, e_ref, cum_ref, lv_ref, s0f_ref, s0b_ref,
                 of_ref, ob_ref, sf_ref, sb_ref,
                 st_scr, qe_scr, kd_scr, sc_scr, v_scr, oi_scr, dec_scr, *, lt, slot, has_state):
    i = pl.program_id(1)
    n_i = pl.num_programs(1)
    n_chunks = lt // CHUNK
    n_groups = lt // GROUP_ROWS
    chunks_per_group = GROUP_ROWS // CHUNK

    @pl.when(i == 0)
    def _():
        for d, s0_ref in enumerate((s0f_ref, s0b_ref)):
            for h in range(HA_HEADS):
                if has_state:
                    st_scr[d, h] = s0_ref[0, h].T
                else:
                    st_scr[d, h] = jnp.zeros((HA_DV, HA_DK), F32)

    lg = lbl_ref[...]
    n_slots = lg.shape[0]
    mx = lg[0]
    for s in range(1, n_slots):
        mx = jnp.maximum(mx, lg[s])
    ex = [jnp.exp(lg[s] - mx) for s in range(n_slots)]
    den = ex[0]
    for s in range(1, n_slots):
        den = den + ex[s]
    num = ex[0]
    for s in range(1, slot + 1):
        num = num + ex[s]
    lb_all = num / den

    dirs = ((qf_ref, vf_ref, ff_ref), (qb_ref, vb_ref, fb_ref))
    head_cols = [slice(h * HA_DK, (h + 1) * HA_DK) for h in range(HA_HEADS)]

    def gates(d, rows, cols):
        q_ref, _, f_ref = dirs[d]
        q = _silu(q_ref[0, rows, cols])
        lb_h = lb_all[d:d + 1, cols]
        f = lb_h + (1.0 - lb_h) * _sigmoid(f_ref[0, rows, cols])
        g = jnp.log(f)
        g_hi = g.astype(BF16)
        g_lo = (g - g_hi.astype(F32)).astype(BF16)
        return q, 1.0 - f, g_hi, g_lo

    def group_step(grp, tot_min):
        r0 = pl.multiple_of(grp * GROUP_ROWS, GROUP_ROWS)
        rows = pl.ds(r0, GROUP_ROWS)
        for d in range(2):
            cum = cum_ref[d]
            lv = lv_ref[d]
            for cols in head_cols:
                q, k, g_hi, g_lo = gates(d, rows, cols)
                bcum = _dot(cum, g_hi) + _dot(cum, g_lo)
                v_scr[d, rows, cols] = dirs[d][1][0, rows, cols].astype(BF16)
                qe = (q * jnp.exp(bcum)).astype(BF16)
                qe_scr[d, rows, cols] = qe
                kx = []
                for cc in range(chunks_per_group):
                    edge = cc * CHUNK + (CHUNK - 1 if d == 0 else 0)
                    tot = bcum[edge:edge + 1]
                    b_c = bcum[cc * CHUNK:(cc + 1) * CHUNK]
                    kd = k[cc * CHUNK:(cc + 1) * CHUNK] * jnp.exp(tot - b_c)
                    kd_scr[d, pl.ds(r0 + cc * CHUNK, CHUNK), cols] = kd.astype(BF16)
                    dec_scr[d, grp * chunks_per_group + cc, :, cols] = jnp.exp(tot)
                    kx.append(kd * jnp.exp(-tot))
                    tot_min = jnp.minimum(tot_min, tot)
                p = _dot_nt(qe, jnp.concatenate(kx, axis=0).astype(BF16))
                sc_scr[d, rows, cols] = jnp.where(lv >= 0, p, 0.0).astype(BF16)
        return tot_min

    tot_min = lax.fori_loop(0, n_groups, group_step, jnp.zeros((1, HA_DK), F32), unroll=True)

    def one_factor():
        for d in range(2):
            for grp in range(n_groups):
                rows = slice(grp * GROUP_ROWS, (grp + 1) * GROUP_ROWS)
                for cols in head_cols:
                    oi_scr[d, rows, cols] = _dot(sc_scr[d, rows, cols], v_scr[d, rows, cols])

    def per_level():
        for d in range(2):

            def group_step(grp, carry, d=d):
                rows = pl.ds(pl.multiple_of(grp * GROUP_ROWS, GROUP_ROWS), GROUP_ROWS)
                lv = lv_ref[d]
                for cols in head_cols:
                    q, k, g_hi, g_lo = gates(d, rows, cols)
                    x = jnp.exp(_dot(e_ref[d], g_hi) + _dot(e_ref[d], g_lo))
                    sc = jnp.where(lv == N_LEVELS, _dot_nt(q.astype(BF16), k.astype(BF16)), 0.0)
                    for lvl in range(N_LEVELS):
                        xl = x[lvl * GROUP_ROWS:(lvl + 1) * GROUP_ROWS]
                        p = _dot_nt((q * xl).astype(BF16), (k * xl).astype(BF16))
                        sc = jnp.where(lv == lvl, p, sc)
                    oi_scr[d, rows, cols] = _dot(sc.astype(BF16), v_scr[d, rows, cols])
                return carry

            lax.fori_loop(0, n_groups, group_step, 0)

    lax.cond(jnp.min(tot_min) > -SAFE_LOG_DECAY, one_factor, per_level)

    out_refs = (of_ref, ob_ref)

    def chunk_step(c, carry):
        for d in range(2):
            cc = c if d == 0 else n_chunks - 1 - c
            r0 = pl.multiple_of(cc * CHUNK, CHUNK)
            rows = pl.ds(r0, CHUNK)
            for h in range(HA_HEADS):
                cols = slice(h * HA_DK, (h + 1) * HA_DK)
                st = st_scr[d, h]
                o_state = _dot_nt(qe_scr[d, rows, cols], st.astype(BF16))
                out_refs[d][0, rows, cols] = oi_scr[d, rows, cols] + o_state
                upd = _dot_tn(v_scr[d, rows, cols], kd_scr[d, rows, cols])
                st_scr[d, h] = st * dec_scr[d, cc, :, cols] + upd
        return carry

    lax.fori_loop(0, n_chunks, chunk_step, 0, unroll=True)

    @pl.when(i == n_i - 1)
    def _():
        for d, s_ref in enumerate((sf_ref, sb_ref)):
            for h in range(HA_HEADS):
                s_ref[0, h] = st_scr[d, h].T


def _hgrn(p_h, lb_logits, s0_f, s0_b, slot):
    b, l, _ = p_h.shape
    lt = min(SEQ_TILE, l)
    assert l % lt == 0 and lt % GROUP_ROWS == 0
    n_t = l // lt
    has_state = s0_f is not None
    if not has_state:
        s0_f = jnp.zeros((1, HA_HEADS, HA_DK, HA_DV), F32)
        s0_b = s0_f
    e_np, cum_np, lv_np = _hgrn_tables()
    e_mat = jnp.asarray(e_np, BF16)
    cum = jnp.asarray(cum_np, BF16)
    lv = jnp.asarray(lv_np, jnp.int32)
    w = HA_W

    def sec(idx, rev):
        if rev:
            return pl.BlockSpec((1, lt, w), lambda bi, i: (bi, n_t - 1 - i, idx))
        return pl.BlockSpec((1, lt, w), lambda bi, i: (bi, i, idx))

    state_map = (lambda bi, i: (bi, 0, 0, 0)) if has_state else (lambda bi, i: (0, 0, 0, 0))
    in_specs = [
        sec(0, False), sec(3, False), sec(1, False),
        sec(0, True), sec(3, True), sec(2, True),
        pl.BlockSpec(lb_logits.shape, lambda bi, i: (0, 0, 0)),
        pl.BlockSpec(e_mat.shape, lambda bi, i: (0, 0, 0)),
        pl.BlockSpec(cum.shape, lambda bi, i: (0, 0, 0)),
        pl.BlockSpec(lv.shape, lambda bi, i: (0, 0, 0)),
        pl.BlockSpec((1, HA_HEADS, HA_DK, HA_DV), state_map),
        pl.BlockSpec((1, HA_HEADS, HA_DK, HA_DV), state_map),
    ]
    out_shape = [
        jax.ShapeDtypeStruct((b, l, w), F32),
        jax.ShapeDtypeStruct((b, l, w), F32),
        jax.ShapeDtypeStruct((b, HA_HEADS, HA_DK, HA_DV), F32),
        jax.ShapeDtypeStruct((b, HA_HEADS, HA_DK, HA_DV), F32),
    ]
    out_specs = [
        pl.BlockSpec((1, lt, w), lambda bi, i: (bi, i, 0)),
        pl.BlockSpec((1, lt, w), lambda bi, i: (bi, n_t - 1 - i, 0)),
        pl.BlockSpec((1, HA_HEADS, HA_DK, HA_DV), lambda bi, i: (bi, 0, 0, 0)),
        pl.BlockSpec((1, HA_HEADS, HA_DK, HA_DV), lambda bi, i: (bi, 0, 0, 0)),
    ]
    scratch = [
        pltpu.VMEM((2, HA_HEADS, HA_DV, HA_DK), F32),
        pltpu.VMEM((2, lt, w), BF16),
        pltpu.VMEM((2, lt, w), BF16),
        pltpu.VMEM((2, lt, w), BF16),
        pltpu.VMEM((2, lt, w), BF16),
        pltpu.VMEM((2, lt, w), F32),
        pltpu.VMEM((2, lt // CHUNK, 1, w), F32),
    ]
    return pl.pallas_call(
        functools.partial(_hgrn_kernel, lt=lt, slot=slot, has_state=has_state),
        out_shape=out_shape,
        grid=(b, n_t),
        in_specs=in_specs,
        out_specs=out_specs,
        scratch_shapes=scratch,
        compiler_params=_cparams(("parallel", "arbitrary")),
        name="hgrn2_scan",
    )(p_h, p_h, p_h, p_h, p_h, p_h, lb_logits, e_mat, cum, lv, s0_f, s0_b)


def _ab_inproj_kernel(*refs, rope, key_major):
    if rope:
        (x_ref, mod_ref, wh_ref, wm_ref, cos_ref, sin_ref, qg_ref, kvg_ref, wq1_ref, wq2_ref, wk_ref,
         wv_ref) = refs[:12]
        ph_ref, q_ref, k_ref, v_ref, ckv_ref, kr_ref = refs[12:]
    else:
        x_ref, mod_ref, wh_ref, wm_ref, qg_ref, kvg_ref, wq1_ref, wk_ref, wv_ref = refs[:9]
        ph_ref, q_ref, k_ref, v_ref, ckv_ref, kr_ref = refs[9:]
    scale = (MLA_DN + MLA_DR) ** -0.5 * LOG2_E
    mod = mod_ref[0]
    hmod = (x_ref[0] * (1.0 + mod[1:2]) + mod[0:1]).astype(BF16)
    ph_ref[0] = _dot(hmod, wh_ref[...])
    pm = _dot(hmod, wm_ref[...])
    kr_ref[0] = pm[:, MLA_Q_RANK + MLA_KV_RANK:MLA_Q_RANK + MLA_KV_RANK + LANES]
    cq = pm[:, :MLA_Q_RANK]
    cq = cq * lax.rsqrt(jnp.mean(cq * cq, axis=-1, keepdims=True) + RMS_EPS) * qg_ref[...]
    cq = cq.astype(BF16)
    ckv = pm[:, MLA_Q_RANK:MLA_Q_RANK + MLA_KV_RANK]
    ckv = ckv * lax.rsqrt(jnp.mean(ckv * ckv, axis=-1, keepdims=True) + RMS_EPS) * kvg_ref[...]
    ckv_ref[0] = ckv
    ckv = ckv.astype(BF16)
    kr0 = MLA_Q_RANK + MLA_KV_RANK
    kr = pm[:, kr0:kr0 + LANES]
    qa = _dot(cq, wq1_ref[...])
    kn = _dot(ckv, wk_ref[...])
    if key_major:
        v_ref[0] = _dot_nt(wv_ref[...], ckv).astype(BF16)
    else:
        v_ref[0] = _dot(ckv, wv_ref[...]).astype(BF16)
    if rope:
        cos = cos_ref[...]
        sin = sin_ref[...]
        qb = _dot(cq, wq2_ref[...])
        kr = kr * cos + pm[:, kr0 + LANES:kr0 + 2 * LANES] * sin
    for h in range(MLA_HEADS):
        sl = slice(h * HEAD_PAD, (h + 1) * HEAD_PAD)
        qh = qa[:, sl]
        if rope:
            qh = qh * cos + qb[:, sl] * sin
        q_ref[0, :, sl] = (qh * scale).astype(BF16)
        k_ref[0, :, sl] = (kn[:, sl] + kr).astype(BF16)


def _ab_inproj(x, mod, w_h, w_m, rope_tabs, qg, kvg, wq1, wq2, wk, wv):
    b, l, d = x.shape
    tm = PROJ_TILE
    assert l % tm == 0
    rope = rope_tabs is not None
    per_batch = mod.shape[0] > 1
    mod_map = (lambda i, j: (i, 0, 0)) if per_batch else (lambda i, j: (0, 0, 0))
    full = lambda a: pl.BlockSpec(a.shape, lambda i, j: (0,) * a.ndim, pipeline_mode=pl.Buffered(1))
    row = lambda width: pl.BlockSpec((1, tm, width), lambda i, j: (i, j, 0))
    hw = MLA_HEADS * HEAD_PAD
    inputs = [x, mod, w_h, w_m]
    in_specs = [row(d), pl.BlockSpec((1, 6, d), mod_map), full(w_h), full(w_m)]
    if rope:
        inputs += list(rope_tabs)
        in_specs += [pl.BlockSpec((tm, LANES), lambda i, j: (j, 0))] * 2
    ws = [qg, kvg, wq1] + ([wq2] if rope else []) + [wk, wv]
    inputs += ws
    in_specs += [full(a) for a in ws]
    key_major = wv.shape[1] == MLA_KV_RANK
    vw = MLA_HEADS * MLA_DV
    shapes = [(b, l, w_h.shape[1]), (b, l, hw), (b, l, hw), (b, vw, l) if key_major else (b, l, hw),
              (b, l, MLA_KV_RANK), (b, l, LANES)]
    dtypes = [F32, BF16, BF16, BF16, F32, F32]
    out_specs = [row(s[2]) for s in shapes]
    if key_major:
        out_specs[3] = pl.BlockSpec((1, vw, tm), lambda i, j: (i, 0, j))
    return pl.pallas_call(
        functools.partial(_ab_inproj_kernel, rope=rope, key_major=key_major),
        out_shape=[jax.ShapeDtypeStruct(s, dt) for s, dt in zip(shapes, dtypes)],
        grid=(b, l // tm),
        in_specs=in_specs,
        out_specs=out_specs,
        compiler_params=_cparams(("parallel", "parallel")),
        name="ab_inproj",
    )(*inputs)


def _mla_ctx_kernel(ckv_ref, kr_ref, place_ref, wk_ref, wvt_ref, k_ref, vt_ref):
    ckv = ckv_ref[0].astype(BF16)
    kr = _dot(kr_ref[0].astype(BF16), place_ref[...])
    kn = _dot(ckv, wk_ref[...])
    vt_ref[0] = _dot_nt(wvt_ref[...], ckv).astype(BF16)
    for h in range(MLA_HEADS):
        sl = slice(h * HEAD_PAD, (h + 1) * HEAD_PAD)
        k_ref[0, :, sl] = (kn[:, sl] + kr).astype(BF16)


def _mla_ctx(ctx_ckv, ctx_kr, wk, wvt):
    b, lc, _ = ctx_ckv.shape
    hw = MLA_HEADS * HEAD_PAD
    vw = MLA_HEADS * MLA_DV
    place = np.zeros((MLA_DR, LANES), np.float32)
    place[np.arange(MLA_DR), MLA_DN + np.arange(MLA_DR)] = 1.0
    place = jnp.asarray(place, BF16)
    full = lambda a: pl.BlockSpec(a.shape, lambda i: (0,) * a.ndim)
    return pl.pallas_call(
        _mla_ctx_kernel,
        out_shape=[jax.ShapeDtypeStruct((b, lc, hw), BF16), jax.ShapeDtypeStruct((b, vw, lc), BF16)],
        grid=(b,),
        in_specs=[pl.BlockSpec((1, lc, MLA_KV_RANK), lambda i: (i, 0, 0)),
                  pl.BlockSpec((1, lc, MLA_DR), lambda i: (i, 0, 0)),
                  full(place), full(wk), full(wvt)],
        out_specs=[pl.BlockSpec((1, lc, hw), lambda i: (i, 0, 0)), pl.BlockSpec((1, vw, lc), lambda i: (i, 0, 0))],
        compiler_params=_cparams(("parallel",)),
        name="mla_ctx_keys",
    )(ctx_ckv, ctx_kr, place, wk, wvt)


def _attn_long_kernel(q_ref, k_ref, vt_ref, kc_ref, vtc_ref, o_ref, s_a, s_b, m_a, m_b, *, n_q):
    i = pl.program_id(2)
    tq = q_ref.shape[1]
    lk = k_ref.shape[1]
    tk = ATTN_KEY_BLOCK
    n_blk = lk // tk
    slabs = [slice(hh * HEAD_PAD, (hh + 1) * HEAD_PAD) for hh in range(2)]
    vrows = [slice(hh * MLA_DV, (hh + 1) * MLA_DV) for hh in range(2)]
    q_t = [q_ref[0, :, sl].astype(F32).T.astype(BF16) for sl in slabs]
    crow = slice(lk, lk + kc_ref.shape[1])

    def scorer(s_cur):
        def scores(hh, k_blk, rows, m_run):
            s = _dot(k_blk, q_t[hh])
            s_cur[hh, rows, :] = s
            return jnp.maximum(m_run, jnp.max(s.reshape(s.shape[0] // SUBLANES, SUBLANES, tq), axis=0))
        return scores

    m_init = (jnp.full((SUBLANES, tq), -jnp.inf, F32),) * 2

    @pl.when(i == 0)
    def _():
        scores = scorer(s_a)

        def step(j, m_runs):
            rows = pl.ds(pl.multiple_of(j * tk, tk), tk)
            return tuple(scores(hh, k_ref[0, rows, slabs[hh]], rows, m_runs[hh]) for hh in range(2))

        m_runs = lax.fori_loop(0, n_blk, step, m_init, unroll=2)
        for hh in range(2):
            m_a[hh] = scores(hh, kc_ref[0, :, slabs[hh]], crow, m_runs[hh])

    def weigher(s_prv):
        def weigh(hh, vt_blk, rows, m, acc):
            p = jnp.exp2(s_prv[hh, rows, :] - m).astype(BF16)
            lhs = jnp.concatenate([vt_blk, jnp.ones((2 * SUBLANES, vt_blk.shape[1]), BF16)], axis=0)
            return acc + _dot(lhs, p)
        return weigh

    acc_init = (jnp.zeros((MLA_DV + 2 * SUBLANES, tq), F32),) * 2

    def finish(accs):
        outs = [acc[:MLA_DV] / acc[MLA_DV:MLA_DV + 1] for acc in accs]
        o_ref[0] = jnp.concatenate(outs, axis=0).T.astype(BF16)

    def run(s_cur, m_cur, s_prv, m_prv):
        scores = scorer(s_cur)
        weigh = weigher(s_prv)
        ms = [jnp.max(m_prv[hh], axis=0, keepdims=True) for hh in range(2)]

        def step(j, carry):
            m_runs, accs = carry
            rows = pl.ds(pl.multiple_of(j * tk, tk), tk)
            m_runs = tuple(scores(hh, k_ref[0, rows, slabs[hh]], rows, m_runs[hh]) for hh in range(2))
            accs = tuple(weigh(hh, vt_ref[0, vrows[hh], rows], rows, ms[hh], accs[hh]) for hh in range(2))
            return m_runs, accs

        m_runs, accs = lax.fori_loop(0, n_blk, step, (m_init, acc_init), unroll=2)
        m_runs = tuple(scores(hh, kc_ref[0, :, slabs[hh]], crow, m_runs[hh]) for hh in range(2))
        accs = tuple(weigh(hh, vtc_ref[0, vrows[hh], :], crow, ms[hh], accs[hh]) for hh in range(2))
        for hh in range(2):
            m_cur[hh] = m_runs[hh]
        finish(accs)

    @pl.when((i % 2 == 0) & (i > 0) & (i < n_q))
    def _():
        run(s_a, m_a, s_b, m_b)

    @pl.when((i % 2 == 1) & (i < n_q))
    def _():
        run(s_b, m_b, s_a, m_a)

    @pl.when(i == n_q)
    def _():
        s_prv, m_prv = (s_b, m_b) if n_q % 2 == 0 else (s_a, m_a)
        weigh = weigher(s_prv)
        ms = [jnp.max(m_prv[hh], axis=0, keepdims=True) for hh in range(2)]

        def step(j, accs):
            rows = pl.ds(pl.multiple_of(j * tk, tk), tk)
            return tuple(weigh(hh, vt_ref[0, vrows[hh], rows], rows, ms[hh], accs[hh]) for hh in range(2))

        accs = lax.fori_loop(0, n_blk, step, acc_init, unroll=2)
        finish(tuple(weigh(hh, vtc_ref[0, vrows[hh], :], crow, ms[hh], accs[hh]) for hh in range(2)))


def _attention_long(q, k, vt, kc, vtc):
    b, l, _ = q.shape
    lk = k.shape[1]
    lc = kc.shape[1]
    tq = ATTN_Q_TILE
    pw = 2 * HEAD_PAD
    vpw = 2 * MLA_DV
    assert lk % (2 * ATTN_KEY_BLOCK) == 0 and l % tq == 0
    n_q = l // tq
    return pl.pallas_call(
        functools.partial(_attn_long_kernel, n_q=n_q),
        out_shape=jax.ShapeDtypeStruct((b, l, MLA_HEADS * MLA_DV), BF16),
        grid=(b, MLA_HEADS // 2, n_q + 1),
        in_specs=[pl.BlockSpec((1, tq, pw), lambda bi, p, i: (bi, jnp.minimum(i, n_q - 1), p)),
                  pl.BlockSpec((1, lk, pw), lambda bi, p, i: (bi, 0, p)),
                  pl.BlockSpec((1, vpw, lk), lambda bi, p, i: (bi, p, 0)),
                  pl.BlockSpec((1, lc, pw), lambda bi, p, i: (bi, 0, p)),
                  pl.BlockSpec((1, vpw, lc), lambda bi, p, i: (bi, p, 0))],
        out_specs=pl.BlockSpec((1, tq, LANES), lambda bi, p, i: (bi, jnp.maximum(i - 1, 0), p)),
        scratch_shapes=[pltpu.VMEM((2, lk + lc, tq), F32), pltpu.VMEM((2, lk + lc, tq), F32),
                        pltpu.VMEM((2, SUBLANES, tq), F32), pltpu.VMEM((2, SUBLANES, tq), F32)],
        compiler_params=_cparams(("parallel", "parallel", "arbitrary")),
        name="mla_attention_long",
    )(q, k, vt, kc, vtc)


def _attn_kernel(q_ref, k_ref, v_ref, o_ref):
    for pair in range(MLA_HEADS // 2):
        out = None
        for hh in range(2):
            h0 = (2 * pair + hh) * HEAD_PAD
            sl = slice(h0, h0 + HEAD_PAD)
            s = _dot_nt(q_ref[0, :, sl], k_ref[0, :, sl])
            p = jnp.exp2(s - jnp.max(s, axis=-1, keepdims=True))
            den = jnp.sum(p, axis=-1, keepdims=True)
            o = _dot(p.astype(BF16), v_ref[0, :, sl]) / den
            out = o if out is None else out + o
        o_ref[0, :, pair * LANES:(pair + 1) * LANES] = out.astype(BF16)


def _attention(q, k, v):
    b, l, hw = q.shape
    assert k.shape[1] == l and l <= TOK_TILE
    blk = pl.BlockSpec((1, l, hw), lambda bi: (bi, 0, 0))
    return pl.pallas_call(
        _attn_kernel,
        out_shape=jax.ShapeDtypeStruct((b, l, MLA_HEADS * MLA_DV), BF16),
        grid=(b,),
        in_specs=[blk, blk, blk],
        out_specs=pl.BlockSpec((1, l, MLA_HEADS * MLA_DV), lambda bi: (bi, 0, 0)),
        compiler_params=_cparams(("parallel",)),
        name="mla_attention",
    )(q, k, v)


def _outproj_kernel(*refs, hgrn):
    if hgrn:
        x_ref, mod_ref, of_ref, ob_ref, gl_ref, gn_ref, b_ref, w_ref, lg_ref, lb_ref, o_ref = refs
        o = of_ref[0] + ob_ref[0]
        gate = _silu(gl_ref[0])
        gn = gn_ref[...]
        parts = []
        for h in range(HA_HEADS):
            sl = slice(h * HA_DV, (h + 1) * HA_DV)
            oh = o[:, sl]
            oh = oh * lax.rsqrt(jnp.mean(oh * oh, axis=-1, keepdims=True) + RMS_EPS) * gn
            parts.append((oh * gate[:, sl]).astype(BF16))
        a = jnp.concatenate(parts, axis=-1)
    else:
        x_ref, mod_ref, a_ref, b_ref, w_ref, lg_ref, lb_ref, o_ref = refs
        a = a_ref[0]
    half = a.shape[-1]
    y = _dot(a, w_ref[:half, :]) + _dot(b_ref[0], w_ref[half:, :])
    m = mod_ref[0]
    r = ALPHA * x_ref[0] + m[2:3] * y
    o_ref[0] = _layer_norm(r, lg_ref[...], lb_ref[...])


def _outproj(x, mod, a_inputs, b_in, w, ln_g, ln_b, hgrn):
    b, l, d = x.shape
    tm = PROJ_TILE
    assert l % tm == 0
    per_batch = mod.shape[0] > 1
    mod_map = (lambda i, j: (i, 0, 0)) if per_batch else (lambda i, j: (0, 0, 0))
    row = lambda width: pl.BlockSpec((1, tm, width), lambda i, j: (i, j, 0))
    full = lambda a: pl.BlockSpec(a.shape, lambda i, j: (0,) * a.ndim, pipeline_mode=pl.Buffered(1))
    inputs = [x, mod]
    in_specs = [row(d), pl.BlockSpec((1, 6, d), mod_map)]
    if hgrn:
        o_f, o_b, p_h, g_norm = a_inputs
        inputs += [o_f, o_b, p_h, g_norm]
        in_specs += [row(HA_W), row(HA_W),
                     pl.BlockSpec((1, tm, HA_W), lambda i, j: (i, j, 4)), full(g_norm)]
    else:
        inputs += [a_inputs]
        in_specs += [row(a_inputs.shape[-1])]
    inputs += [b_in, w, ln_g, ln_b]
    in_specs += [row(b_in.shape[-1]), full(w), full(ln_g), full(ln_b)]
    return pl.pallas_call(
        functools.partial(_outproj_kernel, hgrn=hgrn),
        out_shape=jax.ShapeDtypeStruct((b, l, d), F32),
        grid=(b, l // tm),
        in_specs=in_specs,
        out_specs=row(d),
        compiler_params=_cparams(("parallel", "parallel")),
        name="outproj_ln",
    )(*inputs)


def _conv_kernel(pm_ref, pp_ref, pn_ref, scw_ref, cfw_ref, cfb_ref, cfg_ref, cfbeta_ref,
                 ysc_ref, ycf_ref, ext_sc, ext_cf, *, lt):
    i = pl.program_id(1)
    n_i = pl.num_programs(1)
    w = SC_W

    def sc_in(p):
        return p[:, w:2 * w] * p[:, 2 * w:3 * w]

    def cf_in(p):
        return p[:, 3 * w:3 * w + CF_W] * _sigmoid(p[:, 3 * w + CF_W:3 * w + 2 * CF_W])

    pm = pm_ref[0]
    pp = pp_ref[0]
    pn = pn_ref[0]
    has_prev = i > 0
    has_next = i < n_i - 1
    n_ext = lt + 2 * HALO
    for ext, conv_in in ((ext_sc, sc_in), (ext_cf, cf_in)):
        ext[0, 0:HALO, :] = jnp.where(has_prev, conv_in(pp), 0.0)
        ext[0, HALO:HALO + lt, :] = conv_in(pm)
        ext[0, HALO + lt:, :] = jnp.where(has_next, conv_in(pn), 0.0)
    for s in range(1, SUBLANES):
        ext_cf[s, 0:n_ext - SUBLANES, :] = ext_cf[0, s:s + n_ext - SUBLANES, :]
    sc_shifts = sorted({(HALO - SC_K // 2 + j) % SUBLANES for j in range(SC_K)} - {0})
    for s in sc_shifts:
        ext_sc[s, 0:n_ext - SUBLANES, :] = ext_sc[0, s:s + n_ext - SUBLANES, :]

    def tap(ext, off, rows):
        s = off % SUBLANES
        return ext[s, off - s:off - s + rows, :]

    rb = 32
    for r in range(lt // rb):
        base = HALO + r * rb
        acc = None
        for j in range(SC_K):
            term = jnp.tile(scw_ref[j], (rb // SUBLANES, 1)) * tap(ext_sc, base - SC_K // 2 + j, rb)
            acc = term if acc is None else acc + term
        ysc_ref[0, r * rb:(r + 1) * rb, :] = (pm[r * rb:(r + 1) * rb, 0:w] * acc).astype(BF16)
        acc = None
        for j in range(CF_K):
            term = jnp.tile(cfw_ref[j], (rb // SUBLANES, 1)) * tap(ext_cf, base - CF_K // 2 + j, rb)
            acc = term if acc is None else acc + term
        u = _layer_norm(acc + cfb_ref[...], cfg_ref[...], cfbeta_ref[...])
        ycf_ref[0, r * rb:(r + 1) * rb, :] = _silu(u).astype(BF16)


def _conv_mixers(p1, sc_w, cf_w, cf_b, cf_g, cf_beta):
    b, l, width = p1.shape
    lt = min(SEQ_TILE, l)
    assert l % lt == 0 and lt % HALO == 0
    hb = lt // HALO
    n_h = l // HALO
    sc_w = jnp.broadcast_to(sc_w[:, None, :], (sc_w.shape[0], SUBLANES, sc_w.shape[1]))
    cf_w = jnp.broadcast_to(cf_w[:, None, :], (cf_w.shape[0], SUBLANES, cf_w.shape[1]))
    full = lambda a: pl.BlockSpec(a.shape, lambda bi, i: (0,) * a.ndim)
    return pl.pallas_call(
        functools.partial(_conv_kernel, lt=lt),
        out_shape=[jax.ShapeDtypeStruct((b, l, SC_W), BF16), jax.ShapeDtypeStruct((b, l, CF_W), BF16)],
        grid=(b, l // lt),
        in_specs=[
            pl.BlockSpec((1, lt, width), lambda bi, i: (bi, i, 0)),
            pl.BlockSpec((1, HALO, width), lambda bi, i: (bi, jnp.maximum(i * hb - 1, 0), 0)),
            pl.BlockSpec((1, HALO, width), lambda bi, i: (bi, jnp.minimum((i + 1) * hb, n_h - 1), 0)),
            full(sc_w), full(cf_w), full(cf_b), full(cf_g), full(cf_beta),
        ],
        out_specs=[pl.BlockSpec((1, lt, SC_W), lambda bi, i: (bi, i, 0)),
                   pl.BlockSpec((1, lt, CF_W), lambda bi, i: (bi, i, 0))],
        scratch_shapes=[pltpu.VMEM((SUBLANES, lt + 2 * HALO, SC_W), F32),
                        pltpu.VMEM((SUBLANES, lt + 2 * HALO, CF_W), F32)],
        compiler_params=_cparams(("parallel", "parallel")),
        name="conv_mixers",
    )(p1, p1, p1, sc_w, cf_w, cf_b, cf_g, cf_beta)


def _route_t(lt):
    t = lt.shape[1]
    row = lax.broadcasted_iota(jnp.int32, (SUBLANES, t), 0).astype(F32)
    neg = -jnp.inf
    big = float(LANES)
    gl = jnp.where(row < N_GROUPS, lt[N_EXPERTS:N_EXPERTS + SUBLANES], neg)
    gmax = jnp.max(gl, axis=0, keepdims=True)
    p_g = 1.0 / jnp.sum(jnp.exp(gl - gmax), axis=0, keepdims=True)
    g_sel = jnp.min(jnp.where(gl == gmax, row, big), axis=0, keepdims=True)
    el = lt[0:EXP_PER_GROUP]
    for gi in range(1, N_GROUPS):
        el = jnp.where(g_sel == gi, lt[gi * EXP_PER_GROUP:(gi + 1) * EXP_PER_GROUP], el)
    v1 = jnp.max(el, axis=0, keepdims=True)
    i1 = jnp.min(jnp.where(el == v1, row, big), axis=0, keepdims=True)
    el2 = jnp.where(row == i1, neg, el)
    v2 = jnp.max(el2, axis=0, keepdims=True)
    i2 = jnp.min(jnp.where(el2 == v2, row, big), axis=0, keepdims=True)
    e2 = jnp.exp(v2 - v1)
    w1 = p_g / (1.0 + e2)
    w2 = p_g * e2 / (1.0 + e2)
    comb = jnp.where(row == i1, w1, 0.0) + jnp.where(row == i2, w2, 0.0)
    onehot = jnp.where(row == g_sel, 1.0, 0.0)
    return onehot, comb


def _moe_kernel(xc_ref, xl_ref, mod_ref, wr_ref, tri_ref, wg_ref, wu_ref, wd_ref, lg_ref, lb_ref, oc_ref, ol_ref,
                hb_scr, os_scr, pt_scr, cc_scr, *, n_ctx):
    is_ctx = pl.program_id(0) < n_ctx
    m = mod_ref[0]
    tm, cap = pt_scr.shape
    d = xc_ref.shape[-1]
    half = d // 2
    cw = EXP_PER_GROUP * EXP_FF

    h = jnp.where(is_ctx, xc_ref[0], xl_ref[0]) * (1.0 + m[4:5]) + m[3:4]
    h_hi = h.astype(BF16)
    hb_scr[...] = h_hi
    h_lo = (h - h_hi.astype(F32)).astype(BF16)
    l2 = _dot(h_hi, wr_ref[...])
    logits = l2[:, :LANES] + l2[:, LANES:] + _dot(h_lo, wr_ref[:, :LANES])
    onehot, comb = _route_t(logits.T)
    rank = _dot(onehot.astype(BF16), tri_ref[...])
    cnt = jnp.sum(onehot, axis=1, keepdims=True)
    n_blocks = jnp.floor((cnt + (MOE_BLOCK - 0.5)) * (1.0 / MOE_BLOCK))
    padded = n_blocks * MOE_BLOCK
    start = jnp.zeros((1, 1), F32)
    first_block = jnp.zeros((1, 1), F32)
    dest = jnp.zeros((1, tm), F32)
    first, count = [], []
    for gi in range(N_GROUPS):
        dest = dest + onehot[gi:gi + 1] * (start + rank[gi:gi + 1])
        first.append(first_block[0, 0].astype(jnp.int32))
        count.append(n_blocks[gi, 0].astype(jnp.int32))
        start = start + padded[gi:gi + 1]
        first_block = first_block + n_blocks[gi:gi + 1]
    aux = jnp.concatenate([jnp.broadcast_to(dest, (SUBLANES, tm)), comb,
                           jnp.zeros((LANES - 2 * SUBLANES, tm), F32)], axis=0).T
    lane = lax.broadcasted_iota(jnp.int32, (tm, LANES), 1)
    comb_tok = jnp.where((lane >= SUBLANES) & (lane < 2 * SUBLANES), aux, 0.0)
    c_hi = comb_tok.astype(BF16)
    c_mid = (comb_tok - c_hi.astype(F32)).astype(BF16)
    slot_t = lax.broadcasted_iota(jnp.int32, (tm, cap), 1).astype(F32)
    pt_scr[...] = jnp.where(slot_t == aux[:, 0:1], 1.0, 0.0).astype(BF16)
    cc_scr[...] = jnp.concatenate([c_hi, c_mid], axis=1)
    used_blocks = first_block[0, 0].astype(jnp.int32)
    max_blocks = cap // MOE_BLOCK

    def clear(bk, carry):
        os_scr[pl.ds(pl.multiple_of(bk * MOE_BLOCK, 2 * SUBLANES), MOE_BLOCK), :] = jnp.zeros((MOE_BLOCK, d), BF16)
        return carry

    lax.fori_loop(used_blocks, max_blocks, clear, 0)
    if cap > max_blocks * MOE_BLOCK:
        os_scr[max_blocks * MOE_BLOCK:, :] = jnp.zeros((cap - max_blocks * MOE_BLOCK, d), BF16)
    block_slot =lax.broadcasted_iota(jnp.int32, (MOE_BLOCK, tm), 0).astype(F32)

    for gi in range(N_GROUPS):
        cols = slice(gi * cw, (gi + 1) * cw)

        def block(i, carry, gi=gi, cols=cols):
            blk = first[gi] + i
            r0 = pl.multiple_of(blk * MOE_BLOCK, 2 * SUBLANES)
            rows = pl.ds(r0, MOE_BLOCK)
            perm = jnp.where(block_slot + (blk * MOE_BLOCK).astype(F32) == dest, 1.0, 0.0).astype(BF16)
            hs = _dot(perm, hb_scr[...]).astype(BF16)
            cs2 = _dot(perm, cc_scr[...])
            cs = cs2[:, :LANES] + cs2[:, LANES:]
            hid = _silu(_dot(hs, wg_ref[:, cols])) * _dot(hs, wu_ref[:, cols])
            hid = jnp.concatenate(
                [hid[:, e * EXP_FF:(e + 1) * EXP_FF] * cs[:, SUBLANES + e:SUBLANES + e + 1]
                 for e in range(EXP_PER_GROUP)], axis=1)
            os_scr[rows, :] = _dot(hid.astype(BF16), wd_ref[cols, :]).astype(BF16)
            return carry

        lax.fori_loop(0, count[gi], block, 0)

    y = jnp.concatenate([_dot(pt_scr[...], os_scr[:, :half]), _dot(pt_scr[...], os_scr[:, half:])], axis=1)
    r = ALPHA * jnp.where(is_ctx, xc_ref[0], xl_ref[0]) + m[5:6] * y
    out = _layer_norm(r, lg_ref[...], lb_ref[...])

    @pl.when(is_ctx)
    def _():
        oc_ref[0] = out

    @pl.when(jnp.logical_not(is_ctx))
    def _():
        ol_ref[0] = out


def _moe(x_ctx, x_lat, mods, wr, wg, wu, wd, ln_g, ln_b):
    tm = MOE_TILE
    d = x_lat.shape[-1]
    x_c = x_ctx.reshape(-1, tm, d)
    n_ctx = x_c.shape[0]
    b, l, _ = x_lat.shape
    assert l % tm == 0 and mods.shape[0] == b + 1
    per_b = l // tm
    cap = (tm + N_GROUPS * (MOE_BLOCK - 1)) // MOE_BLOCK * MOE_BLOCK
    cap = -(-cap // LANES) * LANES
    tri = jnp.asarray(np.triu(np.ones((tm, tm), np.float32), k=1), BF16)
    ctx_map = lambda t: (jnp.minimum(t, n_ctx - 1), 0, 0)
    lat_t = lambda t: jnp.maximum(t - n_ctx, 0)
    lat_map = lambda t: (lat_t(t) // per_b, lat_t(t) % per_b, 0)
    mod_map = lambda t: (jnp.where(t < n_ctx, b, lat_t(t) // per_b), 0, 0)
    full = lambda a: pl.BlockSpec(a.shape, lambda t: (0,) * a.ndim, pipeline_mode=pl.Buffered(1))
    y_c, y_l = pl.pallas_call(
        functools.partial(_moe_kernel, n_ctx=n_ctx),
        out_shape=[jax.ShapeDtypeStruct(x_c.shape, F32), jax.ShapeDtypeStruct(x_lat.shape, F32)],
        grid=(n_ctx + b * per_b,),
        in_specs=[
            pl.BlockSpec((1, tm, d), ctx_map), pl.BlockSpec((1, tm, d), lat_map), pl.BlockSpec((1, 6, d), mod_map),
            full(wr), full(tri), full(wg), full(wu), full(wd), full(ln_g), full(ln_b),
        ],
        out_specs=[pl.BlockSpec((1, tm, d), ctx_map), pl.BlockSpec((1, tm, d), lat_map)],
        scratch_shapes=[pltpu.VMEM((tm, d), BF16), pltpu.VMEM((cap, d), BF16),
                        pltpu.VMEM((tm, cap), BF16), pltpu.VMEM((tm, 2 * LANES), BF16)],
        compiler_params=_cparams(("arbitrary",)),
        name="hier_moe_ln",
    )(x_c, x_lat, mods, wr, tri, wg, wu, wd, ln_g, ln_b)
    return y_c.reshape(x_ctx.shape), y_l


def _rope_swap_perm():
    idx = np.arange(MLA_DR)
    return idx ^ (MLA_DR // 4)


def _rope_tables(n_tok):
    rows = n_tok // GRID_W
    pos_r = jnp.repeat(jnp.arange(rows, dtype=F32), GRID_W)
    pos_c = (jnp.arange(rows * GRID_W) % GRID_W).astype(F32)
    n_freq = MLA_DR // 4
    inv = ROPE_BASE ** (-jnp.arange(n_freq, dtype=F32) / n_freq)
    ang = jnp.stack([pos_r[:, None] * inv, pos_c[:, None] * inv], axis=1)
    cos, sin = jnp.cos(ang), jnp.sin(ang)
    cos32 = jnp.stack([cos, cos], axis=2).reshape(n_tok, MLA_DR)
    sin32 = jnp.stack([-sin, sin], axis=2).reshape(n_tok, MLA_DR)
    pad_hi = LANES - MLA_DN - MLA_DR
    cos_t = jnp.concatenate([jnp.ones((n_tok, MLA_DN), F32), cos32, jnp.zeros((n_tok, pad_hi), F32)], axis=1)
    sin_t = jnp.concatenate([jnp.zeros((n_tok, MLA_DN), F32), sin32, jnp.zeros((n_tok, pad_hi), F32)], axis=1)
    return cos_t, sin_t


def _ab_weights(w_in, w_uq, w_ukv):
    perm = _rope_swap_perm()
    pad_hi = LANES - MLA_DN - MLA_DR
    w_h = w_in[:, :5 * HA_W]
    cq = w_in[:, 5 * HA_W:5 * HA_W + MLA_Q_RANK]
    ckv = w_in[:, 5 * HA_W + MLA_Q_RANK:5 * HA_W + MLA_Q_RANK + MLA_KV_RANK]
    kr = w_in[:, 5 * HA_W + MLA_Q_RANK + MLA_KV_RANK:]
    d = w_in.shape[0]
    z_lo = jnp.zeros((d, MLA_DN), F32)
    z_hi = jnp.zeros((d, pad_hi), F32)
    w_m = jnp.concatenate([cq, ckv, z_lo, kr, z_hi, z_lo, kr[:, perm], z_hi], axis=1)
    uq = w_uq.reshape(MLA_Q_RANK, MLA_HEADS, MLA_DN + MLA_DR)
    q_nope, q_rope = uq[..., :MLA_DN], uq[..., MLA_DN:]
    zq_hi = jnp.zeros((MLA_Q_RANK, MLA_HEADS, pad_hi), F32)
    wq1 = jnp.concatenate([q_nope, q_rope, zq_hi], axis=-1).reshape(MLA_Q_RANK, -1)
    wq2 = jnp.concatenate([jnp.zeros_like(q_nope), q_rope[..., perm], zq_hi], axis=-1).reshape(MLA_Q_RANK, -1)
    ukv = w_ukv.reshape(MLA_KV_RANK, MLA_HEADS, MLA_DN + MLA_DV)
    k_nope, v = ukv[..., :MLA_DN], ukv[..., MLA_DN:]
    wk = jnp.concatenate([k_nope, jnp.zeros((MLA_KV_RANK, MLA_HEADS, LANES - MLA_DN), F32)], axis=-1)
    wk = wk.reshape(MLA_KV_RANK, -1)
    v_pairs = v.reshape(MLA_KV_RANK, MLA_HEADS // 2, 2, MLA_DV)
    zv = jnp.zeros_like(v_pairs[:, :, 0])
    wv = jnp.stack([jnp.concatenate([v_pairs[:, :, 0], zv], axis=-1),
                    jnp.concatenate([zv, v_pairs[:, :, 1]], axis=-1)], axis=2).reshape(MLA_KV_RANK, -1)
    wvt = v.reshape(MLA_KV_RANK, MLA_HEADS * MLA_DV).T
    bf = lambda a: a.astype(BF16)
    return bf(w_h), bf(w_m), bf(wq1), bf(wq2), bf(wk), bf(wv), bf(wvt)


def _moe_weights(w_group, w_expert, w_gate, w_up, w_down):
    d = w_group.shape[0]
    wr = jnp.concatenate([w_expert, w_group, jnp.zeros((d, LANES - N_EXPERTS - N_GROUPS), F32)], axis=1)
    wr_hi = wr.astype(BF16)
    wr = jnp.concatenate([wr_hi, (wr - wr_hi.astype(F32)).astype(BF16)], axis=1)
    wg = jnp.transpose(w_gate, (1, 0, 2)).reshape(d, N_EXPERTS * EXP_FF).astype(BF16)
    wu = jnp.transpose(w_up, (1, 0, 2)).reshape(d, N_EXPERTS * EXP_FF).astype(BF16)
    wd = w_down.reshape(N_EXPERTS * EXP_FF, d).astype(BF16)
    return wr, wg, wu, wd


def kernel(x_prompt, x_sample, state_hgrn_fwd, state_hgrn_bwd, cache_mla_ckv, cache_mla_krope, c, c_ctx, mod_w, mod_b, ln_g, ln_b, ab_w_in, ab_w_out, hgrn_lb_logits, hgrn_norm_g, mla_q_norm_g, mla_w_uq, mla_kv_norm_g, mla_w_ukv, cd_w_in, cd_w_out, sc_conv_w, cf_conv_w, cf_conv_b, cf_ln_g, cf_ln_b, moe_w_group, moe_w_expert, moe_w_gate, moe_w_up, moe_w_down):
    dec_b = x_sample.shape[0]
    d = D_MODEL
    cvec = jnp.concatenate([c, c_ctx[None, :], jnp.zeros((SUBLANES - dec_b - 1, d), F32)], axis=0)
    mods = _mod_vectors(cvec, mod_w, mod_b)
    rope_tabs = _rope_tables(x_sample.shape[1])
    xp, xs = x_prompt, x_sample
    new_sf = new_sb = new_ckv = new_kr = None
    for l in range(DEPTH):
        mod_lat = mods[l, :dec_b].reshape(dec_b, 6, d)
        mod_ctx = mods[l, dec_b:dec_b + 1].reshape(1, 6, d)
        row = lambda a: a.reshape(1, -1)
        fold = lambda a: a.reshape(-1, PROJ_TILE, a.shape[-1])
        unfold = lambda a: a.reshape(x_prompt.shape[0], -1, a.shape[-1])
        if l % 2 == 0:
            e = l // 2
            w_h, w_m, wq1, wq2, wk, wv, wvt = _ab_weights(ab_w_in[e], mla_w_uq[e], mla_w_ukv[e])
            w_out = ab_w_out[e].astype(BF16)
            qg, kvg, gn = row(mla_q_norm_g[e]), row(mla_kv_norm_g[e]), row(hgrn_norm_g[e])
            ph_p, q_p, k_p, v_p, ckv_p, kr_p = map(unfold, _ab_inproj(
                fold(xp), mod_ctx, w_h, w_m, None, qg, kvg, wq1, None, wk, wv))
            of_p, ob_p, sf, sb = _hgrn(ph_p, hgrn_lb_logits, None, None, e)
            om_p = _attention(q_p, k_p, v_p)
            xp = unfold(_outproj(fold(xp), mod_ctx, (fold(of_p), fold(ob_p), fold(ph_p), gn), fold(om_p), w_out,
                                 row(ln_g[l, 0]), row(ln_b[l, 0]), True))
            ph_s, q_s, k_s, vt_s, _, _ = _ab_inproj(xs, mod_lat, w_h, w_m, rope_tabs, qg, kvg, wq1, wq2, wk, wvt)
            of_s, ob_s, _, _ = _hgrn(ph_s, hgrn_lb_logits, state_hgrn_fwd[:, e], state_hgrn_bwd[:, e], e)
            kc, vtc = _mla_ctx(cache_mla_ckv[:, e], cache_mla_krope[:, e], wk, wvt)
            om_s = _attention_long(q_s, k_s, vt_s, kc, vtc)
            xs = _outproj(xs, mod_lat, (of_s, ob_s, ph_s, gn), om_s, w_out, row(ln_g[l, 0]), row(ln_b[l, 0]), True)
            new_sf, new_sb, new_ckv = sf, sb, ckv_p
            new_kr = kr_p[:, :, MLA_DN:MLA_DN + MLA_DR]
        else:
            jx = l // 2
            w1 = cd_w_in[jx].astype(BF16)
            w_out = cd_w_out[jx].astype(BF16)
            cd = (sc_conv_w[jx], cf_conv_w[jx], row(cf_conv_b[jx]), row(cf_ln_g[jx]), row(cf_ln_b[jx]))
            (p1_p,) = _inproj(fold(xp), mod_ctx, [w1])
            ysc_p, ycf_p = _conv_mixers(unfold(p1_p), *cd)
            xp = unfold(_outproj(fold(xp), mod_ctx, fold(ysc_p), fold(ycf_p), w_out,
                                 row(ln_g[l, 0]), row(ln_b[l, 0]), False))
            (p1_s,) = _inproj(xs, mod_lat, [w1])
            ysc_s, ycf_s = _conv_mixers(p1_s, *cd)
            xs = _outproj(xs, mod_lat, ysc_s, ycf_s, w_out, row(ln_g[l, 0]), row(ln_b[l, 0]), False)
        wr, wg, wu, wd = _moe_weights(moe_w_group[l], moe_w_expert[l], moe_w_gate[l], moe_w_up[l], moe_w_down[l])
        mods_l = mods[l, :dec_b + 1].reshape(dec_b + 1, 6, d)
        xp, xs = _moe(xp, xs, mods_l, wr, wg, wu, wd, row(ln_g[l, 1]), row(ln_b[l, 1]))
    return (xp, xs, new_sf[:, None], new_sb[:, None], new_ckv[:, None], new_kr[:, None])
```
